```python
import jax, jax.numpy as jnp
from jax import lax
import numpy as np

D_MODEL = 1024
BATCH = 2
SEQ = 8192
DEPTH = 1
DEC_BATCH = 32
DEC_SEQ = 32
PAST_LEN = 4096

CHUNK = 64
Q_BLOCK = 128
EPS = 1e-6
A_HEADS = 8
Q_LORA = 384
KV_LORA = 256
NOPE_DIM = 64
ROPE_DIM = 32
V_DIM = 64
ROPE_THETA = 10000.0
B_HEADS = 8
B_DK = 128
B_DV = 64
N_GROUPS = 4
EXPERTS_PER_GROUP = 8
N_EXPERTS = N_GROUPS * EXPERTS_PER_GROUP
TOP_K = 2
D_FF_EXPERT = 256
IN_SPLITS = (Q_LORA, KV_LORA, ROPE_DIM, B_HEADS * B_DK, B_HEADS * B_DK,
             B_HEADS * B_DV, B_HEADS * B_DV, D_MODEL, D_MODEL)
D_IN = Q_LORA + KV_LORA + ROPE_DIM + 2 * B_HEADS * B_DK + 2 * B_HEADS * B_DV + 2 * D_MODEL
SPLIT_POINTS = [int(v) for v in np.cumsum(IN_SPLITS)[:-1]]

kernel_name = "mla_hgrn2_gated_hmoe_stream_step"


def rmsnorm(x, g):
    xf = x.astype(jnp.float32)
    y = xf * lax.rsqrt(jnp.mean(xf * xf, axis=-1, keepdims=True) + EPS)
    return (y * g.astype(jnp.float32)).astype(x.dtype)


def rope_cos_sin(pos):
    half = ROPE_DIM // 2
    inv = jnp.power(ROPE_THETA, -jnp.arange(half, dtype=jnp.float32) / half)
    ang = pos.astype(jnp.float32)[:, None] * inv[None, :]
    return jnp.cos(ang), jnp.sin(ang)


def apply_rope(x, cos, sin):
    half = ROPE_DIM // 2
    xf = x.astype(jnp.float32)
    x1, x2 = xf[..., :half], xf[..., half:]
    return jnp.concatenate([x1 * cos - x2 * sin, x1 * sin + x2 * cos], axis=-1).astype(x.dtype)


def mla_attention(q_nope, q_pe, q_pos, k_nope, k_pe, v):
    B, L, H, _ = q_nope.shape
    qb = min(Q_BLOCK, L)
    nb = L // qb
    k_chunk = jnp.arange(k_nope.shape[1]) // CHUNK
    scale = (NOPE_DIM + ROPE_DIM) ** -0.5

    def one_block(args):
        qn, qp, pb = args
        s = (jnp.einsum('bqhn,bkhn->bhqk', qn, k_nope)
             + jnp.einsum('bqhr,bkr->bhqk', qp, k_pe)).astype(jnp.float32) * scale
        vis = k_chunk[None, :] <= (pb // CHUNK)[:, None]
        s = jnp.where(vis[None, None], s, -jnp.inf)
        p = jax.nn.softmax(s, axis=-1).astype(v.dtype)
        return jnp.einsum('bhqk,bkhv->bqhv', p, v)

    blocks = (q_nope.reshape(B, nb, qb, H, NOPE_DIM).swapaxes(0, 1),
              q_pe.reshape(B, nb, qb, H, ROPE_DIM).swapaxes(0, 1),
              q_pos.reshape(nb, qb))
    out = lax.map(one_block, blocks)
    return out.swapaxes(0, 1).reshape(B, L, H, V_DIM)


def hgrn2_chunk(S, blk):
    q, k, v, logf = blk
    C = q.shape[1]
    cum = jnp.cumsum(logf, axis=1)
    causal = jnp.tril(jnp.ones((C, C), dtype=bool))
    diff = cum[:, :, None] - cum[:, None, :]
    decay = jnp.exp(jnp.where(causal[None, :, :, None, None], diff, -jnp.inf))
    attn = jnp.einsum('bthd,btshd,bshd->bhts', q, decay, k)
    o = (jnp.einsum('bhts,bshv->bthv', attn, v)
         + jnp.einsum('bthd,bhdv->bthv', q * jnp.exp(cum), S))
    last = cum[:, -1]
    S_new = (S * jnp.exp(last)[..., None]
             + jnp.einsum('bshd,bshv->bhdv', k * jnp.exp(last[:, None] - cum), v))
    return S_new, o


def hgrn2_scan(q, k, v, logf, S0):
    B, L = q.shape[:2]
    c = min(CHUNK, L)
    nc = L // c

    def to_chunks(t):
        return t.reshape(B, nc, c, *t.shape[2:]).swapaxes(0, 1)

    S, o = lax.scan(hgrn2_chunk, S0, (to_chunks(q), to_chunks(k), to_chunks(v), to_chunks(logf)))
    return o.swapaxes(0, 1).reshape(B, L, B_HEADS, B_DV), S


def token_mixers(h, pos, past_ckv, past_kpe, S0, lb, w_in, g_q, w_uq, g_kv, w_ukv,
                 g_onorm, w_a_out, w_b_out, w_o):
    B, L, _ = h.shape
    z = h @ w_in
    c_q, c_kv, k_pe_raw, zq, zf, zi, zg, gate_a, gate_b = jnp.split(z, SPLIT_POINTS, axis=-1)

    cos, sin = rope_cos_sin(pos)
    q = (rmsnorm(c_q, g_q) @ w_uq).reshape(B, L, A_HEADS, NOPE_DIM + ROPE_DIM)
    q_nope = q[..., :NOPE_DIM]
    q_pe = apply_rope(q[..., NOPE_DIM:], cos[:, None], sin[:, None])
    ckv = rmsnorm(c_kv, g_kv)
    kpe = apply_rope(k_pe_raw, cos, sin)
    if past_ckv is None:
        keys_lat, keys_pe = ckv, kpe
    else:
        keys_lat = jnp.concatenate([past_ckv.astype(ckv.dtype), ckv], axis=1)
        keys_pe = jnp.concatenate([past_kpe.astype(kpe.dtype), kpe], axis=1)
    kv = jnp.einsum('bkc,chx->bkhx', keys_lat,
                    w_ukv.reshape(KV_LORA, A_HEADS, NOPE_DIM + V_DIM))
    o_a = mla_attention(q_nope, q_pe, pos, kv[..., :NOPE_DIM], keys_pe, kv[..., NOPE_DIM:])
    y_a = o_a.reshape(B, L, A_HEADS * V_DIM) @ w_a_out

    lbh = lb.reshape(B_HEADS, B_DK)
    fpre = zf.astype(jnp.float32).reshape(B, L, B_HEADS, B_DK)
    logf = jnp.log(lbh + (1.0 - lbh) * jax.nn.sigmoid(fpre))
    kb = (1.0 - lbh) * jax.nn.sigmoid(-fpre)
    qb = jax.nn.silu(zq.astype(jnp.float32)).reshape(B, L, B_HEADS, B_DK)
    vb = zi.astype(jnp.float32).reshape(B, L, B_HEADS, B_DV)
    o_b, S_new = hgrn2_scan(qb, kb, vb, logf, S0)
    o_b = rmsnorm(o_b, g_onorm) * jax.nn.silu(zg.astype(jnp.float32).reshape(B, L, B_HEADS, B_DV))
    y_b = o_b.reshape(B, L, B_HEADS * B_DV).astype(h.dtype) @ w_b_out

    m = jax.nn.sigmoid(gate_a) * y_a + jax.nn.sigmoid(gate_b) * y_b
    return m @ w_o, ckv, kpe, S_new


def hier_moe(h, w_rg, b_rg, w_re, b_re, w_gate, w_up, w_down):
    B, L, _ = h.shape
    lg = (h @ w_rg + b_rg).astype(jnp.float32)
    p_grp, grp = lax.top_k(jax.nn.softmax(lg, axis=-1), 1)
    le = (jnp.einsum('bld,gde->blge', h, w_re) + b_re).astype(jnp.float32)
    le_sel = jnp.einsum('blge,blg->ble', le, jax.nn.one_hot(grp[..., 0], N_GROUPS, dtype=jnp.float32))
    top_v, top_i = lax.top_k(le_sel, TOP_K)
    w_tok = p_grp * jax.nn.softmax(top_v, axis=-1)
    expert_id = grp * EXPERTS_PER_GROUP + top_i
    combine = jnp.sum(jax.nn.one_hot(expert_id, N_EXPERTS, dtype=jnp.float32) * w_tok[..., None], axis=-2)
    combine = combine.reshape(B, L, N_GROUPS, EXPERTS_PER_GROUP).astype(h.dtype)
    wg = w_gate.reshape(N_GROUPS, EXPERTS_PER_GROUP, D_MODEL, D_FF_EXPERT)
    wu = w_up.reshape(N_GROUPS, EXPERTS_PER_GROUP, D_MODEL, D_FF_EXPERT)
    wd = w_down.reshape(N_GROUPS, EXPERTS_PER_GROUP, D_FF_EXPERT, D_MODEL)
    out = jnp.zeros_like(h)
    for gi in range(N_GROUPS):
        a = jnp.einsum('bld,edf->blef', h, wg[gi])
        u = jnp.einsum('bld,edf->blef', h, wu[gi])
        hid = jax.nn.silu(a) * u * combine[:, :, gi, :, None]
        out = out + jnp.einsum('blef,efd->bld', hid, wd[gi])
    return out


def run_group(x, past_ckv, past_kpe, past_state, g_mix, w_in, g_q, w_uq, g_kv, w_ukv,
              lb_hgrn, g_onorm, w_a_out, w_b_out, w_o, g_ffn, w_rg, b_rg, w_re, b_re,
              w_gate, w_up, w_down, g_final):
    B, L, _ = x.shape
    past_len = 0 if past_ckv is None else past_ckv.shape[2]
    pos = past_len + jnp.arange(L, dtype=jnp.int32)
    lb_all = jnp.cumsum(jax.nn.softmax(lb_hgrn.astype(jnp.float32), axis=0), axis=0)
    ckvs, kpes, states = [], [], []
    for l in range(DEPTH):
        if past_ckv is None:
            pc, pk = None, None
            s0 = jnp.zeros((B, B_HEADS, B_DK, B_DV), jnp.float32)
        else:
            pc, pk = past_ckv[l], past_kpe[l]
            s0 = past_state[l].astype(jnp.float32)
        mix, ckv, kpe, s_new = token_mixers(rmsnorm(x, g_mix[l]), pos, pc, pk, s0, lb_all[l],
                                            w_in[l], g_q[l], w_uq[l], g_kv[l], w_ukv[l],
                                            g_onorm[l], w_a_out[l], w_b_out[l], w_o[l])
        x = x + mix
        x = x + hier_moe(rmsnorm(x, g_ffn[l]), w_rg[l], b_rg[l], w_re[l], b_re[l],
                         w_gate[l], w_up[l], w_down[l])
        ckvs.append(ckv)
        kpes.append(kpe)
        states.append(s_new)
    state_dtype = x.dtype if past_state is None else past_state.dtype
    return (rmsnorm(x, g_final), jnp.stack(ckvs), jnp.stack(kpes),
            jnp.stack(states).astype(state_dtype))


def setup_inputs(seed: int = 0) -> dict:
    key = jax.random.key(seed)
    ks = jax.random.split(key, 25)

    def nrm(k, shape, scale):
        return jax.random.normal(k, shape, jnp.float32) * scale

    def gain(k, shape):
        return 1.0 + 0.01 * jax.random.normal(k, shape, jnp.float32)

    return {
        "x_prompt": nrm(ks[0], (BATCH, SEQ, D_MODEL), 1.0),
        "x_sample": nrm(ks[1], (DEC_BATCH, DEC_SEQ, D_MODEL), 1.0),
        "cache_ckv": nrm(ks[2], (DEPTH, DEC_BATCH, PAST_LEN, KV_LORA), 1.0),
        "cache_kpe": nrm(ks[3], (DEPTH, DEC_BATCH, PAST_LEN, ROPE_DIM), 1.0),
        "state_hgrn": nrm(ks[4], (DEPTH, DEC_BATCH, B_HEADS, B_DK, B_DV), 0.5),
        "g_mix": gain(ks[5], (DEPTH, D_MODEL)),
        "w_in": nrm(ks[6], (DEPTH, D_MODEL, D_IN), D_MODEL ** -0.5),
        "g_q": gain(ks[7], (DEPTH, Q_LORA)),
        "w_uq": nrm(ks[8], (DEPTH, Q_LORA, A_HEADS * (NOPE_DIM + ROPE_DIM)), Q_LORA ** -0.5),
        "g_kv": gain(ks[9], (DEPTH, KV_LORA)),
        "w_ukv": nrm(ks[10], (DEPTH, KV_LORA, A_HEADS * (NOPE_DIM + V_DIM)), KV_LORA ** -0.5),
        "lb_hgrn": nrm(ks[11], (DEPTH + 1, B_HEADS * B_DK), 0.1),
        "g_onorm": gain(ks[12], (DEPTH, B_DV)),
        "w_a_out": nrm(ks[13], (DEPTH, A_HEADS * V_DIM, D_MODEL), (A_HEADS * V_DIM) ** -0.5),
        "w_b_out": nrm(ks[14], (DEPTH, B_HEADS * B_DV, D_MODEL), (B_HEADS * B_DV) ** -0.5),
        "w_o": nrm(ks[15], (DEPTH, D_MODEL, D_MODEL), D_MODEL ** -0.5),
        "g_ffn": gain(ks[16], (DEPTH, D_MODEL)),
        "w_rg": nrm(ks[17], (DEPTH, D_MODEL, N_GROUPS), D_MODEL ** -0.5),
        "b_rg": nrm(ks[18], (DEPTH, N_GROUPS), 0.01),
        "w_re": nrm(ks[19], (DEPTH, N_GROUPS, D_MODEL, EXPERTS_PER_GROUP), D_MODEL ** -0.5),
        "b_re": nrm(ks[20], (DEPTH, N_GROUPS, EXPERTS_PER_GROUP), 0.01),
        "w_gate": nrm(ks[21], (DEPTH, N_EXPERTS, D_MODEL, D_FF_EXPERT), D_MODEL ** -0.5),
        "w_up": nrm(ks[22], (DEPTH, N_EXPERTS, D_MODEL, D_FF_EXPERT), D_MODEL ** -0.5),
        "w_down": nrm(ks[23], (DEPTH, N_EXPERTS, D_FF_EXPERT, D_MODEL), D_FF_EXPERT ** -0.5),
        "g_final": gain(ks[24], (D_MODEL,)),
    }


def reference(x_prompt, x_sample, cache_ckv, cache_kpe, state_hgrn, g_mix, w_in, g_q, w_uq,
              g_kv, w_ukv, lb_hgrn, g_onorm, w_a_out, w_b_out, w_o, g_ffn, w_rg, b_rg,
              w_re, b_re, w_gate, w_up, w_down, g_final):
    y_prompt, ckv_prompt, kpe_prompt, hgrn_prompt = run_group(
        x_prompt, None, None, None, g_mix, w_in, g_q, w_uq, g_kv, w_ukv, lb_hgrn, g_onorm,
        w_a_out, w_b_out, w_o, g_ffn, w_rg, b_rg, w_re, b_re, w_gate, w_up, w_down, g_final)
    y_sample, ckv_sample, kpe_sample, hgrn_sample = run_group(
        x_sample, cache_ckv, cache_kpe, state_hgrn, g_mix, w_in, g_q, w_uq, g_kv, w_ukv,
        lb_hgrn, g_onorm, w_a_out, w_b_out, w_o, g_ffn, w_rg, b_rg, w_re, b_re, w_gate,
        w_up, w_down, g_final)
    return (y_prompt, y_sample, ckv_prompt, kpe_prompt, hgrn_prompt,
            ckv_sample, kpe_sample, hgrn_sample)
```

```python
import functools

import numpy as np
import jax
import jax.numpy as jnp
from jax import lax
from jax.experimental import pallas as pl
from jax.experimental.pallas import tpu as pltpu

F32 = jnp.float32
BF16 = jnp.bfloat16

EPS = 1e-6
CHUNK = 64
A_HEADS = 8
Q_LORA = 384
KV_LORA = 256
NOPE_DIM = 64
ROPE_DIM = 32
V_DIM = 64
ROPE_THETA = 10000.0
B_HEADS = 8
B_DK = 128
B_DV = 64
N_GROUPS = 4
EXPERTS_PER_GROUP = 8
N_EXPERTS = N_GROUPS * EXPERTS_PER_GROUP
D_FF_EXPERT = 256

LANES = 128
VMEM_BYTES_V7X = 64 * 1024 * 1024
VMEM_LIMIT = VMEM_BYTES_V7X - 8 * 1024 * 1024

HEAD_PAD = LANES
ROPE_LO = NOPE_DIM
ROPE_HI = NOPE_DIM + ROPE_DIM
HALF = ROPE_DIM // 2
SCALE = (NOPE_DIM + ROPE_DIM) ** -0.5
NEG_INF = float("-inf")


def _params(sem):
    return pltpu.CompilerParams(dimension_semantics=sem, vmem_limit_bytes=VMEM_LIMIT)


def _const_spec(shape):
    nd = len(shape)
    return pl.BlockSpec(shape, lambda *_: (0,) * nd, pipeline_mode=pl.Buffered(1))


def _rms(x, g):
    ms = jnp.mean(x * x, axis=-1, keepdims=True)
    return x * lax.rsqrt(ms + EPS) * g


def _sig_pair(x):
    e = jnp.exp(-jnp.abs(x))
    r = 1.0 / (1.0 + e)
    er = e * r
    pos = x >= 0
    return jnp.where(pos, r, er), jnp.where(pos, er, r)


def _nt(a, b):
    return lax.dot_general(a, b, (((1,), (1,)), ((), ())), preferred_element_type=F32)


def _tn(a, b):
    return lax.dot_general(a, b, (((0,), (0,)), ((), ())), preferred_element_type=F32)


def _dot(a, b):
    return jnp.dot(a, b, preferred_element_type=F32)


def _rope(x, cosp, sinlo, sinhi):
    return x * cosp + pltpu.roll(x, LANES - HALF, 1) * sinlo + pltpu.roll(x, HALF, 1) * sinhi


def _mla_proj_kernel(x_ref, gmix_ref, win_ref, gq_ref, wuq_ref, gkv_ref, wuk_ref, wuvt_ref,
                     cos_ref, sinlo_ref, sinhi_ref,
                     h_ref, ckv_ref, kpe_ref, q_ref, k_ref, vt_ref):
    h = _rms(x_ref[...], gmix_ref[...]).astype(BF16)
    h_ref[...] = h
    z = _dot(h, win_ref[...])
    cosp, sinlo, sinhi = cos_ref[...], sinlo_ref[...], sinhi_ref[...]
    lane = lax.broadcasted_iota(jnp.int32, (1, LANES), 1)
    rope_lanes = (lane // ROPE_DIM) == (ROPE_LO // ROPE_DIM)

    cqn = _rms(z[:, :Q_LORA], gq_ref[...]).astype(BF16)
    q = _dot(cqn, wuq_ref[...])
    for hd in range(A_HEADS):
        sl = slice(hd * HEAD_PAD, (hd + 1) * HEAD_PAD)
        qh = q[:, sl]
        qh = jnp.where(rope_lanes, _rope(qh, cosp, sinlo, sinhi), qh)
        q_ref[:, sl] = (qh * SCALE).astype(BF16)

    ckv = _rms(z[:, Q_LORA:Q_LORA + KV_LORA], gkv_ref[...])
    ckv_ref[...] = ckv
    ckv_bf = ckv.astype(BF16)
    kpe_rot = _rope(z[:, Q_LORA + KV_LORA:], cosp, sinlo, sinhi)
    kpe_ref[...] = kpe_rot[:, :ROPE_DIM]
    kpe_placed = jnp.where(rope_lanes, kpe_rot, 0.0)
    kn = _dot(ckv_bf, wuk_ref[...])
    for hd in range(A_HEADS):
        sl = slice(hd * HEAD_PAD, (hd + 1) * HEAD_PAD)
        k_ref[:, sl] = (kn[:, sl] + kpe_placed).astype(BF16)
    vt_ref[...] = _nt(wuvt_ref[...], ckv_bf).astype(BF16)


def _mla_proj(x, gmix, win_a, gq, wuq, gkv, wuk, wuvt, cosp, sinlo, sinhi, tm):
    b, l, d = x.shape
    hq = A_HEADS * HEAD_PAD
    hv = A_HEADS * V_DIM
    tok = lambda w: pl.BlockSpec((None, tm, w), lambda bi, i: (bi, i, 0))
    tab = pl.BlockSpec((tm, LANES), lambda bi, i: (i, 0))
    return pl.pallas_call(
        _mla_proj_kernel,
        grid=(b, l // tm),
        in_specs=[tok(d), _const_spec(gmix.shape), _const_spec(win_a.shape), _const_spec(gq.shape),
                  _const_spec(wuq.shape), _const_spec(gkv.shape), _const_spec(wuk.shape),
                  _const_spec(wuvt.shape), tab, tab, tab],
        out_specs=[tok(d), tok(KV_LORA), tok(ROPE_DIM), tok(hq), tok(hq),
                   pl.BlockSpec((None, hv, tm), lambda bi, i: (bi, 0, i))],
        out_shape=[jax.ShapeDtypeStruct((b, l, d), BF16),
                   jax.ShapeDtypeStruct((b, l, KV_LORA), F32),
                   jax.ShapeDtypeStruct((b, l, ROPE_DIM), F32),
                   jax.ShapeDtypeStruct((b, l, hq), BF16),
                   jax.ShapeDtypeStruct((b, l, hq), BF16),
                   jax.ShapeDtypeStruct((b, hv, l), BF16)],
        compiler_params=_params(("parallel", "parallel")),
        name="mla_proj",
    )(x, gmix, win_a, gq, wuq, gkv, wuk, wuvt, cosp, sinlo, sinhi)


def _hgrn_proj_kernel(h_ref, w_ref, lb_ref, qb_ref, kb_ref, lf_ref, vb_ref, og_ref, sa_ref, sb_ref):
    h = h_ref[...]
    dk = B_HEADS * B_DK
    dv = B_HEADS * B_DV
    dm = sa_ref.shape[-1]
    o = 0
    zq = _dot(h, w_ref[:, o:o + dk]); o += dk
    s, _ = _sig_pair(zq)
    qb_ref[...] = zq * s
    zf = _dot(h, w_ref[:, o:o + dk]); o += dk
    sf, snf = _sig_pair(zf)
    lb = lb_ref[...]
    lf_ref[...] = jnp.log(lb + (1.0 - lb) * sf)
    kb_ref[...] = (1.0 - lb) * snf
    vb_ref[...] = _dot(h, w_ref[:, o:o + dv]).astype(BF16); o += dv
    zg = _dot(h, w_ref[:, o:o + dv]); o += dv
    s, _ = _sig_pair(zg)
    og_ref[...] = (zg * s).astype(BF16)
    s, _ = _sig_pair(_dot(h, w_ref[:, o:o + dm])); o += dm
    sa_ref[...] = s.astype(BF16)
    s, _ = _sig_pair(_dot(h, w_ref[:, o:o + dm]))
    sb_ref[...] = s.astype(BF16)


def _hgrn_proj(h, win_b, lb, tm):
    b, l, d = h.shape
    dk = B_HEADS * B_DK
    dv = B_HEADS * B_DV
    tok = lambda w: pl.BlockSpec((None, tm, w), lambda bi, i: (bi, i, 0))
    sds = lambda w, dt: jax.ShapeDtypeStruct((b, l, w), dt)
    return pl.pallas_call(
        _hgrn_proj_kernel,
        grid=(b, l // tm),
        in_specs=[tok(d), _const_spec(win_b.shape), _const_spec(lb.shape)],
        out_specs=[tok(dk), tok(dk), tok(dk), tok(dv), tok(dv), tok(d), tok(d)],
        out_shape=[sds(dk, F32), sds(dk, F32), sds(dk, F32), sds(dv, BF16), sds(dv, BF16),
                   sds(d, BF16), sds(d, BF16)],
        compiler_params=_params(("parallel", "parallel")),
        name="hgrn_proj",
    )(h, win_b, lb)


def _attn_kernel(q_ref, k_ref, vt_ref, o_ref, m_sc, l_sc, acc_sc, *, t):
    qi = pl.program_id(2)
    q = q_ref[...]
    m_sc[...] = jnp.full(m_sc.shape, NEG_INF, F32)
    l_sc[...] = jnp.zeros(l_sc.shape, F32)
    acc_sc[...] = jnp.zeros(acc_sc.shape, F32)

    def block(kb, masked):
        start = pl.multiple_of(kb * t, t)
        s = _nt(k_ref[pl.ds(start, t), :], q)
        if masked:
            r = lax.broadcasted_iota(jnp.int32, (t, t), 0) // CHUNK
            c = lax.broadcasted_iota(jnp.int32, (t, t), 1) // CHUNK
            s = jnp.where(r <= c, s, NEG_INF)
        m_prev = m_sc[...]
        m_new = jnp.maximum(m_prev, jnp.max(s, axis=0, keepdims=True))
        alpha = jnp.exp(m_prev - m_new)
        p = jnp.exp(s - m_new)
        l_sc[...] = alpha * l_sc[...] + jnp.sum(p, axis=0, keepdims=True)
        acc_sc[...] = alpha * acc_sc[...] + _dot(vt_ref[:, pl.ds(start, t)], p.astype(BF16))
        m_sc[...] = m_new

    def body(kb, carry):
        block(kb, False)
        return carry

    lax.fori_loop(0, qi, body, 0)
    block(qi, True)
    o_ref[...] = (acc_sc[...] / l_sc[...]).astype(BF16)


def _attn(q, k, vt, t):
    b, l, _ = q.shape
    return pl.pallas_call(
        functools.partial(_attn_kernel, t=t),
        grid=(b, A_HEADS, l // t),
        in_specs=[pl.BlockSpec((None, t, HEAD_PAD), lambda bi, h, i: (bi, i, h)),
                  pl.BlockSpec((None, l, HEAD_PAD), lambda bi, h, i: (bi, 0, h)),
                  pl.BlockSpec((None, V_DIM, l), lambda bi, h, i: (bi, h, 0))],
        out_specs=pl.BlockSpec((None, V_DIM, t), lambda bi, h, i: (bi, h, i)),
        out_shape=jax.ShapeDtypeStruct((b, A_HEADS * V_DIM, l), BF16),
        scratch_shapes=[pltpu.VMEM((1, t), F32), pltpu.VMEM((1, t), F32), pltpu.VMEM((V_DIM, t), F32)],
        compiler_params=_params(("parallel", "parallel", "arbitrary")),
        name="attn",
    )(q, k, vt)


def _sample_attn_kernel(q_ref, cc_ref, ck_ref, nc_ref, nk_ref, wq2l_ref, esel_ref, wuvp_ref, o_ref,
                        *, past, n_new):
    qs = q_ref[...]
    heads = [qs[:, hd * HEAD_PAD:(hd + 1) * HEAD_PAD] for hd in range(A_HEADS)]
    ql = jnp.concatenate([_dot(heads[hd], wq2l_ref[hd]) for hd in range(A_HEADS)], axis=0).astype(BF16)
    qp = jnp.concatenate([_dot(heads[hd], esel_ref[...]) for hd in range(A_HEADS)], axis=0).astype(BF16)
    cc = cc_ref[...].astype(BF16)
    ck = ck_ref[...].astype(BF16)
    nc = nc_ref[...].astype(BF16)
    nk = nk_ref[...].astype(BF16)
    s_c = _nt(ql, cc) + _nt(qp, ck)
    s_n = _nt(ql, nc) + _nt(qp, nk)
    rows = A_HEADS * n_new
    tq = lax.broadcasted_iota(jnp.int32, (rows, n_new), 0) % n_new
    tk = lax.broadcasted_iota(jnp.int32, (rows, n_new), 1)
    s_n = jnp.where((past + tk) // CHUNK <= (past + tq) // CHUNK, s_n, NEG_INF)
    m = jnp.maximum(jnp.max(s_c, axis=-1, keepdims=True), jnp.max(s_n, axis=-1, keepdims=True))
    p_c = jnp.exp(s_c - m)
    p_n = jnp.exp(s_n - m)
    denom = jnp.sum(p_c, axis=-1, keepdims=True) + jnp.sum(p_n, axis=-1, keepdims=True)
    o_lat = ((_dot(p_c.astype(BF16), cc) + _dot(p_n.astype(BF16), nc)) / denom).astype(BF16)
    out = _dot(o_lat[0:n_new], wuvp_ref[0])
    for hd in range(1, A_HEADS):
        out = out + _dot(o_lat[hd * n_new:(hd + 1) * n_new], wuvp_ref[hd])
    o_ref[...] = out.astype(BF16)


def _sample_attn(q, cache_ckv, cache_kpe, ckv_new, kpe_new, wq2l, esel, wuvp):
    nb, n_new, hq = q.shape
    past = cache_ckv.shape[1]
    hv = A_HEADS * V_DIM
    per = lambda r, w: pl.BlockSpec((None, r, w), lambda bi: (bi, 0, 0))
    return pl.pallas_call(
        functools.partial(_sample_attn_kernel, past=past, n_new=n_new),
        grid=(nb,),
        in_specs=[per(n_new, hq), per(past, KV_LORA), per(past, ROPE_DIM), per(n_new, KV_LORA),
                  per(n_new, ROPE_DIM), _const_spec(wq2l.shape), _const_spec(esel.shape),
                  _const_spec(wuvp.shape)],
        out_specs=per(n_new, hv),
        out_shape=jax.ShapeDtypeStruct((nb, n_new, hv), BF16),
        compiler_params=_params(("parallel",)),
        name="sample_attn",
    )(q, cache_ckv, cache_kpe, ckv_new, kpe_new, wq2l, esel, wuvp)


def _level_sizes(c):
    sizes = []
    p = c
    while p >= 2:
        sizes.append(p)
        p //= 2
    return sizes


def _level_masks(c):
    t = np.arange(c)[:, None]
    s = np.arange(c)[None, :]
    out = []
    for p in _level_sizes(c):
        out.append((t // p == s // p) & (t % p >= p // 2) & (s % p < p // 2))
    out.append(t == s)
    return np.stack(out).astype(np.float32)


def _boundary(cum, p, c):
    half = p // 2
    if p >= 8:
        parts = [jnp.broadcast_to(cum[i * p + half - 1:i * p + half, :], (p, cum.shape[1]))
                 for i in range(c // p)]
        return jnp.concatenate(parts, axis=0) if len(parts) > 1 else parts[0]
    sub = lax.broadcasted_iota(jnp.int32, (8, cum.shape[1]), 0)
    parts = []
    for g in range(c // 8):
        lo = jnp.broadcast_to(cum[g * 8 + 1:g * 8 + 2, :], (8, cum.shape[1]))
        hi = jnp.broadcast_to(cum[g * 8 + 5:g * 8 + 6, :], (8, cum.shape[1]))
        parts.append(jnp.where(sub < 4, lo, hi))
    return jnp.concatenate(parts, axis=0)


def _hgrn_kernel(q_ref, k_ref, lf_ref, v_ref, og_ref, s0t_ref, gon_ref, mask_ref, tri_ref,
                 o_ref, snt_ref, st_sc, *, c, n_chunks):
    step = pl.program_id(2)
    sizes = _level_sizes(c)
    lane = lax.broadcasted_iota(jnp.int32, (1, LANES), 1)
    low = lane < B_DV
    keep_lo = jnp.where(low, 1.0, 0.0).astype(BF16)
    keep_hi = jnp.where(low, 0.0, 1.0).astype(BF16)

    @pl.when(step == 0)
    def _():
        st_sc[...] = jnp.zeros(st_sc.shape, F32)
        st_sc[0, 0:B_DV, :] = s0t_ref[0]
        st_sc[1, B_DV:2 * B_DV, :] = s0t_ref[1]

    tri = tri_ref[...]

    def chunk(ci, carry):
        r0 = pl.multiple_of(ci * c, c)
        rows = pl.ds(r0, c)
        v_pair = v_ref[rows, :]
        o_pair = jnp.zeros((c, LANES), F32)
        for j in range(2):
            sl = slice(j * B_DK, (j + 1) * B_DK)
            q = q_ref[rows, sl]
            k = k_ref[rows, sl]
            lf = lf_ref[rows, sl]
            hi = lf.astype(BF16)
            r1 = lf - hi.astype(F32)
            mid = r1.astype(BF16)
            lo = (r1 - mid.astype(F32)).astype(BF16)
            cum = _dot(tri, hi) + _dot(tri, mid) + _dot(tri, lo)
            kbf = k.astype(BF16)
            a = _nt(q.astype(BF16), kbf) * mask_ref[len(sizes)]
            for li, p in enumerate(sizes):
                if p == 2:
                    qt = (q * jnp.exp(lf)).astype(BF16)
                    kt = kbf
                else:
                    bnd = _boundary(cum, p, c)
                    qt = (q * jnp.exp(jnp.minimum(cum - bnd, 0.0))).astype(BF16)
                    kt = (k * jnp.exp(jnp.minimum(bnd - cum, 0.0))).astype(BF16)
                a = a + _nt(qt, kt) * mask_ref[li]
            last = cum[c - 1:c, :]
            qhat = (q * jnp.exp(cum)).astype(BF16)
            khat = (k * jnp.exp(last - cum)).astype(BF16)
            v_j = v_pair * (keep_lo if j == 0 else keep_hi)
            st = st_sc[j]
            o_pair = o_pair + _dot(a.astype(BF16), v_j) + _nt(qhat, st.astype(BF16))
            st_sc[j] = st * jnp.exp(last) + _tn(v_j, khat)
        o2 = o_pair * o_pair
        s_lo = jnp.sum(jnp.where(low, o2, 0.0), axis=-1, keepdims=True)
        s_hi = jnp.sum(jnp.where(low, 0.0, o2), axis=-1, keepdims=True)
        ms = jnp.where(low, s_lo, s_hi) * (1.0 / B_DV)
        ob = o_pair * lax.rsqrt(ms + EPS) * gon_ref[...] * og_ref[rows, :].astype(F32)
        o_ref[rows, :] = ob.astype(BF16)
        return carry

    lax.fori_loop(0, n_chunks, chunk, 0)

    @pl.when(step == pl.num_programs(2) - 1)
    def _():
        snt_ref[0] = st_sc[0, 0:B_DV, :]
        snt_ref[1] = st_sc[1, B_DV:2 * B_DV, :]


def _hgrn(qb, kb, lf, vb, og, s0t, gon, tc):
    b, l, dk = qb.shape
    c = min(CHUNK, l)
    tc = min(tc, l)
    masks = jnp.asarray(_level_masks(c))
    tri = jnp.asarray(np.tril(np.ones((c, c), np.float32)), dtype=BF16)
    pairs = B_HEADS // 2
    wide = pl.BlockSpec((None, tc, 2 * B_DK), lambda bi, p, i: (bi, i, p))
    narrow = pl.BlockSpec((None, tc, 2 * B_DV), lambda bi, p, i: (bi, i, p))
    state = pl.BlockSpec((None, 2, B_DV, B_DK), lambda bi, p, i: (bi, p, 0, 0))
    return pl.pallas_call(
        functools.partial(_hgrn_kernel, c=c, n_chunks=tc // c),
        grid=(b, pairs, l // tc),
        in_specs=[wide, wide, wide, narrow, narrow, state, _const_spec(gon.shape),
                  _const_spec(masks.shape), _const_spec(tri.shape)],
        out_specs=[narrow, state],
        out_shape=[jax.ShapeDtypeStruct((b, l, B_HEADS * B_DV), BF16),
                   jax.ShapeDtypeStruct((b, B_HEADS, B_DV, B_DK), F32)],
        scratch_shapes=[pltpu.VMEM((2, 2 * B_DV, B_DK), F32)],
        compiler_params=_params(("parallel", "parallel", "arbitrary")),
        name="hgrn",
    )(qb, kb, lf, vb, og, s0t, gon, masks, tri)


def _route(logits):
    lane = lax.broadcasted_iota(jnp.int32, logits.shape, 1).astype(F32)
    big = float(1 << 20)
    is_grp = jnp.abs(lane - (N_EXPERTS + 0.5 * (N_GROUPS - 1))) < 0.5 * N_GROUPS
    lg = jnp.where(is_grp, logits, NEG_INF)
    gmax = jnp.max(lg, axis=-1, keepdims=True)
    p_grp = 1.0 / jnp.sum(jnp.exp(lg - gmax), axis=-1, keepdims=True)
    grp = jnp.min(jnp.where(lg == gmax, lane, big), axis=-1, keepdims=True) - N_EXPERTS
    first = grp * EXPERTS_PER_GROUP
    in_grp = jnp.abs(lane - first - 0.5 * (EXPERTS_PER_GROUP - 1)) < 0.5 * EXPERTS_PER_GROUP
    le = jnp.where(in_grp, logits, NEG_INF)
    v1 = jnp.max(le, axis=-1, keepdims=True)
    i1 = jnp.min(jnp.where(le == v1, lane, big), axis=-1, keepdims=True)
    le2 = jnp.where(lane == i1, NEG_INF, le)
    v2 = jnp.max(le2, axis=-1, keepdims=True)
    i2 = jnp.min(jnp.where(le2 == v2, lane, big), axis=-1, keepdims=True)
    e2 = jnp.exp(v2 - v1)
    w1 = p_grp / (1.0 + e2)
    w2 = w1 * e2
    return jnp.where(lane == i1, w1, 0.0) + jnp.where(lane == i2, w2, 0.0)


def _out_proj_kernel(oa_ref, ob_ref, sa_ref, sb_ref, x_ref, wa_ref, wb_ref, wo_ref, gffn_ref,
                     wr_ref, br_ref, x1_ref, h2_ref, comb_ref, *, oa_transposed):
    if oa_transposed:
        ya = _tn(oa_ref[...], wa_ref[...])
    else:
        ya = _dot(oa_ref[...], wa_ref[...])
    yb = _dot(ob_ref[...], wb_ref[...])
    m = (sa_ref[...].astype(F32) * ya + sb_ref[...].astype(F32) * yb).astype(BF16)
    x1 = x_ref[...] + _dot(m, wo_ref[...])
    x1_ref[...] = x1
    h2 = _rms(x1, gffn_ref[...]).astype(BF16)
    h2_ref[...] = h2
    comb = _route(_dot(h2, wr_ref[...]) + br_ref[...])
    lane = lax.broadcasted_iota(jnp.int32, (1, LANES), 1)
    for g in range(N_GROUPS):
        shifted = comb if g == 0 else pltpu.roll(comb, LANES - g * EXPERTS_PER_GROUP, 1)
        comb_ref[:, g * LANES:(g + 1) * LANES] = jnp.where(lane < EXPERTS_PER_GROUP, shifted, 0.0)


def _out_proj(oa, ob, sa, sb, x, wa, wb, wo, gffn, wr, br, tm, oa_transposed):
    b, l, d = x.shape
    hv = A_HEADS * V_DIM
    tok = lambda w: pl.BlockSpec((None, tm, w), lambda bi, i: (bi, i, 0))
    oa_spec = pl.BlockSpec((None, hv, tm), lambda bi, i: (bi, 0, i)) if oa_transposed else tok(hv)
    return pl.pallas_call(
        functools.partial(_out_proj_kernel, oa_transposed=oa_transposed),
        grid=(b, l // tm),
        in_specs=[oa_spec, tok(B_HEADS * B_DV), tok(d), tok(d), tok(d), _const_spec(wa.shape),
                  _const_spec(wb.shape), _const_spec(wo.shape), _const_spec(gffn.shape),
                  _const_spec(wr.shape), _const_spec(br.shape)],
        out_specs=[tok(d), tok(d), tok(N_GROUPS * LANES)],
        out_shape=[jax.ShapeDtypeStruct((b, l, d), F32), jax.ShapeDtypeStruct((b, l, d), BF16),
                   jax.ShapeDtypeStruct((b, l, N_GROUPS * LANES), F32)],
        compiler_params=_params(("parallel", "parallel")),
        name="out_proj",
    )(oa, ob, sa, sb, x, wa, wb, wo, gffn, wr, br)


def _moe_kernel(h2_ref, comb_ref, x1_ref, wg_ref, wu_ref, wd_ref, gfin_ref, y_ref, acc_sc):
    g = pl.program_id(2)

    @pl.when(g == 0)
    def _():
        acc_sc[...] = jnp.zeros(acc_sc.shape, F32)

    h2 = h2_ref[...]
    comb = comb_ref[...]
    hid = []
    for e in range(EXPERTS_PER_GROUP):
        a = _dot(h2, wg_ref[e])
        u = _dot(h2, wu_ref[e])
        s, _ = _sig_pair(a)
        hid.append((a * s * u * comb[:, e:e + 1]).astype(BF16))
    hid = jnp.concatenate(hid, axis=-1)
    wd = wd_ref[...].reshape(EXPERTS_PER_GROUP * D_FF_EXPERT, wd_ref.shape[-1])
    acc_sc[...] += _dot(hid, wd)

    @pl.when(g == pl.num_programs(2) - 1)
    def _():
        y_ref[...] = _rms(x1_ref[...] + acc_sc[...], gfin_ref[...])


def _moe(h2, comb, x1, wg, wu, wd, gfin, tm):
    b, l, d = x1.shape
    tok = lambda w: pl.BlockSpec((None, tm, w), lambda bi, i, g: (bi, i, 0))
    return pl.pallas_call(
        _moe_kernel,
        grid=(b, l // tm, N_GROUPS),
        in_specs=[tok(d), pl.BlockSpec((None, tm, LANES), lambda bi, i, g: (bi, i, g)), tok(d),
                  pl.BlockSpec((EXPERTS_PER_GROUP, d, D_FF_EXPERT), lambda bi, i, g: (g, 0, 0)),
                  pl.BlockSpec((EXPERTS_PER_GROUP, d, D_FF_EXPERT), lambda bi, i, g: (g, 0, 0)),
                  pl.BlockSpec((EXPERTS_PER_GROUP, D_FF_EXPERT, d), lambda bi, i, g: (g, 0, 0)),
                  pl.BlockSpec(gfin.shape, lambda bi, i, g: (0, 0))],
        out_specs=tok(d),
        out_shape=jax.ShapeDtypeStruct((b, l, d), F32),
        scratch_shapes=[pltpu.VMEM((tm, d), F32)],
        compiler_params=_params(("parallel", "parallel", "arbitrary")),
        name="moe",
    )(h2, comb, x1, wg, wu, wd, gfin)


def _rope_tables(pos):
    inv = jnp.power(ROPE_THETA, -jnp.arange(HALF, dtype=F32) / HALF)
    ang = pos.astype(F32)[:, None] * inv[None, :]
    cos, sin = jnp.cos(ang), jnp.sin(ang)
    zeros = jnp.zeros_like(sin)
    reps = LANES // ROPE_DIM
    cosp = jnp.tile(jnp.concatenate([cos, cos], axis=1), (1, reps))
    sinlo = jnp.tile(jnp.concatenate([-sin, zeros], axis=1), (1, reps))
    sinhi = jnp.tile(jnp.concatenate([zeros, sin], axis=1), (1, reps))
    return cosp, sinlo, sinhi


def _prep_weights(g_mix, w_in, g_q, w_uq, g_kv, w_ukv, lb_hgrn, g_onorm, w_a_out, w_b_out, w_o,
                  g_ffn, w_rg, b_rg, w_re, b_re, w_gate, w_up, w_down, g_final):
    assert w_in.shape[0] == 1, "single-layer step"
    d = w_in.shape[1]
    w = w_in[0]
    n_a = Q_LORA + KV_LORA
    win_a = jnp.concatenate([w[:, :n_a]] + [w[:, n_a:n_a + ROPE_DIM]] * (LANES // ROPE_DIM), axis=1).astype(BF16)
    win_b = w[:, n_a + ROPE_DIM:].astype(BF16)
    per_q = NOPE_DIM + ROPE_DIM
    wuq = jnp.pad(w_uq[0].reshape(Q_LORA, A_HEADS, per_q), ((0, 0), (0, 0), (0, HEAD_PAD - per_q)))
    wuq = wuq.reshape(Q_LORA, A_HEADS * HEAD_PAD).astype(BF16)
    wukv = w_ukv[0].reshape(KV_LORA, A_HEADS, NOPE_DIM + V_DIM)
    w_uk, w_uv = wukv[..., :NOPE_DIM], wukv[..., NOPE_DIM:]
    wuk = jnp.pad(w_uk, ((0, 0), (0, 0), (0, HEAD_PAD - NOPE_DIM))).reshape(KV_LORA, A_HEADS * HEAD_PAD).astype(BF16)
    wuvt = w_uv.reshape(KV_LORA, A_HEADS * V_DIM).T.astype(BF16)
    wq2l = jnp.pad(jnp.transpose(w_uk, (1, 2, 0)), ((0, 0), (0, HEAD_PAD - NOPE_DIM), (0, 0))).astype(BF16)
    esel = jnp.zeros((HEAD_PAD, ROPE_DIM), F32).at[ROPE_LO + jnp.arange(ROPE_DIM), jnp.arange(ROPE_DIM)].set(1.0).astype(BF16)
    eye = jnp.eye(A_HEADS, dtype=F32)
    wuvp = jnp.einsum("chv,hg->hcgv", w_uv, eye).reshape(A_HEADS, KV_LORA, A_HEADS * V_DIM).astype(BF16)
    lb = jnp.cumsum(jax.nn.softmax(lb_hgrn.astype(F32), axis=0), axis=0)[0][None, :]
    gon = jnp.tile(g_onorm[0], 2)[None, :]
    wr = jnp.concatenate([jnp.transpose(w_re[0], (1, 0, 2)).reshape(d, N_EXPERTS), w_rg[0],
                          jnp.zeros((d, LANES - N_EXPERTS - N_GROUPS), F32)], axis=1).astype(BF16)
    br = jnp.concatenate([b_re[0].reshape(N_EXPERTS), b_rg[0],
                          jnp.zeros((LANES - N_EXPERTS - N_GROUPS,), F32)])[None, :]
    return dict(
        gmix=g_mix[0][None, :], win_a=win_a, win_b=win_b, gq=g_q[0][None, :], wuq=wuq,
        gkv=g_kv[0][None, :], wuk=wuk, wuvt=wuvt, wq2l=wq2l, esel=esel, wuvp=wuvp, lb=lb, gon=gon,
        wa=w_a_out[0].astype(BF16), wb=w_b_out[0].astype(BF16), wo=w_o[0].astype(BF16),
        gffn=g_ffn[0][None, :], wr=wr, br=br, wg=w_gate[0].astype(BF16), wu=w_up[0].astype(BF16),
        wd=w_down[0].astype(BF16), gfin=g_final[None, :])


def _tile(n, want):
    t = min(n, want)
    assert n % t == 0
    return t


def _run_group(x, pos, streams, past, w):
    b, l, d = x.shape
    ns, ls = streams
    tm = _tile(l, 512)
    cosp, sinlo, sinhi = _rope_tables(pos)
    h, ckv, kpe, q, k, vt = _mla_proj(x, w["gmix"], w["win_a"], w["gq"], w["wuq"], w["gkv"], w["wuk"],
                                      w["wuvt"], cosp, sinlo, sinhi, tm)
    qb, kb, lf, vb, og, sa, sb = _hgrn_proj(h, w["win_b"], w["lb"], tm)
    as_streams = lambda a: a.reshape(ns, ls, a.shape[-1])
    if past is None:
        oa = _attn(q, k, vt, _tile(l, 512))
        s0t = jnp.zeros((ns, B_HEADS, B_DV, B_DK), F32)
    else:
        cache_ckv, cache_kpe, state = past
        oa = _sample_attn(as_streams(q), cache_ckv, cache_kpe, as_streams(ckv), as_streams(kpe),
                          w["wq2l"], w["esel"], w["wuvp"]).reshape(b, l, A_HEADS * V_DIM)
        s0t = jnp.swapaxes(state.astype(F32), -1, -2)
    ob, snt = _hgrn(as_streams(qb), as_streams(kb), as_streams(lf), as_streams(vb), as_streams(og),
                    s0t, w["gon"], 512)
    ob = ob.reshape(b, l, B_HEADS * B_DV)
    x1, h2, comb = _out_proj(oa, ob, sa, sb, x, w["wa"], w["wb"], w["wo"], w["gffn"], w["wr"], w["br"],
                             tm, past is None)
    y = _moe(h2, comb, x1, w["wg"], w["wu"], w["wd"], w["gfin"], tm)
    return y, ckv, kpe, jnp.swapaxes(snt, -1, -2)


def kernel(x_prompt, x_sample, cache_ckv, cache_kpe, state_hgrn, g_mix, w_in, g_q, w_uq, g_kv, w_ukv,
           lb_hgrn, g_onorm, w_a_out, w_b_out, w_o, g_ffn, w_rg, b_rg, w_re, b_re, w_gate, w_up, w_down,
           g_final):
    w = _prep_weights(g_mix, w_in, g_q, w_uq, g_kv, w_ukv, lb_hgrn, g_onorm, w_a_out, w_b_out, w_o,
                      g_ffn, w_rg, b_rg, w_re, b_re, w_gate, w_up, w_down, g_final)
    bp, lp, d = x_prompt.shape
    y_p, ckv_p, kpe_p, st_p = _run_group(x_prompt, jnp.arange(lp, dtype=jnp.int32), (bp, lp), None, w)

    bs, ls, _ = x_sample.shape
    past_len = cache_ckv.shape[2]
    pos_s = past_len + (jnp.arange(bs * ls, dtype=jnp.int32) % ls)
    y_s, ckv_s, kpe_s, st_s = _run_group(x_sample.reshape(1, bs * ls, d), pos_s, (bs, ls),
                                         (cache_ckv[0], cache_kpe[0], state_hgrn[0]), w)
    return (y_p, y_s.reshape(bs, ls, d),
            ckv_p[None], kpe_p[None], st_p[None].astype(x_prompt.dtype),
            ckv_s.reshape(1, bs, ls, KV_LORA), kpe_s.reshape(1, bs, ls, ROPE_DIM),
            st_s[None].astype(state_hgrn.dtype))
```

```python
import functools

import numpy as np
import jax
import jax.numpy as jnp
from jax import lax
from jax.experimental import pallas as pl
from jax.experimental.pallas import tpu as pltpu

F32 = jnp.float32
BF16 = jnp.bfloat16

EPS = 1e-6
CHUNK = 64
A_HEADS = 8
Q_LORA = 384
KV_LORA = 256
NOPE_DIM = 64
ROPE_DIM = 32
V_DIM = 64
ROPE_THETA = 10000.0
B_HEADS = 8
B_DK = 128
B_DV = 64
N_GROUPS = 4
EXPERTS_PER_GROUP = 8
N_EXPERTS = N_GROUPS * EXPERTS_PER_GROUP
D_FF_EXPERT = 256

LANES = 128
VMEM_BYTES_V7X = 64 * 1024 * 1024
VMEM_LIMIT = VMEM_BYTES_V7X - 8 * 1024 * 1024

HEAD_PAD = LANES
ROPE_LO = NOPE_DIM
ROPE_HI = NOPE_DIM + ROPE_DIM
HALF = ROPE_DIM // 2
SCALE = (NOPE_DIM + ROPE_DIM) ** -0.5
LOG2E = 1.4426950408889634
NEG_INF = float("-inf")


def _params(sem):
    return pltpu.CompilerParams(dimension_semantics=sem, vmem_limit_bytes=VMEM_LIMIT)


def _const_spec(shape):
    nd = len(shape)
    return pl.BlockSpec(shape, lambda *_: (0,) * nd, pipeline_mode=pl.Buffered(1))


def _rms(x, g):
    ms = jnp.mean(x * x, axis=-1, keepdims=True)
    return x * lax.rsqrt(ms + EPS) * g


def _sig_pair(x):
    e = jnp.exp(-jnp.abs(x))
    r = 1.0 / (1.0 + e)
    er = e * r
    pos = x >= 0
    return jnp.where(pos, r, er), jnp.where(pos, er, r)


def _nt(a, b):
    return lax.dot_general(a, b, (((1,), (1,)), ((), ())), preferred_element_type=F32)


def _tn(a, b):
    return lax.dot_general(a, b, (((0,), (0,)), ((), ())), preferred_element_type=F32)


def _dot(a, b):
    return jnp.dot(a, b, preferred_element_type=F32)


def _rope(x, cosp, sinlo, sinhi):
    return x * cosp + pltpu.roll(x, LANES - HALF, 1) * sinlo + pltpu.roll(x, HALF, 1) * sinhi


def _mla_proj_kernel(x_ref, gmix_ref, win_ref, gq_ref, wuq_ref, gkv_ref, wuk_ref, wuvt_ref,
                     cos_ref, sinlo_ref, sinhi_ref,
                     h_ref, ckv_ref, kpe_ref, q_ref, k_ref, vt_ref, *, q_transposed):
    h = _rms(x_ref[...], gmix_ref[...]).astype(BF16)
    h_ref[...] = h
    z = _dot(h, win_ref[...])
    cosp, sinlo, sinhi = cos_ref[...], sinlo_ref[...], sinhi_ref[...]
    lane = lax.broadcasted_iota(jnp.int32, (1, LANES), 1)
    rope_lanes = (lane // ROPE_DIM) == (ROPE_LO // ROPE_DIM)

    cqn = _rms(z[:, :Q_LORA], gq_ref[...]).astype(BF16)
    q = _dot(cqn, wuq_ref[...])
    for hd in range(A_HEADS):
        sl = slice(hd * HEAD_PAD, (hd + 1) * HEAD_PAD)
        qh = q[:, sl]
        qh = jnp.where(rope_lanes, _rope(qh, cosp, sinlo, sinhi), qh) * (SCALE * LOG2E)
        if q_transposed:
            q_ref[sl, :] = qh.T.astype(BF16)
        else:
            q_ref[:, sl] = qh.astype(BF16)

    ckv = _rms(z[:, Q_LORA:Q_LORA + KV_LORA], gkv_ref[...])
    ckv_ref[...] = ckv
    ckv_bf = ckv.astype(BF16)
    kpe_rot = _rope(z[:, Q_LORA + KV_LORA:], cosp, sinlo, sinhi)
    kpe_ref[...] = kpe_rot[:, :ROPE_DIM]
    kpe_placed = jnp.where(rope_lanes, kpe_rot, 0.0)
    kn = _dot(ckv_bf, wuk_ref[...])
    for hd in range(A_HEADS):
        sl = slice(hd * HEAD_PAD, (hd + 1) * HEAD_PAD)
        k_ref[:, sl] = (kn[:, sl] + kpe_placed).astype(BF16)
    vt_ref[...] = _nt(wuvt_ref[...], ckv_bf).astype(BF16)


def _mla_proj(x, gmix, win_a, gq, wuq, gkv, wuk, wuvt, cosp, sinlo, sinhi, tm, q_transposed):
    b, l, d = x.shape
    hq = A_HEADS * HEAD_PAD
    hv = A_HEADS * V_DIM
    tok = lambda w: pl.BlockSpec((None, tm, w), lambda bi, i: (bi, i, 0))
    tok_t = lambda w: pl.BlockSpec((None, w, tm), lambda bi, i: (bi, 0, i))
    tab = pl.BlockSpec((tm, LANES), lambda bi, i: (i, 0))
    return pl.pallas_call(
        functools.partial(_mla_proj_kernel, q_transposed=q_transposed),
        grid=(b, l // tm),
        in_specs=[tok(d), _const_spec(gmix.shape), _const_spec(win_a.shape), _const_spec(gq.shape),
                  _const_spec(wuq.shape), _const_spec(gkv.shape), _const_spec(wuk.shape),
                  _const_spec(wuvt.shape), tab, tab, tab],
        out_specs=[tok(d), tok(KV_LORA), tok(ROPE_DIM), tok_t(hq) if q_transposed else tok(hq), tok(hq),
                   tok_t(hv)],
        out_shape=[jax.ShapeDtypeStruct((b, l, d), BF16),
                   jax.ShapeDtypeStruct((b, l, KV_LORA), F32),
                   jax.ShapeDtypeStruct((b, l, ROPE_DIM), F32),
                   jax.ShapeDtypeStruct((b, hq, l) if q_transposed else (b, l, hq), BF16),
                   jax.ShapeDtypeStruct((b, l, hq), BF16),
                   jax.ShapeDtypeStruct((b, hv, l), BF16)],
        compiler_params=_params(("parallel", "parallel")),
        name="mla_proj",
    )(x, gmix, win_a, gq, wuq, gkv, wuk, wuvt, cosp, sinlo, sinhi)


def _hgrn_proj_kernel(h_ref, w_ref, lb_ref, qb_ref, kb_ref, lf_ref, vb_ref, og_ref, sa_ref, sb_ref):
    h = h_ref[...]
    dk = B_HEADS * B_DK
    dv = B_HEADS * B_DV
    dm = sa_ref.shape[-1]
    o = 0
    zq = _dot(h, w_ref[:, o:o + dk]); o += dk
    s, _ = _sig_pair(zq)
    qb_ref[...] = zq * s
    zf = _dot(h, w_ref[:, o:o + dk]); o += dk
    sf, snf = _sig_pair(zf)
    lb = lb_ref[...]
    lf_ref[...] = jnp.log(lb + (1.0 - lb) * sf)
    kb_ref[...] = (1.0 - lb) * snf
    vb_ref[...] = _dot(h, w_ref[:, o:o + dv]).astype(BF16); o += dv
    zg = _dot(h, w_ref[:, o:o + dv]); o += dv
    s, _ = _sig_pair(zg)
    og_ref[...] = (zg * s).astype(BF16)
    s, _ = _sig_pair(_dot(h, w_ref[:, o:o + dm])); o += dm
    sa_ref[...] = s.astype(BF16)
    s, _ = _sig_pair(_dot(h, w_ref[:, o:o + dm]))
    sb_ref[...] = s.astype(BF16)


def _hgrn_proj(h, win_b, lb, tm):
    b, l, d = h.shape
    dk = B_HEADS * B_DK
    dv = B_HEADS * B_DV
    tok = lambda w: pl.BlockSpec((None, tm, w), lambda bi, i: (bi, i, 0))
    sds = lambda w, dt: jax.ShapeDtypeStruct((b, l, w), dt)
    return pl.pallas_call(
        _hgrn_proj_kernel,
        grid=(b, l // tm),
        in_specs=[tok(d), _const_spec(win_b.shape), _const_spec(lb.shape)],
        out_specs=[tok(dk), tok(dk), tok(dk), tok(dv), tok(dv), tok(d), tok(d)],
        out_shape=[sds(dk, F32), sds(dk, F32), sds(dk, F32), sds(dv, BF16), sds(dv, BF16),
                   sds(d, BF16), sds(d, BF16)],
        compiler_params=_params(("parallel", "parallel")),
        name="hgrn_proj",
    )(h, win_b, lb)


def _attn_kernel(qt_ref, k_ref, vt_ref, o_ref, s_sc, m_sc, l_sc, acc_sc, *, t, tc):
    qi = pl.program_id(2)
    n_chains = t // tc
    m_sc[...] = jnp.full(m_sc.shape, NEG_INF, F32)
    l_sc[...] = jnp.zeros(l_sc.shape, F32)
    acc_sc[...] = jnp.zeros(acc_sc.shape, F32)

    def scores(kb, slot):
        start = pl.multiple_of(kb * t, t)
        k = k_ref[pl.ds(start, t), :]
        for ch in range(n_chains):
            cols = slice(ch * tc, (ch + 1) * tc)
            s_sc[slot, :, cols] = _dot(k, qt_ref[:, cols])

    def consume(kb, slot, masked):
        start = pl.multiple_of(kb * t, t)
        vt = vt_ref[:, pl.ds(start, t)]
        for ch in range(n_chains):
            cols = slice(ch * tc, (ch + 1) * tc)
            s = s_sc[slot, :, cols]
            if masked:
                r = lax.broadcasted_iota(jnp.int32, (t, tc), 0) // CHUNK
                c = (lax.broadcasted_iota(jnp.int32, (t, tc), 1) + ch * tc) // CHUNK
                s = jnp.where(r <= c, s, NEG_INF)
            m_prev = m_sc[:, cols]
            m_new = jnp.maximum(m_prev, jnp.max(s, axis=0, keepdims=True))
            alpha = jnp.exp2(m_prev - m_new)
            p = jnp.exp2(s - m_new)
            l_sc[:, cols] = alpha * l_sc[:, cols] + jnp.sum(p, axis=0, keepdims=True)
            acc_sc[:, cols] = alpha * acc_sc[:, cols] + _dot(vt, p.astype(BF16))
            m_sc[:, cols] = m_new

    scores(0, 0)

    def body(j, carry):
        kb = 2 * j
        scores(kb + 1, 1)
        consume(kb, 0, False)
        scores(kb + 2, 0)
        consume(kb + 1, 1, False)
        return carry

    lax.fori_loop(0, qi // 2, body, 0)

    @pl.when(qi % 2 == 0)
    def _():
        consume(qi, 0, True)

    @pl.when(qi % 2 == 1)
    def _():
        scores(qi, 1)
        consume(qi - 1, 0, False)
        consume(qi, 1, True)

    o_ref[...] = (acc_sc[...] / l_sc[...]).astype(BF16)


def _attn(qt, k, vt, t, tc):
    b, l, _ = k.shape
    return pl.pallas_call(
        functools.partial(_attn_kernel, t=t, tc=min(tc, t)),
        grid=(b, A_HEADS, l // t),
        in_specs=[pl.BlockSpec((None, HEAD_PAD, t), lambda bi, h, i: (bi, h, i)),
                  pl.BlockSpec((None, l, HEAD_PAD), lambda bi, h, i: (bi, 0, h)),
                  pl.BlockSpec((None, V_DIM, l), lambda bi, h, i: (bi, h, 0))],
        out_specs=pl.BlockSpec((None, V_DIM, t), lambda bi, h, i: (bi, h, i)),
        out_shape=jax.ShapeDtypeStruct((b, A_HEADS * V_DIM, l), BF16),
        scratch_shapes=[pltpu.VMEM((2, t, t), F32), pltpu.VMEM((1, t), F32), pltpu.VMEM((1, t), F32),
                        pltpu.VMEM((V_DIM, t), F32)],
        compiler_params=_params(("parallel", "parallel", "arbitrary")),
        name="attn",
    )(qt, k, vt)


def _sample_attn_kernel(q_ref, cc_ref, ck_ref, nc_ref, nk_ref, wq2l_ref, esel_ref, wuvp_ref, o_ref,
                        *, past, n_new):
    qs = q_ref[...]
    heads = [qs[:, hd * HEAD_PAD:(hd + 1) * HEAD_PAD] for hd in range(A_HEADS)]
    ql = jnp.concatenate([_dot(heads[hd], wq2l_ref[hd]) for hd in range(A_HEADS)], axis=0).astype(BF16)
    qp = jnp.concatenate([_dot(heads[hd], esel_ref[...]) for hd in range(A_HEADS)], axis=0).astype(BF16)
    cc = cc_ref[...].astype(BF16)
    ck = ck_ref[...].astype(BF16)
    nc = nc_ref[...].astype(BF16)
    nk = nk_ref[...].astype(BF16)
    s_c = _nt(ql, cc) + _nt(qp, ck)
    s_n = _nt(ql, nc) + _nt(qp, nk)
    rows = A_HEADS * n_new
    tq = lax.broadcasted_iota(jnp.int32, (rows, n_new), 0) % n_new
    tk = lax.broadcasted_iota(jnp.int32, (rows, n_new), 1)
    s_n = jnp.where((past + tk) // CHUNK <= (past + tq) // CHUNK, s_n, NEG_INF)
    m = jnp.maximum(jnp.max(s_c, axis=-1, keepdims=True), jnp.max(s_n, axis=-1, keepdims=True))
    p_c = jnp.exp2(s_c - m)
    p_n = jnp.exp2(s_n - m)
    denom = jnp.sum(p_c, axis=-1, keepdims=True) + jnp.sum(p_n, axis=-1, keepdims=True)
    o_lat = ((_dot(p_c.astype(BF16), cc) + _dot(p_n.astype(BF16), nc)) / denom).astype(BF16)
    out = _dot(o_lat[0:n_new], wuvp_ref[0])
    for hd in range(1, A_HEADS):
        out = out + _dot(o_lat[hd * n_new:(hd + 1) * n_new], wuvp_ref[hd])
    o_ref[...] = out.astype(BF16)


def _sample_attn(q, cache_ckv, cache_kpe, ckv_new, kpe_new, wq2l, esel, wuvp):
    nb, n_new, hq = q.shape
    past = cache_ckv.shape[1]
    hv = A_HEADS * V_DIM
    per = lambda r, w: pl.BlockSpec((None, r, w), lambda bi: (bi, 0, 0))
    return pl.pallas_call(
        functools.partial(_sample_attn_kernel, past=past, n_new=n_new),
        grid=(nb,),
        in_specs=[per(n_new, hq), per(past, KV_LORA), per(past, ROPE_DIM), per(n_new, KV_LORA),
                  per(n_new, ROPE_DIM), _const_spec(wq2l.shape), _const_spec(esel.shape),
                  _const_spec(wuvp.shape)],
        out_specs=per(n_new, hv),
        out_shape=jax.ShapeDtypeStruct((nb, n_new, hv), BF16),
        compiler_params=_params(("parallel",)),
        name="sample_attn",
    )(q, cache_ckv, cache_kpe, ckv_new, kpe_new, wq2l, esel, wuvp)


def _level_sizes(c):
    sizes = []
    p = c
    while p >= 2:
        sizes.append(p)
        p //= 2
    return sizes


def _level_masks(c):
    t = np.arange(c)[:, None]
    s = np.arange(c)[None, :]
    out = []
    for p in _level_sizes(c):
        out.append((t // p == s // p) & (t % p >= p // 2) & (s % p < p // 2))
    out.append(t == s)
    return np.stack(out).astype(np.float32)


def _boundary(cum, p, c):
    half = p // 2
    if p >= 8:
        parts = [jnp.broadcast_to(cum[i * p + half - 1:i * p + half, :], (p, cum.shape[1]))
                 for i in range(c // p)]
        return jnp.concatenate(parts, axis=0) if len(parts) > 1 else parts[0]
    sub = lax.broadcasted_iota(jnp.int32, (8, cum.shape[1]), 0)
    parts = []
    for g in range(c // 8):
        lo = jnp.broadcast_to(cum[g * 8 + 1:g * 8 + 2, :], (8, cum.shape[1]))
        hi = jnp.broadcast_to(cum[g * 8 + 5:g * 8 + 6, :], (8, cum.shape[1]))
        parts.append(jnp.where(sub < 4, lo, hi))
    return jnp.concatenate(parts, axis=0)


def _hgrn_kernel(q_ref, k_ref, lf_ref, v_ref, og_ref, s0t_ref, gon_ref, mask_ref, tri_ref,
                 o_ref, snt_ref, st_sc, *, c, n_chunks, per_chunk_state):
    step = pl.program_id(2)
    sizes = _level_sizes(c)
    lane = lax.broadcasted_iota(jnp.int32, (1, LANES), 1)
    low = lane < B_DV
    keep_lo = jnp.where(low, 1.0, 0.0).astype(BF16)
    keep_hi = jnp.where(low, 0.0, 1.0).astype(BF16)

    if not per_chunk_state:
        @pl.when(step == 0)
        def _():
            st_sc[...] = jnp.zeros(st_sc.shape, F32)
            st_sc[0, 0:B_DV, :] = s0t_ref[0]
            st_sc[1, B_DV:2 * B_DV, :] = s0t_ref[1]

    tri = tri_ref[...]
    rows = [slice(ci * c, (ci + 1) * c) for ci in range(n_chunks)]
    heads = [slice(j * B_DK, (j + 1) * B_DK) for j in range(2)]
    items = [(ci, j) for ci in range(n_chunks) for j in range(2)]


    cums = []
    for ci in range(n_chunks):
        lf = lf_ref[rows[ci], :]
        hi = lf.astype(BF16)
        r1 = lf - hi.astype(F32)
        mid = r1.astype(BF16)
        lo = (r1 - mid.astype(F32)).astype(BF16)
        cums.append(_dot(tri, hi) + _dot(tri, mid) + _dot(tri, lo))

    a_mats = {}
    for ci, j in items:
        q = q_ref[rows[ci], heads[j]]
        k = k_ref[rows[ci], heads[j]]
        cum = cums[ci][:, heads[j]]
        kbf = k.astype(BF16)
        a = _nt(q.astype(BF16), kbf) * mask_ref[len(sizes)]
        for li, p in enumerate(sizes):
            if p == 2:
                qt = (q * jnp.exp(lf_ref[rows[ci], heads[j]])).astype(BF16)
                kt = kbf
            else:
                bnd = _boundary(cum, p, c)
                qt = (q * jnp.exp(jnp.minimum(cum - bnd, 0.0))).astype(BF16)
                kt = (k * jnp.exp(jnp.minimum(bnd - cum, 0.0))).astype(BF16)
            a = a + _nt(qt, kt) * mask_ref[li]
        a_mats[ci, j] = a.astype(BF16)

    qhats, kvs, decays, vs = {}, {}, {}, {}
    for ci, j in items:
        cum = cums[ci][:, heads[j]]
        last = cum[c - 1:c, :]
        vs[ci, j] = v_ref[rows[ci], :] * (keep_lo if j == 0 else keep_hi)
        qhats[ci, j] = (q_ref[rows[ci], heads[j]] * jnp.exp(cum)).astype(BF16)
        khat = (k_ref[rows[ci], heads[j]] * jnp.exp(last - cum)).astype(BF16)
        kvs[ci, j] = _tn(vs[ci, j], khat)
        decays[ci, j] = jnp.exp(last)

    pad = jnp.zeros((B_DV, B_DK), F32)
    if not per_chunk_state:
        st = [st_sc[0], st_sc[1]]
    for ci in range(n_chunks):
        if per_chunk_state:
            st = [jnp.concatenate([s0t_ref[ci, 0], pad], axis=0), jnp.concatenate([pad, s0t_ref[ci, 1]], axis=0)]
        o_pair = jnp.zeros((c, LANES), F32)
        for j in range(2):
            o_pair = o_pair + _dot(a_mats[ci, j], vs[ci, j]) + _nt(qhats[ci, j], st[j].astype(BF16))
            st[j] = st[j] * decays[ci, j] + kvs[ci, j]
        if per_chunk_state:
            snt_ref[ci, 0] = st[0][0:B_DV, :]
            snt_ref[ci, 1] = st[1][B_DV:2 * B_DV, :]
        o2 = o_pair * o_pair
        s_lo = jnp.sum(jnp.where(low, o2, 0.0), axis=-1, keepdims=True)
        s_hi = jnp.sum(jnp.where(low, 0.0, o2), axis=-1, keepdims=True)
        ms = jnp.where(low, s_lo, s_hi) * (1.0 / B_DV)
        ob = o_pair * lax.rsqrt(ms + EPS) * gon_ref[...] * og_ref[rows[ci], :].astype(F32)
        o_ref[rows[ci], :] = ob.astype(BF16)
    if not per_chunk_state:
        st_sc[0] = st[0]
        st_sc[1] = st[1]

        @pl.when(step == pl.num_programs(2) - 1)
        def _():
            snt_ref[0] = st_sc[0, 0:B_DV, :]
            snt_ref[1] = st_sc[1, B_DV:2 * B_DV, :]


def _hgrn(qb, kb, lf, vb, og, s0t, gon, n_streams, tc):
    b, l, dk = qb.shape
    per_chunk_state = n_streams != b
    if per_chunk_state:
        assert b == 1 and l % n_streams == 0 and l // n_streams <= CHUNK
        c = l // n_streams
    else:
        c = min(CHUNK, l)
    tc = min(tc, l)
    n_chunks = tc // c
    masks = jnp.asarray(_level_masks(c))
    tri = jnp.asarray(np.tril(np.ones((c, c), np.float32)), dtype=BF16)
    pairs = B_HEADS // 2
    wide = pl.BlockSpec((None, tc, 2 * B_DK), lambda bi, p, i: (bi, i, p))
    narrow = pl.BlockSpec((None, tc, 2 * B_DV), lambda bi, p, i: (bi, i, p))
    if per_chunk_state:
        state = pl.BlockSpec((n_chunks, 2, B_DV, B_DK), lambda bi, p, i: (i, p, 0, 0))
    else:
        state = pl.BlockSpec((None, 2, B_DV, B_DK), lambda bi, p, i: (bi, p, 0, 0))
    return pl.pallas_call(
        functools.partial(_hgrn_kernel, c=c, n_chunks=n_chunks, per_chunk_state=per_chunk_state),
        grid=(b, pairs, l // tc),
        in_specs=[wide, wide, wide, narrow, narrow, state, _const_spec(gon.shape),
                  _const_spec(masks.shape), _const_spec(tri.shape)],
        out_specs=[narrow, state],
        out_shape=[jax.ShapeDtypeStruct((b, l, B_HEADS * B_DV), BF16),
                   jax.ShapeDtypeStruct((n_streams, B_HEADS, B_DV, B_DK), F32)],
        scratch_shapes=[pltpu.VMEM((2, 2 * B_DV, B_DK), F32)],
        compiler_params=_params(("parallel", "parallel", "parallel" if per_chunk_state else "arbitrary")),
        name="hgrn",
    )(qb, kb, lf, vb, og, s0t, gon, masks, tri)


def _route(logits):
    lane = lax.broadcasted_iota(jnp.int32, logits.shape, 1).astype(F32)
    big = float(1 << 20)
    is_grp = jnp.abs(lane - (N_EXPERTS + 0.5 * (N_GROUPS - 1))) < 0.5 * N_GROUPS
    lg = jnp.where(is_grp, logits, NEG_INF)
    gmax = jnp.max(lg, axis=-1, keepdims=True)
    p_grp = 1.0 / jnp.sum(jnp.exp(lg - gmax), axis=-1, keepdims=True)
    grp = jnp.min(jnp.where(lg == gmax, lane, big), axis=-1, keepdims=True) - N_EXPERTS
    first = grp * EXPERTS_PER_GROUP
    in_grp = jnp.abs(lane - first - 0.5 * (EXPERTS_PER_GROUP - 1)) < 0.5 * EXPERTS_PER_GROUP
    le = jnp.where(in_grp, logits, NEG_INF)
    v1 = jnp.max(le, axis=-1, keepdims=True)
    i1 = jnp.min(jnp.where(le == v1, lane, big), axis=-1, keepdims=True)
    le2 = jnp.where(lane == i1, NEG_INF, le)
    v2 = jnp.max(le2, axis=-1, keepdims=True)
    i2 = jnp.min(jnp.where(le2 == v2, lane, big), axis=-1, keepdims=True)
    e2 = jnp.exp(v2 - v1)
    w1 = p_grp / (1.0 + e2)
    w2 = w1 * e2
    return jnp.where(lane == i1, w1, 0.0) + jnp.where(lane == i2, w2, 0.0)


def _out_proj_kernel(oa_ref, ob_ref, sa_ref, sb_ref, x_ref, wa_ref, wb_ref, wo_ref, gffn_ref,
                     wr_ref, br_ref, x1_ref, h2_ref, comb_ref, *, oa_transposed):
    if oa_transposed:
        ya = _tn(oa_ref[...], wa_ref[...])
    else:
        ya = _dot(oa_ref[...], wa_ref[...])
    yb = _dot(ob_ref[...], wb_ref[...])
    m = (sa_ref[...].astype(F32) * ya + sb_ref[...].astype(F32) * yb).astype(BF16)
    x1 = x_ref[...] + _dot(m, wo_ref[...])
    x1_ref[...] = x1
    h2 = _rms(x1, gffn_ref[...]).astype(BF16)
    h2_ref[...] = h2
    comb = _route(_dot(h2, wr_ref[...]) + br_ref[...])
    lane = lax.broadcasted_iota(jnp.int32, (1, LANES), 1)
    for g in range(N_GROUPS):
        shifted = comb if g == 0 else pltpu.roll(comb, LANES - g * EXPERTS_PER_GROUP, 1)
        comb_ref[:, g * LANES:(g + 1) * LANES] = jnp.where(lane < EXPERTS_PER_GROUP, shifted, 0.0)


def _out_proj(oa, ob, sa, sb, x, wa, wb, wo, gffn, wr, br, tm, oa_transposed):
    b, l, d = x.shape
    hv = A_HEADS * V_DIM
    tok = lambda w: pl.BlockSpec((None, tm, w), lambda bi, i: (bi, i, 0))
    oa_spec = pl.BlockSpec((None, hv, tm), lambda bi, i: (bi, 0, i)) if oa_transposed else tok(hv)
    return pl.pallas_call(
        functools.partial(_out_proj_kernel, oa_transposed=oa_transposed),
        grid=(b, l // tm),
        in_specs=[oa_spec, tok(B_HEADS * B_DV), tok(d), tok(d), tok(d), _const_spec(wa.shape),
                  _const_spec(wb.shape), _const_spec(wo.shape), _const_spec(gffn.shape),
                  _const_spec(wr.shape), _const_spec(br.shape)],
        out_specs=[tok(d), tok(d), tok(N_GROUPS * LANES)],
        out_shape=[jax.ShapeDtypeStruct((b, l, d), F32), jax.ShapeDtypeStruct((b, l, d), BF16),
                   jax.ShapeDtypeStruct((b, l, N_GROUPS * LANES), F32)],
        compiler_params=_params(("parallel", "parallel")),
        name="out_proj",
    )(oa, ob, sa, sb, x, wa, wb, wo, gffn, wr, br)


def _moe_kernel(h2_ref, comb_ref, x1_ref, wg_ref, wu_ref, wd_ref, gfin_ref, y_ref, acc_sc):
    g = pl.program_id(2)

    @pl.when(g == 0)
    def _():
        acc_sc[...] = jnp.zeros(acc_sc.shape, F32)

    h2 = h2_ref[...]
    comb = comb_ref[...]
    hid = []
    for e in range(EXPERTS_PER_GROUP):
        a = _dot(h2, wg_ref[e])
        u = _dot(h2, wu_ref[e])
        s, _ = _sig_pair(a)
        hid.append((a * s * u * comb[:, e:e + 1]).astype(BF16))
    hid = jnp.concatenate(hid, axis=-1)
    wd = wd_ref[...].reshape(EXPERTS_PER_GROUP * D_FF_EXPERT, wd_ref.shape[-1])
    acc_sc[...] += _dot(hid, wd)

    @pl.when(g == pl.num_programs(2) - 1)
    def _():
        y_ref[...] = _rms(x1_ref[...] + acc_sc[...], gfin_ref[...])


def _moe(h2, comb, x1, wg, wu, wd, gfin, tm):
    b, l, d = x1.shape
    tok = lambda w: pl.BlockSpec((None, tm, w), lambda bi, i, g: (bi, i, 0))
    return pl.pallas_call(
        _moe_kernel,
        grid=(b, l // tm, N_GROUPS),
        in_specs=[tok(d), pl.BlockSpec((None, tm, LANES), lambda bi, i, g: (bi, i, g)), tok(d),
                  pl.BlockSpec((EXPERTS_PER_GROUP, d, D_FF_EXPERT), lambda bi, i, g: (g, 0, 0)),
                  pl.BlockSpec((EXPERTS_PER_GROUP, d, D_FF_EXPERT), lambda bi, i, g: (g, 0, 0)),
                  pl.BlockSpec((EXPERTS_PER_GROUP, D_FF_EXPERT, d), lambda bi, i, g: (g, 0, 0)),
                  pl.BlockSpec(gfin.shape, lambda bi, i, g: (0, 0))],
        out_specs=tok(d),
        out_shape=jax.ShapeDtypeStruct((b, l, d), F32),
        scratch_shapes=[pltpu.VMEM((tm, d), F32)],
        compiler_params=_params(("parallel", "parallel", "arbitrary")),
        name="moe",
    )(h2, comb, x1, wg, wu, wd, gfin)


def _rope_tables(pos):
    inv = jnp.power(ROPE_THETA, -jnp.arange(HALF, dtype=F32) / HALF)
    ang = pos.astype(F32)[:, None] * inv[None, :]
    cos, sin = jnp.cos(ang), jnp.sin(ang)
    zeros = jnp.zeros_like(sin)
    reps = LANES // ROPE_DIM
    cosp = jnp.tile(jnp.concatenate([cos, cos], axis=1), (1, reps))
    sinlo = jnp.tile(jnp.concatenate([-sin, zeros], axis=1), (1, reps))
    sinhi = jnp.tile(jnp.concatenate([zeros, sin], axis=1), (1, reps))
    return cosp, sinlo, sinhi


def _prep_weights(g_mix, w_in, g_q, w_uq, g_kv, w_ukv, lb_hgrn, g_onorm, w_a_out, w_b_out, w_o,
                  g_ffn, w_rg, b_rg, w_re, b_re, w_gate, w_up, w_down, g_final):
    assert w_in.shape[0] == 1, "single-layer step"
    d = w_in.shape[1]
    w = w_in[0]
    n_a = Q_LORA + KV_LORA
    win_a = jnp.concatenate([w[:, :n_a]] + [w[:, n_a:n_a + ROPE_DIM]] * (LANES // ROPE_DIM), axis=1).astype(BF16)
    win_b = w[:, n_a + ROPE_DIM:].astype(BF16)
    per_q = NOPE_DIM + ROPE_DIM
    wuq = jnp.pad(w_uq[0].reshape(Q_LORA, A_HEADS, per_q), ((0, 0), (0, 0), (0, HEAD_PAD - per_q)))
    wuq = wuq.reshape(Q_LORA, A_HEADS * HEAD_PAD).astype(BF16)
    wukv = w_ukv[0].reshape(KV_LORA, A_HEADS, NOPE_DIM + V_DIM)
    w_uk, w_uv = wukv[..., :NOPE_DIM], wukv[..., NOPE_DIM:]
    wuk = jnp.pad(w_uk, ((0, 0), (0, 0), (0, HEAD_PAD - NOPE_DIM))).reshape(KV_LORA, A_HEADS * HEAD_PAD).astype(BF16)
    wuvt = w_uv.reshape(KV_LORA, A_HEADS * V_DIM).T.astype(BF16)
    wq2l = jnp.pad(jnp.transpose(w_uk, (1, 2, 0)), ((0, 0), (0, HEAD_PAD - NOPE_DIM), (0, 0))).astype(BF16)
    esel = jnp.zeros((HEAD_PAD, ROPE_DIM), F32).at[ROPE_LO + jnp.arange(ROPE_DIM), jnp.arange(ROPE_DIM)].set(1.0).astype(BF16)
    eye = jnp.eye(A_HEADS, dtype=F32)
    wuvp = jnp.einsum("chv,hg->hcgv", w_uv, eye).reshape(A_HEADS, KV_LORA, A_HEADS * V_DIM).astype(BF16)
    lb = jnp.cumsum(jax.nn.softmax(lb_hgrn.astype(F32), axis=0), axis=0)[0][None, :]
    gon = jnp.tile(g_onorm[0], 2)[None, :]
    wr = jnp.concatenate([jnp.transpose(w_re[0], (1, 0, 2)).reshape(d, N_EXPERTS), w_rg[0],
                          jnp.zeros((d, LANES - N_EXPERTS - N_GROUPS), F32)], axis=1).astype(BF16)
    br = jnp.concatenate([b_re[0].reshape(N_EXPERTS), b_rg[0],
                          jnp.zeros((LANES - N_EXPERTS - N_GROUPS,), F32)])[None, :]
    return dict(
        gmix=g_mix[0][None, :], win_a=win_a, win_b=win_b, gq=g_q[0][None, :], wuq=wuq,
        gkv=g_kv[0][None, :], wuk=wuk, wuvt=wuvt, wq2l=wq2l, esel=esel, wuvp=wuvp, lb=lb, gon=gon,
        wa=w_a_out[0].astype(BF16), wb=w_b_out[0].astype(BF16), wo=w_o[0].astype(BF16),
        gffn=g_ffn[0][None, :], wr=wr, br=br, wg=w_gate[0].astype(BF16), wu=w_up[0].astype(BF16),
        wd=w_down[0].astype(BF16), gfin=g_final[None, :])


def _tile(n, want):
    t = min(n, want)
    assert n % t == 0
    return t


def _run_group(x, pos, streams, past, w):
    b, l, d = x.shape
    ns, ls = streams
    tm = _tile(l, 512)
    cosp, sinlo, sinhi = _rope_tables(pos)
    h, ckv, kpe, q, k, vt = _mla_proj(x, w["gmix"], w["win_a"], w["gq"], w["wuq"], w["gkv"], w["wuk"],
                                      w["wuvt"], cosp, sinlo, sinhi, tm, past is None)
    qb, kb, lf, vb, og, sa, sb = _hgrn_proj(h, w["win_b"], w["lb"], tm)
    as_streams = lambda a: a.reshape(ns, ls, a.shape[-1])
    if past is None:
        oa = _attn(q, k, vt, _tile(l, 512), 256)
        s0t = jnp.zeros((ns, B_HEADS, B_DV, B_DK), F32)
    else:
        cache_ckv, cache_kpe, state = past
        oa = _sample_attn(as_streams(q), cache_ckv, cache_kpe, as_streams(ckv), as_streams(kpe),
                          w["wq2l"], w["esel"], w["wuvp"]).reshape(b, l, A_HEADS * V_DIM)
        s0t = jnp.swapaxes(state.astype(F32), -1, -2)
    ob, snt = _hgrn(qb, kb, lf, vb, og, s0t, w["gon"], ns, 512 if past is None else 8 * ls)
    x1, h2, comb = _out_proj(oa, ob, sa, sb, x, w["wa"], w["wb"], w["wo"], w["gffn"], w["wr"], w["br"],
                             tm, past is None)
    y = _moe(h2, comb, x1, w["wg"], w["wu"], w["wd"], w["gfin"], tm)
    return y, ckv, kpe, jnp.swapaxes(snt, -1, -2)


def kernel(x_prompt, x_sample, cache_ckv, cache_kpe, state_hgrn, g_mix, w_in, g_q, w_uq, g_kv, w_ukv,
           lb_hgrn, g_onorm, w_a_out, w_b_out, w_o, g_ffn, w_rg, b_rg, w_re, b_re, w_gate, w_up, w_down,
           g_final):
    w = _prep_weights(g_mix, w_in, g_q, w_uq, g_kv, w_ukv, lb_hgrn, g_onorm, w_a_out, w_b_out, w_o,
                      g_ffn, w_rg, b_rg, w_re, b_re, w_gate, w_up, w_down, g_final)
    bp, lp, d = x_prompt.shape
    y_p, ckv_p, kpe_p, st_p = _run_group(x_prompt, jnp.arange(lp, dtype=jnp.int32), (bp, lp), None, w)

    bs, ls, _ = x_sample.shape
    past_len = cache_ckv.shape[2]
    pos_s = past_len + (jnp.arange(bs * ls, dtype=jnp.int32) % ls)
    y_s, ckv_s, kpe_s, st_s = _run_group(x_sample.reshape(1, bs * ls, d), pos_s, (bs, ls),
                                         (cache_ckv[0], cache_kpe[0], state_hgrn[0]), w)
    return (y_p, y_s.reshape(bs, ls, d),
            ckv_p[None], kpe_p[None], st_p[None].astype(x_prompt.dtype),
            ckv_s.reshape(1, bs, ls, KV_LORA), kpe_s.reshape(1, bs, ls, ROPE_DIM),
            st_s[None].astype(state_hgrn.dtype))
```

```python
import functools

import numpy as np
import jax
import jax.numpy as jnp
from jax import lax
from jax.experimental import pallas as pl
from jax.experimental.pallas import tpu as pltpu

F32 = jnp.float32
BF16 = jnp.bfloat16

EPS = 1e-6
CHUNK = 64
A_HEADS = 8
Q_LORA = 384
KV_LORA = 256
NOPE_DIM = 64
ROPE_DIM = 32
V_DIM = 64
ROPE_THETA = 10000.0
B_HEADS = 8
B_DK = 128
B_DV = 64
N_GROUPS = 4
EXPERTS_PER_GROUP = 8
N_EXPERTS = N_GROUPS * EXPERTS_PER_GROUP
D_FF_EXPERT = 256

LANES = 128
VMEM_BYTES_V7X = 64 * 1024 * 1024
VMEM_LIMIT = VMEM_BYTES_V7X - 8 * 1024 * 1024

HEAD_PAD = LANES
ROPE_LO = NOPE_DIM
ROPE_HI = NOPE_DIM + ROPE_DIM
HALF = ROPE_DIM // 2
SCALE = (NOPE_DIM + ROPE_DIM) ** -0.5
LOG2E = 1.4426950408889634
NEG_INF = float("-inf")


def _params(sem):
    return pltpu.CompilerParams(dimension_semantics=sem, vmem_limit_bytes=VMEM_LIMIT)


def _const_spec(shape):
    nd = len(shape)
    return pl.BlockSpec(shape, lambda *_: (0,) * nd, pipeline_mode=pl.Buffered(1))


def _rms(x, g):
    ms = jnp.mean(x * x, axis=-1, keepdims=True)
    return x * lax.rsqrt(ms + EPS) * g


def _sig_pair(x):
    e = jnp.exp(-jnp.abs(x))
    r = 1.0 / (1.0 + e)
    er = e * r
    pos = x >= 0
    return jnp.where(pos, r, er), jnp.where(pos, er, r)


def _nt(a, b):
    return lax.dot_general(a, b, (((1,), (1,)), ((), ())), preferred_element_type=F32)


def _tn(a, b):
    return lax.dot_general(a, b, (((0,), (0,)), ((), ())), preferred_element_type=F32)


def _dot(a, b):
    return jnp.dot(a, b, preferred_element_type=F32)


def _rope(x, cosp, sinlo, sinhi):
    return x * cosp + pltpu.roll(x, LANES - HALF, 1) * sinlo + pltpu.roll(x, HALF, 1) * sinhi


def _mla_proj_kernel(x_ref, gmix_ref, win_ref, gq_ref, wuq_ref, gkv_ref, wuk_ref, wuvt_ref,
                     cos_ref, sinlo_ref, sinhi_ref,
                     h_ref, ckv_ref, kpe_ref, q_ref, k_ref, vt_ref, *, q_transposed):
    h = _rms(x_ref[...], gmix_ref[...]).astype(BF16)
    h_ref[...] = h
    z = _dot(h, win_ref[...])
    cosp, sinlo, sinhi = cos_ref[...], sinlo_ref[...], sinhi_ref[...]
    lane = lax.broadcasted_iota(jnp.int32, (1, LANES), 1)
    rope_lanes = (lane // ROPE_DIM) == (ROPE_LO // ROPE_DIM)

    cqn = _rms(z[:, :Q_LORA], gq_ref[...]).astype(BF16)
    q = _dot(cqn, wuq_ref[...])
    for hd in range(A_HEADS):
        sl = slice(hd * HEAD_PAD, (hd + 1) * HEAD_PAD)
        qh = q[:, sl]
        qh = jnp.where(rope_lanes, _rope(qh, cosp, sinlo, sinhi), qh) * (SCALE * LOG2E)
        if q_transposed:
            q_ref[sl, :] = qh.T.astype(BF16)
        else:
            q_ref[:, sl] = qh.astype(BF16)

    ckv = _rms(z[:, Q_LORA:Q_LORA + KV_LORA], gkv_ref[...])
    ckv_ref[...] = ckv
    ckv_bf = ckv.astype(BF16)
    kpe_rot = _rope(z[:, Q_LORA + KV_LORA:], cosp, sinlo, sinhi)
    kpe_ref[...] = kpe_rot[:, :ROPE_DIM]
    kpe_placed = jnp.where(rope_lanes, kpe_rot, 0.0)
    kn = _dot(ckv_bf, wuk_ref[...])
    for hd in range(A_HEADS):
        sl = slice(hd * HEAD_PAD, (hd + 1) * HEAD_PAD)
        k_ref[:, sl] = (kn[:, sl] + kpe_placed).astype(BF16)
    vt_ref[...] = _nt(wuvt_ref[...], ckv_bf).astype(BF16)


def _mla_proj(x, gmix, win_a, gq, wuq, gkv, wuk, wuvt, cosp, sinlo, sinhi, tm, q_transposed):
    b, l, d = x.shape
    hq = A_HEADS * HEAD_PAD
    hv = A_HEADS * V_DIM
    tok = lambda w: pl.BlockSpec((None, tm, w), lambda bi, i: (bi, i, 0))
    tok_t = lambda w: pl.BlockSpec((None, w, tm), lambda bi, i: (bi, 0, i))
    tab = pl.BlockSpec((tm, LANES), lambda bi, i: (i, 0))
    return pl.pallas_call(
        functools.partial(_mla_proj_kernel, q_transposed=q_transposed),
        grid=(b, l // tm),
        in_specs=[tok(d), _const_spec(gmix.shape), _const_spec(win_a.shape), _const_spec(gq.shape),
                  _const_spec(wuq.shape), _const_spec(gkv.shape), _const_spec(wuk.shape),
                  _const_spec(wuvt.shape), tab, tab, tab],
        out_specs=[tok(d), tok(KV_LORA), tok(ROPE_DIM), tok_t(hq) if q_transposed else tok(hq), tok(hq),
                   tok_t(hv)],
        out_shape=[jax.ShapeDtypeStruct((b, l, d), BF16),
                   jax.ShapeDtypeStruct((b, l, KV_LORA), F32),
                   jax.ShapeDtypeStruct((b, l, ROPE_DIM), F32),
                   jax.ShapeDtypeStruct((b, hq, l) if q_transposed else (b, l, hq), BF16),
                   jax.ShapeDtypeStruct((b, l, hq), BF16),
                   jax.ShapeDtypeStruct((b, hv, l), BF16)],
        compiler_params=_params(("parallel", "parallel")),
        name="mla_proj",
    )(x, gmix, win_a, gq, wuq, gkv, wuk, wuvt, cosp, sinlo, sinhi)


def _hgrn_proj_kernel(h_ref, w_ref, lb_ref, qb_ref, kb_ref, lf_ref, vb_ref, og_ref, sa_ref, sb_ref):
    h = h_ref[...]
    dk = B_HEADS * B_DK
    dv = B_HEADS * B_DV
    dm = sa_ref.shape[-1]
    o = 0
    zq = _dot(h, w_ref[:, o:o + dk]); o += dk
    s, _ = _sig_pair(zq)
    qb_ref[...] = zq * s
    zf = _dot(h, w_ref[:, o:o + dk]); o += dk
    sf, snf = _sig_pair(zf)
    lb = lb_ref[...]
    lf_ref[...] = jnp.log(lb + (1.0 - lb) * sf)
    kb_ref[...] = (1.0 - lb) * snf
    vb_ref[...] = _dot(h, w_ref[:, o:o + dv]).astype(BF16); o += dv
    zg = _dot(h, w_ref[:, o:o + dv]); o += dv
    s, _ = _sig_pair(zg)
    og_ref[...] = (zg * s).astype(BF16)
    s, _ = _sig_pair(_dot(h, w_ref[:, o:o + dm])); o += dm
    sa_ref[...] = s.astype(BF16)
    s, _ = _sig_pair(_dot(h, w_ref[:, o:o + dm]))
    sb_ref[...] = s.astype(BF16)


def _hgrn_proj(h, win_b, lb, tm):
    b, l, d = h.shape
    dk = B_HEADS * B_DK
    dv = B_HEADS * B_DV
    tok = lambda w: pl.BlockSpec((None, tm, w), lambda bi, i: (bi, i, 0))
    sds = lambda w, dt: jax.ShapeDtypeStruct((b, l, w), dt)
    return pl.pallas_call(
        _hgrn_proj_kernel,
        grid=(b, l // tm),
        in_specs=[tok(d), _const_spec(win_b.shape), _const_spec(lb.shape)],
        out_specs=[tok(dk), tok(dk), tok(dk), tok(dv), tok(dv), tok(d), tok(d)],
        out_shape=[sds(dk, F32), sds(dk, F32), sds(dk, F32), sds(dv, BF16), sds(dv, BF16),
                   sds(d, BF16), sds(d, BF16)],
        compiler_params=_params(("parallel", "parallel")),
        name="hgrn_proj",
    )(h, win_b, lb)


def _attn_kernel(qt_ref, k_ref, vt_ref, o_ref, s_sc, m_sc, l_sc, acc_sc, *, t, tc):
    qi = pl.program_id(2)
    n_chains = t // tc
    m_sc[...] = jnp.full(m_sc.shape, NEG_INF, F32)
    l_sc[...] = jnp.zeros(l_sc.shape, F32)
    acc_sc[...] = jnp.zeros(acc_sc.shape, F32)

    def scores(kb, slot):
        start = pl.multiple_of(kb * t, t)
        k = k_ref[pl.ds(start, t), :]
        for ch in range(n_chains):
            cols = slice(ch * tc, (ch + 1) * tc)
            s_sc[slot, :, cols] = _dot(k, qt_ref[:, cols])

    def consume(kb, slot, masked):
        start = pl.multiple_of(kb * t, t)
        vt = vt_ref[:, pl.ds(start, t)]
        for ch in range(n_chains):
            cols = slice(ch * tc, (ch + 1) * tc)
            s = s_sc[slot, :, cols]
            if masked:
                r = lax.broadcasted_iota(jnp.int32, (t, tc), 0) // CHUNK
                c = (lax.broadcasted_iota(jnp.int32, (t, tc), 1) + ch * tc) // CHUNK
                s = jnp.where(r <= c, s, NEG_INF)
            m_prev = m_sc[:, cols]
            m_new = jnp.maximum(m_prev, jnp.max(s, axis=0, keepdims=True))
            alpha = jnp.exp2(m_prev - m_new)
            p = jnp.exp2(s - m_new)
            l_sc[:, cols] = alpha * l_sc[:, cols] + jnp.sum(p, axis=0, keepdims=True)
            acc_sc[:, cols] = alpha * acc_sc[:, cols] + _dot(vt, p.astype(BF16))
            m_sc[:, cols] = m_new

    scores(0, 0)

    def body(j, carry):
        kb = 2 * j
        scores(kb + 1, 1)
        consume(kb, 0, False)
        scores(kb + 2, 0)
        consume(kb + 1, 1, False)
        return carry

    lax.fori_loop(0, qi // 2, body, 0)

    @pl.when(qi % 2 == 0)
    def _():
        consume(qi, 0, True)

    @pl.when(qi % 2 == 1)
    def _():
        scores(qi, 1)
        consume(qi - 1, 0, False)
        consume(qi, 1, True)

    o_ref[...] = (acc_sc[...] / l_sc[...]).astype(BF16)


def _attn(qt, k, vt, t, tc):
    b, l, _ = k.shape
    return pl.pallas_call(
        functools.partial(_attn_kernel, t=t, tc=min(tc, t)),
        grid=(b, A_HEADS, l // t),
        in_specs=[pl.BlockSpec((None, HEAD_PAD, t), lambda bi, h, i: (bi, h, i)),
                  pl.BlockSpec((None, l, HEAD_PAD), lambda bi, h, i: (bi, 0, h)),
                  pl.BlockSpec((None, V_DIM, l), lambda bi, h, i: (bi, h, 0))],
        out_specs=pl.BlockSpec((None, V_DIM, t), lambda bi, h, i: (bi, h, i)),
        out_shape=jax.ShapeDtypeStruct((b, A_HEADS * V_DIM, l), BF16),
        scratch_shapes=[pltpu.VMEM((2, t, t), F32), pltpu.VMEM((1, t), F32), pltpu.VMEM((1, t), F32),
                        pltpu.VMEM((V_DIM, t), F32)],
        compiler_params=_params(("parallel", "parallel", "arbitrary")),
        name="attn",
    )(qt, k, vt)


def _sample_attn_kernel(q_ref, cc_ref, ck_ref, nc_ref, nk_ref, wq2l_ref, esel_ref, wuvp_ref, o_ref,
                        *, past, n_new):
    qs = q_ref[...]
    heads = [qs[:, hd * HEAD_PAD:(hd + 1) * HEAD_PAD] for hd in range(A_HEADS)]
    ql = jnp.concatenate([_dot(heads[hd], wq2l_ref[hd]) for hd in range(A_HEADS)], axis=0).astype(BF16)
    qp = jnp.concatenate([_dot(heads[hd], esel_ref[...]) for hd in range(A_HEADS)], axis=0).astype(BF16)
    cc = cc_ref[...].astype(BF16)
    ck = ck_ref[...].astype(BF16)
    nc = nc_ref[...].astype(BF16)
    nk = nk_ref[...].astype(BF16)
    s_c = _nt(ql, cc) + _nt(qp, ck)
    s_n = _nt(ql, nc) + _nt(qp, nk)
    rows = A_HEADS * n_new
    tq = lax.broadcasted_iota(jnp.int32, (rows, n_new), 0) % n_new
    tk = lax.broadcasted_iota(jnp.int32, (rows, n_new), 1)
    s_n = jnp.where((past + tk) // CHUNK <= (past + tq) // CHUNK, s_n, NEG_INF)
    m = jnp.maximum(jnp.max(s_c, axis=-1, keepdims=True), jnp.max(s_n, axis=-1, keepdims=True))
    p_c = jnp.exp2(s_c - m)
    p_n = jnp.exp2(s_n - m)
    denom = jnp.sum(p_c, axis=-1, keepdims=True) + jnp.sum(p_n, axis=-1, keepdims=True)
    o_lat = ((_dot(p_c.astype(BF16), cc) + _dot(p_n.astype(BF16), nc)) / denom).astype(BF16)
    out = _dot(o_lat[0:n_new], wuvp_ref[0])
    for hd in range(1, A_HEADS):
        out = out + _dot(o_lat[hd * n_new:(hd + 1) * n_new], wuvp_ref[hd])
    o_ref[...] = out.astype(BF16)


def _sample_attn(q, cache_ckv, cache_kpe, ckv_new, kpe_new, wq2l, esel, wuvp):
    nb, n_new, hq = q.shape
    past = cache_ckv.shape[1]
    hv = A_HEADS * V_DIM
    per = lambda r, w: pl.BlockSpec((None, r, w), lambda bi: (bi, 0, 0))
    return pl.pallas_call(
        functools.partial(_sample_attn_kernel, past=past, n_new=n_new),
        grid=(nb,),
        in_specs=[per(n_new, hq), per(past, KV_LORA), per(past, ROPE_DIM), per(n_new, KV_LORA),
                  per(n_new, ROPE_DIM), _const_spec(wq2l.shape), _const_spec(esel.shape),
                  _const_spec(wuvp.shape)],
        out_specs=per(n_new, hv),
        out_shape=jax.ShapeDtypeStruct((nb, n_new, hv), BF16),
        compiler_params=_params(("parallel",)),
        name="sample_attn",
    )(q, cache_ckv, cache_kpe, ckv_new, kpe_new, wq2l, esel, wuvp)


def _level_sizes(c):
    sizes = []
    p = c
    while p >= 2:
        sizes.append(p)
        p //= 2
    return sizes


def _level_masks(c):
    t = np.arange(c)[:, None]
    s = np.arange(c)[None, :]
    out = []
    for p in _level_sizes(c):
        out.append((t // p == s // p) & (t % p >= p // 2) & (s % p < p // 2))
    out.append(t == s)
    return np.stack(out).astype(np.float32)


def _boundary(cum, p, c):
    half = p // 2
    if p >= 8:
        parts = [jnp.broadcast_to(cum[i * p + half - 1:i * p + half, :], (p, cum.shape[1]))
                 for i in range(c // p)]
        return jnp.concatenate(parts, axis=0) if len(parts) > 1 else parts[0]
    sub = lax.broadcasted_iota(jnp.int32, (8, cum.shape[1]), 0)
    parts = []
    for g in range(c // 8):
        lo = jnp.broadcast_to(cum[g * 8 + 1:g * 8 + 2, :], (8, cum.shape[1]))
        hi = jnp.broadcast_to(cum[g * 8 + 5:g * 8 + 6, :], (8, cum.shape[1]))
        parts.append(jnp.where(sub < 4, lo, hi))
    return jnp.concatenate(parts, axis=0)


def _hgrn_kernel(q_ref, k_ref, lf_ref, v_ref, og_ref, s0t_ref, gon_ref, mask_ref, tri_ref,
                 o_ref, snt_ref, st_sc, *, c, n_chunks, per_chunk_state):
    step = pl.program_id(2)
    sizes = _level_sizes(c)
    lane = lax.broadcasted_iota(jnp.int32, (1, LANES), 1)
    low = lane < B_DV
    keep_lo = jnp.where(low, 1.0, 0.0).astype(BF16)
    keep_hi = jnp.where(low, 0.0, 1.0).astype(BF16)

    if not per_chunk_state:
        @pl.when(step == 0)
        def _():
            st_sc[...] = jnp.zeros(st_sc.shape, F32)
            st_sc[0, 0:B_DV, :] = s0t_ref[0]
            st_sc[1, B_DV:2 * B_DV, :] = s0t_ref[1]

    tri = tri_ref[...]
    rows = [slice(ci * c, (ci + 1) * c) for ci in range(n_chunks)]
    heads = [slice(j * B_DK, (j + 1) * B_DK) for j in range(2)]
    items = [(ci, j) for ci in range(n_chunks) for j in range(2)]


    cums = []
    for ci in range(n_chunks):
        lf = lf_ref[rows[ci], :]
        hi = lf.astype(BF16)
        r1 = lf - hi.astype(F32)
        mid = r1.astype(BF16)
        lo = (r1 - mid.astype(F32)).astype(BF16)
        cums.append(_dot(tri, hi) + _dot(tri, mid) + _dot(tri, lo))

    a_mats = {}
    for ci, j in items:
        q = q_ref[rows[ci], heads[j]]
        k = k_ref[rows[ci], heads[j]]
        cum = cums[ci][:, heads[j]]
        kbf = k.astype(BF16)
        a = _nt(q.astype(BF16), kbf) * mask_ref[len(sizes)]
        for li, p in enumerate(sizes):
            if p == 2:
                qt = (q * jnp.exp(lf_ref[rows[ci], heads[j]])).astype(BF16)
                kt = kbf
            else:
                bnd = _boundary(cum, p, c)
                qt = (q * jnp.exp(jnp.minimum(cum - bnd, 0.0))).astype(BF16)
                kt = (k * jnp.exp(jnp.minimum(bnd - cum, 0.0))).astype(BF16)
            a = a + _nt(qt, kt) * mask_ref[li]
        a_mats[ci, j] = a.astype(BF16)

    qhats, kvs, decays, vs = {}, {}, {}, {}
    for ci, j in items:
        cum = cums[ci][:, heads[j]]
        last = cum[c - 1:c, :]
        vs[ci, j] = v_ref[rows[ci], :] * (keep_lo if j == 0 else keep_hi)
        qhats[ci, j] = (q_ref[rows[ci], heads[j]] * jnp.exp(cum)).astype(BF16)
        khat = (k_ref[rows[ci], heads[j]] * jnp.exp(last - cum)).astype(BF16)
        kvs[ci, j] = _tn(vs[ci, j], khat)
        decays[ci, j] = jnp.exp(last)

    pad = jnp.zeros((B_DV, B_DK), F32)
    if not per_chunk_state:
        st = [st_sc[0], st_sc[1]]
    for ci in range(n_chunks):
        if per_chunk_state:
            st = [jnp.concatenate([s0t_ref[ci, 0], pad], axis=0), jnp.concatenate([pad, s0t_ref[ci, 1]], axis=0)]
        o_pair = jnp.zeros((c, LANES), F32)
        for j in range(2):
            o_pair = o_pair + _dot(a_mats[ci, j], vs[ci, j]) + _nt(qhats[ci, j], st[j].astype(BF16))
            st[j] = st[j] * decays[ci, j] + kvs[ci, j]
        if per_chunk_state:
            snt_ref[ci, 0] = st[0][0:B_DV, :]
            snt_ref[ci, 1] = st[1][B_DV:2 * B_DV, :]
        o2 = o_pair * o_pair
        s_lo = jnp.sum(jnp.where(low, o2, 0.0), axis=-1, keepdims=True)
        s_hi = jnp.sum(jnp.where(low, 0.0, o2), axis=-1, keepdims=True)
        ms = jnp.where(low, s_lo, s_hi) * (1.0 / B_DV)
        ob = o_pair * lax.rsqrt(ms + EPS) * gon_ref[...] * og_ref[rows[ci], :].astype(F32)
        o_ref[rows[ci], :] = ob.astype(BF16)
    if not per_chunk_state:
        st_sc[0] = st[0]
        st_sc[1] = st[1]

        @pl.when(step == pl.num_programs(2) - 1)
        def _():
            snt_ref[0] = st_sc[0, 0:B_DV, :]
            snt_ref[1] = st_sc[1, B_DV:2 * B_DV, :]


def _hgrn(qb, kb, lf, vb, og, s0t, gon, n_streams, tc):
    b, l, dk = qb.shape
    per_chunk_state = n_streams != b
    if per_chunk_state:
        assert b == 1 and l % n_streams == 0 and l // n_streams <= CHUNK
        c = l // n_streams
    else:
        c = min(CHUNK, l)
    tc = min(tc, l)
    n_chunks = tc // c
    masks = jnp.asarray(_level_masks(c))
    tri = jnp.asarray(np.tril(np.ones((c, c), np.float32)), dtype=BF16)
    pairs = B_HEADS // 2
    wide = pl.BlockSpec((None, tc, 2 * B_DK), lambda bi, p, i: (bi, i, p))
    narrow = pl.BlockSpec((None, tc, 2 * B_DV), lambda bi, p, i: (bi, i, p))
    if per_chunk_state:
        state = pl.BlockSpec((n_chunks, 2, B_DV, B_DK), lambda bi, p, i: (i, p, 0, 0))
    else:
        state = pl.BlockSpec((None, 2, B_DV, B_DK), lambda bi, p, i: (bi, p, 0, 0))
    return pl.pallas_call(
        functools.partial(_hgrn_kernel, c=c, n_chunks=n_chunks, per_chunk_state=per_chunk_state),
        grid=(b, pairs, l // tc),
        in_specs=[wide, wide, wide, narrow, narrow, state, _const_spec(gon.shape),
                  _const_spec(masks.shape), _const_spec(tri.shape)],
        out_specs=[narrow, state],
        out_shape=[jax.ShapeDtypeStruct((b, l, B_HEADS * B_DV), BF16),
                   jax.ShapeDtypeStruct((n_streams, B_HEADS, B_DV, B_DK), F32)],
        scratch_shapes=[pltpu.VMEM((2, 2 * B_DV, B_DK), F32)],
        compiler_params=_params(("parallel", "parallel", "parallel" if per_chunk_state else "arbitrary")),
        name="hgrn",
    )(qb, kb, lf, vb, og, s0t, gon, masks, tri)


MOE_SUB = 144
MOE_ROWS = 1024
_BIG_LANE = float(1 << 20)


def _max_items(tm):
    return (tm + N_GROUPS * (MOE_SUB - 1)) // MOE_SUB


def _first_lane_of_max(x, lane):
    v = jnp.max(x, axis=-1, keepdims=True)
    return v, jnp.min(jnp.where(x == v, lane, _BIG_LANE), axis=-1, keepdims=True)


def _out_proj_kernel(oa_ref, ob_ref, sa_ref, sb_ref, x_ref, wa_ref, wb_ref, wo_ref, gffn_ref,
                     wr_ref, br_ref, tri_ref, upper_ref, x1_ref, xs_ref, slot_ref, meta_ref, *, oa_transposed):
    tm = x_ref.shape[0]
    if oa_transposed:
        ya = _tn(oa_ref[...], wa_ref[...])
    else:
        ya = _dot(oa_ref[...], wa_ref[...])
    yb = _dot(ob_ref[...], wb_ref[...])
    m = (sa_ref[...].astype(F32) * ya + sb_ref[...].astype(F32) * yb).astype(BF16)
    x1 = x_ref[...] + _dot(m, wo_ref[...])
    x1_ref[...] = x1
    h2 = _rms(x1, gffn_ref[...]).astype(BF16)

    logits = _dot(h2, wr_ref[...]) + br_ref[...]
    lane = lax.broadcasted_iota(jnp.int32, (tm, LANES), 1).astype(F32)
    _, grp = _first_lane_of_max(jnp.where(lane < N_GROUPS, logits, NEG_INF), lane)
    onehot = jnp.where(lane == grp, 1.0, 0.0)
    rank = _dot(tri_ref[...], onehot.astype(BF16))
    count = rank[tm - 1:tm, :]
    items = jnp.zeros_like(count)
    for j in range(_max_items(tm)):
        items = items + jnp.where(count > j * MOE_SUB, 1.0, 0.0)
    base = _dot(jnp.broadcast_to(items, (8, LANES)).astype(BF16), upper_ref[...])[0:1, :] * MOE_SUB
    slot = jnp.sum(onehot * (base + rank - 1.0), axis=-1, keepdims=True)
    col = lax.broadcasted_iota(jnp.int32, (tm, MOE_ROWS), 1).astype(F32)
    place = jnp.where(col == slot, 1.0, 0.0).astype(BF16)
    xs_ref[...] = _tn(place, h2).astype(BF16)
    slot_ref[...] = jnp.broadcast_to(slot, (tm, LANES))
    meta_ref[...] = jnp.broadcast_to(items, (8, LANES))


def _out_proj(oa, ob, sa, sb, x, wa, wb, wo, gffn, wr, br, tm, oa_transposed):
    b, l, d = x.shape
    nt = l // tm
    assert _max_items(tm) * MOE_SUB <= MOE_ROWS
    hv = A_HEADS * V_DIM
    tri = jnp.asarray(np.tril(np.ones((tm, tm), np.float32)), dtype=BF16)
    upper = jnp.asarray(np.triu(np.ones((LANES, LANES), np.float32), 1), dtype=BF16)
    tok = lambda w: pl.BlockSpec((None, tm, w), lambda bi, i: (bi, i, 0))
    per_tile = lambda r, w: pl.BlockSpec((None, r, w), lambda bi, i: (bi * nt + i, 0, 0))
    oa_spec = pl.BlockSpec((None, hv, tm), lambda bi, i: (bi, 0, i)) if oa_transposed else tok(hv)
    return pl.pallas_call(
        functools.partial(_out_proj_kernel, oa_transposed=oa_transposed),
        grid=(b, nt),
        in_specs=[oa_spec, tok(B_HEADS * B_DV), tok(d), tok(d), tok(d), _const_spec(wa.shape),
                  _const_spec(wb.shape), _const_spec(wo.shape), _const_spec(gffn.shape),
                  _const_spec(wr.shape), _const_spec(br.shape), _const_spec(tri.shape),
                  _const_spec(upper.shape)],
        out_specs=[tok(d), per_tile(MOE_ROWS, d), tok(LANES), per_tile(8, LANES)],
        out_shape=[jax.ShapeDtypeStruct((b, l, d), F32),
                   jax.ShapeDtypeStruct((b * nt, MOE_ROWS, d), BF16),
                   jax.ShapeDtypeStruct((b, l, LANES), F32),
                   jax.ShapeDtypeStruct((b * nt, 8, LANES), F32)],
        compiler_params=_params(("parallel", "parallel")),
        name="out_proj",
    )(oa, ob, sa, sb, x, wa, wb, wo, gffn, wr, br, tri, upper)


def _work_list(meta, n_items):
    n = meta[:, 0, :N_GROUPS].astype(jnp.int32)
    tiles = n.shape[0]
    first = jnp.cumsum(n, axis=1) - n
    per_seg = n.T.reshape(-1)
    ends = jnp.cumsum(per_seg)
    total = ends[-1]
    i = jnp.minimum(jnp.arange(n_items, dtype=jnp.int32), total - 1)
    seg = jnp.sum((ends[None, :] <= i[:, None]).astype(jnp.int32), axis=1)
    grp, tile = seg // tiles, seg % tiles
    block = first[tile, grp] + i - (ends[seg] - per_seg[seg])
    return tile, block.astype(jnp.int32), grp, total[None].astype(jnp.int32)


def _moe_expert_kernel(tile_ref, block_ref, grp_ref, total_ref, xs_ref, wr_ref, br_ref, wg_ref, wu_ref,
                       wd_ref, ys_ref):
    i = pl.program_id(0)

    @pl.when(i < total_ref[0])
    def _():
        g = grp_ref[i]
        x = xs_ref[...]
        logits = _dot(x, wr_ref[...]) + br_ref[...]
        lane_i = lax.broadcasted_iota(jnp.int32, logits.shape, 1)
        lane = lane_i.astype(F32)
        is_grp = jnp.abs(lane - (EXPERTS_PER_GROUP + 0.5 * (N_GROUPS - 1))) < 0.5 * N_GROUPS
        lg = jnp.where(is_grp, logits, NEG_INF)
        gmax = jnp.max(lg, axis=-1, keepdims=True)
        denom = jnp.sum(jnp.exp(lg - gmax), axis=-1, keepdims=True)
        lg_own = jnp.sum(jnp.where(lane_i == EXPERTS_PER_GROUP + g, logits, 0.0), axis=-1, keepdims=True)
        p_grp = jnp.exp(lg_own - gmax) / denom
        le = jnp.where(lane < EXPERTS_PER_GROUP, logits, NEG_INF)
        v1, i1 = _first_lane_of_max(le, lane)
        le2 = jnp.where(lane == i1, NEG_INF, le)
        v2, i2 = _first_lane_of_max(le2, lane)
        e2 = jnp.exp(v2 - v1)
        w1 = p_grp / (1.0 + e2)
        comb = jnp.where(lane == i1, w1, 0.0) + jnp.where(lane == i2, w1 * e2, 0.0)
        hid = []
        for e in range(EXPERTS_PER_GROUP):
            a = _dot(x, wg_ref[e])
            u = _dot(x, wu_ref[e])
            s, _ = _sig_pair(a)
            hid.append((a * s * u * comb[:, e:e + 1]).astype(BF16))
        hid = jnp.concatenate(hid, axis=-1)
        wd = wd_ref[...].reshape(EXPERTS_PER_GROUP * D_FF_EXPERT, wd_ref.shape[-1])
        ys_ref[...] = _dot(hid, wd).astype(BF16)


def _moe_experts(xs, work, wrg, brg, wg, wu, wd):
    tiles, _, d = xs.shape
    tile, block, grp, total = work
    item = lambda i, t, k, g, n: (t[i], k[i], 0)
    by_group = lambda i, t, k, g, n: (g[i], 0, 0)
    grid_spec = pltpu.PrefetchScalarGridSpec(
        num_scalar_prefetch=4,
        grid=(tile.shape[0],),
        in_specs=[pl.BlockSpec((None, MOE_SUB, d), item),
                  pl.BlockSpec((None, d, LANES), by_group),
                  pl.BlockSpec((None, 1, LANES), by_group),
                  pl.BlockSpec((EXPERTS_PER_GROUP, d, D_FF_EXPERT), by_group),
                  pl.BlockSpec((EXPERTS_PER_GROUP, d, D_FF_EXPERT), by_group),
                  pl.BlockSpec((EXPERTS_PER_GROUP, D_FF_EXPERT, d), by_group)],
        out_specs=pl.BlockSpec((None, MOE_SUB, d), item))
    return pl.pallas_call(
        _moe_expert_kernel,
        grid_spec=grid_spec,
        out_shape=jax.ShapeDtypeStruct(xs.shape, BF16),
        compiler_params=_params(("arbitrary",)),
        name="moe_experts",
    )(tile, block, grp, total, xs, wrg, brg, wg, wu, wd)


def _moe_combine_kernel(ys_ref, slot_ref, meta_ref, x1_ref, gfin_ref, y_ref):
    tm, d = x1_ref.shape
    used = jnp.sum(meta_ref[0:1, :], axis=-1, keepdims=True) * MOE_SUB
    row = lax.broadcasted_iota(jnp.int32, (MOE_ROWS, d), 0).astype(F32)
    ys = jnp.where(row < used, ys_ref[...].astype(F32), 0.0).astype(BF16)
    col = lax.broadcasted_iota(jnp.int32, (tm, MOE_ROWS), 1).astype(F32)
    slot = jnp.concatenate([slot_ref[...]] * (MOE_ROWS // LANES), axis=-1)
    place = jnp.where(col == slot, 1.0, 0.0).astype(BF16)
    y_ref[...] = _rms(x1_ref[...] + _dot(place, ys), gfin_ref[...])


def _moe_combine(ys, slot, meta, x1, gfin, tm):
    b, l, d = x1.shape
    nt = l // tm
    tok = lambda w: pl.BlockSpec((None, tm, w), lambda bi, i: (bi, i, 0))
    per_tile = lambda r, w: pl.BlockSpec((None, r, w), lambda bi, i: (bi * nt + i, 0, 0))
    return pl.pallas_call(
        _moe_combine_kernel,
        grid=(b, nt),
        in_specs=[per_tile(MOE_ROWS, d), tok(LANES), per_tile(8, LANES), tok(d), _const_spec(gfin.shape)],
        out_specs=tok(d),
        out_shape=jax.ShapeDtypeStruct((b, l, d), F32),
        compiler_params=_params(("parallel", "parallel")),
        name="moe_combine",
    )(ys, slot, meta, x1, gfin)


def _rope_tables(pos):
    inv = jnp.power(ROPE_THETA, -jnp.arange(HALF, dtype=F32) / HALF)
    ang = pos.astype(F32)[:, None] * inv[None, :]
    cos, sin = jnp.cos(ang), jnp.sin(ang)
    zeros = jnp.zeros_like(sin)
    reps = LANES // ROPE_DIM
    cosp = jnp.tile(jnp.concatenate([cos, cos], axis=1), (1, reps))
    sinlo = jnp.tile(jnp.concatenate([-sin, zeros], axis=1), (1, reps))
    sinhi = jnp.tile(jnp.concatenate([zeros, sin], axis=1), (1, reps))
    return cosp, sinlo, sinhi


def _prep_weights(g_mix, w_in, g_q, w_uq, g_kv, w_ukv, lb_hgrn, g_onorm, w_a_out, w_b_out, w_o,
                  g_ffn, w_rg, b_rg, w_re, b_re, w_gate, w_up, w_down, g_final):
    assert w_in.shape[0] == 1, "single-layer step"
    d = w_in.shape[1]
    w = w_in[0]
    n_a = Q_LORA + KV_LORA
    win_a = jnp.concatenate([w[:, :n_a]] + [w[:, n_a:n_a + ROPE_DIM]] * (LANES // ROPE_DIM), axis=1).astype(BF16)
    win_b = w[:, n_a + ROPE_DIM:].astype(BF16)
    per_q = NOPE_DIM + ROPE_DIM
    wuq = jnp.pad(w_uq[0].reshape(Q_LORA, A_HEADS, per_q), ((0, 0), (0, 0), (0, HEAD_PAD - per_q)))
    wuq = wuq.reshape(Q_LORA, A_HEADS * HEAD_PAD).astype(BF16)
    wukv = w_ukv[0].reshape(KV_LORA, A_HEADS, NOPE_DIM + V_DIM)
    w_uk, w_uv = wukv[..., :NOPE_DIM], wukv[..., NOPE_DIM:]
    wuk = jnp.pad(w_uk, ((0, 0), (0, 0), (0, HEAD_PAD - NOPE_DIM))).reshape(KV_LORA, A_HEADS * HEAD_PAD).astype(BF16)
    wuvt = w_uv.reshape(KV_LORA, A_HEADS * V_DIM).T.astype(BF16)
    wq2l = jnp.pad(jnp.transpose(w_uk, (1, 2, 0)), ((0, 0), (0, HEAD_PAD - NOPE_DIM), (0, 0))).astype(BF16)
    esel = jnp.zeros((HEAD_PAD, ROPE_DIM), F32).at[ROPE_LO + jnp.arange(ROPE_DIM), jnp.arange(ROPE_DIM)].set(1.0).astype(BF16)
    eye = jnp.eye(A_HEADS, dtype=F32)
    wuvp = jnp.einsum("chv,hg->hcgv", w_uv, eye).reshape(A_HEADS, KV_LORA, A_HEADS * V_DIM).astype(BF16)
    lb = jnp.cumsum(jax.nn.softmax(lb_hgrn.astype(F32), axis=0), axis=0)[0][None, :]
    gon = jnp.tile(g_onorm[0], 2)[None, :]
    wr = jnp.pad(w_rg[0], ((0, 0), (0, LANES - N_GROUPS))).astype(BF16)
    br = jnp.pad(b_rg[0], (0, LANES - N_GROUPS))[None, :]
    rest = LANES - EXPERTS_PER_GROUP - N_GROUPS
    wrg = jnp.concatenate([w_re[0], jnp.broadcast_to(w_rg[0], (N_GROUPS, d, N_GROUPS)),
                           jnp.zeros((N_GROUPS, d, rest), F32)], axis=2).astype(BF16)
    brg = jnp.concatenate([b_re[0], jnp.broadcast_to(b_rg[0], (N_GROUPS, N_GROUPS)),
                           jnp.zeros((N_GROUPS, rest), F32)], axis=1)[:, None, :]
    return dict(
        wrg=wrg, brg=brg,
        gmix=g_mix[0][None, :], win_a=win_a, win_b=win_b, gq=g_q[0][None, :], wuq=wuq,
        gkv=g_kv[0][None, :], wuk=wuk, wuvt=wuvt, wq2l=wq2l, esel=esel, wuvp=wuvp, lb=lb, gon=gon,
        wa=w_a_out[0].astype(BF16), wb=w_b_out[0].astype(BF16), wo=w_o[0].astype(BF16),
        gffn=g_ffn[0][None, :], wr=wr, br=br, wg=w_gate[0].astype(BF16), wu=w_up[0].astype(BF16),
        wd=w_down[0].astype(BF16), gfin=g_final[None, :])


def _tile(n, want):
    t = min(n, want)
    assert n % t == 0
    return t


def _run_group(x, pos, streams, past, w):
    b, l, d = x.shape
    ns, ls = streams
    tm = _tile(l, 512)
    cosp, sinlo, sinhi = _rope_tables(pos)
    h, ckv, kpe, q, k, vt = _mla_proj(x, w["gmix"], w["win_a"], w["gq"], w["wuq"], w["gkv"], w["wuk"],
                                      w["wuvt"], cosp, sinlo, sinhi, tm, past is None)
    qb, kb, lf, vb, og, sa, sb = _hgrn_proj(h, w["win_b"], w["lb"], tm)
    as_streams = lambda a: a.reshape(ns, ls, a.shape[-1])
    if past is None:
        oa = _attn(q, k, vt, _tile(l, 512), 256)
        s0t = jnp.zeros((ns, B_HEADS, B_DV, B_DK), F32)
    else:
        cache_ckv, cache_kpe, state = past
        oa = _sample_attn(as_streams(q), cache_ckv, cache_kpe, as_streams(ckv), as_streams(kpe),
                          w["wq2l"], w["esel"], w["wuvp"]).reshape(b, l, A_HEADS * V_DIM)
        s0t = jnp.swapaxes(state.astype(F32), -1, -2)
    ob, snt = _hgrn(qb, kb, lf, vb, og, s0t, w["gon"], ns, 512 if past is None else 8 * ls)
    x1, xs, slot, meta = _out_proj(oa, ob, sa, sb, x, w["wa"], w["wb"], w["wo"], w["gffn"], w["wr"],
                                   w["br"], tm, past is None)
    work = _work_list(meta, meta.shape[0] * _max_items(tm))
    ys = _moe_experts(xs, work, w["wrg"], w["brg"], w["wg"], w["wu"], w["wd"])
    y = _moe_combine(ys, slot, meta, x1, w["gfin"], tm)
    return y, ckv, kpe, jnp.swapaxes(snt, -1, -2)


def kernel(x_prompt, x_sample, cache_ckv, cache_kpe, state_hgrn, g_mix, w_in, g_q, w_uq, g_kv, w_ukv,
           lb_hgrn, g_onorm, w_a_out, w_b_out, w_o, g_ffn, w_rg, b_rg, w_re, b_re, w_gate, w_up, w_down,
           g_final):
    w = _prep_weights(g_mix, w_in, g_q, w_uq, g_kv, w_ukv, lb_hgrn, g_onorm, w_a_out, w_b_out, w_o,
                      g_ffn, w_rg, b_rg, w_re, b_re, w_gate, w_up, w_down, g_final)
    bp, lp, d = x_prompt.shape
    y_p, ckv_p, kpe_p, st_p = _run_group(x_prompt, jnp.arange(lp, dtype=jnp.int32), (bp, lp), None, w)

    bs, ls, _ = x_sample.shape
    past_len = cache_ckv.shape[2]
    pos_s = past_len + (jnp.arange(bs * ls, dtype=jnp.int32) % ls)
    y_s, ckv_s, kpe_s, st_s = _run_group(x_sample.reshape(1, bs * ls, d), pos_s, (bs, ls),
                                         (cache_ckv[0], cache_kpe[0], state_hgrn[0]), w)
    return (y_p, y_s.reshape(bs, ls, d),
            ckv_p[None], kpe_p[None], st_p[None].astype(x_prompt.dtype),
            ckv_s.reshape(1, bs, ls, KV_LORA), kpe_s.reshape(1, bs, ls, ROPE_DIM),
            st_s[None].astype(state_hgrn.dtype))
```

```python
import functools

import numpy as np
import jax
import jax.numpy as jnp
from jax import lax
from jax.experimental import pallas as pl
from jax.experimental.pallas import tpu as pltpu

F32 = jnp.float32
BF16 = jnp.bfloat16

EPS = 1e-6
CHUNK = 64
A_HEADS = 8
Q_LORA = 384
KV_LORA = 256
NOPE_DIM = 64
ROPE_DIM = 32
V_DIM = 64
ROPE_THETA = 10000.0
B_HEADS = 8
B_DK = 128
B_DV = 64
N_GROUPS = 4
EXPERTS_PER_GROUP = 8
N_EXPERTS = N_GROUPS * EXPERTS_PER_GROUP
D_FF_EXPERT = 256

LANES = 128
VMEM_BYTES_V7X = 64 * 1024 * 1024
VMEM_LIMIT = VMEM_BYTES_V7X - 8 * 1024 * 1024

HEAD_PAD = LANES
ROPE_LO = NOPE_DIM
ROPE_HI = NOPE_DIM + ROPE_DIM
HALF = ROPE_DIM // 2
SCALE = (NOPE_DIM + ROPE_DIM) ** -0.5
LOG2E = 1.4426950408889634
NEG_INF = float("-inf")


def _params(sem):
    return pltpu.CompilerParams(dimension_semantics=sem, vmem_limit_bytes=VMEM_LIMIT)


def _const_spec(shape):
    nd = len(shape)
    return pl.BlockSpec(shape, lambda *_: (0,) * nd, pipeline_mode=pl.Buffered(1))


def _rms(x, g):
    ms = jnp.mean(x * x, axis=-1, keepdims=True)
    return x * lax.rsqrt(ms + EPS) * g


def _sig_pair(x):
    e = jnp.exp(-jnp.abs(x))
    r = 1.0 / (1.0 + e)
    er = e * r
    pos = x >= 0
    return jnp.where(pos, r, er), jnp.where(pos, er, r)


def _nt(a, b):
    return lax.dot_general(a, b, (((1,), (1,)), ((), ())), preferred_element_type=F32)


def _tn(a, b):
    return lax.dot_general(a, b, (((0,), (0,)), ((), ())), preferred_element_type=F32)


def _dot(a, b):
    return jnp.dot(a, b, preferred_element_type=F32)


def _rope(x, cosp, sinlo, sinhi):
    return x * cosp + pltpu.roll(x, LANES - HALF, 1) * sinlo + pltpu.roll(x, HALF, 1) * sinhi


def _mla_proj_kernel(x_ref, gmix_ref, win_ref, gq_ref, wuq_ref, gkv_ref, wuk_ref, wuvt_ref,
                     cos_ref, sinlo_ref, sinhi_ref,
                     h_ref, ckv_ref, kpe_ref, q_ref, k_ref, vt_ref, *, q_transposed):
    h = _rms(x_ref[...], gmix_ref[...]).astype(BF16)
    h_ref[...] = h
    z = _dot(h, win_ref[...])
    cosp, sinlo, sinhi = cos_ref[...], sinlo_ref[...], sinhi_ref[...]
    lane = lax.broadcasted_iota(jnp.int32, (1, LANES), 1)
    rope_lanes = (lane // ROPE_DIM) == (ROPE_LO // ROPE_DIM)

    cqn = _rms(z[:, :Q_LORA], gq_ref[...]).astype(BF16)
    q = _dot(cqn, wuq_ref[...])
    for hd in range(A_HEADS):
        sl = slice(hd * HEAD_PAD, (hd + 1) * HEAD_PAD)
        qh = q[:, sl]
        qh = jnp.where(rope_lanes, _rope(qh, cosp, sinlo, sinhi), qh) * (SCALE * LOG2E)
        if q_transposed:
            q_ref[sl, :] = qh.T.astype(BF16)
        else:
            q_ref[:, sl] = qh.astype(BF16)

    ckv = _rms(z[:, Q_LORA:Q_LORA + KV_LORA], gkv_ref[...])
    ckv_ref[...] = ckv
    ckv_bf = ckv.astype(BF16)
    kpe_rot = _rope(z[:, Q_LORA + KV_LORA:], cosp, sinlo, sinhi)
    kpe_ref[...] = kpe_rot[:, :ROPE_DIM]
    kpe_placed = jnp.where(rope_lanes, kpe_rot, 0.0)
    kn = _dot(ckv_bf, wuk_ref[...])
    for hd in range(A_HEADS):
        sl = slice(hd * HEAD_PAD, (hd + 1) * HEAD_PAD)
        k_ref[:, sl] = (kn[:, sl] + kpe_placed).astype(BF16)
    vt_ref[...] = _nt(wuvt_ref[...], ckv_bf).astype(BF16)


def _mla_proj(x, gmix, win_a, gq, wuq, gkv, wuk, wuvt, cosp, sinlo, sinhi, tm, q_transposed):
    b, l, d = x.shape
    hq = A_HEADS * HEAD_PAD
    hv = A_HEADS * V_DIM
    tok = lambda w: pl.BlockSpec((None, tm, w), lambda bi, i: (bi, i, 0))
    tok_t = lambda w: pl.BlockSpec((None, w, tm), lambda bi, i: (bi, 0, i))
    tab = pl.BlockSpec((tm, LANES), lambda bi, i: (i, 0))
    return pl.pallas_call(
        functools.partial(_mla_proj_kernel, q_transposed=q_transposed),
        grid=(b, l // tm),
        in_specs=[tok(d), _const_spec(gmix.shape), _const_spec(win_a.shape), _const_spec(gq.shape),
                  _const_spec(wuq.shape), _const_spec(gkv.shape), _const_spec(wuk.shape),
                  _const_spec(wuvt.shape), tab, tab, tab],
        out_specs=[tok(d), tok(KV_LORA), tok(ROPE_DIM), tok_t(hq) if q_transposed else tok(hq), tok(hq),
                   tok_t(hv)],
        out_shape=[jax.ShapeDtypeStruct((b, l, d), BF16),
                   jax.ShapeDtypeStruct((b, l, KV_LORA), F32),
                   jax.ShapeDtypeStruct((b, l, ROPE_DIM), F32),
                   jax.ShapeDtypeStruct((b, hq, l) if q_transposed else (b, l, hq), BF16),
                   jax.ShapeDtypeStruct((b, l, hq), BF16),
                   jax.ShapeDtypeStruct((b, hv, l), BF16)],
        compiler_params=_params(("parallel", "parallel")),
        name="mla_proj",
    )(x, gmix, win_a, gq, wuq, gkv, wuk, wuvt, cosp, sinlo, sinhi)


def _hgrn_proj_kernel(h_ref, w_ref, lb_ref, qb_ref, kb_ref, lf_ref, vb_ref, og_ref, sa_ref, sb_ref):
    h = h_ref[...]
    dk = B_HEADS * B_DK
    dv = B_HEADS * B_DV
    dm = sa_ref.shape[-1]
    o = 0
    zq = _dot(h, w_ref[:, o:o + dk]); o += dk
    s, _ = _sig_pair(zq)
    qb_ref[...] = zq * s
    zf = _dot(h, w_ref[:, o:o + dk]); o += dk
    sf, snf = _sig_pair(zf)
    lb = lb_ref[...]
    lf_ref[...] = jnp.log(lb + (1.0 - lb) * sf)
    kb_ref[...] = (1.0 - lb) * snf
    vb_ref[...] = _dot(h, w_ref[:, o:o + dv]).astype(BF16); o += dv
    zg = _dot(h, w_ref[:, o:o + dv]); o += dv
    s, _ = _sig_pair(zg)
    og_ref[...] = (zg * s).astype(BF16)
    s, _ = _sig_pair(_dot(h, w_ref[:, o:o + dm])); o += dm
    sa_ref[...] = s.astype(BF16)
    s, _ = _sig_pair(_dot(h, w_ref[:, o:o + dm]))
    sb_ref[...] = s.astype(BF16)


def _hgrn_proj(h, win_b, lb, tm):
    b, l, d = h.shape
    dk = B_HEADS * B_DK
    dv = B_HEADS * B_DV
    tok = lambda w: pl.BlockSpec((None, tm, w), lambda bi, i: (bi, i, 0))
    sds = lambda w, dt: jax.ShapeDtypeStruct((b, l, w), dt)
    return pl.pallas_call(
        _hgrn_proj_kernel,
        grid=(b, l // tm),
        in_specs=[tok(d), _const_spec(win_b.shape), _const_spec(lb.shape)],
        out_specs=[tok(dk), tok(dk), tok(dk), tok(dv), tok(dv), tok(d), tok(d)],
        out_shape=[sds(dk, F32), sds(dk, F32), sds(dk, F32), sds(dv, BF16), sds(dv, BF16),
                   sds(d, BF16), sds(d, BF16)],
        compiler_params=_params(("parallel", "parallel")),
        name="hgrn_proj",
    )(h, win_b, lb)


def _attn_kernel(qt_ref, k_ref, vt_ref, o_ref, s_sc, m_sc, l_sc, acc_sc, *, t, tc):
    qi = pl.program_id(2)
    n_chains = t // tc
    m_sc[...] = jnp.full(m_sc.shape, NEG_INF, F32)
    l_sc[...] = jnp.zeros(l_sc.shape, F32)
    acc_sc[...] = jnp.zeros(acc_sc.shape, F32)

    def scores(kb, slot):
        start = pl.multiple_of(kb * t, t)
        k = k_ref[pl.ds(start, t), :]
        for ch in range(n_chains):
            cols = slice(ch * tc, (ch + 1) * tc)
            s_sc[slot, :, cols] = _dot(k, qt_ref[:, cols])

    def consume(kb, slot, masked):
        start = pl.multiple_of(kb * t, t)
        vt = vt_ref[:, pl.ds(start, t)]
        for ch in range(n_chains):
            cols = slice(ch * tc, (ch + 1) * tc)
            s = s_sc[slot, :, cols]
            if masked:
                r = lax.broadcasted_iota(jnp.int32, (t, tc), 0) // CHUNK
                c = (lax.broadcasted_iota(jnp.int32, (t, tc), 1) + ch * tc) // CHUNK
                s = jnp.where(r <= c, s, NEG_INF)
            m_prev = m_sc[:, cols]
            m_new = jnp.maximum(m_prev, jnp.max(s, axis=0, keepdims=True))
            alpha = jnp.exp2(m_prev - m_new)
            p = jnp.exp2(s - m_new)
            l_sc[:, cols] = alpha * l_sc[:, cols] + jnp.sum(p, axis=0, keepdims=True)
            acc_sc[:, cols] = alpha * acc_sc[:, cols] + _dot(vt, p.astype(BF16))
            m_sc[:, cols] = m_new

    scores(0, 0)

    def body(j, carry):
        kb = 2 * j
        scores(kb + 1, 1)
        consume(kb, 0, False)
        scores(kb + 2, 0)
        consume(kb + 1, 1, False)
        return carry

    lax.fori_loop(0, qi // 2, body, 0)

    @pl.when(qi % 2 == 0)
    def _():
        consume(qi, 0, True)

    @pl.when(qi % 2 == 1)
    def _():
        scores(qi, 1)
        consume(qi - 1, 0, False)
        consume(qi, 1, True)

    o_ref[...] = (acc_sc[...] / l_sc[...]).astype(BF16)


def _attn(qt, k, vt, t, tc):
    b, l, _ = k.shape
    return pl.pallas_call(
        functools.partial(_attn_kernel, t=t, tc=min(tc, t)),
        grid=(b, A_HEADS, l // t),
        in_specs=[pl.BlockSpec((None, HEAD_PAD, t), lambda bi, h, i: (bi, h, i)),
                  pl.BlockSpec((None, l, HEAD_PAD), lambda bi, h, i: (bi, 0, h)),
                  pl.BlockSpec((None, V_DIM, l), lambda bi, h, i: (bi, h, 0))],
        out_specs=pl.BlockSpec((None, V_DIM, t), lambda bi, h, i: (bi, h, i)),
        out_shape=jax.ShapeDtypeStruct((b, A_HEADS * V_DIM, l), BF16),
        scratch_shapes=[pltpu.VMEM((2, t, t), F32), pltpu.VMEM((1, t), F32), pltpu.VMEM((1, t), F32),
                        pltpu.VMEM((V_DIM, t), F32)],
        compiler_params=_params(("parallel", "parallel", "arbitrary")),
        name="attn",
    )(qt, k, vt)


def _sample_attn_kernel(q_ref, cc_ref, ck_ref, nc_ref, nk_ref, wq2l_ref, esel_ref, wuv_ref, o_ref,
                        *, past, n_new):
    qs = q_ref[...]
    heads = [qs[:, hd * HEAD_PAD:(hd + 1) * HEAD_PAD] for hd in range(A_HEADS)]
    ql = jnp.concatenate([_dot(heads[hd], wq2l_ref[hd]) for hd in range(A_HEADS)], axis=0).astype(BF16)
    qp = jnp.concatenate([_dot(heads[hd], esel_ref[...]) for hd in range(A_HEADS)], axis=0).astype(BF16)
    cc = cc_ref[...].astype(BF16)
    ck = ck_ref[...].astype(BF16)
    nc = nc_ref[...].astype(BF16)
    nk = nk_ref[...].astype(BF16)
    s_c = _nt(ql, cc) + _nt(qp, ck)
    s_n = _nt(ql, nc) + _nt(qp, nk)
    rows = A_HEADS * n_new
    tq = lax.broadcasted_iota(jnp.int32, (rows, n_new), 0) % n_new
    tk = lax.broadcasted_iota(jnp.int32, (rows, n_new), 1)
    s_n = jnp.where((past + tk) // CHUNK <= (past + tq) // CHUNK, s_n, NEG_INF)
    m = jnp.maximum(jnp.max(s_c, axis=-1, keepdims=True), jnp.max(s_n, axis=-1, keepdims=True))
    p_c = jnp.exp2(s_c - m)
    p_n = jnp.exp2(s_n - m)
    denom = jnp.sum(p_c, axis=-1, keepdims=True) + jnp.sum(p_n, axis=-1, keepdims=True)
    o_lat = ((_dot(p_c.astype(BF16), cc) + _dot(p_n.astype(BF16), nc)) / denom).astype(BF16)
    full = _dot(o_lat, wuv_ref[...])
    col_head = lax.broadcasted_iota(jnp.int32, (n_new, A_HEADS * V_DIM), 1) // V_DIM
    out = jnp.zeros((n_new, A_HEADS * V_DIM), F32)
    for hd in range(A_HEADS):
        out = out + jnp.where(col_head == hd, full[hd * n_new:(hd + 1) * n_new], 0.0)
    o_ref[...] = out.astype(BF16)


def _sample_attn(q, cache_ckv, cache_kpe, ckv_new, kpe_new, wq2l, esel, wuvp):
    nb, n_new, hq = q.shape
    past = cache_ckv.shape[1]
    hv = A_HEADS * V_DIM
    per = lambda r, w: pl.BlockSpec((None, r, w), lambda bi: (bi, 0, 0))
    return pl.pallas_call(
        functools.partial(_sample_attn_kernel, past=past, n_new=n_new),
        grid=(nb,),
        in_specs=[per(n_new, hq), per(past, KV_LORA), per(past, ROPE_DIM), per(n_new, KV_LORA),
                  per(n_new, ROPE_DIM), _const_spec(wq2l.shape), _const_spec(esel.shape),
                  _const_spec(wuvp.shape)],
        out_specs=per(n_new, hv),
        out_shape=jax.ShapeDtypeStruct((nb, n_new, hv), BF16),
        compiler_params=_params(("parallel",)),
        name="sample_attn",
    )(q, cache_ckv, cache_kpe, ckv_new, kpe_new, wq2l, esel, wuvp)


def _level_sizes(c):
    sizes = []
    p = c
    while p >= 2:
        sizes.append(p)
        p //= 2
    return sizes


def _level_masks(c):
    t = np.arange(c)[:, None]
    s = np.arange(c)[None, :]
    out = []
    for p in _level_sizes(c):
        out.append((t // p == s // p) & (t % p >= p // 2) & (s % p < p // 2))
    out.append(t == s)
    return np.stack(out).astype(np.float32)


def _boundary(cum, p, c):
    half = p // 2
    if p >= 8:
        parts = [jnp.broadcast_to(cum[i * p + half - 1:i * p + half, :], (p, cum.shape[1]))
                 for i in range(c // p)]
        return jnp.concatenate(parts, axis=0) if len(parts) > 1 else parts[0]
    sub = lax.broadcasted_iota(jnp.int32, (8, cum.shape[1]), 0)
    parts = []
    for g in range(c // 8):
        lo = jnp.broadcast_to(cum[g * 8 + 1:g * 8 + 2, :], (8, cum.shape[1]))
        hi = jnp.broadcast_to(cum[g * 8 + 5:g * 8 + 6, :], (8, cum.shape[1]))
        parts.append(jnp.where(sub < 4, lo, hi))
    return jnp.concatenate(parts, axis=0)


def _hgrn_kernel(q_ref, k_ref, lf_ref, v_ref, og_ref, s0t_ref, gon_ref, mask_ref, tri_ref,
                 o_ref, snt_ref, st_sc, *, c, n_chunks, per_chunk_state):
    step = pl.program_id(2)
    sizes = _level_sizes(c)
    lane = lax.broadcasted_iota(jnp.int32, (1, LANES), 1)
    low = lane < B_DV
    keep_lo = jnp.where(low, 1.0, 0.0).astype(BF16)
    keep_hi = jnp.where(low, 0.0, 1.0).astype(BF16)

    if not per_chunk_state:
        @pl.when(step == 0)
        def _():
            st_sc[...] = jnp.zeros(st_sc.shape, F32)
            st_sc[0, 0:B_DV, :] = s0t_ref[0]
            st_sc[1, B_DV:2 * B_DV, :] = s0t_ref[1]

    tri = tri_ref[...]
    rows = [slice(ci * c, (ci + 1) * c) for ci in range(n_chunks)]
    heads = [slice(j * B_DK, (j + 1) * B_DK) for j in range(2)]
    items = [(ci, j) for ci in range(n_chunks) for j in range(2)]


    cums = []
    for ci in range(n_chunks):
        lf = lf_ref[rows[ci], :]
        hi = lf.astype(BF16)
        r1 = lf - hi.astype(F32)
        mid = r1.astype(BF16)
        lo = (r1 - mid.astype(F32)).astype(BF16)
        cums.append((_dot(tri, hi) + _dot(tri, mid) + _dot(tri, lo)) * LOG2E)

    def zeros(n):
        return jnp.zeros((n, B_DK), F32)

    a_mats = {}
    for ci, j in items:
        q = q_ref[rows[ci], heads[j]]
        k = k_ref[rows[ci], heads[j]]
        cum = cums[ci][:, heads[j]]
        kbf = k.astype(BF16)
        a = _nt(q.astype(BF16), kbf) * mask_ref[len(sizes)]
        q_cols, k_cols = [], []
        for li, p in enumerate(sizes):
            half = p // 2
            if half % 8 == 0:
                for lo in range(0, c, p):
                    mid, hi = lo + half, lo + p
                    bnd = cum[mid - 1:mid, :]
                    qu = q[mid:hi] * jnp.exp2(cum[mid:hi] - bnd)
                    kl = k[lo:mid] * jnp.exp2(bnd - cum[lo:mid])
                    q_cols.append(jnp.concatenate([x for x in (zeros(mid), qu, zeros(c - hi)) if x.shape[0]], axis=0))
                    k_cols.append(jnp.concatenate([x for x in (zeros(lo), kl, zeros(c - mid)) if x.shape[0]], axis=0))
                continue
            if p == 2:
                qt = (q * jnp.exp2(lf_ref[rows[ci], heads[j]] * LOG2E)).astype(BF16)
                kt = kbf
            else:
                bnd = _boundary(cum, p, c)
                qt = (q * jnp.exp2(jnp.minimum(cum - bnd, 0.0))).astype(BF16)
                kt = (k * jnp.exp2(jnp.minimum(bnd - cum, 0.0))).astype(BF16)
            a = a + _nt(qt, kt) * mask_ref[li]
        if q_cols:
            a = a + _nt(jnp.concatenate(q_cols, axis=1).astype(BF16), jnp.concatenate(k_cols, axis=1).astype(BF16))
        a_mats[ci, j] = a.astype(BF16)

    qhats, kvs, decays, vs = {}, {}, {}, {}
    for ci, j in items:
        cum = cums[ci][:, heads[j]]
        last = cum[c - 1:c, :]
        vs[ci, j] = v_ref[rows[ci], :] * (keep_lo if j == 0 else keep_hi)
        qhats[ci, j] = (q_ref[rows[ci], heads[j]] * jnp.exp2(cum)).astype(BF16)
        khat = (k_ref[rows[ci], heads[j]] * jnp.exp2(last - cum)).astype(BF16)
        kvs[ci, j] = _tn(vs[ci, j], khat)
        decays[ci, j] = jnp.exp2(last)

    pad = jnp.zeros((B_DV, B_DK), F32)
    if not per_chunk_state:
        st = [st_sc[0], st_sc[1]]
    for ci in range(n_chunks):
        if per_chunk_state:
            st = [jnp.concatenate([s0t_ref[ci, 0], pad], axis=0), jnp.concatenate([pad, s0t_ref[ci, 1]], axis=0)]
        o_pair = jnp.zeros((c, LANES), F32)
        for j in range(2):
            o_pair = o_pair + _dot(a_mats[ci, j], vs[ci, j]) + _nt(qhats[ci, j], st[j].astype(BF16))
            st[j] = st[j] * decays[ci, j] + kvs[ci, j]
        if per_chunk_state:
            snt_ref[ci, 0] = st[0][0:B_DV, :]
            snt_ref[ci, 1] = st[1][B_DV:2 * B_DV, :]
        o2 = o_pair * o_pair
        s_lo = jnp.sum(jnp.where(low, o2, 0.0), axis=-1, keepdims=True)
        s_hi = jnp.sum(jnp.where(low, 0.0, o2), axis=-1, keepdims=True)
        ms = jnp.where(low, s_lo, s_hi) * (1.0 / B_DV)
        ob = o_pair * lax.rsqrt(ms + EPS) * gon_ref[...] * og_ref[rows[ci], :].astype(F32)
        o_ref[rows[ci], :] = ob.astype(BF16)
    if not per_chunk_state:
        st_sc[0] = st[0]
        st_sc[1] = st[1]

        @pl.when(step == pl.num_programs(2) - 1)
        def _():
            snt_ref[0] = st_sc[0, 0:B_DV, :]
            snt_ref[1] = st_sc[1, B_DV:2 * B_DV, :]


def _hgrn(qb, kb, lf, vb, og, s0t, gon, n_streams, tc):
    b, l, dk = qb.shape
    per_chunk_state = n_streams != b
    if per_chunk_state:
        assert b == 1 and l % n_streams == 0 and l // n_streams <= CHUNK
        c = l // n_streams
    else:
        c = min(CHUNK, l)
    tc = min(tc, l)
    n_chunks = tc // c
    masks = jnp.asarray(_level_masks(c))
    tri = jnp.asarray(np.tril(np.ones((c, c), np.float32)), dtype=BF16)
    pairs = B_HEADS // 2
    wide = pl.BlockSpec((None, tc, 2 * B_DK), lambda bi, p, i: (bi, i, p))
    narrow = pl.BlockSpec((None, tc, 2 * B_DV), lambda bi, p, i: (bi, i, p))
    if per_chunk_state:
        state = pl.BlockSpec((n_chunks, 2, B_DV, B_DK), lambda bi, p, i: (i, p, 0, 0))
    else:
        state = pl.BlockSpec((None, 2, B_DV, B_DK), lambda bi, p, i: (bi, p, 0, 0))
    return pl.pallas_call(
        functools.partial(_hgrn_kernel, c=c, n_chunks=n_chunks, per_chunk_state=per_chunk_state),
        grid=(b, pairs, l // tc),
        in_specs=[wide, wide, wide, narrow, narrow, state, _const_spec(gon.shape),
                  _const_spec(masks.shape), _const_spec(tri.shape)],
        out_specs=[narrow, state],
        out_shape=[jax.ShapeDtypeStruct((b, l, B_HEADS * B_DV), BF16),
                   jax.ShapeDtypeStruct((n_streams, B_HEADS, B_DV, B_DK), F32)],
        scratch_shapes=[pltpu.VMEM((2, 2 * B_DV, B_DK), F32)],
        compiler_params=_params(("parallel", "parallel", "parallel" if per_chunk_state else "arbitrary")),
        name="hgrn",
    )(qb, kb, lf, vb, og, s0t, gon, masks, tri)


MOE_SUB = 144
MOE_ROWS = 1024
_BIG_LANE = float(1 << 20)


def _max_items(tm):
    return (tm + N_GROUPS * (MOE_SUB - 1)) // MOE_SUB


def _first_lane_of_max(x, lane):
    v = jnp.max(x, axis=-1, keepdims=True)
    return v, jnp.min(jnp.where(x == v, lane, _BIG_LANE), axis=-1, keepdims=True)


def _out_proj_kernel(oa_ref, ob_ref, sa_ref, sb_ref, x_ref, wa_ref, wb_ref, wo_ref, gffn_ref,
                     wr_ref, br_ref, tri_ref, upper_ref, x1_ref, xs_ref, slot_ref, meta_ref, *, oa_transposed):
    tm = x_ref.shape[0]
    if oa_transposed:
        ya = _tn(oa_ref[...], wa_ref[...])
    else:
        ya = _dot(oa_ref[...], wa_ref[...])
    yb = _dot(ob_ref[...], wb_ref[...])
    m = (sa_ref[...].astype(F32) * ya + sb_ref[...].astype(F32) * yb).astype(BF16)
    x1 = x_ref[...] + _dot(m, wo_ref[...])
    x1_ref[...] = x1
    h2 = _rms(x1, gffn_ref[...]).astype(BF16)

    logits = _dot(h2, wr_ref[...]) + br_ref[...]
    lane = lax.broadcasted_iota(jnp.int32, (tm, LANES), 1).astype(F32)
    _, grp = _first_lane_of_max(jnp.where(lane < N_GROUPS, logits, NEG_INF), lane)
    onehot = jnp.where(lane == grp, 1.0, 0.0)
    rank = _dot(tri_ref[...], onehot.astype(BF16))
    count = rank[tm - 1:tm, :]
    items = jnp.zeros_like(count)
    for j in range(_max_items(tm)):
        items = items + jnp.where(count > j * MOE_SUB, 1.0, 0.0)
    base = _dot(jnp.broadcast_to(items, (8, LANES)).astype(BF16), upper_ref[...])[0:1, :] * MOE_SUB
    slot = jnp.sum(onehot * (base + rank - 1.0), axis=-1, keepdims=True)
    col = lax.broadcasted_iota(jnp.int32, (tm, MOE_ROWS), 1).astype(F32)
    place = jnp.where(col == slot, 1.0, 0.0).astype(BF16)
    xs_ref[...] = _tn(place, h2).astype(BF16)
    slot_ref[...] = jnp.broadcast_to(slot, (tm, LANES))
    meta_ref[...] = jnp.broadcast_to(items, (8, LANES))


def _out_proj(oa, ob, sa, sb, x, wa, wb, wo, gffn, wr, br, tm, oa_transposed):
    b, l, d = x.shape
    nt = l // tm
    assert _max_items(tm) * MOE_SUB <= MOE_ROWS
    hv = A_HEADS * V_DIM
    tri = jnp.asarray(np.tril(np.ones((tm, tm), np.float32)), dtype=BF16)
    upper = jnp.asarray(np.triu(np.ones((LANES, LANES), np.float32), 1), dtype=BF16)
    tok = lambda w: pl.BlockSpec((None, tm, w), lambda bi, i: (bi, i, 0))
    per_tile = lambda r, w: pl.BlockSpec((None, r, w), lambda bi, i: (bi * nt + i, 0, 0))
    oa_spec = pl.BlockSpec((None, hv, tm), lambda bi, i: (bi, 0, i)) if oa_transposed else tok(hv)
    return pl.pallas_call(
        functools.partial(_out_proj_kernel, oa_transposed=oa_transposed),
        grid=(b, nt),
        in_specs=[oa_spec, tok(B_HEADS * B_DV), tok(d), tok(d), tok(d), _const_spec(wa.shape),
                  _const_spec(wb.shape), _const_spec(wo.shape), _const_spec(gffn.shape),
                  _const_spec(wr.shape), _const_spec(br.shape), _const_spec(tri.shape),
                  _const_spec(upper.shape)],
        out_specs=[tok(d), per_tile(MOE_ROWS, d), tok(LANES), per_tile(8, LANES)],
        out_shape=[jax.ShapeDtypeStruct((b, l, d), F32),
                   jax.ShapeDtypeStruct((b * nt, MOE_ROWS, d), BF16),
                   jax.ShapeDtypeStruct((b, l, LANES), F32),
                   jax.ShapeDtypeStruct((b * nt, 8, LANES), F32)],
        compiler_params=_params(("parallel", "parallel")),
        name="out_proj",
    )(oa, ob, sa, sb, x, wa, wb, wo, gffn, wr, br, tri, upper)


def _work_list(meta, n_items):
    n = meta[:, 0, :N_GROUPS].astype(jnp.int32)
    tiles = n.shape[0]
    first = jnp.cumsum(n, axis=1) - n
    per_seg = n.T.reshape(-1)
    ends = jnp.cumsum(per_seg)
    total = ends[-1]
    i = jnp.minimum(jnp.arange(n_items, dtype=jnp.int32), total - 1)
    seg = jnp.sum((ends[None, :] <= i[:, None]).astype(jnp.int32), axis=1)
    grp, tile = seg // tiles, seg % tiles
    block = first[tile, grp] + i - (ends[seg] - per_seg[seg])
    return tile, block.astype(jnp.int32), grp, total[None].astype(jnp.int32)


def _moe_expert_kernel(tile_ref, block_ref, grp_ref, total_ref, xs_ref, wr_ref, br_ref, wg_ref, wu_ref,
                       wd_ref, ys_ref):
    i = pl.program_id(0)

    @pl.when(i < total_ref[0])
    def _():
        g = grp_ref[i]
        x = xs_ref[...]
        logits = _dot(x, wr_ref[...]) + br_ref[...]
        lane_i = lax.broadcasted_iota(jnp.int32, logits.shape, 1)
        lane = lane_i.astype(F32)
        is_grp = jnp.abs(lane - (EXPERTS_PER_GROUP + 0.5 * (N_GROUPS - 1))) < 0.5 * N_GROUPS
        lg = jnp.where(is_grp, logits, NEG_INF)
        gmax = jnp.max(lg, axis=-1, keepdims=True)
        denom = jnp.sum(jnp.exp(lg - gmax), axis=-1, keepdims=True)
        lg_own = jnp.sum(jnp.where(lane_i == EXPERTS_PER_GROUP + g, logits, 0.0), axis=-1, keepdims=True)
        p_grp = jnp.exp(lg_own - gmax) / denom
        le = jnp.where(lane < EXPERTS_PER_GROUP, logits, NEG_INF)
        v1, i1 = _first_lane_of_max(le, lane)
        le2 = jnp.where(lane == i1, NEG_INF, le)
        v2, i2 = _first_lane_of_max(le2, lane)
        e2 = jnp.exp(v2 - v1)
        w1 = p_grp / (1.0 + e2)
        comb = jnp.where(lane == i1, w1, 0.0) + jnp.where(lane == i2, w1 * e2, 0.0)
        hid = []
        for e in range(EXPERTS_PER_GROUP):
            a = _dot(x, wg_ref[e])
            u = _dot(x, wu_ref[e])
            s, _ = _sig_pair(a)
            hid.append((a * s * u * comb[:, e:e + 1]).astype(BF16))
        hid = jnp.concatenate(hid, axis=-1)
        wd = wd_ref[...].reshape(EXPERTS_PER_GROUP * D_FF_EXPERT, wd_ref.shape[-1])
        ys_ref[...] = _dot(hid, wd).astype(BF16)


def _moe_experts(xs, work, wrg, brg, wg, wu, wd):
    tiles, _, d = xs.shape
    tile, block, grp, total = work
    item = lambda i, t, k, g, n: (t[i], k[i], 0)
    by_group = lambda i, t, k, g, n: (g[i], 0, 0)
    grid_spec = pltpu.PrefetchScalarGridSpec(
        num_scalar_prefetch=4,
        grid=(tile.shape[0],),
        in_specs=[pl.BlockSpec((None, MOE_SUB, d), item),
                  pl.BlockSpec((None, d, LANES), by_group),
                  pl.BlockSpec((None, 1, LANES), by_group),
                  pl.BlockSpec((EXPERTS_PER_GROUP, d, D_FF_EXPERT), by_group),
                  pl.BlockSpec((EXPERTS_PER_GROUP, d, D_FF_EXPERT), by_group),
                  pl.BlockSpec((EXPERTS_PER_GROUP, D_FF_EXPERT, d), by_group)],
        out_specs=pl.BlockSpec((None, MOE_SUB, d), item))
    return pl.pallas_call(
        _moe_expert_kernel,
        grid_spec=grid_spec,
        out_shape=jax.ShapeDtypeStruct(xs.shape, BF16),
        input_output_aliases={4: 0},
        compiler_params=_params(("arbitrary",)),
        name="moe_experts",
    )(tile, block, grp, total, xs, wrg, brg, wg, wu, wd)


def _moe_combine_kernel(ys_ref, slot_ref, x1_ref, gfin_ref, y_ref):
    tm = x1_ref.shape[0]
    col = lax.broadcasted_iota(jnp.int32, (tm, MOE_ROWS), 1).astype(F32)
    slot = jnp.concatenate([slot_ref[...]] * (MOE_ROWS // LANES), axis=-1)
    place = jnp.where(col == slot, 1.0, 0.0).astype(BF16)
    y_ref[...] = _rms(x1_ref[...] + _dot(place, ys_ref[...]), gfin_ref[...])


def _moe_combine(ys, slot, x1, gfin, tm):
    b, l, d = x1.shape
    nt = l // tm
    tok = lambda w: pl.BlockSpec((None, tm, w), lambda bi, i: (bi, i, 0))
    per_tile = lambda r, w: pl.BlockSpec((None, r, w), lambda bi, i: (bi * nt + i, 0, 0))
    return pl.pallas_call(
        _moe_combine_kernel,
        grid=(b, nt),
        in_specs=[per_tile(MOE_ROWS, d), tok(LANES), tok(d), _const_spec(gfin.shape)],
        out_specs=tok(d),
        out_shape=jax.ShapeDtypeStruct((b, l, d), F32),
        compiler_params=_params(("parallel", "parallel")),
        name="moe_combine",
    )(ys, slot, x1, gfin)


def _rope_tables(pos):
    inv = jnp.power(ROPE_THETA, -jnp.arange(HALF, dtype=F32) / HALF)
    ang = pos.astype(F32)[:, None] * inv[None, :]
    cos, sin = jnp.cos(ang), jnp.sin(ang)
    zeros = jnp.zeros_like(sin)
    reps = LANES // ROPE_DIM
    cosp = jnp.tile(jnp.concatenate([cos, cos], axis=1), (1, reps))
    sinlo = jnp.tile(jnp.concatenate([-sin, zeros], axis=1), (1, reps))
    sinhi = jnp.tile(jnp.concatenate([zeros, sin], axis=1), (1, reps))
    return cosp, sinlo, sinhi


def _prep_weights(g_mix, w_in, g_q, w_uq, g_kv, w_ukv, lb_hgrn, g_onorm, w_a_out, w_b_out, w_o,
                  g_ffn, w_rg, b_rg, w_re, b_re, w_gate, w_up, w_down, g_final):
    assert w_in.shape[0] == 1, "single-layer step"
    d = w_in.shape[1]
    w = w_in[0]
    n_a = Q_LORA + KV_LORA
    win_a = jnp.concatenate([w[:, :n_a]] + [w[:, n_a:n_a + ROPE_DIM]] * (LANES // ROPE_DIM), axis=1).astype(BF16)
    win_b = w[:, n_a + ROPE_DIM:].astype(BF16)
    per_q = NOPE_DIM + ROPE_DIM
    wuq = jnp.pad(w_uq[0].reshape(Q_LORA, A_HEADS, per_q), ((0, 0), (0, 0), (0, HEAD_PAD - per_q)))
    wuq = wuq.reshape(Q_LORA, A_HEADS * HEAD_PAD).astype(BF16)
    wukv = w_ukv[0].reshape(KV_LORA, A_HEADS, NOPE_DIM + V_DIM)
    w_uk, w_uv = wukv[..., :NOPE_DIM], wukv[..., NOPE_DIM:]
    wuk = jnp.pad(w_uk, ((0, 0), (0, 0), (0, HEAD_PAD - NOPE_DIM))).reshape(KV_LORA, A_HEADS * HEAD_PAD).astype(BF16)
    wuvt = w_uv.reshape(KV_LORA, A_HEADS * V_DIM).T.astype(BF16)
    wq2l = jnp.pad(jnp.transpose(w_uk, (1, 2, 0)), ((0, 0), (0, HEAD_PAD - NOPE_DIM), (0, 0))).astype(BF16)
    esel = jnp.zeros((HEAD_PAD, ROPE_DIM), F32).at[ROPE_LO + jnp.arange(ROPE_DIM), jnp.arange(ROPE_DIM)].set(1.0).astype(BF16)
    wuvp = w_uv.reshape(KV_LORA, A_HEADS * V_DIM).astype(BF16)
    lb = jnp.cumsum(jax.nn.softmax(lb_hgrn.astype(F32), axis=0), axis=0)[0][None, :]
    gon = jnp.tile(g_onorm[0], 2)[None, :]
    wr = jnp.pad(w_rg[0], ((0, 0), (0, LANES - N_GROUPS))).astype(BF16)
    br = jnp.pad(b_rg[0], (0, LANES - N_GROUPS))[None, :]
    rest = LANES - EXPERTS_PER_GROUP - N_GROUPS
    wrg = jnp.concatenate([w_re[0], jnp.broadcast_to(w_rg[0], (N_GROUPS, d, N_GROUPS)),
                           jnp.zeros((N_GROUPS, d, rest), F32)], axis=2).astype(BF16)
    brg = jnp.concatenate([b_re[0], jnp.broadcast_to(b_rg[0], (N_GROUPS, N_GROUPS)),
                           jnp.zeros((N_GROUPS, rest), F32)], axis=1)[:, None, :]
    return dict(
        wrg=wrg, brg=brg,
        gmix=g_mix[0][None, :], win_a=win_a, win_b=win_b, gq=g_q[0][None, :], wuq=wuq,
        gkv=g_kv[0][None, :], wuk=wuk, wuvt=wuvt, wq2l=wq2l, esel=esel, wuvp=wuvp, lb=lb, gon=gon,
        wa=w_a_out[0].astype(BF16), wb=w_b_out[0].astype(BF16), wo=w_o[0].astype(BF16),
        gffn=g_ffn[0][None, :], wr=wr, br=br, wg=w_gate[0].astype(BF16), wu=w_up[0].astype(BF16),
        wd=w_down[0].astype(BF16), gfin=g_final[None, :])


def _tile(n, want):
    t = min(n, want)
    assert n % t == 0
    return t


def _run_group(x, pos, streams, past, w):
    b, l, d = x.shape
    ns, ls = streams
    tm = _tile(l, 512)
    cosp, sinlo, sinhi = _rope_tables(pos)
    h, ckv, kpe, q, k, vt = _mla_proj(x, w["gmix"], w["win_a"], w["gq"], w["wuq"], w["gkv"], w["wuk"],
                                      w["wuvt"], cosp, sinlo, sinhi, tm, past is None)
    qb, kb, lf, vb, og, sa, sb = _hgrn_proj(h, w["win_b"], w["lb"], tm)
    as_streams = lambda a: a.reshape(ns, ls, a.shape[-1])
    if past is None:
        oa = _attn(q, k, vt, _tile(l, 512), 256)
        s0t = jnp.zeros((ns, B_HEADS, B_DV, B_DK), F32)
    else:
        cache_ckv, cache_kpe, state = past
        oa = _sample_attn(as_streams(q), cache_ckv, cache_kpe, as_streams(ckv), as_streams(kpe),
                          w["wq2l"], w["esel"], w["wuvp"]).reshape(b, l, A_HEADS * V_DIM)
        s0t = jnp.swapaxes(state.astype(F32), -1, -2)
    ob, snt = _hgrn(qb, kb, lf, vb, og, s0t, w["gon"], ns, 512 if past is None else 8 * ls)
    x1, xs, slot, meta = _out_proj(oa, ob, sa, sb, x, w["wa"], w["wb"], w["wo"], w["gffn"], w["wr"],
                                   w["br"], tm, past is None)
    work = _work_list(meta, meta.shape[0] * _max_items(tm))
    ys = _moe_experts(xs, work, w["wrg"], w["brg"], w["wg"], w["wu"], w["wd"])
    y = _moe_combine(ys, slot, x1, w["gfin"], tm)
    return y, ckv, kpe, jnp.swapaxes(snt, -1, -2)


def kernel(x_prompt, x_sample, cache_ckv, cache_kpe, state_hgrn, g_mix, w_in, g_q, w_uq, g_kv, w_ukv,
           lb_hgrn, g_onorm, w_a_out, w_b_out, w_o, g_ffn, w_rg, b_rg, w_re, b_re, w_gate, w_up, w_down,
           g_final):
    w = _prep_weights(g_mix, w_in, g_q, w_uq, g_kv, w_ukv, lb_hgrn, g_onorm, w_a_out, w_b_out, w_o,
                      g_ffn, w_rg, b_rg, w_re, b_re, w_gate, w_up, w_down, g_final)
    bp, lp, d = x_prompt.shape
    y_p, ckv_p, kpe_p, st_p = _run_group(x_prompt, jnp.arange(lp, dtype=jnp.int32), (bp, lp), None, w)

    bs, ls, _ = x_sample.shape
    past_len = cache_ckv.shape[2]
    pos_s = past_len + (jnp.arange(bs * ls, dtype=jnp.int32) % ls)
    y_s, ckv_s, kpe_s, st_s = _run_group(x_sample.reshape(1, bs * ls, d), pos_s, (bs, ls),
                                         (cache_ckv[0], cache_kpe[0], state_hgrn[0]), w)
    return (y_p, y_s.reshape(bs, ls, d),
            ckv_p[None], kpe_p[None], st_p[None].astype(x_prompt.dtype),
            ckv_s.reshape(1, bs, ls, KV_LORA), kpe_s.reshape(1, bs, ls, ROPE_DIM),
            st_s[None].astype(state_hgrn.dtype))
```

```python
import functools

import numpy as np
import jax
import jax.numpy as jnp
from jax import lax
from jax.experimental import pallas as pl
from jax.experimental.pallas import tpu as pltpu

F32 = jnp.float32
BF16 = jnp.bfloat16

EPS = 1e-6
CHUNK = 64
A_HEADS = 8
Q_LORA = 384
KV_LORA = 256
NOPE_DIM = 64
ROPE_DIM = 32
V_DIM = 64
ROPE_THETA = 10000.0
B_HEADS = 8
B_DK = 128
B_DV = 64
N_GROUPS = 4
EXPERTS_PER_GROUP = 8
N_EXPERTS = N_GROUPS * EXPERTS_PER_GROUP
D_FF_EXPERT = 256

LANES = 128
VMEM_BYTES_V7X = 64 * 1024 * 1024
VMEM_LIMIT = VMEM_BYTES_V7X - 8 * 1024 * 1024

HEAD_PAD = LANES
ROPE_LO = NOPE_DIM
ROPE_HI = NOPE_DIM + ROPE_DIM
HALF = ROPE_DIM // 2
SCALE = (NOPE_DIM + ROPE_DIM) ** -0.5
LOG2E = 1.4426950408889634
NEG_INF = float("-inf")


def _params(sem):
    return pltpu.CompilerParams(dimension_semantics=sem, vmem_limit_bytes=VMEM_LIMIT)


def _const_spec(shape):
    nd = len(shape)
    return pl.BlockSpec(shape, lambda *_: (0,) * nd, pipeline_mode=pl.Buffered(1))


def _rms(x, g):
    ms = jnp.mean(x * x, axis=-1, keepdims=True)
    return x * lax.rsqrt(ms + EPS) * g


def _sig_pair(x):
    e = jnp.exp(-jnp.abs(x))
    r = 1.0 / (1.0 + e)
    er = e * r
    pos = x >= 0
    return jnp.where(pos, r, er), jnp.where(pos, er, r)


def _nt(a, b):
    return lax.dot_general(a, b, (((1,), (1,)), ((), ())), preferred_element_type=F32)


def _tn(a, b):
    return lax.dot_general(a, b, (((0,), (0,)), ((), ())), preferred_element_type=F32)


def _dot(a, b):
    return jnp.dot(a, b, preferred_element_type=F32)


def _rope(x, cosp, sinlo, sinhi):
    return x * cosp + pltpu.roll(x, LANES - HALF, 1) * sinlo + pltpu.roll(x, HALF, 1) * sinhi


def _mla_proj_kernel(x_ref, gmix_ref, win_ref, gq_ref, wuq_ref, gkv_ref, wuk_ref, wuvt_ref,
                     cos_ref, sinlo_ref, sinhi_ref,
                     h_ref, ckv_ref, kpe_ref, q_ref, k_ref, vt_ref, *, q_transposed):
    h = _rms(x_ref[...], gmix_ref[...]).astype(BF16)
    h_ref[...] = h
    z = _dot(h, win_ref[...])
    cosp, sinlo, sinhi = cos_ref[...], sinlo_ref[...], sinhi_ref[...]
    lane = lax.broadcasted_iota(jnp.int32, (1, LANES), 1)
    rope_lanes = (lane // ROPE_DIM) == (ROPE_LO // ROPE_DIM)

    cqn = _rms(z[:, :Q_LORA], gq_ref[...]).astype(BF16)
    q = _dot(cqn, wuq_ref[...])
    for hd in range(A_HEADS):
        sl = slice(hd * HEAD_PAD, (hd + 1) * HEAD_PAD)
        qh = q[:, sl]
        qh = jnp.where(rope_lanes, _rope(qh, cosp, sinlo, sinhi), qh) * (SCALE * LOG2E)
        if q_transposed:
            q_ref[sl, :] = qh.T.astype(BF16)
        else:
            q_ref[:, sl] = qh.astype(BF16)

    ckv = _rms(z[:, Q_LORA:Q_LORA + KV_LORA], gkv_ref[...])
    ckv_ref[...] = ckv
    ckv_bf = ckv.astype(BF16)
    kpe_rot = _rope(z[:, Q_LORA + KV_LORA:], cosp, sinlo, sinhi)
    kpe_ref[...] = kpe_rot[:, :ROPE_DIM]
    kpe_placed = jnp.where(rope_lanes, kpe_rot, 0.0)
    kn = _dot(ckv_bf, wuk_ref[...])
    for hd in range(A_HEADS):
        sl = slice(hd * HEAD_PAD, (hd + 1) * HEAD_PAD)
        k_ref[:, sl] = (kn[:, sl] + kpe_placed).astype(BF16)
    vt_ref[...] = _nt(wuvt_ref[...], ckv_bf).astype(BF16)


def _mla_proj(x, gmix, win_a, gq, wuq, gkv, wuk, wuvt, cosp, sinlo, sinhi, tm, q_transposed):
    b, l, d = x.shape
    hq = A_HEADS * HEAD_PAD
    hv = A_HEADS * V_DIM
    tok = lambda w: pl.BlockSpec((None, tm, w), lambda bi, i: (bi, i, 0))
    tok_t = lambda w: pl.BlockSpec((None, w, tm), lambda bi, i: (bi, 0, i))
    tab = pl.BlockSpec((tm, LANES), lambda bi, i: (i, 0))
    return pl.pallas_call(
        functools.partial(_mla_proj_kernel, q_transposed=q_transposed),
        grid=(b, l // tm),
        in_specs=[tok(d), _const_spec(gmix.shape), _const_spec(win_a.shape), _const_spec(gq.shape),
                  _const_spec(wuq.shape), _const_spec(gkv.shape), _const_spec(wuk.shape),
                  _const_spec(wuvt.shape), tab, tab, tab],
        out_specs=[tok(d), tok(KV_LORA), tok(ROPE_DIM), tok_t(hq) if q_transposed else tok(hq), tok(hq),
                   tok_t(hv)],
        out_shape=[jax.ShapeDtypeStruct((b, l, d), BF16),
                   jax.ShapeDtypeStruct((b, l, KV_LORA), F32),
                   jax.ShapeDtypeStruct((b, l, ROPE_DIM), F32),
                   jax.ShapeDtypeStruct((b, hq, l) if q_transposed else (b, l, hq), BF16),
                   jax.ShapeDtypeStruct((b, l, hq), BF16),
                   jax.ShapeDtypeStruct((b, hv, l), BF16)],
        compiler_params=_params(("parallel", "parallel")),
        name="mla_proj",
    )(x, gmix, win_a, gq, wuq, gkv, wuk, wuvt, cosp, sinlo, sinhi)


def _hgrn_proj_kernel(h_ref, w_ref, lb_ref, qb_ref, kb_ref, lf_ref, vb_ref, og_ref, sa_ref, sb_ref):
    h = h_ref[...]
    dk = B_HEADS * B_DK
    dv = B_HEADS * B_DV
    dm = sa_ref.shape[-1]
    nb = 256
    lb_all = lb_ref[...]

    def z(base, i):
        return _dot(h, w_ref[:, base + i * nb:base + (i + 1) * nb])

    for i in range(dk // nb):
        cols = slice(i * nb, (i + 1) * nb)
        zq = z(0, i)
        qb_ref[:, cols] = zq * jax.nn.sigmoid(zq)
        sf, snf = _sig_pair(z(dk, i))
        lb = lb_all[:, cols]
        lf_ref[:, cols] = jnp.log(lb + (1.0 - lb) * sf)
        kb_ref[:, cols] = (1.0 - lb) * snf
    for i in range(dv // nb):
        cols = slice(i * nb, (i + 1) * nb)
        vb_ref[:, cols] = z(2 * dk, i).astype(BF16)
        zg = z(2 * dk + dv, i)
        og_ref[:, cols] = (zg * jax.nn.sigmoid(zg)).astype(BF16)
    for i in range(dm // nb):
        cols = slice(i * nb, (i + 1) * nb)
        sa_ref[:, cols] = jax.nn.sigmoid(z(2 * dk + 2 * dv, i)).astype(BF16)
        sb_ref[:, cols] = jax.nn.sigmoid(z(2 * dk + 2 * dv + dm, i)).astype(BF16)


def _hgrn_proj(h, win_b, lb, tm):
    b, l, d = h.shape
    dk = B_HEADS * B_DK
    dv = B_HEADS * B_DV
    tok = lambda w: pl.BlockSpec((None, tm, w), lambda bi, i: (bi, i, 0))
    sds = lambda w, dt: jax.ShapeDtypeStruct((b, l, w), dt)
    return pl.pallas_call(
        _hgrn_proj_kernel,
        grid=(b, l // tm),
        in_specs=[tok(d), _const_spec(win_b.shape), _const_spec(lb.shape)],
        out_specs=[tok(dk), tok(dk), tok(dk), tok(dv), tok(dv), tok(d), tok(d)],
        out_shape=[sds(dk, F32), sds(dk, F32), sds(dk, F32), sds(dv, BF16), sds(dv, BF16),
                   sds(d, BF16), sds(d, BF16)],
        compiler_params=_params(("parallel", "parallel")),
        name="hgrn_proj",
    )(h, win_b, lb)


def _attn_kernel(qt_ref, k_ref, vt_ref, o_ref, s_sc, m_sc, l_sc, acc_sc, *, t, tc, hp):
    qi = pl.program_id(2)
    n_chains = t // tc
    m_sc[...] = jnp.full(m_sc.shape, NEG_INF, F32)
    l_sc[...] = jnp.zeros(l_sc.shape, F32)
    acc_sc[...] = jnp.zeros(acc_sc.shape, F32)
    qk_rows = [slice(hd * HEAD_PAD, (hd + 1) * HEAD_PAD) for hd in range(hp)]
    v_rows = [slice(hd * V_DIM, (hd + 1) * V_DIM) for hd in range(hp)]

    def scores(kb, slot):
        start = pl.multiple_of(kb * t, t)
        for hd in range(hp):
            k = k_ref[pl.ds(start, t), qk_rows[hd]]
            for ch in range(n_chains):
                cols = slice(ch * tc, (ch + 1) * tc)
                s_sc[hd, slot, :, cols] = _dot(k, qt_ref[qk_rows[hd], cols])

    def consume(kb, slot, masked):
        start = pl.multiple_of(kb * t, t)
        for hd in range(hp):
            vt = vt_ref[v_rows[hd], pl.ds(start, t)]
            for ch in range(n_chains):
                cols = slice(ch * tc, (ch + 1) * tc)
                s = s_sc[hd, slot, :, cols]
                if masked:
                    r = lax.broadcasted_iota(jnp.int32, (t, tc), 0) // CHUNK
                    c = (lax.broadcasted_iota(jnp.int32, (t, tc), 1) + ch * tc) // CHUNK
                    s = jnp.where(r <= c, s, NEG_INF)
                m_prev = m_sc[hd, :, cols]
                m_new = jnp.maximum(m_prev, jnp.max(s, axis=0, keepdims=True))
                alpha = jnp.exp2(m_prev - m_new)
                p = jnp.exp2(s - m_new)
                l_sc[hd, :, cols] = alpha * l_sc[hd, :, cols] + jnp.sum(p, axis=0, keepdims=True)
                acc_sc[hd, :, cols] = alpha * acc_sc[hd, :, cols] + _dot(vt, p.astype(BF16))
                m_sc[hd, :, cols] = m_new

    scores(0, 0)

    def body(j, carry):
        kb = 2 * j
        scores(kb + 1, 1)
        consume(kb, 0, False)
        scores(kb + 2, 0)
        consume(kb + 1, 1, False)
        return carry

    lax.fori_loop(0, qi // 2, body, 0)

    @pl.when(qi % 2 == 0)
    def _():
        consume(qi, 0, True)

    @pl.when(qi % 2 == 1)
    def _():
        scores(qi, 1)
        consume(qi - 1, 0, False)
        consume(qi, 1, True)

    for hd in range(hp):
        o_ref[v_rows[hd], :] = (acc_sc[hd] / l_sc[hd]).astype(BF16)


def _attn(qt, k, vt, t, tc, hp):
    b, l, _ = k.shape
    return pl.pallas_call(
        functools.partial(_attn_kernel, t=t, tc=min(tc, t), hp=hp),
        grid=(b, A_HEADS // hp, l // t),
        in_specs=[pl.BlockSpec((None, hp * HEAD_PAD, t), lambda bi, h, i: (bi, h, i)),
                  pl.BlockSpec((None, l, hp * HEAD_PAD), lambda bi, h, i: (bi, 0, h)),
                  pl.BlockSpec((None, hp * V_DIM, l), lambda bi, h, i: (bi, h, 0))],
        out_specs=pl.BlockSpec((None, hp * V_DIM, t), lambda bi, h, i: (bi, h, i)),
        out_shape=jax.ShapeDtypeStruct((b, A_HEADS * V_DIM, l), BF16),
        scratch_shapes=[pltpu.VMEM((hp, 2, t, t), F32), pltpu.VMEM((hp, 1, t), F32),
                        pltpu.VMEM((hp, 1, t), F32), pltpu.VMEM((hp, V_DIM, t), F32)],
        compiler_params=_params(("parallel", "parallel", "arbitrary")),
        name="attn",
    )(qt, k, vt)


def _sample_attn_kernel(q_ref, cc_ref, ck_ref, nc_ref, nk_ref, wq2l_ref, esel_ref, wuv_ref, o_ref,
                        *, past, n_new):
    qs = q_ref[...]
    heads = [qs[:, hd * HEAD_PAD:(hd + 1) * HEAD_PAD] for hd in range(A_HEADS)]
    ql = jnp.concatenate([_dot(heads[hd], wq2l_ref[hd]) for hd in range(A_HEADS)], axis=0).astype(BF16)
    qp = jnp.concatenate([_dot(heads[hd], esel_ref[...]) for hd in range(A_HEADS)], axis=0).astype(BF16)
    cc = cc_ref[...].astype(BF16)
    ck = ck_ref[...].astype(BF16)
    nc = nc_ref[...].astype(BF16)
    nk = nk_ref[...].astype(BF16)
    s_c = _nt(ql, cc) + _nt(qp, ck)
    s_n = _nt(ql, nc) + _nt(qp, nk)
    rows = A_HEADS * n_new
    tq = lax.broadcasted_iota(jnp.int32, (rows, n_new), 0) % n_new
    tk = lax.broadcasted_iota(jnp.int32, (rows, n_new), 1)
    s_n = jnp.where((past + tk) // CHUNK <= (past + tq) // CHUNK, s_n, NEG_INF)
    m = jnp.maximum(jnp.max(s_c, axis=-1, keepdims=True), jnp.max(s_n, axis=-1, keepdims=True))
    p_c = jnp.exp2(s_c - m)
    p_n = jnp.exp2(s_n - m)
    denom = jnp.sum(p_c, axis=-1, keepdims=True) + jnp.sum(p_n, axis=-1, keepdims=True)
    o_lat = ((_dot(p_c.astype(BF16), cc) + _dot(p_n.astype(BF16), nc)) / denom).astype(BF16)
    full = _dot(o_lat, wuv_ref[...])
    col_head = lax.broadcasted_iota(jnp.int32, (n_new, A_HEADS * V_DIM), 1) // V_DIM
    out = jnp.zeros((n_new, A_HEADS * V_DIM), F32)
    for hd in range(A_HEADS):
        out = out + jnp.where(col_head == hd, full[hd * n_new:(hd + 1) * n_new], 0.0)
    o_ref[...] = out.astype(BF16)


def _sample_attn(q, cache_ckv, cache_kpe, ckv_new, kpe_new, wq2l, esel, wuvp):
    nb, n_new, hq = q.shape
    past = cache_ckv.shape[1]
    hv = A_HEADS * V_DIM
    per = lambda r, w: pl.BlockSpec((None, r, w), lambda bi: (bi, 0, 0))
    return pl.pallas_call(
        functools.partial(_sample_attn_kernel, past=past, n_new=n_new),
        grid=(nb,),
        in_specs=[per(n_new, hq), per(past, KV_LORA), per(past, ROPE_DIM), per(n_new, KV_LORA),
                  per(n_new, ROPE_DIM), _const_spec(wq2l.shape), _const_spec(esel.shape),
                  _const_spec(wuvp.shape)],
        out_specs=per(n_new, hv),
        out_shape=jax.ShapeDtypeStruct((nb, n_new, hv), BF16),
        compiler_params=_params(("parallel",)),
        name="sample_attn",
    )(q, cache_ckv, cache_kpe, ckv_new, kpe_new, wq2l, esel, wuvp)


def _level_sizes(c):
    sizes = []
    p = c
    while p >= 2:
        sizes.append(p)
        p //= 2
    return sizes


def _level_masks(c):
    t = np.arange(c)[:, None]
    s = np.arange(c)[None, :]
    out = []
    for p in _level_sizes(c):
        out.append((t // p == s // p) & (t % p >= p // 2) & (s % p < p // 2))
    out.append(t == s)
    return np.stack(out).astype(np.float32)


def _boundary(cum, p, c):
    half = p // 2
    if p >= 8:
        parts = [jnp.broadcast_to(cum[i * p + half - 1:i * p + half, :], (p, cum.shape[1]))
                 for i in range(c // p)]
        return jnp.concatenate(parts, axis=0) if len(parts) > 1 else parts[0]
    sub = lax.broadcasted_iota(jnp.int32, (8, cum.shape[1]), 0)
    parts = []
    for g in range(c // 8):
        lo = jnp.broadcast_to(cum[g * 8 + 1:g * 8 + 2, :], (8, cum.shape[1]))
        hi = jnp.broadcast_to(cum[g * 8 + 5:g * 8 + 6, :], (8, cum.shape[1]))
        parts.append(jnp.where(sub < 4, lo, hi))
    return jnp.concatenate(parts, axis=0)


def _hgrn_kernel(q_ref, k_ref, lf_ref, v_ref, og_ref, s0t_ref, gon_ref, mask_ref, tri_ref,
                 o_ref, snt_ref, st_sc, *, c, n_chunks, per_chunk_state):
    step = pl.program_id(2)
    sizes = _level_sizes(c)
    lane = lax.broadcasted_iota(jnp.int32, (1, LANES), 1)
    low = lane < B_DV
    keep_lo = jnp.where(low, 1.0, 0.0).astype(BF16)
    keep_hi = jnp.where(low, 0.0, 1.0).astype(BF16)

    if not per_chunk_state:
        @pl.when(step == 0)
        def _():
            st_sc[...] = jnp.zeros(st_sc.shape, F32)
            st_sc[0, 0:B_DV, :] = s0t_ref[0]
            st_sc[1, B_DV:2 * B_DV, :] = s0t_ref[1]

    tri = tri_ref[...]
    rows = [slice(ci * c, (ci + 1) * c) for ci in range(n_chunks)]
    heads = [slice(j * B_DK, (j + 1) * B_DK) for j in range(2)]
    items = [(ci, j) for ci in range(n_chunks) for j in range(2)]


    cums = []
    for ci in range(n_chunks):
        lf = lf_ref[rows[ci], :]
        hi = lf.astype(BF16)
        r1 = lf - hi.astype(F32)
        mid = r1.astype(BF16)
        lo = (r1 - mid.astype(F32)).astype(BF16)
        cums.append((_dot(tri, hi) + _dot(tri, mid) + _dot(tri, lo)) * LOG2E)

    def zeros(n):
        return jnp.zeros((n, B_DK), F32)

    a_mats = {}
    for ci, j in items:
        q = q_ref[rows[ci], heads[j]]
        k = k_ref[rows[ci], heads[j]]
        cum = cums[ci][:, heads[j]]
        kbf = k.astype(BF16)
        a = _nt(q.astype(BF16), kbf) * mask_ref[len(sizes)]
        q_cols, k_cols = [], []
        for li, p in enumerate(sizes):
            half = p // 2
            if half % 8 == 0:
                for lo in range(0, c, p):
                    mid, hi = lo + half, lo + p
                    bnd = cum[mid - 1:mid, :]
                    qu = q[mid:hi] * jnp.exp2(cum[mid:hi] - bnd)
                    kl = k[lo:mid] * jnp.exp2(bnd - cum[lo:mid])
                    q_cols.append(jnp.concatenate([x for x in (zeros(mid), qu, zeros(c - hi)) if x.shape[0]], axis=0))
                    k_cols.append(jnp.concatenate([x for x in (zeros(lo), kl, zeros(c - mid)) if x.shape[0]], axis=0))
                continue
            if p == 2:
                qt = (q * jnp.exp2(lf_ref[rows[ci], heads[j]] * LOG2E)).astype(BF16)
                kt = kbf
            else:
                bnd = _boundary(cum, p, c)
                qt = (q * jnp.exp2(jnp.minimum(cum - bnd, 0.0))).astype(BF16)
                kt = (k * jnp.exp2(jnp.minimum(bnd - cum, 0.0))).astype(BF16)
            a = a + _nt(qt, kt) * mask_ref[li]
        if q_cols:
            a = a + _nt(jnp.concatenate(q_cols, axis=1).astype(BF16), jnp.concatenate(k_cols, axis=1).astype(BF16))
        a_mats[ci, j] = a.astype(BF16)

    qhats, kvs, decays, vs = {}, {}, {}, {}
    for ci, j in items:
        cum = cums[ci][:, heads[j]]
        last = cum[c - 1:c, :]
        vs[ci, j] = v_ref[rows[ci], :] * (keep_lo if j == 0 else keep_hi)
        qhats[ci, j] = (q_ref[rows[ci], heads[j]] * jnp.exp2(cum)).astype(BF16)
        khat = (k_ref[rows[ci], heads[j]] * jnp.exp2(last - cum)).astype(BF16)
        kvs[ci, j] = _tn(vs[ci, j], khat)
        decays[ci, j] = jnp.exp2(last)

    pad = jnp.zeros((B_DV, B_DK), F32)
    if not per_chunk_state:
        st = [st_sc[0], st_sc[1]]
    for ci in range(n_chunks):
        if per_chunk_state:
            st = [jnp.concatenate([s0t_ref[ci, 0], pad], axis=0), jnp.concatenate([pad, s0t_ref[ci, 1]], axis=0)]
        o_pair = jnp.zeros((c, LANES), F32)
        for j in range(2):
            o_pair = o_pair + _dot(a_mats[ci, j], vs[ci, j]) + _nt(qhats[ci, j], st[j].astype(BF16))
            st[j] = st[j] * decays[ci, j] + kvs[ci, j]
        if per_chunk_state:
            snt_ref[ci, 0] = st[0][0:B_DV, :]
            snt_ref[ci, 1] = st[1][B_DV:2 * B_DV, :]
        o2 = o_pair * o_pair
        s_lo = jnp.sum(jnp.where(low, o2, 0.0), axis=-1, keepdims=True)
        s_hi = jnp.sum(jnp.where(low, 0.0, o2), axis=-1, keepdims=True)
        ms = jnp.where(low, s_lo, s_hi) * (1.0 / B_DV)
        ob = o_pair * lax.rsqrt(ms + EPS) * gon_ref[...] * og_ref[rows[ci], :].astype(F32)
        o_ref[rows[ci], :] = ob.astype(BF16)
    if not per_chunk_state:
        st_sc[0] = st[0]
        st_sc[1] = st[1]

        @pl.when(step == pl.num_programs(2) - 1)
        def _():
            snt_ref[0] = st_sc[0, 0:B_DV, :]
            snt_ref[1] = st_sc[1, B_DV:2 * B_DV, :]


def _hgrn(qb, kb, lf, vb, og, s0t, gon, n_streams, tc):
    b, l, dk = qb.shape
    per_chunk_state = n_streams != b
    if per_chunk_state:
        assert b == 1 and l % n_streams == 0 and l // n_streams <= CHUNK
        c = l // n_streams
    else:
        c = min(CHUNK, l)
    tc = min(tc, l)
    n_chunks = tc // c
    masks = jnp.asarray(_level_masks(c))
    tri = jnp.asarray(np.tril(np.ones((c, c), np.float32)), dtype=BF16)
    pairs = B_HEADS // 2
    wide = pl.BlockSpec((None, tc, 2 * B_DK), lambda bi, p, i: (bi, i, p))
    narrow = pl.BlockSpec((None, tc, 2 * B_DV), lambda bi, p, i: (bi, i, p))
    if per_chunk_state:
        state = pl.BlockSpec((n_chunks, 2, B_DV, B_DK), lambda bi, p, i: (i, p, 0, 0))
    else:
        state = pl.BlockSpec((None, 2, B_DV, B_DK), lambda bi, p, i: (bi, p, 0, 0))
    return pl.pallas_call(
        functools.partial(_hgrn_kernel, c=c, n_chunks=n_chunks, per_chunk_state=per_chunk_state),
        grid=(b, pairs, l // tc),
        in_specs=[wide, wide, wide, narrow, narrow, state, _const_spec(gon.shape),
                  _const_spec(masks.shape), _const_spec(tri.shape)],
        out_specs=[narrow, state],
        out_shape=[jax.ShapeDtypeStruct((b, l, B_HEADS * B_DV), BF16),
                   jax.ShapeDtypeStruct((n_streams, B_HEADS, B_DV, B_DK), F32)],
        scratch_shapes=[pltpu.VMEM((2, 2 * B_DV, B_DK), F32)],
        compiler_params=_params(("parallel", "parallel", "parallel" if per_chunk_state else "arbitrary")),
        name="hgrn",
    )(qb, kb, lf, vb, og, s0t, gon, masks, tri)


MOE_SUB = 144
MOE_ROWS = 1024
_BIG_LANE = float(1 << 20)


def _max_items(tm):
    return (tm + N_GROUPS * (MOE_SUB - 1)) // MOE_SUB


def _first_lane_of_max(x, lane):
    v = jnp.max(x, axis=-1, keepdims=True)
    return v, jnp.min(jnp.where(x == v, lane, _BIG_LANE), axis=-1, keepdims=True)


def _out_proj_kernel(oa_ref, ob_ref, sa_ref, sb_ref, x_ref, wa_ref, wb_ref, wo_ref, gffn_ref,
                     wr_ref, br_ref, tri_ref, upper_ref, x1_ref, xs_ref, slot_ref, meta_ref, *, oa_transposed):
    tm = x_ref.shape[0]
    oa, ob = oa_ref[...], ob_ref[...]
    nb = 256
    m = []
    for i in range(x_ref.shape[1] // nb):
        cols = slice(i * nb, (i + 1) * nb)
        ya = _tn(oa, wa_ref[:, cols]) if oa_transposed else _dot(oa, wa_ref[:, cols])
        yb = _dot(ob, wb_ref[:, cols])
        m.append((sa_ref[:, cols].astype(F32) * ya + sb_ref[:, cols].astype(F32) * yb).astype(BF16))
    x1 = x_ref[...] + _dot(jnp.concatenate(m, axis=-1), wo_ref[...])
    x1_ref[...] = x1
    h2 = _rms(x1, gffn_ref[...]).astype(BF16)

    logits = _dot(h2, wr_ref[...]) + br_ref[...]
    lane = lax.broadcasted_iota(jnp.int32, (tm, LANES), 1).astype(F32)
    _, grp = _first_lane_of_max(jnp.where(lane < N_GROUPS, logits, NEG_INF), lane)
    onehot = jnp.where(lane == grp, 1.0, 0.0)
    rank = _dot(tri_ref[...], onehot.astype(BF16))
    count = rank[tm - 1:tm, :]
    items = jnp.zeros_like(count)
    for j in range(_max_items(tm)):
        items = items + jnp.where(count > j * MOE_SUB, 1.0, 0.0)
    base = _dot(jnp.broadcast_to(items, (8, LANES)).astype(BF16), upper_ref[...])[0:1, :] * MOE_SUB
    slot = jnp.sum(onehot * (base + rank - 1.0), axis=-1, keepdims=True)
    col = lax.broadcasted_iota(jnp.int32, (tm, MOE_ROWS), 1).astype(F32)
    place = jnp.where(col == slot, 1.0, 0.0).astype(BF16)
    xs_ref[...] = _tn(place, h2).astype(BF16)
    slot_ref[...] = jnp.broadcast_to(slot, (tm, LANES))
    meta_ref[...] = jnp.broadcast_to(items, (8, LANES))


def _out_proj(oa, ob, sa, sb, x, wa, wb, wo, gffn, wr, br, tm, oa_transposed):
    b, l, d = x.shape
    nt = l // tm
    assert _max_items(tm) * MOE_SUB <= MOE_ROWS
    hv = A_HEADS * V_DIM
    tri = jnp.asarray(np.tril(np.ones((tm, tm), np.float32)), dtype=BF16)
    upper = jnp.asarray(np.triu(np.ones((LANES, LANES), np.float32), 1), dtype=BF16)
    tok = lambda w: pl.BlockSpec((None, tm, w), lambda bi, i: (bi, i, 0))
    per_tile = lambda r, w: pl.BlockSpec((None, r, w), lambda bi, i: (bi * nt + i, 0, 0))
    oa_spec = pl.BlockSpec((None, hv, tm), lambda bi, i: (bi, 0, i)) if oa_transposed else tok(hv)
    return pl.pallas_call(
        functools.partial(_out_proj_kernel, oa_transposed=oa_transposed),
        grid=(b, nt),
        in_specs=[oa_spec, tok(B_HEADS * B_DV), tok(d), tok(d), tok(d), _const_spec(wa.shape),
                  _const_spec(wb.shape), _const_spec(wo.shape), _const_spec(gffn.shape),
                  _const_spec(wr.shape), _const_spec(br.shape), _const_spec(tri.shape),
                  _const_spec(upper.shape)],
        out_specs=[tok(d), per_tile(MOE_ROWS, d), tok(LANES), per_tile(8, LANES)],
        out_shape=[jax.ShapeDtypeStruct((b, l, d), F32),
                   jax.ShapeDtypeStruct((b * nt, MOE_ROWS, d), BF16),
                   jax.ShapeDtypeStruct((b, l, LANES), F32),
                   jax.ShapeDtypeStruct((b * nt, 8, LANES), F32)],
        compiler_params=_params(("parallel", "parallel")),
        name="out_proj",
    )(oa, ob, sa, sb, x, wa, wb, wo, gffn, wr, br, tri, upper)


def _work_list(meta, n_items):
    n = meta[:, 0, :N_GROUPS].astype(jnp.int32)
    tiles = n.shape[0]
    first = jnp.cumsum(n, axis=1) - n
    per_seg = n.T.reshape(-1)
    ends = jnp.cumsum(per_seg)
    total = ends[-1]
    i = jnp.minimum(jnp.arange(n_items, dtype=jnp.int32), total - 1)
    seg = jnp.sum((ends[None, :] <= i[:, None]).astype(jnp.int32), axis=1)
    grp, tile = seg // tiles, seg % tiles
    block = first[tile, grp] + i - (ends[seg] - per_seg[seg])
    return tile, block.astype(jnp.int32), grp, total[None].astype(jnp.int32)


def _moe_expert_kernel(tile_ref, block_ref, grp_ref, total_ref, xs_ref, wr_ref, br_ref, wg_ref, wu_ref,
                       wd_ref, ys_ref):
    i = pl.program_id(0)

    @pl.when(i < total_ref[0])
    def _():
        g = grp_ref[i]
        x = xs_ref[...]
        logits = _dot(x, wr_ref[...]) + br_ref[...]
        lane_i = lax.broadcasted_iota(jnp.int32, logits.shape, 1)
        lane = lane_i.astype(F32)
        is_grp = jnp.abs(lane - (EXPERTS_PER_GROUP + 0.5 * (N_GROUPS - 1))) < 0.5 * N_GROUPS
        lg = jnp.where(is_grp, logits, NEG_INF)
        gmax = jnp.max(lg, axis=-1, keepdims=True)
        denom = jnp.sum(jnp.exp(lg - gmax), axis=-1, keepdims=True)
        lg_own = jnp.sum(jnp.where(lane_i == EXPERTS_PER_GROUP + g, logits, 0.0), axis=-1, keepdims=True)
        p_grp = jnp.exp(lg_own - gmax) / denom
        le = jnp.where(lane < EXPERTS_PER_GROUP, logits, NEG_INF)
        v1, i1 = _first_lane_of_max(le, lane)
        le2 = jnp.where(lane == i1, NEG_INF, le)
        v2, i2 = _first_lane_of_max(le2, lane)
        e2 = jnp.exp(v2 - v1)
        w1 = p_grp / (1.0 + e2)
        comb = jnp.where(lane == i1, w1, 0.0) + jnp.where(lane == i2, w1 * e2, 0.0)
        hid = []
        for e in range(EXPERTS_PER_GROUP):
            a = _dot(x, wg_ref[e])
            u = _dot(x, wu_ref[e])
            s, _ = _sig_pair(a)
            hid.append((a * s * u * comb[:, e:e + 1]).astype(BF16))
        hid = jnp.concatenate(hid, axis=-1)
        wd = wd_ref[...].reshape(EXPERTS_PER_GROUP * D_FF_EXPERT, wd_ref.shape[-1])
        ys_ref[...] = _dot(hid, wd).astype(BF16)


def _moe_experts(xs, work, wrg, brg, wg, wu, wd):
    tiles, _, d = xs.shape
    tile, block, grp, total = work
    item = lambda i, t, k, g, n: (t[i], k[i], 0)
    by_group = lambda i, t, k, g, n: (g[i], 0, 0)
    grid_spec = pltpu.PrefetchScalarGridSpec(
        num_scalar_prefetch=4,
        grid=(tile.shape[0],),
        in_specs=[pl.BlockSpec((None, MOE_SUB, d), item),
                  pl.BlockSpec((None, d, LANES), by_group),
                  pl.BlockSpec((None, 1, LANES), by_group),
                  pl.BlockSpec((EXPERTS_PER_GROUP, d, D_FF_EXPERT), by_group),
                  pl.BlockSpec((EXPERTS_PER_GROUP, d, D_FF_EXPERT), by_group),
                  pl.BlockSpec((EXPERTS_PER_GROUP, D_FF_EXPERT, d), by_group)],
        out_specs=pl.BlockSpec((None, MOE_SUB, d), item))
    return pl.pallas_call(
        _moe_expert_kernel,
        grid_spec=grid_spec,
        out_shape=jax.ShapeDtypeStruct(xs.shape, BF16),
        input_output_aliases={4: 0},
        compiler_params=_params(("arbitrary",)),
        name="moe_experts",
    )(tile, block, grp, total, xs, wrg, brg, wg, wu, wd)


def _moe_combine_kernel(ys_ref, slot_ref, x1_ref, gfin_ref, y_ref):
    tm = x1_ref.shape[0]
    col = lax.broadcasted_iota(jnp.int32, (tm, MOE_ROWS), 1).astype(F32)
    slot = jnp.concatenate([slot_ref[...]] * (MOE_ROWS // LANES), axis=-1)
    place = jnp.where(col == slot, 1.0, 0.0).astype(BF16)
    y_ref[...] = _rms(x1_ref[...] + _dot(place, ys_ref[...]), gfin_ref[...])


def _moe_combine(ys, slot, x1, gfin, tm):
    b, l, d = x1.shape
    nt = l // tm
    tok = lambda w: pl.BlockSpec((None, tm, w), lambda bi, i: (bi, i, 0))
    per_tile = lambda r, w: pl.BlockSpec((None, r, w), lambda bi, i: (bi * nt + i, 0, 0))
    return pl.pallas_call(
        _moe_combine_kernel,
        grid=(b, nt),
        in_specs=[per_tile(MOE_ROWS, d), tok(LANES), tok(d), _const_spec(gfin.shape)],
        out_specs=tok(d),
        out_shape=jax.ShapeDtypeStruct((b, l, d), F32),
        compiler_params=_params(("parallel", "parallel")),
        name="moe_combine",
    )(ys, slot, x1, gfin)


def _rope_tables(pos):
    inv = jnp.power(ROPE_THETA, -jnp.arange(HALF, dtype=F32) / HALF)
    ang = pos.astype(F32)[:, None] * inv[None, :]
    cos, sin = jnp.cos(ang), jnp.sin(ang)
    zeros = jnp.zeros_like(sin)
    reps = LANES // ROPE_DIM
    cosp = jnp.tile(jnp.concatenate([cos, cos], axis=1), (1, reps))
    sinlo = jnp.tile(jnp.concatenate([-sin, zeros], axis=1), (1, reps))
    sinhi = jnp.tile(jnp.concatenate([zeros, sin], axis=1), (1, reps))
    return cosp, sinlo, sinhi


def _prep_weights(g_mix, w_in, g_q, w_uq, g_kv, w_ukv, lb_hgrn, g_onorm, w_a_out, w_b_out, w_o,
                  g_ffn, w_rg, b_rg, w_re, b_re, w_gate, w_up, w_down, g_final):
    assert w_in.shape[0] == 1, "single-layer step"
    d = w_in.shape[1]
    w = w_in[0]
    n_a = Q_LORA + KV_LORA
    win_a = jnp.concatenate([w[:, :n_a]] + [w[:, n_a:n_a + ROPE_DIM]] * (LANES // ROPE_DIM), axis=1).astype(BF16)
    win_b = w[:, n_a + ROPE_DIM:].astype(BF16)
    per_q = NOPE_DIM + ROPE_DIM
    wuq = jnp.pad(w_uq[0].reshape(Q_LORA, A_HEADS, per_q), ((0, 0), (0, 0), (0, HEAD_PAD - per_q)))
    wuq = wuq.reshape(Q_LORA, A_HEADS * HEAD_PAD).astype(BF16)
    wukv = w_ukv[0].reshape(KV_LORA, A_HEADS, NOPE_DIM + V_DIM)
    w_uk, w_uv = wukv[..., :NOPE_DIM], wukv[..., NOPE_DIM:]
    wuk = jnp.pad(w_uk, ((0, 0), (0, 0), (0, HEAD_PAD - NOPE_DIM))).reshape(KV_LORA, A_HEADS * HEAD_PAD).astype(BF16)
    wuvt = w_uv.reshape(KV_LORA, A_HEADS * V_DIM).T.astype(BF16)
    wq2l = jnp.pad(jnp.transpose(w_uk, (1, 2, 0)), ((0, 0), (0, HEAD_PAD - NOPE_DIM), (0, 0))).astype(BF16)
    esel = jnp.zeros((HEAD_PAD, ROPE_DIM), F32).at[ROPE_LO + jnp.arange(ROPE_DIM), jnp.arange(ROPE_DIM)].set(1.0).astype(BF16)
    wuvp = w_uv.reshape(KV_LORA, A_HEADS * V_DIM).astype(BF16)
    lb = jnp.cumsum(jax.nn.softmax(lb_hgrn.astype(F32), axis=0), axis=0)[0][None, :]
    gon = jnp.tile(g_onorm[0], 2)[None, :]
    wr = jnp.pad(w_rg[0], ((0, 0), (0, LANES - N_GROUPS))).astype(BF16)
    br = jnp.pad(b_rg[0], (0, LANES - N_GROUPS))[None, :]
    rest = LANES - EXPERTS_PER_GROUP - N_GROUPS
    wrg = jnp.concatenate([w_re[0], jnp.broadcast_to(w_rg[0], (N_GROUPS, d, N_GROUPS)),
                           jnp.zeros((N_GROUPS, d, rest), F32)], axis=2).astype(BF16)
    brg = jnp.concatenate([b_re[0], jnp.broadcast_to(b_rg[0], (N_GROUPS, N_GROUPS)),
                           jnp.zeros((N_GROUPS, rest), F32)], axis=1)[:, None, :]
    return dict(
        wrg=wrg, brg=brg,
        gmix=g_mix[0][None, :], win_a=win_a, win_b=win_b, gq=g_q[0][None, :], wuq=wuq,
        gkv=g_kv[0][None, :], wuk=wuk, wuvt=wuvt, wq2l=wq2l, esel=esel, wuvp=wuvp, lb=lb, gon=gon,
        wa=w_a_out[0].astype(BF16), wb=w_b_out[0].astype(BF16), wo=w_o[0].astype(BF16),
        gffn=g_ffn[0][None, :], wr=wr, br=br, wg=w_gate[0].astype(BF16), wu=w_up[0].astype(BF16),
        wd=w_down[0].astype(BF16), gfin=g_final[None, :])


def _tile(n, want):
    t = min(n, want)
    assert n % t == 0
    return t


def _run_group(x, pos, streams, past, w):
    b, l, d = x.shape
    ns, ls = streams
    tm = _tile(l, 512)
    cosp, sinlo, sinhi = _rope_tables(pos)
    h, ckv, kpe, q, k, vt = _mla_proj(x, w["gmix"], w["win_a"], w["gq"], w["wuq"], w["gkv"], w["wuk"],
                                      w["wuvt"], cosp, sinlo, sinhi, tm, past is None)
    qb, kb, lf, vb, og, sa, sb = _hgrn_proj(h, w["win_b"], w["lb"], tm)
    as_streams = lambda a: a.reshape(ns, ls, a.shape[-1])
    if past is None:
        oa = _attn(q, k, vt, _tile(l, 512), 256, 4)
        s0t = jnp.zeros((ns, B_HEADS, B_DV, B_DK), F32)
    else:
        cache_ckv, cache_kpe, state = past
        oa = _sample_attn(as_streams(q), cache_ckv, cache_kpe, as_streams(ckv), as_streams(kpe),
                          w["wq2l"], w["esel"], w["wuvp"]).reshape(b, l, A_HEADS * V_DIM)
        s0t = jnp.swapaxes(state.astype(F32), -1, -2)
    ob, snt = _hgrn(qb, kb, lf, vb, og, s0t, w["gon"], ns, 512 if past is None else 8 * ls)
    x1, xs, slot, meta = _out_proj(oa, ob, sa, sb, x, w["wa"], w["wb"], w["wo"], w["gffn"], w["wr"],
                                   w["br"], tm, past is None)
    work = _work_list(meta, meta.shape[0] * _max_items(tm))
    ys = _moe_experts(xs, work, w["wrg"], w["brg"], w["wg"], w["wu"], w["wd"])
    y = _moe_combine(ys, slot, x1, w["gfin"], tm)
    return y, ckv, kpe, jnp.swapaxes(snt, -1, -2)


def kernel(x_prompt, x_sample, cache_ckv, cache_kpe, state_hgrn, g_mix, w_in, g_q, w_uq, g_kv, w_ukv,
           lb_hgrn, g_onorm, w_a_out, w_b_out, w_o, g_ffn, w_rg, b_rg, w_re, b_re, w_gate, w_up, w_down,
           g_final):
    w = _prep_weights(g_mix, w_in, g_q, w_uq, g_kv, w_ukv, lb_hgrn, g_onorm, w_a_out, w_b_out, w_o,
                      g_ffn, w_rg, b_rg, w_re, b_re, w_gate, w_up, w_down, g_final)
    bp, lp, d = x_prompt.shape
    y_p, ckv_p, kpe_p, st_p = _run_group(x_prompt, jnp.arange(lp, dtype=jnp.int32), (bp, lp), None, w)

    bs, ls, _ = x_sample.shape
    past_len = cache_ckv.shape[2]
    pos_s = past_len + (jnp.arange(bs * ls, dtype=jnp.int32) % ls)
    y_s, ckv_s, kpe_s, st_s = _run_group(x_sample.reshape(1, bs * ls, d), pos_s, (bs, ls),
                                         (cache_ckv[0], cache_kpe[0], state_hgrn[0]), w)
    return (y_p, y_s.reshape(bs, ls, d),
            ckv_p[None], kpe_p[None], st_p[None].astype(x_prompt.dtype),
            ckv_s.reshape(1, bs, ls, KV_LORA), kpe_s.reshape(1, bs, ls, ROPE_DIM),
            st_s[None].astype(state_hgrn.dtype))
```

```python
import functools

import numpy as np
import jax
import jax.numpy as jnp
from jax import lax
from jax.experimental import pallas as pl
from jax.experimental.pallas import tpu as pltpu

F32 = jnp.float32
BF16 = jnp.bfloat16

EPS = 1e-6
CHUNK = 64
A_HEADS = 8
Q_LORA = 384
KV_LORA = 256
NOPE_DIM = 64
ROPE_DIM = 32
V_DIM = 64
ROPE_THETA = 10000.0
B_HEADS = 8
B_DK = 128
B_DV = 64
N_GROUPS = 4
EXPERTS_PER_GROUP = 8
N_EXPERTS = N_GROUPS * EXPERTS_PER_GROUP
D_FF_EXPERT = 256

LANES = 128
VMEM_BYTES_V7X = 64 * 1024 * 1024
VMEM_LIMIT = VMEM_BYTES_V7X - 8 * 1024 * 1024

HEAD_PAD = LANES
V_AUG = V_DIM + 16
ROPE_LO = NOPE_DIM
ROPE_HI = NOPE_DIM + ROPE_DIM
HALF = ROPE_DIM // 2
SCALE = (NOPE_DIM + ROPE_DIM) ** -0.5
LOG2E = 1.4426950408889634
NEG_INF = float("-inf")


def _params(sem):
    return pltpu.CompilerParams(dimension_semantics=sem, vmem_limit_bytes=VMEM_LIMIT)


def _const_spec(shape):
    nd = len(shape)
    return pl.BlockSpec(shape, lambda *_: (0,) * nd, pipeline_mode=pl.Buffered(1))


def _rms(x, g):
    ms = jnp.mean(x * x, axis=-1, keepdims=True)
    return x * lax.rsqrt(ms + EPS) * g


def _sig_pair(x):
    e = jnp.exp(-jnp.abs(x))
    r = 1.0 / (1.0 + e)
    er = e * r
    pos = x >= 0
    return jnp.where(pos, r, er), jnp.where(pos, er, r)


def _nt(a, b):
    return lax.dot_general(a, b, (((1,), (1,)), ((), ())), preferred_element_type=F32)


def _tn(a, b):
    return lax.dot_general(a, b, (((0,), (0,)), ((), ())), preferred_element_type=F32)


def _dot(a, b):
    return jnp.dot(a, b, preferred_element_type=F32)


def _rope(x, cosp, sinlo, sinhi):
    return x * cosp + pltpu.roll(x, LANES - HALF, 1) * sinlo + pltpu.roll(x, HALF, 1) * sinhi


def _mla_proj_kernel(x_ref, gmix_ref, win_ref, gq_ref, wuq_ref, gkv_ref, wuk_ref, wuvt_ref,
                     cos_ref, sinlo_ref, sinhi_ref,
                     h_ref, ckv_ref, kpe_ref, q_ref, k_ref, vt_ref, *, q_transposed):
    h = _rms(x_ref[...], gmix_ref[...]).astype(BF16)
    h_ref[...] = h
    z = _dot(h, win_ref[...])
    cosp, sinlo, sinhi = cos_ref[...], sinlo_ref[...], sinhi_ref[...]
    lane = lax.broadcasted_iota(jnp.int32, (1, LANES), 1)
    rope_lanes = (lane // ROPE_DIM) == (ROPE_LO // ROPE_DIM)

    cqn = _rms(z[:, :Q_LORA], gq_ref[...]).astype(BF16)
    q = _dot(cqn, wuq_ref[...])
    for hd in range(A_HEADS):
        sl = slice(hd * HEAD_PAD, (hd + 1) * HEAD_PAD)
        qh = q[:, sl]
        qh = jnp.where(rope_lanes, _rope(qh, cosp, sinlo, sinhi), qh) * (SCALE * LOG2E)
        if q_transposed:
            q_ref[sl, :] = qh.T.astype(BF16)
        else:
            q_ref[:, sl] = qh.astype(BF16)

    ckv = _rms(z[:, Q_LORA:Q_LORA + KV_LORA], gkv_ref[...])
    ckv_ref[...] = ckv
    ckv_bf = ckv.astype(BF16)
    kpe_rot = _rope(z[:, Q_LORA + KV_LORA:], cosp, sinlo, sinhi)
    kpe_ref[...] = kpe_rot[:, :ROPE_DIM]
    kpe_placed = jnp.where(rope_lanes, kpe_rot, 0.0)
    kn = _dot(ckv_bf, wuk_ref[...])
    for hd in range(A_HEADS):
        sl = slice(hd * HEAD_PAD, (hd + 1) * HEAD_PAD)
        k_ref[:, sl] = (kn[:, sl] + kpe_placed).astype(BF16)
    vt = _nt(wuvt_ref[...], ckv_bf).astype(BF16)
    ones = jnp.ones((V_AUG - V_DIM, vt.shape[1]), BF16)
    for hd in range(A_HEADS):
        vt_ref[hd * V_AUG:hd * V_AUG + V_DIM, :] = vt[hd * V_DIM:(hd + 1) * V_DIM]
        vt_ref[hd * V_AUG + V_DIM:(hd + 1) * V_AUG, :] = ones


def _mla_proj(x, gmix, win_a, gq, wuq, gkv, wuk, wuvt, cosp, sinlo, sinhi, tm, q_transposed):
    b, l, d = x.shape
    hq = A_HEADS * HEAD_PAD
    hv = A_HEADS * V_AUG
    tok = lambda w: pl.BlockSpec((None, tm, w), lambda bi, i: (bi, i, 0))
    tok_t = lambda w: pl.BlockSpec((None, w, tm), lambda bi, i: (bi, 0, i))
    tab = pl.BlockSpec((tm, LANES), lambda bi, i: (i, 0))
    return pl.pallas_call(
        functools.partial(_mla_proj_kernel, q_transposed=q_transposed),
        grid=(b, l // tm),
        in_specs=[tok(d), _const_spec(gmix.shape), _const_spec(win_a.shape), _const_spec(gq.shape),
                  _const_spec(wuq.shape), _const_spec(gkv.shape), _const_spec(wuk.shape),
                  _const_spec(wuvt.shape), tab, tab, tab],
        out_specs=[tok(d), tok(KV_LORA), tok(ROPE_DIM), tok_t(hq) if q_transposed else tok(hq), tok(hq),
                   tok_t(hv)],
        out_shape=[jax.ShapeDtypeStruct((b, l, d), BF16),
                   jax.ShapeDtypeStruct((b, l, KV_LORA), F32),
                   jax.ShapeDtypeStruct((b, l, ROPE_DIM), F32),
                   jax.ShapeDtypeStruct((b, hq, l) if q_transposed else (b, l, hq), BF16),
                   jax.ShapeDtypeStruct((b, l, hq), BF16),
                   jax.ShapeDtypeStruct((b, hv, l), BF16)],
        compiler_params=_params(("parallel", "parallel")),
        name="mla_proj",
    )(x, gmix, win_a, gq, wuq, gkv, wuk, wuvt, cosp, sinlo, sinhi)


def _hgrn_proj_kernel(h_ref, wt_ref, lb_ref, qb_ref, kb_ref, lf_ref, vb_ref, og_ref, sa_ref, sb_ref, *, row0):
    h = h_ref[...]
    dk = B_HEADS * B_DK
    dv = B_HEADS * B_DV
    dm = sa_ref.shape[-1]
    nb = 256
    lb_all = lb_ref[...]

    def z(base, i):
        return _nt(h, wt_ref[row0 + base + i * nb:row0 + base + (i + 1) * nb, :])

    for i in range(dk // nb):
        cols = slice(i * nb, (i + 1) * nb)
        zq = z(0, i)
        qb_ref[:, cols] = zq * jax.nn.sigmoid(zq)
        sf, snf = _sig_pair(z(dk, i))
        lb = lb_all[:, cols]
        lf_ref[:, cols] = jnp.log(lb + (1.0 - lb) * sf)
        kb_ref[:, cols] = (1.0 - lb) * snf
    for i in range(dv // nb):
        cols = slice(i * nb, (i + 1) * nb)
        vb_ref[:, cols] = z(2 * dk, i).astype(BF16)
        zg = z(2 * dk + dv, i)
        og_ref[:, cols] = (zg * jax.nn.sigmoid(zg)).astype(BF16)
    for i in range(dm // nb):
        cols = slice(i * nb, (i + 1) * nb)
        sa_ref[:, cols] = jax.nn.sigmoid(z(2 * dk + 2 * dv, i)).astype(BF16)
        sb_ref[:, cols] = jax.nn.sigmoid(z(2 * dk + 2 * dv + dm, i)).astype(BF16)


def _hgrn_proj(h, win_b, lb, tm):
    b, l, d = h.shape
    dk = B_HEADS * B_DK
    dv = B_HEADS * B_DV
    tok = lambda w: pl.BlockSpec((None, tm, w), lambda bi, i: (bi, i, 0))
    sds = lambda w, dt: jax.ShapeDtypeStruct((b, l, w), dt)
    return pl.pallas_call(
        functools.partial(_hgrn_proj_kernel, row0=Q_LORA + KV_LORA + ROPE_DIM),
        grid=(b, l // tm),
        in_specs=[tok(d), _const_spec(win_b.shape), _const_spec(lb.shape)],
        out_specs=[tok(dk), tok(dk), tok(dk), tok(dv), tok(dv), tok(d), tok(d)],
        out_shape=[sds(dk, F32), sds(dk, F32), sds(dk, F32), sds(dv, BF16), sds(dv, BF16),
                   sds(d, BF16), sds(d, BF16)],
        compiler_params=_params(("parallel", "parallel")),
        name="hgrn_proj",
    )(h, win_b, lb)


def _attn_kernel(qt_ref, k_ref, vt_ref, o_ref, s_sc, m_sc, acc_sc, *, t, tc, hp):
    qi = pl.program_id(2)
    n_chains = t // tc
    m_sc[...] = jnp.full(m_sc.shape, NEG_INF, F32)
    acc_sc[...] = jnp.zeros(acc_sc.shape, F32)
    qk_rows = [slice(hd * HEAD_PAD, (hd + 1) * HEAD_PAD) for hd in range(hp)]
    v_rows = [slice(hd * V_AUG, (hd + 1) * V_AUG) for hd in range(hp)]

    def scores(kb, slot):
        start = pl.multiple_of(kb * t, t)
        for hd in range(hp):
            k = k_ref[pl.ds(start, t), qk_rows[hd]]
            for ch in range(n_chains):
                cols = slice(ch * tc, (ch + 1) * tc)
                s_sc[hd, slot, :, cols] = _dot(k, qt_ref[qk_rows[hd], cols])

    def consume(kb, slot, masked):
        start = pl.multiple_of(kb * t, t)
        for hd in range(hp):
            vt = vt_ref[v_rows[hd], pl.ds(start, t)]
            for ch in range(n_chains):
                cols = slice(ch * tc, (ch + 1) * tc)
                s = s_sc[hd, slot, :, cols]
                if masked:
                    r = lax.broadcasted_iota(jnp.int32, (t, tc), 0) // CHUNK
                    c = (lax.broadcasted_iota(jnp.int32, (t, tc), 1) + ch * tc) // CHUNK
                    s = jnp.where(r <= c, s, NEG_INF)
                m_prev = m_sc[hd, :, cols]
                m_new = jnp.maximum(m_prev, jnp.max(s, axis=0, keepdims=True))
                alpha = jnp.exp2(m_prev - m_new)
                p = jnp.exp2(s - m_new)
                acc_sc[hd, :, cols] = alpha * acc_sc[hd, :, cols] + _dot(vt, p.astype(BF16))
                m_sc[hd, :, cols] = m_new

    scores(0, 0)

    def body(j, carry):
        kb = 2 * j
        scores(kb + 1, 1)
        consume(kb, 0, False)
        scores(kb + 2, 0)
        consume(kb + 1, 1, False)
        return carry

    lax.fori_loop(0, qi // 2, body, 0)

    @pl.when(qi % 2 == 0)
    def _():
        consume(qi, 0, True)

    @pl.when(qi % 2 == 1)
    def _():
        scores(qi, 1)
        consume(qi - 1, 0, False)
        consume(qi, 1, True)

    for hd in range(hp):
        acc = acc_sc[hd]
        o_ref[hd * V_DIM:(hd + 1) * V_DIM, :] = (acc[:V_DIM] / acc[V_DIM:V_DIM + 1]).astype(BF16)


def _attn(qt, k, vt, t, tc, hp):
    b, l, _ = k.shape
    return pl.pallas_call(
        functools.partial(_attn_kernel, t=t, tc=min(tc, t), hp=hp),
        grid=(b, A_HEADS // hp, l // t),
        in_specs=[pl.BlockSpec((None, hp * HEAD_PAD, t), lambda bi, h, i: (bi, h, i)),
                  pl.BlockSpec((None, l, hp * HEAD_PAD), lambda bi, h, i: (bi, 0, h)),
                  pl.BlockSpec((None, hp * V_AUG, l), lambda bi, h, i: (bi, h, 0))],
        out_specs=pl.BlockSpec((None, hp * V_DIM, t), lambda bi, h, i: (bi, h, i)),
        out_shape=jax.ShapeDtypeStruct((b, A_HEADS * V_DIM, l), BF16),
        scratch_shapes=[pltpu.VMEM((hp, 2, t, t), F32), pltpu.VMEM((hp, 1, t), F32),
                        pltpu.VMEM((hp, V_AUG, t), F32)],
        compiler_params=_params(("parallel", "parallel", "arbitrary")),
        name="attn",
    )(qt, k, vt)


def _sample_attn_kernel(q_ref, cc_ref, ckt_ref, nc_ref, nk_ref, wq2l_ref, esel_ref, wuv_ref, o_ref,
                        *, past, n_new):
    qs = q_ref[...]
    heads = [qs[:, hd * HEAD_PAD:(hd + 1) * HEAD_PAD] for hd in range(A_HEADS)]
    ql = jnp.concatenate([_dot(heads[hd], wq2l_ref[hd]) for hd in range(A_HEADS)], axis=0).astype(BF16)
    qp = jnp.concatenate([_dot(heads[hd], esel_ref[...]) for hd in range(A_HEADS)], axis=0).astype(BF16)
    cc = cc_ref[...].astype(BF16)
    ckt = ckt_ref[...].astype(BF16)
    nc = nc_ref[...].astype(BF16)
    nk = nk_ref[...].astype(BF16)
    s_c = _nt(ql, cc) + _dot(qp, ckt)
    s_n = _nt(ql, nc) + _nt(qp, nk)
    rows = A_HEADS * n_new
    tq = lax.broadcasted_iota(jnp.int32, (rows, n_new), 0) % n_new
    tk = lax.broadcasted_iota(jnp.int32, (rows, n_new), 1)
    s_n = jnp.where((past + tk) // CHUNK <= (past + tq) // CHUNK, s_n, NEG_INF)
    m = jnp.maximum(jnp.max(s_c, axis=-1, keepdims=True), jnp.max(s_n, axis=-1, keepdims=True))
    p_c = jnp.exp2(s_c - m)
    p_n = jnp.exp2(s_n - m)
    denom = jnp.sum(p_c, axis=-1, keepdims=True) + jnp.sum(p_n, axis=-1, keepdims=True)
    o_lat = ((_dot(p_c.astype(BF16), cc) + _dot(p_n.astype(BF16), nc)) / denom).astype(BF16)
    full = _dot(o_lat, wuv_ref[...])
    col_head = lax.broadcasted_iota(jnp.int32, (n_new, A_HEADS * V_DIM), 1) // V_DIM
    out = jnp.zeros((n_new, A_HEADS * V_DIM), F32)
    for hd in range(A_HEADS):
        out = out + jnp.where(col_head == hd, full[hd * n_new:(hd + 1) * n_new], 0.0)
    o_ref[...] = out.astype(BF16)


def _sample_attn(q, cache_ckv, cache_kpe_t, ckv_new, kpe_new, wq2l, esel, wuvp):
    nb, n_new, hq = q.shape
    past = cache_ckv.shape[1]
    hv = A_HEADS * V_DIM
    per = lambda r, w: pl.BlockSpec((None, r, w), lambda bi: (bi, 0, 0))
    return pl.pallas_call(
        functools.partial(_sample_attn_kernel, past=past, n_new=n_new),
        grid=(nb,),
        in_specs=[per(n_new, hq), per(past, KV_LORA), per(ROPE_DIM, past), per(n_new, KV_LORA),
                  per(n_new, ROPE_DIM), _const_spec(wq2l.shape), _const_spec(esel.shape),
                  _const_spec(wuvp.shape)],
        out_specs=per(n_new, hv),
        out_shape=jax.ShapeDtypeStruct((nb, n_new, hv), BF16),
        compiler_params=_params(("parallel",)),
        name="sample_attn",
    )(q, cache_ckv, cache_kpe_t, ckv_new, kpe_new, wq2l, esel, wuvp)


def _level_sizes(c):
    sizes = []
    p = c
    while p >= 2:
        sizes.append(p)
        p //= 2
    return sizes


def _level_masks(c):
    t = np.arange(c)[:, None]
    s = np.arange(c)[None, :]
    out = []
    for p in _level_sizes(c):
        out.append((t // p == s // p) & (t % p >= p // 2) & (s % p < p // 2))
    out.append(t == s)
    return np.stack(out).astype(np.float32)


def _boundary(cum, p, c):
    half = p // 2
    if p >= 8:
        parts = [jnp.broadcast_to(cum[i * p + half - 1:i * p + half, :], (p, cum.shape[1]))
                 for i in range(c // p)]
        return jnp.concatenate(parts, axis=0) if len(parts) > 1 else parts[0]
    sub = lax.broadcasted_iota(jnp.int32, (8, cum.shape[1]), 0)
    parts = []
    for g in range(c // 8):
        lo = jnp.broadcast_to(cum[g * 8 + 1:g * 8 + 2, :], (8, cum.shape[1]))
        hi = jnp.broadcast_to(cum[g * 8 + 5:g * 8 + 6, :], (8, cum.shape[1]))
        parts.append(jnp.where(sub < 4, lo, hi))
    return jnp.concatenate(parts, axis=0)


def _hgrn_kernel(q_ref, k_ref, lf_ref, v_ref, og_ref, s0t_ref, gon_ref, mask_ref, tri_ref,
                 o_ref, snt_ref, st_sc, *, c, n_chunks, per_chunk_state):
    step = pl.program_id(2)
    sizes = _level_sizes(c)
    lane = lax.broadcasted_iota(jnp.int32, (1, LANES), 1)
    low = lane < B_DV
    keep_lo = jnp.where(low, 1.0, 0.0).astype(BF16)
    keep_hi = jnp.where(low, 0.0, 1.0).astype(BF16)

    if not per_chunk_state:
        @pl.when(step == 0)
        def _():
            st_sc[...] = jnp.zeros(st_sc.shape, F32)
            st_sc[0, 0:B_DV, :] = s0t_ref[0]
            st_sc[1, B_DV:2 * B_DV, :] = s0t_ref[1]

    tri = tri_ref[...]
    rows = [slice(ci * c, (ci + 1) * c) for ci in range(n_chunks)]
    heads = [slice(j * B_DK, (j + 1) * B_DK) for j in range(2)]
    items = [(ci, j) for ci in range(n_chunks) for j in range(2)]


    cums = []
    for ci in range(n_chunks):
        lf = lf_ref[rows[ci], :]
        hi = lf.astype(BF16)
        r1 = lf - hi.astype(F32)
        mid = r1.astype(BF16)
        lo = (r1 - mid.astype(F32)).astype(BF16)
        cums.append((_dot(tri, hi) + _dot(tri, mid) + _dot(tri, lo)) * LOG2E)

    def zeros(n):
        return jnp.zeros((n, B_DK), F32)

    a_mats = {}
    for ci, j in items:
        q = q_ref[rows[ci], heads[j]]
        k = k_ref[rows[ci], heads[j]]
        cum = cums[ci][:, heads[j]]
        kbf = k.astype(BF16)
        a = _nt(q.astype(BF16), kbf) * mask_ref[len(sizes)]
        q_cols, k_cols = [], []
        for li, p in enumerate(sizes):
            half = p // 2
            if half % 8 == 0:
                for lo in range(0, c, p):
                    mid, hi = lo + half, lo + p
                    bnd = cum[mid - 1:mid, :]
                    qu = q[mid:hi] * jnp.exp2(cum[mid:hi] - bnd)
                    kl = k[lo:mid] * jnp.exp2(bnd - cum[lo:mid])
                    q_cols.append(jnp.concatenate([x for x in (zeros(mid), qu, zeros(c - hi)) if x.shape[0]], axis=0))
                    k_cols.append(jnp.concatenate([x for x in (zeros(lo), kl, zeros(c - mid)) if x.shape[0]], axis=0))
                continue
            if p == 2:
                qt = (q * jnp.exp2(lf_ref[rows[ci], heads[j]] * LOG2E)).astype(BF16)
                kt = kbf
            else:
                bnd = _boundary(cum, p, c)
                qt = (q * jnp.exp2(jnp.minimum(cum - bnd, 0.0))).astype(BF16)
                kt = (k * jnp.exp2(jnp.minimum(bnd - cum, 0.0))).astype(BF16)
            a = a + _nt(qt, kt) * mask_ref[li]
        if q_cols:
            a = a + _nt(jnp.concatenate(q_cols, axis=1).astype(BF16), jnp.concatenate(k_cols, axis=1).astype(BF16))
        a_mats[ci, j] = a.astype(BF16)

    qhats, kvs, decays, vs = {}, {}, {}, {}
    for ci, j in items:
        cum = cums[ci][:, heads[j]]
        last = cum[c - 1:c, :]
        vs[ci, j] = v_ref[rows[ci], :] * (keep_lo if j == 0 else keep_hi)
        qhats[ci, j] = (q_ref[rows[ci], heads[j]] * jnp.exp2(cum)).astype(BF16)
        khat = (k_ref[rows[ci], heads[j]] * jnp.exp2(last - cum)).astype(BF16)
        kvs[ci, j] = _tn(vs[ci, j], khat)
        decays[ci, j] = jnp.exp2(last)

    pad = jnp.zeros((B_DV, B_DK), F32)
    if not per_chunk_state:
        st = [st_sc[0], st_sc[1]]
    for ci in range(n_chunks):
        if per_chunk_state:
            st = [jnp.concatenate([s0t_ref[ci, 0], pad], axis=0), jnp.concatenate([pad, s0t_ref[ci, 1]], axis=0)]
        o_pair = jnp.zeros((c, LANES), F32)
        for j in range(2):
            o_pair = o_pair + _dot(a_mats[ci, j], vs[ci, j]) + _nt(qhats[ci, j], st[j].astype(BF16))
            st[j] = st[j] * decays[ci, j] + kvs[ci, j]
        if per_chunk_state:
            snt_ref[ci, 0] = st[0][0:B_DV, :]
            snt_ref[ci, 1] = st[1][B_DV:2 * B_DV, :]
        o2 = o_pair * o_pair
        s_lo = jnp.sum(jnp.where(low, o2, 0.0), axis=-1, keepdims=True)
        s_hi = jnp.sum(jnp.where(low, 0.0, o2), axis=-1, keepdims=True)
        ms = jnp.where(low, s_lo, s_hi) * (1.0 / B_DV)
        ob = o_pair * lax.rsqrt(ms + EPS) * gon_ref[...] * og_ref[rows[ci], :].astype(F32)
        o_ref[rows[ci], :] = ob.astype(BF16)
    if not per_chunk_state:
        st_sc[0] = st[0]
        st_sc[1] = st[1]

        @pl.when(step == pl.num_programs(2) - 1)
        def _():
            snt_ref[0] = st_sc[0, 0:B_DV, :]
            snt_ref[1] = st_sc[1, B_DV:2 * B_DV, :]


def _hgrn(qb, kb, lf, vb, og, s0t, gon, n_streams, tc):
    b, l, dk = qb.shape
    per_chunk_state = n_streams != b
    if per_chunk_state:
        assert b == 1 and l % n_streams == 0 and l // n_streams <= CHUNK
        c = l // n_streams
    else:
        c = min(CHUNK, l)
    assert l % tc == 0 and tc % c == 0
    n_chunks = tc // c
    masks = jnp.asarray(_level_masks(c))
    tri = jnp.asarray(np.tril(np.ones((c, c), np.float32)), dtype=BF16)
    pairs = B_HEADS // 2
    wide = pl.BlockSpec((None, tc, 2 * B_DK), lambda bi, p, i: (bi, i, p))
    narrow = pl.BlockSpec((None, tc, 2 * B_DV), lambda bi, p, i: (bi, i, p))
    if per_chunk_state:
        state = pl.BlockSpec((n_chunks, 2, B_DV, B_DK), lambda bi, p, i: (i, p, 0, 0))
    else:
        state = pl.BlockSpec((None, 2, B_DV, B_DK), lambda bi, p, i: (bi, p, 0, 0))
    return pl.pallas_call(
        functools.partial(_hgrn_kernel, c=c, n_chunks=n_chunks, per_chunk_state=per_chunk_state),
        grid=(b, pairs, l // tc),
        in_specs=[wide, wide, wide, narrow, narrow, state, _const_spec(gon.shape),
                  _const_spec(masks.shape), _const_spec(tri.shape)],
        out_specs=[narrow, state],
        out_shape=[jax.ShapeDtypeStruct((b, l, B_HEADS * B_DV), BF16),
                   jax.ShapeDtypeStruct((n_streams, B_HEADS, B_DV, B_DK), F32)],
        scratch_shapes=[pltpu.VMEM((2, 2 * B_DV, B_DK), F32)],
        compiler_params=_params(("parallel", "parallel", "parallel" if per_chunk_state else "arbitrary")),
        name="hgrn",
    )(qb, kb, lf, vb, og, s0t, gon, masks, tri)


MOE_SUB = 144
MOE_ROWS = 1024
_BIG_LANE = float(1 << 20)


def _max_items(tm):
    return (tm + N_GROUPS * (MOE_SUB - 1)) // MOE_SUB


def _first_lane_of_max(x, lane):
    v = jnp.max(x, axis=-1, keepdims=True)
    return v, jnp.min(jnp.where(x == v, lane, _BIG_LANE), axis=-1, keepdims=True)


def _out_proj_kernel(oa_ref, ob_ref, sa_ref, sb_ref, x_ref, wa_ref, wb_ref, wo_ref, gffn_ref,
                     wr_ref, br_ref, tri_ref, upper_ref, x1_ref, xs_ref, slot_ref, meta_ref, *, oa_transposed):
    tm = x_ref.shape[0]
    oa, ob = oa_ref[...], ob_ref[...]
    nb = 256
    m = []
    for i in range(x_ref.shape[1] // nb):
        cols = slice(i * nb, (i + 1) * nb)
        ya = _tn(oa, wa_ref[:, cols]) if oa_transposed else _dot(oa, wa_ref[:, cols])
        yb = _dot(ob, wb_ref[:, cols])
        m.append((sa_ref[:, cols].astype(F32) * ya + sb_ref[:, cols].astype(F32) * yb).astype(BF16))
    x1 = x_ref[...] + _dot(jnp.concatenate(m, axis=-1), wo_ref[...])
    x1_ref[...] = x1
    h2 = _rms(x1, gffn_ref[...]).astype(BF16)

    logits = _dot(h2, wr_ref[...]) + br_ref[...]
    lane = lax.broadcasted_iota(jnp.int32, (tm, LANES), 1).astype(F32)
    _, grp = _first_lane_of_max(jnp.where(lane < N_GROUPS, logits, NEG_INF), lane)
    onehot = jnp.where(lane == grp, 1.0, 0.0)
    rank = _dot(tri_ref[...], onehot.astype(BF16))
    count = rank[tm - 1:tm, :]
    items = jnp.zeros_like(count)
    for j in range(_max_items(tm)):
        items = items + jnp.where(count > j * MOE_SUB, 1.0, 0.0)
    base = _dot(jnp.broadcast_to(items, (8, LANES)).astype(BF16), upper_ref[...])[0:1, :] * MOE_SUB
    slot = jnp.sum(onehot * (base + rank - 1.0), axis=-1, keepdims=True)
    col = lax.broadcasted_iota(jnp.int32, (tm, MOE_ROWS), 1).astype(F32)
    place = jnp.where(col == slot, 1.0, 0.0).astype(BF16)
    xs_ref[...] = _tn(place, h2).astype(BF16)
    slot_ref[...] = jnp.broadcast_to(slot, (tm, LANES))
    meta_ref[...] = jnp.broadcast_to(items, (8, LANES))


def _out_proj(oa, ob, sa, sb, x, wa, wb, wo, gffn, wr, br, tm, oa_transposed):
    b, l, d = x.shape
    nt = l // tm
    assert _max_items(tm) * MOE_SUB <= MOE_ROWS
    hv = A_HEADS * V_DIM
    tri = jnp.asarray(np.tril(np.ones((tm, tm), np.float32)), dtype=BF16)
    upper = jnp.asarray(np.triu(np.ones((LANES, LANES), np.float32), 1), dtype=BF16)
    tok = lambda w: pl.BlockSpec((None, tm, w), lambda bi, i: (bi, i, 0))
    per_tile = lambda r, w: pl.BlockSpec((None, r, w), lambda bi, i: (bi * nt + i, 0, 0))
    oa_spec = pl.BlockSpec((None, hv, tm), lambda bi, i: (bi, 0, i)) if oa_transposed else tok(hv)
    return pl.pallas_call(
        functools.partial(_out_proj_kernel, oa_transposed=oa_transposed),
        grid=(b, nt),
        in_specs=[oa_spec, tok(B_HEADS * B_DV), tok(d), tok(d), tok(d), _const_spec(wa.shape),
                  _const_spec(wb.shape), _const_spec(wo.shape), _const_spec(gffn.shape),
                  _const_spec(wr.shape), _const_spec(br.shape), _const_spec(tri.shape),
                  _const_spec(upper.shape)],
        out_specs=[tok(d), per_tile(MOE_ROWS, d), tok(LANES), per_tile(8, LANES)],
        out_shape=[jax.ShapeDtypeStruct((b, l, d), F32),
                   jax.ShapeDtypeStruct((b * nt, MOE_ROWS, d), BF16),
                   jax.ShapeDtypeStruct((b, l, LANES), F32),
                   jax.ShapeDtypeStruct((b * nt, 8, LANES), F32)],
        compiler_params=_params(("parallel", "parallel")),
        name="out_proj",
    )(oa, ob, sa, sb, x, wa, wb, wo, gffn, wr, br, tri, upper)


def _work_list(meta, n_items):
    n = meta[:, 0, :N_GROUPS].astype(jnp.int32)
    tiles = n.shape[0]
    first = jnp.cumsum(n, axis=1) - n
    per_seg = n.T.reshape(-1)
    ends = jnp.cumsum(per_seg)
    total = ends[-1]
    i = jnp.minimum(jnp.arange(n_items, dtype=jnp.int32), total - 1)
    seg = jnp.sum((ends[None, :] <= i[:, None]).astype(jnp.int32), axis=1)
    grp, tile = seg // tiles, seg % tiles
    block = first[tile, grp] + i - (ends[seg] - per_seg[seg])
    return tile, block.astype(jnp.int32), grp, total[None].astype(jnp.int32)


def _moe_expert_kernel(tile_ref, block_ref, grp_ref, total_ref, xs_ref, wr_ref, br_ref, wg_ref, wu_ref,
                       wd_ref, ys_ref):
    i = pl.program_id(0)

    @pl.when(i < total_ref[0])
    def _():
        g = grp_ref[i]
        x = xs_ref[...]
        logits = _dot(x, wr_ref[...]) + br_ref[...]
        lane_i = lax.broadcasted_iota(jnp.int32, logits.shape, 1)
        lane = lane_i.astype(F32)
        is_grp = jnp.abs(lane - (EXPERTS_PER_GROUP + 0.5 * (N_GROUPS - 1))) < 0.5 * N_GROUPS
        lg = jnp.where(is_grp, logits, NEG_INF)
        gmax = jnp.max(lg, axis=-1, keepdims=True)
        denom = jnp.sum(jnp.exp(lg - gmax), axis=-1, keepdims=True)
        lg_own = jnp.sum(jnp.where(lane_i == EXPERTS_PER_GROUP + g, logits, 0.0), axis=-1, keepdims=True)
        p_grp = jnp.exp(lg_own - gmax) / denom
        le = jnp.where(lane < EXPERTS_PER_GROUP, logits, NEG_INF)
        v1, i1 = _first_lane_of_max(le, lane)
        le2 = jnp.where(lane == i1, NEG_INF, le)
        v2, i2 = _first_lane_of_max(le2, lane)
        e2 = jnp.exp(v2 - v1)
        w1 = p_grp / (1.0 + e2)
        comb = jnp.where(lane == i1, w1, 0.0) + jnp.where(lane == i2, w1 * e2, 0.0)
        hid = []
        for e in range(EXPERTS_PER_GROUP):
            a = _dot(x, wg_ref[e])
            u = _dot(x, wu_ref[e])
            s, _ = _sig_pair(a)
            hid.append((a * s * u * comb[:, e:e + 1]).astype(BF16))
        hid = jnp.concatenate(hid, axis=-1)
        wd = wd_ref[...].reshape(EXPERTS_PER_GROUP * D_FF_EXPERT, wd_ref.shape[-1])
        ys_ref[...] = _dot(hid, wd).astype(BF16)


def _moe_experts(xs, work, wrg, brg, wg, wu, wd):
    tiles, _, d = xs.shape
    tile, block, grp, total = work
    item = lambda i, t, k, g, n: (t[i], k[i], 0)
    by_group = lambda i, t, k, g, n: (g[i], 0, 0)
    grid_spec = pltpu.PrefetchScalarGridSpec(
        num_scalar_prefetch=4,
        grid=(tile.shape[0],),
        in_specs=[pl.BlockSpec((None, MOE_SUB, d), item),
                  pl.BlockSpec((None, d, LANES), by_group),
                  pl.BlockSpec((None, 1, LANES), by_group),
                  pl.BlockSpec((EXPERTS_PER_GROUP, d, D_FF_EXPERT), by_group),
                  pl.BlockSpec((EXPERTS_PER_GROUP, d, D_FF_EXPERT), by_group),
                  pl.BlockSpec((EXPERTS_PER_GROUP, D_FF_EXPERT, d), by_group)],
        out_specs=pl.BlockSpec((None, MOE_SUB, d), item))
    return pl.pallas_call(
        _moe_expert_kernel,
        grid_spec=grid_spec,
        out_shape=jax.ShapeDtypeStruct(xs.shape, BF16),
        input_output_aliases={4: 0},
        compiler_params=_params(("arbitrary",)),
        name="moe_experts",
    )(tile, block, grp, total, xs, wrg, brg, wg, wu, wd)


def _moe_combine_kernel(ys_ref, slot_ref, x1_ref, gfin_ref, y_ref):
    tm = x1_ref.shape[0]
    col = lax.broadcasted_iota(jnp.int32, (tm, MOE_ROWS), 1).astype(F32)
    slot = jnp.concatenate([slot_ref[...]] * (MOE_ROWS // LANES), axis=-1)
    place = jnp.where(col == slot, 1.0, 0.0).astype(BF16)
    y_ref[...] = _rms(x1_ref[...] + _dot(place, ys_ref[...]), gfin_ref[...])


def _moe_combine(ys, slot, x1, gfin, tm):
    b, l, d = x1.shape
    nt = l // tm
    tok = lambda w: pl.BlockSpec((None, tm, w), lambda bi, i: (bi, i, 0))
    per_tile = lambda r, w: pl.BlockSpec((None, r, w), lambda bi, i: (bi * nt + i, 0, 0))
    return pl.pallas_call(
        _moe_combine_kernel,
        grid=(b, nt),
        in_specs=[per_tile(MOE_ROWS, d), tok(LANES), tok(d), _const_spec(gfin.shape)],
        out_specs=tok(d),
        out_shape=jax.ShapeDtypeStruct((b, l, d), F32),
        compiler_params=_params(("parallel", "parallel")),
        name="moe_combine",
    )(ys, slot, x1, gfin)


def _rope_tables(pos):
    inv = jnp.power(ROPE_THETA, -jnp.arange(HALF, dtype=F32) / HALF)
    ang = pos.astype(F32)[:, None] * inv[None, :]
    cos, sin = jnp.cos(ang), jnp.sin(ang)
    zeros = jnp.zeros_like(sin)
    reps = LANES // ROPE_DIM
    cosp = jnp.tile(jnp.concatenate([cos, cos], axis=1), (1, reps))
    sinlo = jnp.tile(jnp.concatenate([-sin, zeros], axis=1), (1, reps))
    sinhi = jnp.tile(jnp.concatenate([zeros, sin], axis=1), (1, reps))
    return cosp, sinlo, sinhi


def _prep_weights(g_mix, w_in, g_q, w_uq, g_kv, w_ukv, lb_hgrn, g_onorm, w_a_out, w_b_out, w_o,
                  g_ffn, w_rg, b_rg, w_re, b_re, w_gate, w_up, w_down, g_final):
    assert w_in.shape[0] == 1, "single-layer step"
    d = w_in.shape[1]
    w = w_in[0]
    n_a = Q_LORA + KV_LORA
    win_a = jnp.concatenate([w[:, :n_a]] + [w[:, n_a:n_a + ROPE_DIM]] * (LANES // ROPE_DIM), axis=1).astype(BF16)
    win_b = jnp.swapaxes(w, 0, 1).astype(BF16)
    per_q = NOPE_DIM + ROPE_DIM
    wuq = jnp.pad(w_uq[0].reshape(Q_LORA, A_HEADS, per_q), ((0, 0), (0, 0), (0, HEAD_PAD - per_q)))
    wuq = wuq.reshape(Q_LORA, A_HEADS * HEAD_PAD).astype(BF16)
    wukv = w_ukv[0].reshape(KV_LORA, A_HEADS, NOPE_DIM + V_DIM)
    w_uk, w_uv = wukv[..., :NOPE_DIM], wukv[..., NOPE_DIM:]
    wuk = jnp.pad(w_uk, ((0, 0), (0, 0), (0, HEAD_PAD - NOPE_DIM))).reshape(KV_LORA, A_HEADS * HEAD_PAD).astype(BF16)
    wuvt = w_uv.reshape(KV_LORA, A_HEADS * V_DIM).T.astype(BF16)
    wq2l = jnp.pad(jnp.transpose(w_uk, (1, 2, 0)), ((0, 0), (0, HEAD_PAD - NOPE_DIM), (0, 0))).astype(BF16)
    esel = jnp.zeros((HEAD_PAD, ROPE_DIM), F32).at[ROPE_LO + jnp.arange(ROPE_DIM), jnp.arange(ROPE_DIM)].set(1.0).astype(BF16)
    wuvp = w_uv.reshape(KV_LORA, A_HEADS * V_DIM).astype(BF16)
    lb = jnp.cumsum(jax.nn.softmax(lb_hgrn.astype(F32), axis=0), axis=0)[0][None, :]
    gon = jnp.tile(g_onorm[0], 2)[None, :]
    wr = jnp.pad(w_rg[0], ((0, 0), (0, LANES - N_GROUPS))).astype(BF16)
    br = jnp.pad(b_rg[0], (0, LANES - N_GROUPS))[None, :]
    rest = LANES - EXPERTS_PER_GROUP - N_GROUPS
    wrg = jnp.concatenate([w_re[0], jnp.broadcast_to(w_rg[0], (N_GROUPS, d, N_GROUPS)),
                           jnp.zeros((N_GROUPS, d, rest), F32)], axis=2).astype(BF16)
    brg = jnp.concatenate([b_re[0], jnp.broadcast_to(b_rg[0], (N_GROUPS, N_GROUPS)),
                           jnp.zeros((N_GROUPS, rest), F32)], axis=1)[:, None, :]
    return dict(
        wrg=wrg, brg=brg,
        gmix=g_mix[0][None, :], win_a=win_a, win_b=win_b, gq=g_q[0][None, :], wuq=wuq,
        gkv=g_kv[0][None, :], wuk=wuk, wuvt=wuvt, wq2l=wq2l, esel=esel, wuvp=wuvp, lb=lb, gon=gon,
        wa=w_a_out[0].astype(BF16), wb=w_b_out[0].astype(BF16), wo=w_o[0].astype(BF16),
        gffn=g_ffn[0][None, :], wr=wr, br=br, wg=w_gate[0].astype(BF16), wu=w_up[0].astype(BF16),
        wd=w_down[0].astype(BF16), gfin=g_final[None, :])


def _tile(n, want):
    t = min(n, want)
    while n % t:
        t //= 2
    return t


def _run_group(x, pos, streams, past, w):
    b, l, d = x.shape
    ns, ls = streams
    tm = _tile(l, 512)
    cosp, sinlo, sinhi = _rope_tables(pos)
    h, ckv, kpe, q, k, vt = _mla_proj(x, w["gmix"], w["win_a"], w["gq"], w["wuq"], w["gkv"], w["wuk"],
                                      w["wuvt"], cosp, sinlo, sinhi, tm, past is None)
    qb, kb, lf, vb, og, sa, sb = _hgrn_proj(h, w["win_b"], w["lb"], tm)
    as_streams = lambda a: a.reshape(ns, ls, a.shape[-1])
    if past is None:
        oa = _attn(q, k, vt, _tile(l, 512), 256, 4)
        s0t = jnp.zeros((ns, B_HEADS, B_DV, B_DK), F32)
    else:
        cache_ckv, cache_kpe, state = past
        oa = _sample_attn(as_streams(q), cache_ckv, cache_kpe, as_streams(ckv), as_streams(kpe),
                          w["wq2l"], w["esel"], w["wuvp"]).reshape(b, l, A_HEADS * V_DIM)
        s0t = jnp.swapaxes(state.astype(F32), -1, -2)
    ob, snt = _hgrn(qb, kb, lf, vb, og, s0t, w["gon"], ns, _tile(l, 1024 if past is None else 8 * ls))
    x1, xs, slot, meta = _out_proj(oa, ob, sa, sb, x, w["wa"], w["wb"], w["wo"], w["gffn"], w["wr"],
                                   w["br"], tm, past is None)
    work = _work_list(meta, meta.shape[0] * _max_items(tm))
    ys = _moe_experts(xs, work, w["wrg"], w["brg"], w["wg"], w["wu"], w["wd"])
    y = _moe_combine(ys, slot, x1, w["gfin"], tm)
    return y, ckv, kpe, jnp.swapaxes(snt, -1, -2)


def kernel(x_prompt, x_sample, cache_ckv, cache_kpe, state_hgrn, g_mix, w_in, g_q, w_uq, g_kv, w_ukv,
           lb_hgrn, g_onorm, w_a_out, w_b_out, w_o, g_ffn, w_rg, b_rg, w_re, b_re, w_gate, w_up, w_down,
           g_final):
    w = _prep_weights(g_mix, w_in, g_q, w_uq, g_kv, w_ukv, lb_hgrn, g_onorm, w_a_out, w_b_out, w_o,
                      g_ffn, w_rg, b_rg, w_re, b_re, w_gate, w_up, w_down, g_final)
    bp, lp, d = x_prompt.shape
    y_p, ckv_p, kpe_p, st_p = _run_group(x_prompt, jnp.arange(lp, dtype=jnp.int32), (bp, lp), None, w)

    bs, ls, _ = x_sample.shape
    past_len = cache_ckv.shape[2]
    pos_s = past_len + (jnp.arange(bs * ls, dtype=jnp.int32) % ls)
    y_s, ckv_s, kpe_s, st_s = _run_group(x_sample.reshape(1, bs * ls, d), pos_s, (bs, ls),
                                         (cache_ckv[0], jnp.swapaxes(cache_kpe[0], 1, 2), state_hgrn[0]), w)
    return (y_p, y_s.reshape(bs, ls, d),
            ckv_p[None], kpe_p[None], st_p[None].astype(x_prompt.dtype),
            ckv_s.reshape(1, bs, ls, KV_LORA), kpe_s.reshape(1, bs, ls, ROPE_DIM),
            st_s[None].astype(state_hgrn.dtype))
```

```python
import functools

import numpy as np
import jax
import jax.numpy as jnp
from jax import lax
from jax.experimental import pallas as pl
from jax.experimental.pallas import tpu as pltpu

F32 = jnp.float32
BF16 = jnp.bfloat16

EPS = 1e-6
CHUNK = 64
A_HEADS = 8
Q_LORA = 384
KV_LORA = 256
NOPE_DIM = 64
ROPE_DIM = 32
V_DIM = 64
ROPE_THETA = 10000.0
B_HEADS = 8
B_DK = 128
B_DV = 64
N_GROUPS = 4
EXPERTS_PER_GROUP = 8
N_EXPERTS = N_GROUPS * EXPERTS_PER_GROUP
D_FF_EXPERT = 256

LANES = 128
VMEM_BYTES_V7X = 64 * 1024 * 1024
VMEM_LIMIT = VMEM_BYTES_V7X - 8 * 1024 * 1024

HEAD_PAD = LANES
V_AUG = V_DIM + 16
ROPE_LO = NOPE_DIM
ROPE_HI = NOPE_DIM + ROPE_DIM
HALF = ROPE_DIM // 2
SCALE = (NOPE_DIM + ROPE_DIM) ** -0.5
LOG2E = 1.4426950408889634
NEG_INF = float("-inf")


def _params(sem):
    return pltpu.CompilerParams(dimension_semantics=sem, vmem_limit_bytes=VMEM_LIMIT)


def _const_spec(shape):
    nd = len(shape)
    return pl.BlockSpec(shape, lambda *_: (0,) * nd, pipeline_mode=pl.Buffered(1))


def _rms(x, g):
    ms = jnp.mean(x * x, axis=-1, keepdims=True)
    return x * lax.rsqrt(ms + EPS) * g


def _sig_pair(x):
    e = jnp.exp(-jnp.abs(x))
    r = 1.0 / (1.0 + e)
    er = e * r
    pos = x >= 0
    return jnp.where(pos, r, er), jnp.where(pos, er, r)


def _nt(a, b):
    return lax.dot_general(a, b, (((1,), (1,)), ((), ())), preferred_element_type=F32)


def _tn(a, b):
    return lax.dot_general(a, b, (((0,), (0,)), ((), ())), preferred_element_type=F32)


def _dot(a, b):
    return jnp.dot(a, b, preferred_element_type=F32)


def _rope(x, cosp, sinlo, sinhi):
    return x * cosp + pltpu.roll(x, LANES - HALF, 1) * sinlo + pltpu.roll(x, HALF, 1) * sinhi


def _mla_proj_kernel(x_ref, gmix_ref, win_ref, gq_ref, wuq_ref, wuqt_ref, gkv_ref, wuk_ref, wuvt_ref,
                     cos_ref, sinlo_ref, sinhi_ref, cost_ref, sint_ref,
                     h_ref, ckv_ref, kpe_ref, q_ref, k_ref, vt_ref, *, q_transposed):
    h = _rms(x_ref[...], gmix_ref[...]).astype(BF16)
    h_ref[...] = h
    z = _dot(h, win_ref[...])
    cosp, sinlo, sinhi = cos_ref[...], sinlo_ref[...], sinhi_ref[...]
    lane = lax.broadcasted_iota(jnp.int32, (1, LANES), 1)
    rope_lanes = (lane // ROPE_DIM) == (ROPE_LO // ROPE_DIM)

    cqn = _rms(z[:, :Q_LORA], gq_ref[...]).astype(BF16)
    if q_transposed:
        qt = _nt(wuqt_ref[...], cqn)
        cos_t, sin_t = cost_ref[...], sint_ref[...]
        for hd in range(A_HEADS):
            r0 = hd * HEAD_PAD
            q_ref[r0:r0 + ROPE_LO, :] = (qt[r0:r0 + ROPE_LO] * (SCALE * LOG2E)).astype(BF16)
            a = qt[r0 + ROPE_LO:r0 + ROPE_LO + HALF]
            b = qt[r0 + ROPE_LO + HALF:r0 + ROPE_HI]
            q_ref[r0 + ROPE_LO:r0 + ROPE_HI, :] = jnp.concatenate(
                [a * cos_t - b * sin_t, a * sin_t + b * cos_t], axis=0).astype(BF16)
            q_ref[r0 + ROPE_HI:r0 + HEAD_PAD, :] = jnp.zeros((HEAD_PAD - ROPE_HI, qt.shape[1]), BF16)
    else:
        q = _dot(cqn, wuq_ref[...])
        for hd in range(A_HEADS):
            sl = slice(hd * HEAD_PAD, (hd + 1) * HEAD_PAD)
            qh = q[:, sl]
            qh = jnp.where(rope_lanes, _rope(qh, cosp, sinlo, sinhi), qh) * (SCALE * LOG2E)
            q_ref[:, sl] = qh.astype(BF16)

    ckv = _rms(z[:, Q_LORA:Q_LORA + KV_LORA], gkv_ref[...])
    ckv_ref[...] = ckv
    ckv_bf = ckv.astype(BF16)
    kpe_rot = _rope(z[:, Q_LORA + KV_LORA:], cosp, sinlo, sinhi)
    kpe_ref[...] = kpe_rot[:, :ROPE_DIM]
    kpe_placed = jnp.where(rope_lanes, kpe_rot, 0.0)
    kn = _dot(ckv_bf, wuk_ref[...])
    for hd in range(A_HEADS):
        sl = slice(hd * HEAD_PAD, (hd + 1) * HEAD_PAD)
        k_ref[:, sl] = (kn[:, sl] + kpe_placed).astype(BF16)
    vt = _nt(wuvt_ref[...], ckv_bf).astype(BF16)
    ones = jnp.ones((V_AUG - V_DIM, vt.shape[1]), BF16)
    for hd in range(A_HEADS):
        vt_ref[hd * V_AUG:hd * V_AUG + V_DIM, :] = vt[hd * V_DIM:(hd + 1) * V_DIM]
        vt_ref[hd * V_AUG + V_DIM:(hd + 1) * V_AUG, :] = ones


def _mla_proj(x, gmix, win_a, gq, wuq, wuqt, gkv, wuk, wuvt, tables, tm, q_transposed):
    b, l, d = x.shape
    hq = A_HEADS * HEAD_PAD
    hv = A_HEADS * V_AUG
    cosp, sinlo, sinhi, cos_t, sin_t = tables
    tok = lambda w: pl.BlockSpec((None, tm, w), lambda bi, i: (bi, i, 0))
    tok_t = lambda w: pl.BlockSpec((None, w, tm), lambda bi, i: (bi, 0, i))
    tab = pl.BlockSpec((tm, LANES), lambda bi, i: (i, 0))
    tab_t = pl.BlockSpec((HALF, tm), lambda bi, i: (0, i))
    return pl.pallas_call(
        functools.partial(_mla_proj_kernel, q_transposed=q_transposed),
        grid=(b, l // tm),
        in_specs=[tok(d), _const_spec(gmix.shape), _const_spec(win_a.shape), _const_spec(gq.shape),
                  _const_spec(wuq.shape), _const_spec(wuqt.shape), _const_spec(gkv.shape),
                  _const_spec(wuk.shape), _const_spec(wuvt.shape), tab, tab, tab, tab_t, tab_t],
        out_specs=[tok(d), tok(KV_LORA), tok(ROPE_DIM), tok_t(hq) if q_transposed else tok(hq), tok(hq),
                   tok_t(hv)],
        out_shape=[jax.ShapeDtypeStruct((b, l, d), BF16),
                   jax.ShapeDtypeStruct((b, l, KV_LORA), F32),
                   jax.ShapeDtypeStruct((b, l, ROPE_DIM), F32),
                   jax.ShapeDtypeStruct((b, hq, l) if q_transposed else (b, l, hq), BF16),
                   jax.ShapeDtypeStruct((b, l, hq), BF16),
                   jax.ShapeDtypeStruct((b, hv, l), BF16)],
        compiler_params=_params(("parallel", "parallel")),
        name="mla_proj",
    )(x, gmix, win_a, gq, wuq, wuqt, gkv, wuk, wuvt, cosp, sinlo, sinhi, cos_t, sin_t)


def _hgrn_proj_kernel(h_ref, wt_ref, lb_ref, qb_ref, kb_ref, lf_ref, vb_ref, og_ref, sa_ref, sb_ref, *, row0):
    h = h_ref[...]
    dk = B_HEADS * B_DK
    dv = B_HEADS * B_DV
    dm = sa_ref.shape[-1]
    nb = 256
    lb_all = lb_ref[...]

    def z(base, i):
        return _nt(h, wt_ref[row0 + base + i * nb:row0 + base + (i + 1) * nb, :])

    for i in range(dk // nb):
        cols = slice(i * nb, (i + 1) * nb)
        zq = z(0, i)
        qb_ref[:, cols] = zq * jax.nn.sigmoid(zq)
        sf, snf = _sig_pair(z(dk, i))
        lb = lb_all[:, cols]
        lf_ref[:, cols] = jnp.log(lb + (1.0 - lb) * sf)
        kb_ref[:, cols] = (1.0 - lb) * snf
    for i in range(dv // nb):
        cols = slice(i * nb, (i + 1) * nb)
        vb_ref[:, cols] = z(2 * dk, i).astype(BF16)
        zg = z(2 * dk + dv, i)
        og_ref[:, cols] = (zg * jax.nn.sigmoid(zg)).astype(BF16)
    for i in range(dm // nb):
        cols = slice(i * nb, (i + 1) * nb)
        sa_ref[:, cols] = jax.nn.sigmoid(z(2 * dk + 2 * dv, i)).astype(BF16)
        sb_ref[:, cols] = jax.nn.sigmoid(z(2 * dk + 2 * dv + dm, i)).astype(BF16)


def _hgrn_proj(h, win_b, lb, tm):
    b, l, d = h.shape
    dk = B_HEADS * B_DK
    dv = B_HEADS * B_DV
    tok = lambda w: pl.BlockSpec((None, tm, w), lambda bi, i: (bi, i, 0))
    sds = lambda w, dt: jax.ShapeDtypeStruct((b, l, w), dt)
    return pl.pallas_call(
        functools.partial(_hgrn_proj_kernel, row0=Q_LORA + KV_LORA + ROPE_DIM),
        grid=(b, l // tm),
        in_specs=[tok(d), _const_spec(win_b.shape), _const_spec(lb.shape)],
        out_specs=[tok(dk), tok(dk), tok(dk), tok(dv), tok(dv), tok(d), tok(d)],
        out_shape=[sds(dk, F32), sds(dk, F32), sds(dk, F32), sds(dv, BF16), sds(dv, BF16),
                   sds(d, BF16), sds(d, BF16)],
        compiler_params=_params(("parallel", "parallel")),
        name="hgrn_proj",
    )(h, win_b, lb)


def _attn_kernel(qt_ref, k_ref, vt_ref, o_ref, s_sc, m_sc, acc_sc, *, t, tc, hp):
    qi = pl.program_id(2)
    n_chains = t // tc
    m_sc[...] = jnp.full(m_sc.shape, NEG_INF, F32)
    acc_sc[...] = jnp.zeros(acc_sc.shape, F32)
    qk_rows = [slice(hd * HEAD_PAD, (hd + 1) * HEAD_PAD) for hd in range(hp)]
    v_rows = [slice(hd * V_AUG, (hd + 1) * V_AUG) for hd in range(hp)]

    def scores(kb, slot):
        start = pl.multiple_of(kb * t, t)
        for hd in range(hp):
            k = k_ref[pl.ds(start, t), qk_rows[hd]]
            for ch in range(n_chains):
                cols = slice(ch * tc, (ch + 1) * tc)
                s_sc[hd, slot, :, cols] = _dot(k, qt_ref[qk_rows[hd], cols])

    def consume(kb, slot, masked):
        start = pl.multiple_of(kb * t, t)
        for hd in range(hp):
            for ch in range(n_chains):
                cols = slice(ch * tc, (ch + 1) * tc)
                nk = (ch + 1) * tc if masked else t
                vt = vt_ref[v_rows[hd], pl.ds(start, nk)]
                s = s_sc[hd, slot, 0:nk, cols]
                if masked:
                    r = lax.broadcasted_iota(jnp.int32, (nk, tc), 0) // CHUNK
                    c = (lax.broadcasted_iota(jnp.int32, (nk, tc), 1) + ch * tc) // CHUNK
                    s = jnp.where(r <= c, s, NEG_INF)
                m_prev = m_sc[hd, :, cols]
                m_new = jnp.maximum(m_prev, jnp.max(s, axis=0, keepdims=True))
                alpha = jnp.exp2(m_prev - m_new)
                p = jnp.exp2(s - m_new)
                acc_sc[hd, :, cols] = alpha * acc_sc[hd, :, cols] + _dot(vt, p.astype(BF16))
                m_sc[hd, :, cols] = m_new

    scores(0, 0)

    def body(j, carry):
        kb = 2 * j
        scores(kb + 1, 1)
        consume(kb, 0, False)
        scores(kb + 2, 0)
        consume(kb + 1, 1, False)
        return carry

    lax.fori_loop(0, qi // 2, body, 0)

    @pl.when(qi % 2 == 0)
    def _():
        consume(qi, 0, True)

    @pl.when(qi % 2 == 1)
    def _():
        scores(qi, 1)
        consume(qi - 1, 0, False)
        consume(qi, 1, True)

    for hd in range(hp):
        acc = acc_sc[hd]
        o_ref[hd * V_DIM:(hd + 1) * V_DIM, :] = (acc[:V_DIM] / acc[V_DIM:V_DIM + 1]).astype(BF16)


def _attn(qt, k, vt, t, tc, hp):
    b, l, _ = k.shape
    return pl.pallas_call(
        functools.partial(_attn_kernel, t=t, tc=min(tc, t), hp=hp),
        grid=(b, A_HEADS // hp, l // t),
        in_specs=[pl.BlockSpec((None, hp * HEAD_PAD, t), lambda bi, h, i: (bi, h, i)),
                  pl.BlockSpec((None, l, hp * HEAD_PAD), lambda bi, h, i: (bi, 0, h)),
                  pl.BlockSpec((None, hp * V_AUG, l), lambda bi, h, i: (bi, h, 0))],
        out_specs=pl.BlockSpec((None, hp * V_DIM, t), lambda bi, h, i: (bi, h, i)),
        out_shape=jax.ShapeDtypeStruct((b, A_HEADS * V_DIM, l), BF16),
        scratch_shapes=[pltpu.VMEM((hp, 2, t, t), F32), pltpu.VMEM((hp, 1, t), F32),
                        pltpu.VMEM((hp, V_AUG, t), F32)],
        compiler_params=_params(("parallel", "parallel", "arbitrary")),
        name="attn",
    )(qt, k, vt)


def _sample_attn_kernel(q_ref, cc_ref, ckt_ref, nc_ref, nk_ref, wq2l_ref, esel_ref, wuv_ref, o_ref,
                        *, past, n_new):
    qs = q_ref[...]
    heads = [qs[:, hd * HEAD_PAD:(hd + 1) * HEAD_PAD] for hd in range(A_HEADS)]
    ql = jnp.concatenate([_dot(heads[hd], wq2l_ref[hd]) for hd in range(A_HEADS)], axis=0).astype(BF16)
    qp = jnp.concatenate([_dot(heads[hd], esel_ref[...]) for hd in range(A_HEADS)], axis=0).astype(BF16)
    cc = cc_ref[...].astype(BF16)
    ckt = ckt_ref[...].astype(BF16)
    nc = nc_ref[...].astype(BF16)
    nk = nk_ref[...].astype(BF16)
    s_c = _nt(ql, cc) + _dot(qp, ckt)
    s_n = _nt(ql, nc) + _nt(qp, nk)
    rows = A_HEADS * n_new
    tq = lax.broadcasted_iota(jnp.int32, (rows, n_new), 0) % n_new
    tk = lax.broadcasted_iota(jnp.int32, (rows, n_new), 1)
    s_n = jnp.where((past + tk) // CHUNK <= (past + tq) // CHUNK, s_n, NEG_INF)
    m = jnp.maximum(jnp.max(s_c, axis=-1, keepdims=True), jnp.max(s_n, axis=-1, keepdims=True))
    p_c = jnp.exp2(s_c - m)
    p_n = jnp.exp2(s_n - m)
    denom = jnp.sum(p_c, axis=-1, keepdims=True) + jnp.sum(p_n, axis=-1, keepdims=True)
    o_lat = ((_dot(p_c.astype(BF16), cc) + _dot(p_n.astype(BF16), nc)) / denom).astype(BF16)
    full = _dot(o_lat, wuv_ref[...])
    col_head = lax.broadcasted_iota(jnp.int32, (n_new, A_HEADS * V_DIM), 1) // V_DIM
    out = jnp.zeros((n_new, A_HEADS * V_DIM), F32)
    for hd in range(A_HEADS):
        out = out + jnp.where(col_head == hd, full[hd * n_new:(hd + 1) * n_new], 0.0)
    o_ref[...] = out.astype(BF16)


def _sample_attn(q, cache_ckv, cache_kpe_t, ckv_new, kpe_new, wq2l, esel, wuvp):
    nb, n_new, hq = q.shape
    past = cache_ckv.shape[1]
    hv = A_HEADS * V_DIM
    per = lambda r, w: pl.BlockSpec((None, r, w), lambda bi: (bi, 0, 0))
    return pl.pallas_call(
        functools.partial(_sample_attn_kernel, past=past, n_new=n_new),
        grid=(nb,),
        in_specs=[per(n_new, hq), per(past, KV_LORA), per(ROPE_DIM, past), per(n_new, KV_LORA),
                  per(n_new, ROPE_DIM), _const_spec(wq2l.shape), _const_spec(esel.shape),
                  _const_spec(wuvp.shape)],
        out_specs=per(n_new, hv),
        out_shape=jax.ShapeDtypeStruct((nb, n_new, hv), BF16),
        compiler_params=_params(("parallel",)),
        name="sample_attn",
    )(q, cache_ckv, cache_kpe_t, ckv_new, kpe_new, wq2l, esel, wuvp)


def _level_sizes(c):
    sizes = []
    p = c
    while p >= 2:
        sizes.append(p)
        p //= 2
    return sizes


def _level_masks(c):
    t = np.arange(c)[:, None]
    s = np.arange(c)[None, :]
    out = []
    for p in _level_sizes(c):
        out.append((t // p == s // p) & (t % p >= p // 2) & (s % p < p // 2))
    out.append(t == s)
    return np.stack(out).astype(np.float32)


def _boundary(cum, p, c):
    half = p // 2
    if p >= 8:
        parts = [jnp.broadcast_to(cum[i * p + half - 1:i * p + half, :], (p, cum.shape[1]))
                 for i in range(c // p)]
        return jnp.concatenate(parts, axis=0) if len(parts) > 1 else parts[0]
    sub = lax.broadcasted_iota(jnp.int32, (8, cum.shape[1]), 0)
    parts = []
    for g in range(c // 8):
        lo = jnp.broadcast_to(cum[g * 8 + 1:g * 8 + 2, :], (8, cum.shape[1]))
        hi = jnp.broadcast_to(cum[g * 8 + 5:g * 8 + 6, :], (8, cum.shape[1]))
        parts.append(jnp.where(sub < 4, lo, hi))
    return jnp.concatenate(parts, axis=0)


def _hgrn_kernel(q_ref, k_ref, lf_ref, v_ref, og_ref, s0t_ref, gon_ref, mask_ref, tri_ref,
                 o_ref, snt_ref, st_sc, *, c, n_chunks, per_chunk_state):
    step = pl.program_id(2)
    sizes = _level_sizes(c)
    lane = lax.broadcasted_iota(jnp.int32, (1, LANES), 1)
    low = lane < B_DV
    keep_lo = jnp.where(low, 1.0, 0.0).astype(BF16)
    keep_hi = jnp.where(low, 0.0, 1.0).astype(BF16)

    if not per_chunk_state:
        @pl.when(step == 0)
        def _():
            st_sc[...] = jnp.zeros(st_sc.shape, F32)
            st_sc[0, 0:B_DV, :] = s0t_ref[0]
            st_sc[1, B_DV:2 * B_DV, :] = s0t_ref[1]

    tri = tri_ref[...]
    rows = [slice(ci * c, (ci + 1) * c) for ci in range(n_chunks)]
    heads = [slice(j * B_DK, (j + 1) * B_DK) for j in range(2)]
    items = [(ci, j) for ci in range(n_chunks) for j in range(2)]


    cums = []
    for ci in range(n_chunks):
        lf = lf_ref[rows[ci], :]
        hi = lf.astype(BF16)
        r1 = lf - hi.astype(F32)
        mid = r1.astype(BF16)
        lo = (r1 - mid.astype(F32)).astype(BF16)
        cums.append((_dot(tri, hi) + _dot(tri, mid) + _dot(tri, lo)) * LOG2E)

    def zeros(n):
        return jnp.zeros((n, B_DK), F32)

    a_mats = {}
    for ci, j in items:
        q = q_ref[rows[ci], heads[j]]
        k = k_ref[rows[ci], heads[j]]
        cum = cums[ci][:, heads[j]]
        kbf = k.astype(BF16)
        a = _nt(q.astype(BF16), kbf) * mask_ref[len(sizes)]
        q_cols, k_cols = [], []
        for li, p in enumerate(sizes):
            half = p // 2
            if half % 8 == 0:
                for lo in range(0, c, p):
                    mid, hi = lo + half, lo + p
                    bnd = cum[mid - 1:mid, :]
                    qu = q[mid:hi] * jnp.exp2(cum[mid:hi] - bnd)
                    kl = k[lo:mid] * jnp.exp2(bnd - cum[lo:mid])
                    q_cols.append(jnp.concatenate([x for x in (zeros(mid), qu, zeros(c - hi)) if x.shape[0]], axis=0))
                    k_cols.append(jnp.concatenate([x for x in (zeros(lo), kl, zeros(c - mid)) if x.shape[0]], axis=0))
                continue
            if p == 2:
                qt = (q * jnp.exp2(lf_ref[rows[ci], heads[j]] * LOG2E)).astype(BF16)
                kt = kbf
            else:
                bnd = _boundary(cum, p, c)
                qt = (q * jnp.exp2(jnp.minimum(cum - bnd, 0.0))).astype(BF16)
                kt = (k * jnp.exp2(jnp.minimum(bnd - cum, 0.0))).astype(BF16)
            a = a + _nt(qt, kt) * mask_ref[li]
        if q_cols:
            a = a + _nt(jnp.concatenate(q_cols, axis=1).astype(BF16), jnp.concatenate(k_cols, axis=1).astype(BF16))
        a_mats[ci, j] = a.astype(BF16)

    qhats, kvs, decays, vs = {}, {}, {}, {}
    for ci, j in items:
        cum = cums[ci][:, heads[j]]
        last = cum[c - 1:c, :]
        vs[ci, j] = v_ref[rows[ci], :] * (keep_lo if j == 0 else keep_hi)
        qhats[ci, j] = (q_ref[rows[ci], heads[j]] * jnp.exp2(cum)).astype(BF16)
        khat = (k_ref[rows[ci], heads[j]] * jnp.exp2(last - cum)).astype(BF16)
        kvs[ci, j] = _tn(vs[ci, j], khat)
        decays[ci, j] = jnp.exp2(last)

    pad = jnp.zeros((B_DV, B_DK), F32)
    if not per_chunk_state:
        st = [st_sc[0], st_sc[1]]
    for ci in range(n_chunks):
        if per_chunk_state:
            st = [jnp.concatenate([s0t_ref[ci, 0], pad], axis=0), jnp.concatenate([pad, s0t_ref[ci, 1]], axis=0)]
        o_pair = jnp.zeros((c, LANES), F32)
        for j in range(2):
            o_pair = o_pair + _dot(a_mats[ci, j], vs[ci, j]) + _nt(qhats[ci, j], st[j].astype(BF16))
            st[j] = st[j] * decays[ci, j] + kvs[ci, j]
        if per_chunk_state:
            snt_ref[ci, 0] = st[0][0:B_DV, :]
            snt_ref[ci, 1] = st[1][B_DV:2 * B_DV, :]
        o2 = o_pair * o_pair
        s_lo = jnp.sum(jnp.where(low, o2, 0.0), axis=-1, keepdims=True)
        s_hi = jnp.sum(jnp.where(low, 0.0, o2), axis=-1, keepdims=True)
        ms = jnp.where(low, s_lo, s_hi) * (1.0 / B_DV)
        ob = o_pair * lax.rsqrt(ms + EPS) * gon_ref[...] * og_ref[rows[ci], :].astype(F32)
        o_ref[rows[ci], :] = ob.astype(BF16)
    if not per_chunk_state:
        st_sc[0] = st[0]
        st_sc[1] = st[1]

        @pl.when(step == pl.num_programs(2) - 1)
        def _():
            snt_ref[0] = st_sc[0, 0:B_DV, :]
            snt_ref[1] = st_sc[1, B_DV:2 * B_DV, :]


def _hgrn(qb, kb, lf, vb, og, s0t, gon, n_streams, tc):
    b, l, dk = qb.shape
    per_chunk_state = n_streams != b
    if per_chunk_state:
        assert b == 1 and l % n_streams == 0 and l // n_streams <= CHUNK
        c = l // n_streams
    else:
        c = min(CHUNK, l)
    assert l % tc == 0 and tc % c == 0
    n_chunks = tc // c
    masks = jnp.asarray(_level_masks(c))
    tri = jnp.asarray(np.tril(np.ones((c, c), np.float32)), dtype=BF16)
    pairs = B_HEADS // 2
    wide = pl.BlockSpec((None, tc, 2 * B_DK), lambda bi, p, i: (bi, i, p))
    narrow = pl.BlockSpec((None, tc, 2 * B_DV), lambda bi, p, i: (bi, i, p))
    if per_chunk_state:
        state = pl.BlockSpec((n_chunks, 2, B_DV, B_DK), lambda bi, p, i: (i, p, 0, 0))
    else:
        state = pl.BlockSpec((None, 2, B_DV, B_DK), lambda bi, p, i: (bi, p, 0, 0))
    return pl.pallas_call(
        functools.partial(_hgrn_kernel, c=c, n_chunks=n_chunks, per_chunk_state=per_chunk_state),
        grid=(b, pairs, l // tc),
        in_specs=[wide, wide, wide, narrow, narrow, state, _const_spec(gon.shape),
                  _const_spec(masks.shape), _const_spec(tri.shape)],
        out_specs=[narrow, state],
        out_shape=[jax.ShapeDtypeStruct((b, l, B_HEADS * B_DV), BF16),
                   jax.ShapeDtypeStruct((n_streams, B_HEADS, B_DV, B_DK), F32)],
        scratch_shapes=[pltpu.VMEM((2, 2 * B_DV, B_DK), F32)],
        compiler_params=_params(("parallel", "parallel", "parallel" if per_chunk_state else "arbitrary")),
        name="hgrn",
    )(qb, kb, lf, vb, og, s0t, gon, masks, tri)


MOE_SUB = 144
MOE_ROWS = 1024
_BIG_LANE = float(1 << 20)


def _max_items(tm):
    return (tm + N_GROUPS * (MOE_SUB - 1)) // MOE_SUB


def _first_lane_of_max(x, lane):
    v = jnp.max(x, axis=-1, keepdims=True)
    return v, jnp.min(jnp.where(x == v, lane, _BIG_LANE), axis=-1, keepdims=True)


def _out_proj_kernel(oa_ref, ob_ref, sa_ref, sb_ref, x_ref, wa_ref, wb_ref, wo_ref, gffn_ref,
                     wr_ref, br_ref, tri_ref, upper_ref, x1_ref, xs_ref, slot_ref, meta_ref, *, oa_transposed):
    tm = x_ref.shape[0]
    oa, ob = oa_ref[...], ob_ref[...]
    nb = 256
    m = []
    for i in range(x_ref.shape[1] // nb):
        cols = slice(i * nb, (i + 1) * nb)
        ya = _tn(oa, wa_ref[:, cols]) if oa_transposed else _dot(oa, wa_ref[:, cols])
        yb = _dot(ob, wb_ref[:, cols])
        m.append((sa_ref[:, cols].astype(F32) * ya + sb_ref[:, cols].astype(F32) * yb).astype(BF16))
    x1 = x_ref[...] + _dot(jnp.concatenate(m, axis=-1), wo_ref[...])
    x1_ref[...] = x1
    h2 = _rms(x1, gffn_ref[...]).astype(BF16)

    logits = _dot(h2, wr_ref[...]) + br_ref[...]
    lane = lax.broadcasted_iota(jnp.int32, (tm, LANES), 1).astype(F32)
    _, grp = _first_lane_of_max(jnp.where(lane < N_GROUPS, logits, NEG_INF), lane)
    onehot = jnp.where(lane == grp, 1.0, 0.0)
    rank = _dot(tri_ref[...], onehot.astype(BF16))
    count = rank[tm - 1:tm, :]
    items = jnp.zeros_like(count)
    for j in range(_max_items(tm)):
        items = items + jnp.where(count > j * MOE_SUB, 1.0, 0.0)
    base = _dot(jnp.broadcast_to(items, (8, LANES)).astype(BF16), upper_ref[...])[0:1, :] * MOE_SUB
    slot = jnp.sum(onehot * (base + rank - 1.0), axis=-1, keepdims=True)
    col = lax.broadcasted_iota(jnp.int32, (tm, MOE_ROWS), 1).astype(F32)
    place = jnp.where(col == slot, 1.0, 0.0).astype(BF16)
    xs_ref[...] = _tn(place, h2).astype(BF16)
    slot_ref[...] = jnp.broadcast_to(slot, (tm, LANES))
    meta_ref[...] = jnp.broadcast_to(items, (8, LANES))


def _out_proj(oa, ob, sa, sb, x, wa, wb, wo, gffn, wr, br, tm, oa_transposed):
    b, l, d = x.shape
    nt = l // tm
    assert _max_items(tm) * MOE_SUB <= MOE_ROWS
    hv = A_HEADS * V_DIM
    tri = jnp.asarray(np.tril(np.ones((tm, tm), np.float32)), dtype=BF16)
    upper = jnp.asarray(np.triu(np.ones((LANES, LANES), np.float32), 1), dtype=BF16)
    tok = lambda w: pl.BlockSpec((None, tm, w), lambda bi, i: (bi, i, 0))
    per_tile = lambda r, w: pl.BlockSpec((None, r, w), lambda bi, i: (bi * nt + i, 0, 0))
    oa_spec = pl.BlockSpec((None, hv, tm), lambda bi, i: (bi, 0, i)) if oa_transposed else tok(hv)
    return pl.pallas_call(
        functools.partial(_out_proj_kernel, oa_transposed=oa_transposed),
        grid=(b, nt),
        in_specs=[oa_spec, tok(B_HEADS * B_DV), tok(d), tok(d), tok(d), _const_spec(wa.shape),
                  _const_spec(wb.shape), _const_spec(wo.shape), _const_spec(gffn.shape),
                  _const_spec(wr.shape), _const_spec(br.shape), _const_spec(tri.shape),
                  _const_spec(upper.shape)],
        out_specs=[tok(d), per_tile(MOE_ROWS, d), tok(LANES), per_tile(8, LANES)],
        out_shape=[jax.ShapeDtypeStruct((b, l, d), F32),
                   jax.ShapeDtypeStruct((b * nt, MOE_ROWS, d), BF16),
                   jax.ShapeDtypeStruct((b, l, LANES), F32),
                   jax.ShapeDtypeStruct((b * nt, 8, LANES), F32)],
        compiler_params=_params(("parallel", "parallel")),
        name="out_proj",
    )(oa, ob, sa, sb, x, wa, wb, wo, gffn, wr, br, tri, upper)


def _work_list(meta, n_items):
    n = meta[:, 0, :N_GROUPS].astype(jnp.int32)
    tiles = n.shape[0]
    first = jnp.cumsum(n, axis=1) - n
    per_seg = n.T.reshape(-1)
    ends = jnp.cumsum(per_seg)
    total = ends[-1]
    i = jnp.minimum(jnp.arange(n_items, dtype=jnp.int32), total - 1)
    seg = jnp.sum((ends[None, :] <= i[:, None]).astype(jnp.int32), axis=1)
    grp, tile = seg // tiles, seg % tiles
    block = first[tile, grp] + i - (ends[seg] - per_seg[seg])
    return tile, block.astype(jnp.int32), grp, total[None].astype(jnp.int32)


def _moe_expert_kernel(tile_ref, block_ref, grp_ref, total_ref, xs_ref, wr_ref, br_ref, wg_ref, wu_ref,
                       wd_ref, ys_ref):
    i = pl.program_id(0)

    @pl.when(i < total_ref[0])
    def _():
        g = grp_ref[i]
        x = xs_ref[...]
        logits = _dot(x, wr_ref[...]) + br_ref[...]
        lane_i = lax.broadcasted_iota(jnp.int32, logits.shape, 1)
        lane = lane_i.astype(F32)
        is_grp = jnp.abs(lane - (EXPERTS_PER_GROUP + 0.5 * (N_GROUPS - 1))) < 0.5 * N_GROUPS
        lg = jnp.where(is_grp, logits, NEG_INF)
        gmax = jnp.max(lg, axis=-1, keepdims=True)
        denom = jnp.sum(jnp.exp(lg - gmax), axis=-1, keepdims=True)
        lg_own = jnp.sum(jnp.where(lane_i == EXPERTS_PER_GROUP + g, logits, 0.0), axis=-1, keepdims=True)
        p_grp = jnp.exp(lg_own - gmax) / denom
        le = jnp.where(lane < EXPERTS_PER_GROUP, logits, NEG_INF)
        v1, i1 = _first_lane_of_max(le, lane)
        le2 = jnp.where(lane == i1, NEG_INF, le)
        v2, i2 = _first_lane_of_max(le2, lane)
        e2 = jnp.exp(v2 - v1)
        w1 = p_grp / (1.0 + e2)
        comb = jnp.where(lane == i1, w1, 0.0) + jnp.where(lane == i2, w1 * e2, 0.0)
        hid = []
        for e in range(EXPERTS_PER_GROUP):
            a = _dot(x, wg_ref[e])
            u = _dot(x, wu_ref[e])
            s, _ = _sig_pair(a)
            hid.append((a * s * u * comb[:, e:e + 1]).astype(BF16))
        hid = jnp.concatenate(hid, axis=-1)
        wd = wd_ref[...].reshape(EXPERTS_PER_GROUP * D_FF_EXPERT, wd_ref.shape[-1])
        ys_ref[...] = _dot(hid, wd).astype(BF16)


def _moe_experts(xs, work, wrg, brg, wg, wu, wd):
    tiles, _, d = xs.shape
    tile, block, grp, total = work
    item = lambda i, t, k, g, n: (t[i], k[i], 0)
    by_group = lambda i, t, k, g, n: (g[i], 0, 0)
    grid_spec = pltpu.PrefetchScalarGridSpec(
        num_scalar_prefetch=4,
        grid=(tile.shape[0],),
        in_specs=[pl.BlockSpec((None, MOE_SUB, d), item),
                  pl.BlockSpec((None, d, LANES), by_group),
                  pl.BlockSpec((None, 1, LANES), by_group),
                  pl.BlockSpec((EXPERTS_PER_GROUP, d, D_FF_EXPERT), by_group),
                  pl.BlockSpec((EXPERTS_PER_GROUP, d, D_FF_EXPERT), by_group),
                  pl.BlockSpec((EXPERTS_PER_GROUP, D_FF_EXPERT, d), by_group)],
        out_specs=pl.BlockSpec((None, MOE_SUB, d), item))
    return pl.pallas_call(
        _moe_expert_kernel,
        grid_spec=grid_spec,
        out_shape=jax.ShapeDtypeStruct(xs.shape, BF16),
        input_output_aliases={4: 0},
        compiler_params=_params(("arbitrary",)),
        name="moe_experts",
    )(tile, block, grp, total, xs, wrg, brg, wg, wu, wd)


def _moe_combine_kernel(ys_ref, slot_ref, x1_ref, gfin_ref, y_ref):
    tm = x1_ref.shape[0]
    col = lax.broadcasted_iota(jnp.int32, (tm, MOE_ROWS), 1).astype(F32)
    slot = jnp.concatenate([slot_ref[...]] * (MOE_ROWS // LANES), axis=-1)
    place = jnp.where(col == slot, 1.0, 0.0).astype(BF16)
    y_ref[...] = _rms(x1_ref[...] + _dot(place, ys_ref[...]), gfin_ref[...])


def _moe_combine(ys, slot, x1, gfin, tm):
    b, l, d = x1.shape
    nt = l // tm
    tok = lambda w: pl.BlockSpec((None, tm, w), lambda bi, i: (bi, i, 0))
    per_tile = lambda r, w: pl.BlockSpec((None, r, w), lambda bi, i: (bi * nt + i, 0, 0))
    return pl.pallas_call(
        _moe_combine_kernel,
        grid=(b, nt),
        in_specs=[per_tile(MOE_ROWS, d), tok(LANES), tok(d), _const_spec(gfin.shape)],
        out_specs=tok(d),
        out_shape=jax.ShapeDtypeStruct((b, l, d), F32),
        compiler_params=_params(("parallel", "parallel")),
        name="moe_combine",
    )(ys, slot, x1, gfin)


def _rope_tables(pos):
    inv = jnp.power(ROPE_THETA, -jnp.arange(HALF, dtype=F32) / HALF)
    ang = pos.astype(F32)[:, None] * inv[None, :]
    cos, sin = jnp.cos(ang), jnp.sin(ang)
    zeros = jnp.zeros_like(sin)
    reps = LANES // ROPE_DIM
    cosp = jnp.tile(jnp.concatenate([cos, cos], axis=1), (1, reps))
    sinlo = jnp.tile(jnp.concatenate([-sin, zeros], axis=1), (1, reps))
    sinhi = jnp.tile(jnp.concatenate([zeros, sin], axis=1), (1, reps))
    return cosp, sinlo, sinhi, cos.T * (SCALE * LOG2E), sin.T * (SCALE * LOG2E)


def _prep_weights(g_mix, w_in, g_q, w_uq, g_kv, w_ukv, lb_hgrn, g_onorm, w_a_out, w_b_out, w_o,
                  g_ffn, w_rg, b_rg, w_re, b_re, w_gate, w_up, w_down, g_final):
    assert w_in.shape[0] == 1, "single-layer step"
    d = w_in.shape[1]
    w = w_in[0]
    n_a = Q_LORA + KV_LORA
    win_a = jnp.concatenate([w[:, :n_a]] + [w[:, n_a:n_a + ROPE_DIM]] * (LANES // ROPE_DIM), axis=1).astype(BF16)
    win_b = jnp.swapaxes(w, 0, 1).astype(BF16)
    per_q = NOPE_DIM + ROPE_DIM
    wuq = jnp.pad(w_uq[0].reshape(Q_LORA, A_HEADS, per_q), ((0, 0), (0, 0), (0, HEAD_PAD - per_q)))
    wuq = wuq.reshape(Q_LORA, A_HEADS * HEAD_PAD).astype(BF16)
    wukv = w_ukv[0].reshape(KV_LORA, A_HEADS, NOPE_DIM + V_DIM)
    w_uk, w_uv = wukv[..., :NOPE_DIM], wukv[..., NOPE_DIM:]
    wuk = jnp.pad(w_uk, ((0, 0), (0, 0), (0, HEAD_PAD - NOPE_DIM))).reshape(KV_LORA, A_HEADS * HEAD_PAD).astype(BF16)
    wuvt = w_uv.reshape(KV_LORA, A_HEADS * V_DIM).T.astype(BF16)
    wq2l = jnp.pad(jnp.transpose(w_uk, (1, 2, 0)), ((0, 0), (0, HEAD_PAD - NOPE_DIM), (0, 0))).astype(BF16)
    esel = jnp.zeros((HEAD_PAD, ROPE_DIM), F32).at[ROPE_LO + jnp.arange(ROPE_DIM), jnp.arange(ROPE_DIM)].set(1.0).astype(BF16)
    wuvp = w_uv.reshape(KV_LORA, A_HEADS * V_DIM).astype(BF16)
    lb = jnp.cumsum(jax.nn.softmax(lb_hgrn.astype(F32), axis=0), axis=0)[0][None, :]
    gon = jnp.tile(g_onorm[0], 2)[None, :]
    wr = jnp.pad(w_rg[0], ((0, 0), (0, LANES - N_GROUPS))).astype(BF16)
    br = jnp.pad(b_rg[0], (0, LANES - N_GROUPS))[None, :]
    rest = LANES - EXPERTS_PER_GROUP - N_GROUPS
    wrg = jnp.concatenate([w_re[0], jnp.broadcast_to(w_rg[0], (N_GROUPS, d, N_GROUPS)),
                           jnp.zeros((N_GROUPS, d, rest), F32)], axis=2).astype(BF16)
    brg = jnp.concatenate([b_re[0], jnp.broadcast_to(b_rg[0], (N_GROUPS, N_GROUPS)),
                           jnp.zeros((N_GROUPS, rest), F32)], axis=1)[:, None, :]
    return dict(
        wrg=wrg, brg=brg,
        gmix=g_mix[0][None, :], win_a=win_a, win_b=win_b, gq=g_q[0][None, :], wuq=wuq, wuqt=wuq.T,
        gkv=g_kv[0][None, :], wuk=wuk, wuvt=wuvt, wq2l=wq2l, esel=esel, wuvp=wuvp, lb=lb, gon=gon,
        wa=w_a_out[0].astype(BF16), wb=w_b_out[0].astype(BF16), wo=w_o[0].astype(BF16),
        gffn=g_ffn[0][None, :], wr=wr, br=br, wg=w_gate[0].astype(BF16), wu=w_up[0].astype(BF16),
        wd=w_down[0].astype(BF16), gfin=g_final[None, :])


def _tile(n, want):
    t = min(n, want)
    while n % t:
        t //= 2
    return t


def _run_group(x, pos, streams, past, w):
    b, l, d = x.shape
    ns, ls = streams
    tm = _tile(l, 512)
    h, ckv, kpe, q, k, vt = _mla_proj(x, w["gmix"], w["win_a"], w["gq"], w["wuq"], w["wuqt"], w["gkv"],
                                      w["wuk"], w["wuvt"], _rope_tables(pos), tm, past is None)
    qb, kb, lf, vb, og, sa, sb = _hgrn_proj(h, w["win_b"], w["lb"], tm)
    as_streams = lambda a: a.reshape(ns, ls, a.shape[-1])
    if past is None:
        oa = _attn(q, k, vt, _tile(l, 512), 256, 4)
        s0t = jnp.zeros((ns, B_HEADS, B_DV, B_DK), F32)
    else:
        cache_ckv, cache_kpe, state = past
        oa = _sample_attn(as_streams(q), cache_ckv, cache_kpe, as_streams(ckv), as_streams(kpe),
                          w["wq2l"], w["esel"], w["wuvp"]).reshape(b, l, A_HEADS * V_DIM)
        s0t = jnp.swapaxes(state.astype(F32), -1, -2)
    ob, snt = _hgrn(qb, kb, lf, vb, og, s0t, w["gon"], ns, _tile(l, 1024 if past is None else 8 * ls))
    x1, xs, slot, meta = _out_proj(oa, ob, sa, sb, x, w["wa"], w["wb"], w["wo"], w["gffn"], w["wr"],
                                   w["br"], tm, past is None)
    work = _work_list(meta, meta.shape[0] * _max_items(tm))
    ys = _moe_experts(xs, work, w["wrg"], w["brg"], w["wg"], w["wu"], w["wd"])
    y = _moe_combine(ys, slot, x1, w["gfin"], tm)
    return y, ckv, kpe, jnp.swapaxes(snt, -1, -2)


def kernel(x_prompt, x_sample, cache_ckv, cache_kpe, state_hgrn, g_mix, w_in, g_q, w_uq, g_kv, w_ukv,
           lb_hgrn, g_onorm, w_a_out, w_b_out, w_o, g_ffn, w_rg, b_rg, w_re, b_re, w_gate, w_up, w_down,
           g_final):
    w = _prep_weights(g_mix, w_in, g_q, w_uq, g_kv, w_ukv, lb_hgrn, g_onorm, w_a_out, w_b_out, w_o,
                      g_ffn, w_rg, b_rg, w_re, b_re, w_gate, w_up, w_down, g_final)
    bp, lp, d = x_prompt.shape
    y_p, ckv_p, kpe_p, st_p = _run_group(x_prompt, jnp.arange(lp, dtype=jnp.int32), (bp, lp), None, w)

    bs, ls, _ = x_sample.shape
    past_len = cache_ckv.shape[2]
    pos_s = past_len + (jnp.arange(bs * ls, dtype=jnp.int32) % ls)
    y_s, ckv_s, kpe_s, st_s = _run_group(x_sample.reshape(1, bs * ls, d), pos_s, (bs, ls),
                                         (cache_ckv[0], jnp.swapaxes(cache_kpe[0], 1, 2), state_hgrn[0]), w)
    return (y_p, y_s.reshape(bs, ls, d),
            ckv_p[None], kpe_p[None], st_p[None].astype(x_prompt.dtype),
            ckv_s.reshape(1, bs, ls, KV_LORA), kpe_s.reshape(1, bs, ls, ROPE_DIM),
            st_s[None].astype(state_hgrn.dtype))
```

```python
import functools

import numpy as np
import jax
import jax.numpy as jnp
from jax import lax
from jax.experimental import pallas as pl
from jax.experimental.pallas import tpu as pltpu

F32 = jnp.float32
BF16 = jnp.bfloat16

EPS = 1e-6
CHUNK = 64
A_HEADS = 8
Q_LORA = 384
KV_LORA = 256
NOPE_DIM = 64
ROPE_DIM = 32
V_DIM = 64
ROPE_THETA = 10000.0
B_HEADS = 8
B_DK = 128
B_DV = 64
N_GROUPS = 4
EXPERTS_PER_GROUP = 8
N_EXPERTS = N_GROUPS * EXPERTS_PER_GROUP
D_FF_EXPERT = 256

LANES = 128
VMEM_BYTES_V7X = 64 * 1024 * 1024
VMEM_LIMIT = VMEM_BYTES_V7X - 8 * 1024 * 1024

HEAD_PAD = LANES
V_AUG = V_DIM + 16
SAMPLE_KEY_BLOCK = 1024
ROPE_LO = NOPE_DIM
ROPE_HI = NOPE_DIM + ROPE_DIM
HALF = ROPE_DIM // 2
SCALE = (NOPE_DIM + ROPE_DIM) ** -0.5
LOG2E = 1.4426950408889634
NEG_INF = float("-inf")


def _params(sem):
    return pltpu.CompilerParams(dimension_semantics=sem, vmem_limit_bytes=VMEM_LIMIT)


def _const_spec(shape):
    nd = len(shape)
    return pl.BlockSpec(shape, lambda *_: (0,) * nd, pipeline_mode=pl.Buffered(1))


def _rms(x, g):
    ms = jnp.mean(x * x, axis=-1, keepdims=True)
    return x * lax.rsqrt(ms + EPS) * g


def _sig_pair(x):
    e = jnp.exp(-jnp.abs(x))
    r = 1.0 / (1.0 + e)
    er = e * r
    pos = x >= 0
    return jnp.where(pos, r, er), jnp.where(pos, er, r)


def _nt(a, b):
    return lax.dot_general(a, b, (((1,), (1,)), ((), ())), preferred_element_type=F32)


def _tn(a, b):
    return lax.dot_general(a, b, (((0,), (0,)), ((), ())), preferred_element_type=F32)


def _dot(a, b):
    return jnp.dot(a, b, preferred_element_type=F32)


def _rope(x, cosp, sinlo, sinhi):
    return x * cosp + pltpu.roll(x, LANES - HALF, 1) * sinlo + pltpu.roll(x, HALF, 1) * sinhi


def _mla_proj_kernel(x_ref, gmix_ref, win_ref, gq_ref, wuq_ref, wuqt_ref, gkv_ref, wuk_ref, wuvt_ref,
                     cos_ref, sinlo_ref, sinhi_ref, cost_ref, sint_ref,
                     h_ref, ckv_ref, kpe_ref, q_ref, k_ref, vt_ref, *, q_transposed):
    h = _rms(x_ref[...], gmix_ref[...]).astype(BF16)
    h_ref[...] = h
    z = _dot(h, win_ref[...])
    cosp, sinlo, sinhi = cos_ref[...], sinlo_ref[...], sinhi_ref[...]
    lane = lax.broadcasted_iota(jnp.int32, (1, LANES), 1)
    rope_lanes = (lane // ROPE_DIM) == (ROPE_LO // ROPE_DIM)

    cqn = _rms(z[:, :Q_LORA], gq_ref[...]).astype(BF16)
    if q_transposed:
        qt = _nt(wuqt_ref[...], cqn)
        cos_t, sin_t = cost_ref[...], sint_ref[...]
        for hd in range(A_HEADS):
            r0 = hd * HEAD_PAD
            q_ref[r0:r0 + ROPE_LO, :] = (qt[r0:r0 + ROPE_LO] * (SCALE * LOG2E)).astype(BF16)
            a = qt[r0 + ROPE_LO:r0 + ROPE_LO + HALF]
            b = qt[r0 + ROPE_LO + HALF:r0 + ROPE_HI]
            q_ref[r0 + ROPE_LO:r0 + ROPE_HI, :] = jnp.concatenate(
                [a * cos_t - b * sin_t, a * sin_t + b * cos_t], axis=0).astype(BF16)
            q_ref[r0 + ROPE_HI:r0 + HEAD_PAD, :] = jnp.zeros((HEAD_PAD - ROPE_HI, qt.shape[1]), BF16)
    else:
        q = _dot(cqn, wuq_ref[...])
        for hd in range(A_HEADS):
            sl = slice(hd * HEAD_PAD, (hd + 1) * HEAD_PAD)
            qh = q[:, sl]
            qh = jnp.where(rope_lanes, _rope(qh, cosp, sinlo, sinhi), qh) * (SCALE * LOG2E)
            q_ref[:, sl] = qh.astype(BF16)

    ckv = _rms(z[:, Q_LORA:Q_LORA + KV_LORA], gkv_ref[...])
    ckv_ref[...] = ckv
    ckv_bf = ckv.astype(BF16)
    kpe_rot = _rope(z[:, Q_LORA + KV_LORA:], cosp, sinlo, sinhi)
    kpe_ref[...] = kpe_rot[:, :ROPE_DIM]
    kpe_placed = jnp.where(rope_lanes, kpe_rot, 0.0)
    kn = _dot(ckv_bf, wuk_ref[...])
    for hd in range(A_HEADS):
        sl = slice(hd * HEAD_PAD, (hd + 1) * HEAD_PAD)
        k_ref[:, sl] = (kn[:, sl] + kpe_placed).astype(BF16)
    vt = _nt(wuvt_ref[...], ckv_bf).astype(BF16)
    ones = jnp.ones((V_AUG - V_DIM, vt.shape[1]), BF16)
    for hd in range(A_HEADS):
        vt_ref[hd * V_AUG:hd * V_AUG + V_DIM, :] = vt[hd * V_DIM:(hd + 1) * V_DIM]
        vt_ref[hd * V_AUG + V_DIM:(hd + 1) * V_AUG, :] = ones


def _mla_proj(x, gmix, win_a, gq, wuq, wuqt, gkv, wuk, wuvt, tables, tm, q_transposed):
    b, l, d = x.shape
    hq = A_HEADS * HEAD_PAD
    hv = A_HEADS * V_AUG
    cosp, sinlo, sinhi, cos_t, sin_t = tables
    tok = lambda w: pl.BlockSpec((None, tm, w), lambda bi, i: (bi, i, 0))
    tok_t = lambda w: pl.BlockSpec((None, w, tm), lambda bi, i: (bi, 0, i))
    tab = pl.BlockSpec((tm, LANES), lambda bi, i: (i, 0))
    tab_t = pl.BlockSpec((HALF, tm), lambda bi, i: (0, i))
    return pl.pallas_call(
        functools.partial(_mla_proj_kernel, q_transposed=q_transposed),
        grid=(b, l // tm),
        in_specs=[tok(d), _const_spec(gmix.shape), _const_spec(win_a.shape), _const_spec(gq.shape),
                  _const_spec(wuq.shape), _const_spec(wuqt.shape), _const_spec(gkv.shape),
                  _const_spec(wuk.shape), _const_spec(wuvt.shape), tab, tab, tab, tab_t, tab_t],
        out_specs=[tok(d), tok(KV_LORA), tok(ROPE_DIM), tok_t(hq) if q_transposed else tok(hq), tok(hq),
                   tok_t(hv)],
        out_shape=[jax.ShapeDtypeStruct((b, l, d), BF16),
                   jax.ShapeDtypeStruct((b, l, KV_LORA), F32),
                   jax.ShapeDtypeStruct((b, l, ROPE_DIM), F32),
                   jax.ShapeDtypeStruct((b, hq, l) if q_transposed else (b, l, hq), BF16),
                   jax.ShapeDtypeStruct((b, l, hq), BF16),
                   jax.ShapeDtypeStruct((b, hv, l), BF16)],
        compiler_params=_params(("parallel", "parallel")),
        name="mla_proj",
    )(x, gmix, win_a, gq, wuq, wuqt, gkv, wuk, wuvt, cosp, sinlo, sinhi, cos_t, sin_t)


def _hgrn_proj_kernel(h_ref, wt_ref, lb_ref, qb_ref, kb_ref, lf_ref, vb_ref, og_ref, sa_ref, sb_ref, *, row0):
    h = h_ref[...]
    dk = B_HEADS * B_DK
    dv = B_HEADS * B_DV
    dm = sa_ref.shape[-1]
    nb = 256
    lb_all = lb_ref[...]

    def z(base, i):
        return _nt(h, wt_ref[row0 + base + i * nb:row0 + base + (i + 1) * nb, :])

    for i in range(dk // nb):
        cols = slice(i * nb, (i + 1) * nb)
        zq = z(0, i)
        qb_ref[:, cols] = zq * jax.nn.sigmoid(zq)
        sf, snf = _sig_pair(z(dk, i))
        lb = lb_all[:, cols]
        lf_ref[:, cols] = jnp.log(lb + (1.0 - lb) * sf)
        kb_ref[:, cols] = (1.0 - lb) * snf
    for i in range(dv // nb):
        cols = slice(i * nb, (i + 1) * nb)
        vb_ref[:, cols] = z(2 * dk, i).astype(BF16)
        zg = z(2 * dk + dv, i)
        og_ref[:, cols] = (zg * jax.nn.sigmoid(zg)).astype(BF16)
    for i in range(dm // nb):
        cols = slice(i * nb, (i + 1) * nb)
        sa_ref[:, cols] = jax.nn.sigmoid(z(2 * dk + 2 * dv, i)).astype(BF16)
        sb_ref[:, cols] = jax.nn.sigmoid(z(2 * dk + 2 * dv + dm, i)).astype(BF16)


def _hgrn_proj(h, win_b, lb, tm):
    b, l, d = h.shape
    dk = B_HEADS * B_DK
    dv = B_HEADS * B_DV
    tok = lambda w: pl.BlockSpec((None, tm, w), lambda bi, i: (bi, i, 0))
    sds = lambda w, dt: jax.ShapeDtypeStruct((b, l, w), dt)
    return pl.pallas_call(
        functools.partial(_hgrn_proj_kernel, row0=Q_LORA + KV_LORA + ROPE_DIM),
        grid=(b, l // tm),
        in_specs=[tok(d), _const_spec(win_b.shape), _const_spec(lb.shape)],
        out_specs=[tok(dk), tok(dk), tok(dk), tok(dv), tok(dv), tok(d), tok(d)],
        out_shape=[sds(dk, F32), sds(dk, F32), sds(dk, F32), sds(dv, BF16), sds(dv, BF16),
                   sds(d, BF16), sds(d, BF16)],
        compiler_params=_params(("parallel", "parallel")),
        name="hgrn_proj",
    )(h, win_b, lb)


def _attn_kernel(qt_ref, k_ref, vt_ref, o_ref, s_sc, m_sc, acc_sc, *, t, tc, hp):
    qi = pl.program_id(2)
    n_chains = t // tc
    m_sc[...] = jnp.full(m_sc.shape, NEG_INF, F32)
    acc_sc[...] = jnp.zeros(acc_sc.shape, F32)
    qk_rows = [slice(hd * HEAD_PAD, (hd + 1) * HEAD_PAD) for hd in range(hp)]
    v_rows = [slice(hd * V_AUG, (hd + 1) * V_AUG) for hd in range(hp)]

    def scores(kb, slot):
        start = pl.multiple_of(kb * t, t)
        for hd in range(hp):
            k = k_ref[pl.ds(start, t), qk_rows[hd]]
            for ch in range(n_chains):
                cols = slice(ch * tc, (ch + 1) * tc)
                s_sc[hd, slot, :, cols] = _dot(k, qt_ref[qk_rows[hd], cols])

    def consume(kb, slot, masked):
        start = pl.multiple_of(kb * t, t)
        for hd in range(hp):
            for ch in range(n_chains):
                cols = slice(ch * tc, (ch + 1) * tc)
                nk = (ch + 1) * tc if masked else t
                vt = vt_ref[v_rows[hd], pl.ds(start, nk)]
                s = s_sc[hd, slot, 0:nk, cols]
                if masked:
                    r = lax.broadcasted_iota(jnp.int32, (nk, tc), 0) // CHUNK
                    c = (lax.broadcasted_iota(jnp.int32, (nk, tc), 1) + ch * tc) // CHUNK
                    s = jnp.where(r <= c, s, NEG_INF)
                m_prev = m_sc[hd, :, cols]
                m_new = jnp.maximum(m_prev, jnp.max(s, axis=0, keepdims=True))
                alpha = jnp.exp2(m_prev - m_new)
                p = jnp.exp2(s - m_new)
                acc_sc[hd, :, cols] = alpha * acc_sc[hd, :, cols] + _dot(vt, p.astype(BF16))
                m_sc[hd, :, cols] = m_new

    scores(0, 0)

    def body(j, carry):
        kb = 2 * j
        scores(kb + 1, 1)
        consume(kb, 0, False)
        scores(kb + 2, 0)
        consume(kb + 1, 1, False)
        return carry

    lax.fori_loop(0, qi // 2, body, 0)

    @pl.when(qi % 2 == 0)
    def _():
        consume(qi, 0, True)

    @pl.when(qi % 2 == 1)
    def _():
        scores(qi, 1)
        consume(qi - 1, 0, False)
        consume(qi, 1, True)

    for hd in range(hp):
        acc = acc_sc[hd]
        o_ref[hd * V_DIM:(hd + 1) * V_DIM, :] = (acc[:V_DIM] / acc[V_DIM:V_DIM + 1]).astype(BF16)


def _attn(qt, k, vt, t, tc, hp):
    b, l, _ = k.shape
    return pl.pallas_call(
        functools.partial(_attn_kernel, t=t, tc=min(tc, t), hp=hp),
        grid=(b, A_HEADS // hp, l // t),
        in_specs=[pl.BlockSpec((None, hp * HEAD_PAD, t), lambda bi, h, i: (bi, h, i)),
                  pl.BlockSpec((None, l, hp * HEAD_PAD), lambda bi, h, i: (bi, 0, h)),
                  pl.BlockSpec((None, hp * V_AUG, l), lambda bi, h, i: (bi, h, 0))],
        out_specs=pl.BlockSpec((None, hp * V_DIM, t), lambda bi, h, i: (bi, h, i)),
        out_shape=jax.ShapeDtypeStruct((b, A_HEADS * V_DIM, l), BF16),
        scratch_shapes=[pltpu.VMEM((hp, 2, t, t), F32), pltpu.VMEM((hp, 1, t), F32),
                        pltpu.VMEM((hp, V_AUG, t), F32)],
        compiler_params=_params(("parallel", "parallel", "arbitrary")),
        name="attn",
    )(qt, k, vt)


def _sample_attn_kernel(q_ref, cc_ref, ckt_ref, nc_ref, nk_ref, wq2l_ref, esel_ref, wuv_ref, o_ref,
                        *, past, n_new):
    qs = q_ref[...]
    heads = [qs[:, hd * HEAD_PAD:(hd + 1) * HEAD_PAD] for hd in range(A_HEADS)]
    ql = jnp.concatenate([_dot(heads[hd], wq2l_ref[hd]) for hd in range(A_HEADS)], axis=0).astype(BF16)
    qp = jnp.concatenate([_dot(heads[hd], esel_ref[...]) for hd in range(A_HEADS)], axis=0).astype(BF16)
    nc = nc_ref[...].astype(BF16)
    nk = nk_ref[...].astype(BF16)
    rows = A_HEADS * n_new

    s_n = _nt(ql, nc) + _nt(qp, nk)
    tq = lax.broadcasted_iota(jnp.int32, (rows, n_new), 0) % n_new
    tk = lax.broadcasted_iota(jnp.int32, (rows, n_new), 1)
    s_n = jnp.where((past + tk) // CHUNK <= (past + tq) // CHUNK, s_n, NEG_INF)
    m = jnp.max(s_n, axis=-1, keepdims=True)
    p_n = jnp.exp2(s_n - m)
    denom = jnp.sum(p_n, axis=-1, keepdims=True)
    acc = _dot(p_n.astype(BF16), nc)

    n_blocks = max(1, past // SAMPLE_KEY_BLOCK)
    assert past % n_blocks == 0
    kb = past // n_blocks

    def scores(i):
        cc = cc_ref[i * kb:(i + 1) * kb, :].astype(BF16)
        ckt = ckt_ref[:, i * kb:(i + 1) * kb].astype(BF16)
        return _nt(ql, cc) + _dot(qp, ckt), cc

    nxt = scores(0)
    for i in range(n_blocks):
        s_c, cc = nxt
        if i + 1 < n_blocks:
            nxt = scores(i + 1)
        m_new = jnp.maximum(m, jnp.max(s_c, axis=-1, keepdims=True))
        alpha = jnp.exp2(m - m_new)
        p_c = jnp.exp2(s_c - m_new)
        denom = alpha * denom + jnp.sum(p_c, axis=-1, keepdims=True)
        acc = alpha * acc + _dot(p_c.astype(BF16), cc)
        m = m_new
    o_lat = (acc / denom).astype(BF16)
    full = _dot(o_lat, wuv_ref[...])
    col_head = lax.broadcasted_iota(jnp.int32, (n_new, A_HEADS * V_DIM), 1) // V_DIM
    out = jnp.zeros((n_new, A_HEADS * V_DIM), F32)
    for hd in range(A_HEADS):
        out = out + jnp.where(col_head == hd, full[hd * n_new:(hd + 1) * n_new], 0.0)
    o_ref[...] = out.astype(BF16)


def _sample_attn(q, cache_ckv, cache_kpe_t, ckv_new, kpe_new, wq2l, esel, wuvp):
    nb, n_new, hq = q.shape
    past = cache_ckv.shape[1]
    hv = A_HEADS * V_DIM
    per = lambda r, w: pl.BlockSpec((None, r, w), lambda bi: (bi, 0, 0))
    return pl.pallas_call(
        functools.partial(_sample_attn_kernel, past=past, n_new=n_new),
        grid=(nb,),
        in_specs=[per(n_new, hq), per(past, KV_LORA), per(ROPE_DIM, past), per(n_new, KV_LORA),
                  per(n_new, ROPE_DIM), _const_spec(wq2l.shape), _const_spec(esel.shape),
                  _const_spec(wuvp.shape)],
        out_specs=per(n_new, hv),
        out_shape=jax.ShapeDtypeStruct((nb, n_new, hv), BF16),
        compiler_params=_params(("parallel",)),
        name="sample_attn",
    )(q, cache_ckv, cache_kpe_t, ckv_new, kpe_new, wq2l, esel, wuvp)


def _level_sizes(c):
    sizes = []
    p = c
    while p >= 2:
        sizes.append(p)
        p //= 2
    return sizes


def _level_masks(c):
    t = np.arange(c)[:, None]
    s = np.arange(c)[None, :]
    out = []
    for p in _level_sizes(c):
        out.append((t // p == s // p) & (t % p >= p // 2) & (s % p < p // 2))
    out.append(t == s)
    return np.stack(out).astype(np.float32)


def _boundary(cum, p, c):
    half = p // 2
    if p >= 8:
        parts = [jnp.broadcast_to(cum[i * p + half - 1:i * p + half, :], (p, cum.shape[1]))
                 for i in range(c // p)]
        return jnp.concatenate(parts, axis=0) if len(parts) > 1 else parts[0]
    sub = lax.broadcasted_iota(jnp.int32, (8, cum.shape[1]), 0)
    parts = []
    for g in range(c // 8):
        lo = jnp.broadcast_to(cum[g * 8 + 1:g * 8 + 2, :], (8, cum.shape[1]))
        hi = jnp.broadcast_to(cum[g * 8 + 5:g * 8 + 6, :], (8, cum.shape[1]))
        parts.append(jnp.where(sub < 4, lo, hi))
    return jnp.concatenate(parts, axis=0)


def _hgrn_kernel(q_ref, k_ref, lf_ref, v_ref, og_ref, s0t_ref, gon_ref, mask_ref, tri_ref,
                 o_ref, snt_ref, st_sc, *, c, n_chunks, per_chunk_state):
    step = pl.program_id(2)
    sizes = _level_sizes(c)
    lane = lax.broadcasted_iota(jnp.int32, (1, LANES), 1)
    low = lane < B_DV
    keep_lo = jnp.where(low, 1.0, 0.0).astype(BF16)
    keep_hi = jnp.where(low, 0.0, 1.0).astype(BF16)

    if not per_chunk_state:
        @pl.when(step == 0)
        def _():
            st_sc[...] = jnp.zeros(st_sc.shape, F32)
            st_sc[0, 0:B_DV, :] = s0t_ref[0]
            st_sc[1, B_DV:2 * B_DV, :] = s0t_ref[1]

    tri = tri_ref[...]
    rows = [slice(ci * c, (ci + 1) * c) for ci in range(n_chunks)]
    heads = [slice(j * B_DK, (j + 1) * B_DK) for j in range(2)]
    items = [(ci, j) for ci in range(n_chunks) for j in range(2)]


    cums = []
    for ci in range(n_chunks):
        lf = lf_ref[rows[ci], :]
        hi = lf.astype(BF16)
        r1 = lf - hi.astype(F32)
        mid = r1.astype(BF16)
        lo = (r1 - mid.astype(F32)).astype(BF16)
        cums.append((_dot(tri, hi) + _dot(tri, mid) + _dot(tri, lo)) * LOG2E)

    def zeros(n):
        return jnp.zeros((n, B_DK), F32)

    a_mats = {}
    for ci, j in items:
        q = q_ref[rows[ci], heads[j]]
        k = k_ref[rows[ci], heads[j]]
        cum = cums[ci][:, heads[j]]
        kbf = k.astype(BF16)
        a = _nt(q.astype(BF16), kbf) * mask_ref[len(sizes)]
        q_cols, k_cols = [], []
        for li, p in enumerate(sizes):
            half = p // 2
            if half % 8 == 0:
                for lo in range(0, c, p):
                    mid, hi = lo + half, lo + p
                    bnd = cum[mid - 1:mid, :]
                    qu = q[mid:hi] * jnp.exp2(cum[mid:hi] - bnd)
                    kl = k[lo:mid] * jnp.exp2(bnd - cum[lo:mid])
                    q_cols.append(jnp.concatenate([x for x in (zeros(mid), qu, zeros(c - hi)) if x.shape[0]], axis=0))
                    k_cols.append(jnp.concatenate([x for x in (zeros(lo), kl, zeros(c - mid)) if x.shape[0]], axis=0))
                continue
            if p == 2:
                qt = (q * jnp.exp2(lf_ref[rows[ci], heads[j]] * LOG2E)).astype(BF16)
                kt = kbf
            else:
                bnd = _boundary(cum, p, c)
                qt = (q * jnp.exp2(jnp.minimum(cum - bnd, 0.0))).astype(BF16)
                kt = (k * jnp.exp2(jnp.minimum(bnd - cum, 0.0))).astype(BF16)
            a = a + _nt(qt, kt) * mask_ref[li]
        if q_cols:
            a = a + _nt(jnp.concatenate(q_cols, axis=1).astype(BF16), jnp.concatenate(k_cols, axis=1).astype(BF16))
        a_mats[ci, j] = a.astype(BF16)

    qhats, kvs, decays, vs = {}, {}, {}, {}
    for ci, j in items:
        cum = cums[ci][:, heads[j]]
        last = cum[c - 1:c, :]
        vs[ci, j] = v_ref[rows[ci], :] * (keep_lo if j == 0 else keep_hi)
        qhats[ci, j] = (q_ref[rows[ci], heads[j]] * jnp.exp2(cum)).astype(BF16)
        khat = (k_ref[rows[ci], heads[j]] * jnp.exp2(last - cum)).astype(BF16)
        kvs[ci, j] = _tn(vs[ci, j], khat)
        decays[ci, j] = jnp.exp2(last)

    pad = jnp.zeros((B_DV, B_DK), F32)
    if not per_chunk_state:
        st = [st_sc[0], st_sc[1]]
    for ci in range(n_chunks):
        if per_chunk_state:
            st = [jnp.concatenate([s0t_ref[ci, 0], pad], axis=0), jnp.concatenate([pad, s0t_ref[ci, 1]], axis=0)]
        o_pair = jnp.zeros((c, LANES), F32)
        for j in range(2):
            o_pair = o_pair + _dot(a_mats[ci, j], vs[ci, j]) + _nt(qhats[ci, j], st[j].astype(BF16))
            st[j] = st[j] * decays[ci, j] + kvs[ci, j]
        if per_chunk_state:
            snt_ref[ci, 0] = st[0][0:B_DV, :]
            snt_ref[ci, 1] = st[1][B_DV:2 * B_DV, :]
        o2 = o_pair * o_pair
        s_lo = jnp.sum(jnp.where(low, o2, 0.0), axis=-1, keepdims=True)
        s_hi = jnp.sum(jnp.where(low, 0.0, o2), axis=-1, keepdims=True)
        ms = jnp.where(low, s_lo, s_hi) * (1.0 / B_DV)
        ob = o_pair * lax.rsqrt(ms + EPS) * gon_ref[...] * og_ref[rows[ci], :].astype(F32)
        o_ref[rows[ci], :] = ob.astype(BF16)
    if not per_chunk_state:
        st_sc[0] = st[0]
        st_sc[1] = st[1]

        @pl.when(step == pl.num_programs(2) - 1)
        def _():
            snt_ref[0] = st_sc[0, 0:B_DV, :]
            snt_ref[1] = st_sc[1, B_DV:2 * B_DV, :]


def _hgrn(qb, kb, lf, vb, og, s0t, gon, n_streams, tc):
    b, l, dk = qb.shape
    per_chunk_state = n_streams != b
    if per_chunk_state:
        assert b == 1 and l % n_streams == 0 and l // n_streams <= CHUNK
        c = l // n_streams
    else:
        c = min(CHUNK, l)
    assert l % tc == 0 and tc % c == 0
    n_chunks = tc // c
    masks = jnp.asarray(_level_masks(c))
    tri = jnp.asarray(np.tril(np.ones((c, c), np.float32)), dtype=BF16)
    pairs = B_HEADS // 2
    wide = pl.BlockSpec((None, tc, 2 * B_DK), lambda bi, p, i: (bi, i, p))
    narrow = pl.BlockSpec((None, tc, 2 * B_DV), lambda bi, p, i: (bi, i, p))
    if per_chunk_state:
        state = pl.BlockSpec((n_chunks, 2, B_DV, B_DK), lambda bi, p, i: (i, p, 0, 0))
    else:
        state = pl.BlockSpec((None, 2, B_DV, B_DK), lambda bi, p, i: (bi, p, 0, 0))
    return pl.pallas_call(
        functools.partial(_hgrn_kernel, c=c, n_chunks=n_chunks, per_chunk_state=per_chunk_state),
        grid=(b, pairs, l // tc),
        in_specs=[wide, wide, wide, narrow, narrow, state, _const_spec(gon.shape),
                  _const_spec(masks.shape), _const_spec(tri.shape)],
        out_specs=[narrow, state],
        out_shape=[jax.ShapeDtypeStruct((b, l, B_HEADS * B_DV), BF16),
                   jax.ShapeDtypeStruct((n_streams, B_HEADS, B_DV, B_DK), F32)],
        scratch_shapes=[pltpu.VMEM((2, 2 * B_DV, B_DK), F32)],
        compiler_params=_params(("parallel", "parallel", "parallel" if per_chunk_state else "arbitrary")),
        name="hgrn",
    )(qb, kb, lf, vb, og, s0t, gon, masks, tri)


MOE_SUB = 144
MOE_ROWS = 1024
_BIG_LANE = float(1 << 20)


def _max_items(tm):
    return (tm + N_GROUPS * (MOE_SUB - 1)) // MOE_SUB


def _first_lane_of_max(x, lane):
    v = jnp.max(x, axis=-1, keepdims=True)
    return v, jnp.min(jnp.where(x == v, lane, _BIG_LANE), axis=-1, keepdims=True)


def _out_proj_kernel(oa_ref, ob_ref, sa_ref, sb_ref, x_ref, wa_ref, wb_ref, wo_ref, gffn_ref,
                     wr_ref, br_ref, tri_ref, upper_ref, x1_ref, xs_ref, slot_ref, meta_ref, *, oa_transposed):
    tm = x_ref.shape[0]
    oa, ob = oa_ref[...], ob_ref[...]
    nb = 256
    m = []
    for i in range(x_ref.shape[1] // nb):
        cols = slice(i * nb, (i + 1) * nb)
        ya = _tn(oa, wa_ref[:, cols]) if oa_transposed else _dot(oa, wa_ref[:, cols])
        yb = _dot(ob, wb_ref[:, cols])
        m.append((sa_ref[:, cols].astype(F32) * ya + sb_ref[:, cols].astype(F32) * yb).astype(BF16))
    x1 = x_ref[...] + _dot(jnp.concatenate(m, axis=-1), wo_ref[...])
    x1_ref[...] = x1
    h2 = _rms(x1, gffn_ref[...]).astype(BF16)

    logits = _dot(h2, wr_ref[...]) + br_ref[...]
    lane = lax.broadcasted_iota(jnp.int32, (tm, LANES), 1).astype(F32)
    _, grp = _first_lane_of_max(jnp.where(lane < N_GROUPS, logits, NEG_INF), lane)
    onehot = jnp.where(lane == grp, 1.0, 0.0)
    rank = _dot(tri_ref[...], onehot.astype(BF16))
    count = rank[tm - 1:tm, :]
    items = jnp.zeros_like(count)
    for j in range(_max_items(tm)):
        items = items + jnp.where(count > j * MOE_SUB, 1.0, 0.0)
    base = _dot(jnp.broadcast_to(items, (8, LANES)).astype(BF16), upper_ref[...])[0:1, :] * MOE_SUB
    slot = jnp.sum(onehot * (base + rank - 1.0), axis=-1, keepdims=True)
    col = lax.broadcasted_iota(jnp.int32, (tm, MOE_ROWS), 1).astype(F32)
    place = jnp.where(col == slot, 1.0, 0.0).astype(BF16)
    xs_ref[...] = _tn(place, h2).astype(BF16)
    slot_ref[...] = jnp.broadcast_to(slot, (tm, LANES))
    meta_ref[...] = jnp.broadcast_to(items, (8, LANES))


def _out_proj(oa, ob, sa, sb, x, wa, wb, wo, gffn, wr, br, tm, oa_transposed):
    b, l, d = x.shape
    nt = l // tm
    assert _max_items(tm) * MOE_SUB <= MOE_ROWS
    hv = A_HEADS * V_DIM
    tri = jnp.asarray(np.tril(np.ones((tm, tm), np.float32)), dtype=BF16)
    upper = jnp.asarray(np.triu(np.ones((LANES, LANES), np.float32), 1), dtype=BF16)
    tok = lambda w: pl.BlockSpec((None, tm, w), lambda bi, i: (bi, i, 0))
    per_tile = lambda r, w: pl.BlockSpec((None, r, w), lambda bi, i: (bi * nt + i, 0, 0))
    oa_spec = pl.BlockSpec((None, hv, tm), lambda bi, i: (bi, 0, i)) if oa_transposed else tok(hv)
    return pl.pallas_call(
        functools.partial(_out_proj_kernel, oa_transposed=oa_transposed),
        grid=(b, nt),
        in_specs=[oa_spec, tok(B_HEADS * B_DV), tok(d), tok(d), tok(d), _const_spec(wa.shape),
                  _const_spec(wb.shape), _const_spec(wo.shape), _const_spec(gffn.shape),
                  _const_spec(wr.shape), _const_spec(br.shape), _const_spec(tri.shape),
                  _const_spec(upper.shape)],
        out_specs=[tok(d), per_tile(MOE_ROWS, d), tok(LANES), per_tile(8, LANES)],
        out_shape=[jax.ShapeDtypeStruct((b, l, d), F32),
                   jax.ShapeDtypeStruct((b * nt, MOE_ROWS, d), BF16),
                   jax.ShapeDtypeStruct((b, l, LANES), F32),
                   jax.ShapeDtypeStruct((b * nt, 8, LANES), F32)],
        compiler_params=_params(("parallel", "parallel")),
        name="out_proj",
    )(oa, ob, sa, sb, x, wa, wb, wo, gffn, wr, br, tri, upper)


def _work_list(meta, n_items):
    n = meta[:, 0, :N_GROUPS].astype(jnp.int32)
    tiles = n.shape[0]
    first = jnp.cumsum(n, axis=1) - n
    per_seg = n.T.reshape(-1)
    ends = jnp.cumsum(per_seg)
    total = ends[-1]
    i = jnp.minimum(jnp.arange(n_items, dtype=jnp.int32), total - 1)
    seg = jnp.sum((ends[None, :] <= i[:, None]).astype(jnp.int32), axis=1)
    grp, tile = seg // tiles, seg % tiles
    block = first[tile, grp] + i - (ends[seg] - per_seg[seg])
    return tile, block.astype(jnp.int32), grp, total[None].astype(jnp.int32)


def _moe_expert_kernel(tile_ref, block_ref, grp_ref, total_ref, xs_ref, wr_ref, br_ref, wg_ref, wu_ref,
                       wd_ref, ys_ref):
    i = pl.program_id(0)

    @pl.when(i < total_ref[0])
    def _():
        g = grp_ref[i]
        x = xs_ref[...]
        logits = _dot(x, wr_ref[...]) + br_ref[...]
        lane_i = lax.broadcasted_iota(jnp.int32, logits.shape, 1)
        lane = lane_i.astype(F32)
        is_grp = jnp.abs(lane - (EXPERTS_PER_GROUP + 0.5 * (N_GROUPS - 1))) < 0.5 * N_GROUPS
        lg = jnp.where(is_grp, logits, NEG_INF)
        gmax = jnp.max(lg, axis=-1, keepdims=True)
        denom = jnp.sum(jnp.exp(lg - gmax), axis=-1, keepdims=True)
        lg_own = jnp.sum(jnp.where(lane_i == EXPERTS_PER_GROUP + g, logits, 0.0), axis=-1, keepdims=True)
        p_grp = jnp.exp(lg_own - gmax) / denom
        le = jnp.where(lane < EXPERTS_PER_GROUP, logits, NEG_INF)
        v1, i1 = _first_lane_of_max(le, lane)
        le2 = jnp.where(lane == i1, NEG_INF, le)
        v2, i2 = _first_lane_of_max(le2, lane)
        e2 = jnp.exp(v2 - v1)
        w1 = p_grp / (1.0 + e2)
        comb = jnp.where(lane == i1, w1, 0.0) + jnp.where(lane == i2, w1 * e2, 0.0)
        hid = []
        for e in range(EXPERTS_PER_GROUP):
            a = _dot(x, wg_ref[e])
            u = _dot(x, wu_ref[e])
            s, _ = _sig_pair(a)
            hid.append((a * s * u * comb[:, e:e + 1]).astype(BF16))
        hid = jnp.concatenate(hid, axis=-1)
        wd = wd_ref[...].reshape(EXPERTS_PER_GROUP * D_FF_EXPERT, wd_ref.shape[-1])
        ys_ref[...] = _dot(hid, wd).astype(BF16)


def _moe_experts(xs, work, wrg, brg, wg, wu, wd):
    tiles, _, d = xs.shape
    tile, block, grp, total = work
    item = lambda i, t, k, g, n: (t[i], k[i], 0)
    by_group = lambda i, t, k, g, n: (g[i], 0, 0)
    grid_spec = pltpu.PrefetchScalarGridSpec(
        num_scalar_prefetch=4,
        grid=(tile.shape[0],),
        in_specs=[pl.BlockSpec((None, MOE_SUB, d), item),
                  pl.BlockSpec((None, d, LANES), by_group),
                  pl.BlockSpec((None, 1, LANES), by_group),
                  pl.BlockSpec((EXPERTS_PER_GROUP, d, D_FF_EXPERT), by_group),
                  pl.BlockSpec((EXPERTS_PER_GROUP, d, D_FF_EXPERT), by_group),
                  pl.BlockSpec((EXPERTS_PER_GROUP, D_FF_EXPERT, d), by_group)],
        out_specs=pl.BlockSpec((None, MOE_SUB, d), item))
    return pl.pallas_call(
        _moe_expert_kernel,
        grid_spec=grid_spec,
        out_shape=jax.ShapeDtypeStruct(xs.shape, BF16),
        input_output_aliases={4: 0},
        compiler_params=_params(("arbitrary",)),
        name="moe_experts",
    )(tile, block, grp, total, xs, wrg, brg, wg, wu, wd)


def _moe_combine_kernel(ys_ref, slot_ref, x1_ref, gfin_ref, y_ref):
    tm = x1_ref.shape[0]
    col = lax.broadcasted_iota(jnp.int32, (tm, MOE_ROWS), 1).astype(F32)
    slot = jnp.concatenate([slot_ref[...]] * (MOE_ROWS // LANES), axis=-1)
    place = jnp.where(col == slot, 1.0, 0.0).astype(BF16)
    y_ref[...] = _rms(x1_ref[...] + _dot(place, ys_ref[...]), gfin_ref[...])


def _moe_combine(ys, slot, x1, gfin, tm):
    b, l, d = x1.shape
    nt = l // tm
    tok = lambda w: pl.BlockSpec((None, tm, w), lambda bi, i: (bi, i, 0))
    per_tile = lambda r, w: pl.BlockSpec((None, r, w), lambda bi, i: (bi * nt + i, 0, 0))
    return pl.pallas_call(
        _moe_combine_kernel,
        grid=(b, nt),
        in_specs=[per_tile(MOE_ROWS, d), tok(LANES), tok(d), _const_spec(gfin.shape)],
        out_specs=tok(d),
        out_shape=jax.ShapeDtypeStruct((b, l, d), F32),
        compiler_params=_params(("parallel", "parallel")),
        name="moe_combine",
    )(ys, slot, x1, gfin)


def _rope_tables(pos):
    inv = jnp.power(ROPE_THETA, -jnp.arange(HALF, dtype=F32) / HALF)
    ang = pos.astype(F32)[:, None] * inv[None, :]
    cos, sin = jnp.cos(ang), jnp.sin(ang)
    zeros = jnp.zeros_like(sin)
    reps = LANES // ROPE_DIM
    cosp = jnp.tile(jnp.concatenate([cos, cos], axis=1), (1, reps))
    sinlo = jnp.tile(jnp.concatenate([-sin, zeros], axis=1), (1, reps))
    sinhi = jnp.tile(jnp.concatenate([zeros, sin], axis=1), (1, reps))
    return cosp, sinlo, sinhi, cos.T * (SCALE * LOG2E), sin.T * (SCALE * LOG2E)


def _prep_weights(g_mix, w_in, g_q, w_uq, g_kv, w_ukv, lb_hgrn, g_onorm, w_a_out, w_b_out, w_o,
                  g_ffn, w_rg, b_rg, w_re, b_re, w_gate, w_up, w_down, g_final):
    assert w_in.shape[0] == 1, "single-layer step"
    d = w_in.shape[1]
    w = w_in[0]
    n_a = Q_LORA + KV_LORA
    win_a = jnp.concatenate([w[:, :n_a]] + [w[:, n_a:n_a + ROPE_DIM]] * (LANES // ROPE_DIM), axis=1).astype(BF16)
    win_b = jnp.swapaxes(w, 0, 1).astype(BF16)
    per_q = NOPE_DIM + ROPE_DIM
    wuq = jnp.pad(w_uq[0].reshape(Q_LORA, A_HEADS, per_q), ((0, 0), (0, 0), (0, HEAD_PAD - per_q)))
    wuq = wuq.reshape(Q_LORA, A_HEADS * HEAD_PAD).astype(BF16)
    wukv = w_ukv[0].reshape(KV_LORA, A_HEADS, NOPE_DIM + V_DIM)
    w_uk, w_uv = wukv[..., :NOPE_DIM], wukv[..., NOPE_DIM:]
    wuk = jnp.pad(w_uk, ((0, 0), (0, 0), (0, HEAD_PAD - NOPE_DIM))).reshape(KV_LORA, A_HEADS * HEAD_PAD).astype(BF16)
    wuvt = w_uv.reshape(KV_LORA, A_HEADS * V_DIM).T.astype(BF16)
    wq2l = jnp.pad(jnp.transpose(w_uk, (1, 2, 0)), ((0, 0), (0, HEAD_PAD - NOPE_DIM), (0, 0))).astype(BF16)
    esel = jnp.zeros((HEAD_PAD, ROPE_DIM), F32).at[ROPE_LO + jnp.arange(ROPE_DIM), jnp.arange(ROPE_DIM)].set(1.0).astype(BF16)
    wuvp = w_uv.reshape(KV_LORA, A_HEADS * V_DIM).astype(BF16)
    lb = jnp.cumsum(jax.nn.softmax(lb_hgrn.astype(F32), axis=0), axis=0)[0][None, :]
    gon = jnp.tile(g_onorm[0], 2)[None, :]
    wr = jnp.pad(w_rg[0], ((0, 0), (0, LANES - N_GROUPS))).astype(BF16)
    br = jnp.pad(b_rg[0], (0, LANES - N_GROUPS))[None, :]
    rest = LANES - EXPERTS_PER_GROUP - N_GROUPS
    wrg = jnp.concatenate([w_re[0], jnp.broadcast_to(w_rg[0], (N_GROUPS, d, N_GROUPS)),
                           jnp.zeros((N_GROUPS, d, rest), F32)], axis=2).astype(BF16)
    brg = jnp.concatenate([b_re[0], jnp.broadcast_to(b_rg[0], (N_GROUPS, N_GROUPS)),
                           jnp.zeros((N_GROUPS, rest), F32)], axis=1)[:, None, :]
    return dict(
        wrg=wrg, brg=brg,
        gmix=g_mix[0][None, :], win_a=win_a, win_b=win_b, gq=g_q[0][None, :], wuq=wuq, wuqt=wuq.T,
        gkv=g_kv[0][None, :], wuk=wuk, wuvt=wuvt, wq2l=wq2l, esel=esel, wuvp=wuvp, lb=lb, gon=gon,
        wa=w_a_out[0].astype(BF16), wb=w_b_out[0].astype(BF16), wo=w_o[0].astype(BF16),
        gffn=g_ffn[0][None, :], wr=wr, br=br, wg=w_gate[0].astype(BF16), wu=w_up[0].astype(BF16),
        wd=w_down[0].astype(BF16), gfin=g_final[None, :])


def _tile(n, want):
    t = min(n, want)
    while n % t:
        t //= 2
    return t


def _run_group(x, pos, streams, past, w):
    b, l, d = x.shape
    ns, ls = streams
    tm = _tile(l, 512)
    h, ckv, kpe, q, k, vt = _mla_proj(x, w["gmix"], w["win_a"], w["gq"], w["wuq"], w["wuqt"], w["gkv"],
                                      w["wuk"], w["wuvt"], _rope_tables(pos), tm, past is None)
    qb, kb, lf, vb, og, sa, sb = _hgrn_proj(h, w["win_b"], w["lb"], tm)
    as_streams = lambda a: a.reshape(ns, ls, a.shape[-1])
    if past is None:
        oa = _attn(q, k, vt, _tile(l, 512), 256, 4)
        s0t = jnp.zeros((ns, B_HEADS, B_DV, B_DK), F32)
    else:
        cache_ckv, cache_kpe, state = past
        oa = _sample_attn(as_streams(q), cache_ckv, cache_kpe, as_streams(ckv), as_streams(kpe),
                          w["wq2l"], w["esel"], w["wuvp"]).reshape(b, l, A_HEADS * V_DIM)
        s0t = jnp.swapaxes(state.astype(F32), -1, -2)
    ob, snt = _hgrn(qb, kb, lf, vb, og, s0t, w["gon"], ns, _tile(l, 1024 if past is None else 8 * ls))
    x1, xs, slot, meta = _out_proj(oa, ob, sa, sb, x, w["wa"], w["wb"], w["wo"], w["gffn"], w["wr"],
                                   w["br"], tm, past is None)
    work = _work_list(meta, meta.shape[0] * _max_items(tm))
    ys = _moe_experts(xs, work, w["wrg"], w["brg"], w["wg"], w["wu"], w["wd"])
    y = _moe_combine(ys, slot, x1, w["gfin"], tm)
    return y, ckv, kpe, jnp.swapaxes(snt, -1, -2)


def kernel(x_prompt, x_sample, cache_ckv, cache_kpe, state_hgrn, g_mix, w_in, g_q, w_uq, g_kv, w_ukv,
           lb_hgrn, g_onorm, w_a_out, w_b_out, w_o, g_ffn, w_rg, b_rg, w_re, b_re, w_gate, w_up, w_down,
           g_final):
    w = _prep_weights(g_mix, w_in, g_q, w_uq, g_kv, w_ukv, lb_hgrn, g_onorm, w_a_out, w_b_out, w_o,
                      g_ffn, w_rg, b_rg, w_re, b_re, w_gate, w_up, w_down, g_final)
    bp, lp, d = x_prompt.shape
    y_p, ckv_p, kpe_p, st_p = _run_group(x_prompt, jnp.arange(lp, dtype=jnp.int32), (bp, lp), None, w)

    bs, ls, _ = x_sample.shape
    past_len = cache_ckv.shape[2]
    pos_s = past_len + (jnp.arange(bs * ls, dtype=jnp.int32) % ls)
    y_s, ckv_s, kpe_s, st_s = _run_group(x_sample.reshape(1, bs * ls, d), pos_s, (bs, ls),
                                         (cache_ckv[0], jnp.swapaxes(cache_kpe[0], 1, 2), state_hgrn[0]), w)
    return (y_p, y_s.reshape(bs, ls, d),
            ckv_p[None], kpe_p[None], st_p[None].astype(x_prompt.dtype),
            ckv_s.reshape(1, bs, ls, KV_LORA), kpe_s.reshape(1, bs, ls, ROPE_DIM),
            st_s[None].astype(state_hgrn.dtype))
```

```python
import functools

import numpy as np
import jax
import jax.numpy as jnp
from jax import lax
from jax.experimental import pallas as pl
from jax.experimental.pallas import tpu as pltpu

F32 = jnp.float32
BF16 = jnp.bfloat16

EPS = 1e-6
CHUNK = 64
A_HEADS = 8
Q_LORA = 384
KV_LORA = 256
NOPE_DIM = 64
ROPE_DIM = 32
V_DIM = 64
ROPE_THETA = 10000.0
B_HEADS = 8
B_DK = 128
B_DV = 64
N_GROUPS = 4
EXPERTS_PER_GROUP = 8
N_EXPERTS = N_GROUPS * EXPERTS_PER_GROUP
D_FF_EXPERT = 256

LANES = 128
VMEM_BYTES_V7X = 64 * 1024 * 1024
VMEM_LIMIT = VMEM_BYTES_V7X - 8 * 1024 * 1024

HEAD_PAD = LANES
V_AUG = V_DIM + 16
SAMPLE_KEY_BLOCK = 1024
ROPE_LO = NOPE_DIM
ROPE_HI = NOPE_DIM + ROPE_DIM
HALF = ROPE_DIM // 2
SCALE = (NOPE_DIM + ROPE_DIM) ** -0.5
LOG2E = 1.4426950408889634
NEG_INF = float("-inf")


def _params(sem):
    return pltpu.CompilerParams(dimension_semantics=sem, vmem_limit_bytes=VMEM_LIMIT)


def _const_spec(shape):
    nd = len(shape)
    return pl.BlockSpec(shape, lambda *_: (0,) * nd, pipeline_mode=pl.Buffered(1))


def _rms(x, g):
    ms = jnp.mean(x * x, axis=-1, keepdims=True)
    return x * lax.rsqrt(ms + EPS) * g


def _sig_pair(x):
    e = jnp.exp(-jnp.abs(x))
    r = 1.0 / (1.0 + e)
    er = e * r
    pos = x >= 0
    return jnp.where(pos, r, er), jnp.where(pos, er, r)


def _nt(a, b):
    return lax.dot_general(a, b, (((1,), (1,)), ((), ())), preferred_element_type=F32)


def _tn(a, b):
    return lax.dot_general(a, b, (((0,), (0,)), ((), ())), preferred_element_type=F32)


def _dot(a, b):
    return jnp.dot(a, b, preferred_element_type=F32)


def _rope(x, cosp, sinlo, sinhi):
    return x * cosp + pltpu.roll(x, LANES - HALF, 1) * sinlo + pltpu.roll(x, HALF, 1) * sinhi


def _mla_proj_kernel(x_ref, gmix_ref, win_ref, gq_ref, wuq_ref, wuqt_ref, gkv_ref, wuk_ref, wuvt_ref,
                     cos_ref, sinlo_ref, sinhi_ref, cost_ref, sint_ref,
                     h_ref, ckv_ref, kpe_ref, q_ref, k_ref, vt_ref, *, q_transposed):
    h = _rms(x_ref[...], gmix_ref[...]).astype(BF16)
    h_ref[...] = h
    z = _dot(h, win_ref[...])
    cosp, sinlo, sinhi = cos_ref[...], sinlo_ref[...], sinhi_ref[...]
    lane = lax.broadcasted_iota(jnp.int32, (1, LANES), 1)
    rope_lanes = (lane // ROPE_DIM) == (ROPE_LO // ROPE_DIM)

    cqn = _rms(z[:, :Q_LORA], gq_ref[...]).astype(BF16)
    if q_transposed:
        qt = _nt(wuqt_ref[...], cqn)
        cos_t, sin_t = cost_ref[...], sint_ref[...]
        for hd in range(A_HEADS):
            r0 = hd * HEAD_PAD
            q_ref[r0:r0 + ROPE_LO, :] = (qt[r0:r0 + ROPE_LO] * (SCALE * LOG2E)).astype(BF16)
            a = qt[r0 + ROPE_LO:r0 + ROPE_LO + HALF]
            b = qt[r0 + ROPE_LO + HALF:r0 + ROPE_HI]
            q_ref[r0 + ROPE_LO:r0 + ROPE_HI, :] = jnp.concatenate(
                [a * cos_t - b * sin_t, a * sin_t + b * cos_t], axis=0).astype(BF16)
            q_ref[r0 + ROPE_HI:r0 + HEAD_PAD, :] = jnp.zeros((HEAD_PAD - ROPE_HI, qt.shape[1]), BF16)
    else:
        q = _dot(cqn, wuq_ref[...])
        for hd in range(A_HEADS):
            sl = slice(hd * HEAD_PAD, (hd + 1) * HEAD_PAD)
            qh = q[:, sl]
            qh = jnp.where(rope_lanes, _rope(qh, cosp, sinlo, sinhi), qh) * (SCALE * LOG2E)
            q_ref[:, sl] = qh.astype(BF16)

    ckv = _rms(z[:, Q_LORA:Q_LORA + KV_LORA], gkv_ref[...])
    ckv_ref[...] = ckv
    ckv_bf = ckv.astype(BF16)
    kpe_rot = _rope(z[:, Q_LORA + KV_LORA:], cosp, sinlo, sinhi)
    kpe_ref[...] = kpe_rot[:, :ROPE_DIM]
    kpe_placed = jnp.where(rope_lanes, kpe_rot, 0.0)
    kn = _dot(ckv_bf, wuk_ref[...])
    for hd in range(A_HEADS):
        sl = slice(hd * HEAD_PAD, (hd + 1) * HEAD_PAD)
        k_ref[:, sl] = (kn[:, sl] + kpe_placed).astype(BF16)
    vt = _nt(wuvt_ref[...], ckv_bf).astype(BF16)
    ones = jnp.ones((V_AUG - V_DIM, vt.shape[1]), BF16)
    for hd in range(A_HEADS):
        vt_ref[hd * V_AUG:hd * V_AUG + V_DIM, :] = vt[hd * V_DIM:(hd + 1) * V_DIM]
        vt_ref[hd * V_AUG + V_DIM:(hd + 1) * V_AUG, :] = ones


def _mla_proj(x, gmix, win_a, gq, wuq, wuqt, gkv, wuk, wuvt, tables, tm, q_transposed):
    b, l, d = x.shape
    hq = A_HEADS * HEAD_PAD
    hv = A_HEADS * V_AUG
    cosp, sinlo, sinhi, cos_t, sin_t = tables
    tok = lambda w: pl.BlockSpec((None, tm, w), lambda bi, i: (bi, i, 0))
    tok_t = lambda w: pl.BlockSpec((None, w, tm), lambda bi, i: (bi, 0, i))
    tab = pl.BlockSpec((tm, LANES), lambda bi, i: (i, 0))
    tab_t = pl.BlockSpec((HALF, tm), lambda bi, i: (0, i))
    return pl.pallas_call(
        functools.partial(_mla_proj_kernel, q_transposed=q_transposed),
        grid=(b, l // tm),
        in_specs=[tok(d), _const_spec(gmix.shape), _const_spec(win_a.shape), _const_spec(gq.shape),
                  _const_spec(wuq.shape), _const_spec(wuqt.shape), _const_spec(gkv.shape),
                  _const_spec(wuk.shape), _const_spec(wuvt.shape), tab, tab, tab, tab_t, tab_t],
        out_specs=[tok(d), tok(KV_LORA), tok(ROPE_DIM), tok_t(hq) if q_transposed else tok(hq), tok(hq),
                   tok_t(hv)],
        out_shape=[jax.ShapeDtypeStruct((b, l, d), BF16),
                   jax.ShapeDtypeStruct((b, l, KV_LORA), F32),
                   jax.ShapeDtypeStruct((b, l, ROPE_DIM), F32),
                   jax.ShapeDtypeStruct((b, hq, l) if q_transposed else (b, l, hq), BF16),
                   jax.ShapeDtypeStruct((b, l, hq), BF16),
                   jax.ShapeDtypeStruct((b, hv, l), BF16)],
        compiler_params=_params(("parallel", "parallel")),
        name="mla_proj",
    )(x, gmix, win_a, gq, wuq, wuqt, gkv, wuk, wuvt, cosp, sinlo, sinhi, cos_t, sin_t)


def _hgrn_proj_kernel(h_ref, wt_ref, lb_ref, qb_ref, kb_ref, lf_ref, vb_ref, og_ref, sa_ref, sb_ref, *, row0):
    h = h_ref[...]
    dk = B_HEADS * B_DK
    dv = B_HEADS * B_DV
    dm = sa_ref.shape[-1]
    nb = 256
    lb_all = lb_ref[...]

    def z(base, i):
        return _nt(h, wt_ref[row0 + base + i * nb:row0 + base + (i + 1) * nb, :])

    for i in range(dk // nb):
        cols = slice(i * nb, (i + 1) * nb)
        zq = z(0, i)
        qb_ref[:, cols] = zq * jax.nn.sigmoid(zq)
        sf, snf = _sig_pair(z(dk, i))
        lb = lb_all[:, cols]
        lf_ref[:, cols] = jnp.log(lb + (1.0 - lb) * sf)
        kb_ref[:, cols] = (1.0 - lb) * snf
    for i in range(dv // nb):
        cols = slice(i * nb, (i + 1) * nb)
        vb_ref[:, cols] = z(2 * dk, i).astype(BF16)
        zg = z(2 * dk + dv, i)
        og_ref[:, cols] = (zg * jax.nn.sigmoid(zg)).astype(BF16)
    for i in range(dm // nb):
        cols = slice(i * nb, (i + 1) * nb)
        sa_ref[:, cols] = jax.nn.sigmoid(z(2 * dk + 2 * dv, i)).astype(BF16)
        sb_ref[:, cols] = jax.nn.sigmoid(z(2 * dk + 2 * dv + dm, i)).astype(BF16)


def _hgrn_proj(h, win_b, lb, tm):
    b, l, d = h.shape
    dk = B_HEADS * B_DK
    dv = B_HEADS * B_DV
    tok = lambda w: pl.BlockSpec((None, tm, w), lambda bi, i: (bi, i, 0))
    sds = lambda w, dt: jax.ShapeDtypeStruct((b, l, w), dt)
    return pl.pallas_call(
        functools.partial(_hgrn_proj_kernel, row0=Q_LORA + KV_LORA + ROPE_DIM),
        grid=(b, l // tm),
        in_specs=[tok(d), _const_spec(win_b.shape), _const_spec(lb.shape)],
        out_specs=[tok(dk), tok(dk), tok(dk), tok(dv), tok(dv), tok(d), tok(d)],
        out_shape=[sds(dk, F32), sds(dk, F32), sds(dk, F32), sds(dv, BF16), sds(dv, BF16),
                   sds(d, BF16), sds(d, BF16)],
        compiler_params=_params(("parallel", "parallel")),
        name="hgrn_proj",
    )(h, win_b, lb)


def _attn_kernel(qt_ref, k_ref, vt_ref, o_ref, s_sc, m_sc, acc_sc, *, t, tc, hp):
    qi = pl.program_id(2)
    n_chains = t // tc
    m_sc[...] = jnp.full(m_sc.shape, NEG_INF, F32)
    acc_sc[...] = jnp.zeros(acc_sc.shape, F32)
    qk_rows = [slice(hd * HEAD_PAD, (hd + 1) * HEAD_PAD) for hd in range(hp)]
    v_rows = [slice(hd * V_AUG, (hd + 1) * V_AUG) for hd in range(hp)]

    def scores(kb, slot):
        start = pl.multiple_of(kb * t, t)
        for hd in range(hp):
            k = k_ref[pl.ds(start, t), qk_rows[hd]]
            for ch in range(n_chains):
                cols = slice(ch * tc, (ch + 1) * tc)
                s_sc[hd, slot, :, cols] = _dot(k, qt_ref[qk_rows[hd], cols])

    def consume(kb, slot, masked):
        start = pl.multiple_of(kb * t, t)
        for hd in range(hp):
            for ch in range(n_chains):
                cols = slice(ch * tc, (ch + 1) * tc)
                nk = (ch + 1) * tc if masked else t
                vt = vt_ref[v_rows[hd], pl.ds(start, nk)]
                s = s_sc[hd, slot, 0:nk, cols]
                if masked:
                    r = lax.broadcasted_iota(jnp.int32, (nk, tc), 0) // CHUNK
                    c = (lax.broadcasted_iota(jnp.int32, (nk, tc), 1) + ch * tc) // CHUNK
                    s = jnp.where(r <= c, s, NEG_INF)
                m_prev = m_sc[hd, :, cols]
                m_new = jnp.maximum(m_prev, jnp.max(s, axis=0, keepdims=True))
                alpha = jnp.exp2(m_prev - m_new)
                p = jnp.exp2(s - m_new)
                acc_sc[hd, :, cols] = alpha * acc_sc[hd, :, cols] + _dot(vt, p.astype(BF16))
                m_sc[hd, :, cols] = m_new

    scores(0, 0)

    def body(j, carry):
        kb = 2 * j
        scores(kb + 1, 1)
        consume(kb, 0, False)
        scores(kb + 2, 0)
        consume(kb + 1, 1, False)
        return carry

    lax.fori_loop(0, qi // 2, body, 0)

    @pl.when(qi % 2 == 0)
    def _():
        consume(qi, 0, True)

    @pl.when(qi % 2 == 1)
    def _():
        scores(qi, 1)
        consume(qi - 1, 0, False)
        consume(qi, 1, True)

    for hd in range(hp):
        acc = acc_sc[hd]
        o_ref[hd * V_DIM:(hd + 1) * V_DIM, :] = (acc[:V_DIM] / acc[V_DIM:V_DIM + 1]).astype(BF16)


def _attn(qt, k, vt, t, tc, hp):
    b, l, _ = k.shape
    return pl.pallas_call(
        functools.partial(_attn_kernel, t=t, tc=min(tc, t), hp=hp),
        grid=(b, A_HEADS // hp, l // t),
        in_specs=[pl.BlockSpec((None, hp * HEAD_PAD, t), lambda bi, h, i: (bi, h, i)),
                  pl.BlockSpec((None, l, hp * HEAD_PAD), lambda bi, h, i: (bi, 0, h)),
                  pl.BlockSpec((None, hp * V_AUG, l), lambda bi, h, i: (bi, h, 0))],
        out_specs=pl.BlockSpec((None, hp * V_DIM, t), lambda bi, h, i: (bi, h, i)),
        out_shape=jax.ShapeDtypeStruct((b, A_HEADS * V_DIM, l), BF16),
        scratch_shapes=[pltpu.VMEM((hp, 2, t, t), F32), pltpu.VMEM((hp, 1, t), F32),
                        pltpu.VMEM((hp, V_AUG, t), F32)],
        compiler_params=_params(("parallel", "parallel", "arbitrary")),
        name="attn",
    )(qt, k, vt)


def _sample_attn_kernel(q_ref, cc_ref, ckt_ref, nc_ref, nk_ref, wq2l_ref, esel_ref, wuv_ref, o_ref,
                        *, past, n_new):
    qs = q_ref[...]
    heads = [qs[:, hd * HEAD_PAD:(hd + 1) * HEAD_PAD] for hd in range(A_HEADS)]
    ql = jnp.concatenate([_dot(heads[hd], wq2l_ref[hd]) for hd in range(A_HEADS)], axis=0).astype(BF16)
    qp = jnp.concatenate([_dot(heads[hd], esel_ref[...]) for hd in range(A_HEADS)], axis=0).astype(BF16)
    nc = nc_ref[...].astype(BF16)
    nk = nk_ref[...].astype(BF16)
    rows = A_HEADS * n_new

    s_n = _nt(ql, nc) + _nt(qp, nk)
    tq = lax.broadcasted_iota(jnp.int32, (rows, n_new), 0) % n_new
    tk = lax.broadcasted_iota(jnp.int32, (rows, n_new), 1)
    s_n = jnp.where((past + tk) // CHUNK <= (past + tq) // CHUNK, s_n, NEG_INF)
    m = jnp.max(s_n, axis=-1, keepdims=True)
    p_n = jnp.exp2(s_n - m)
    denom = jnp.sum(p_n, axis=-1, keepdims=True)
    acc = _dot(p_n.astype(BF16), nc)

    n_blocks = max(1, past // SAMPLE_KEY_BLOCK)
    assert past % n_blocks == 0
    kb = past // n_blocks

    def scores(i):
        cc = cc_ref[i * kb:(i + 1) * kb, :].astype(BF16)
        ckt = ckt_ref[:, i * kb:(i + 1) * kb].astype(BF16)
        return _nt(ql, cc) + _dot(qp, ckt), cc

    nxt = scores(0)
    for i in range(n_blocks):
        s_c, cc = nxt
        if i + 1 < n_blocks:
            nxt = scores(i + 1)
        m_new = jnp.maximum(m, jnp.max(s_c, axis=-1, keepdims=True))
        alpha = jnp.exp2(m - m_new)
        p_c = jnp.exp2(s_c - m_new)
        denom = alpha * denom + jnp.sum(p_c, axis=-1, keepdims=True)
        acc = alpha * acc + _dot(p_c.astype(BF16), cc)
        m = m_new
    o_lat = (acc / denom).astype(BF16)
    full = _dot(o_lat, wuv_ref[...])
    col_head = lax.broadcasted_iota(jnp.int32, (n_new, A_HEADS * V_DIM), 1) // V_DIM
    out = jnp.zeros((n_new, A_HEADS * V_DIM), F32)
    for hd in range(A_HEADS):
        out = out + jnp.where(col_head == hd, full[hd * n_new:(hd + 1) * n_new], 0.0)
    o_ref[...] = out.astype(BF16)


def _sample_attn(q, cache_ckv, cache_kpe_t, ckv_new, kpe_new, wq2l, esel, wuvp):
    nb, n_new, hq = q.shape
    past = cache_ckv.shape[1]
    hv = A_HEADS * V_DIM
    per = lambda r, w: pl.BlockSpec((None, r, w), lambda bi: (bi, 0, 0))
    return pl.pallas_call(
        functools.partial(_sample_attn_kernel, past=past, n_new=n_new),
        grid=(nb,),
        in_specs=[per(n_new, hq), per(past, KV_LORA), per(ROPE_DIM, past), per(n_new, KV_LORA),
                  per(n_new, ROPE_DIM), _const_spec(wq2l.shape), _const_spec(esel.shape),
                  _const_spec(wuvp.shape)],
        out_specs=per(n_new, hv),
        out_shape=jax.ShapeDtypeStruct((nb, n_new, hv), BF16),
        compiler_params=_params(("parallel",)),
        name="sample_attn",
    )(q, cache_ckv, cache_kpe_t, ckv_new, kpe_new, wq2l, esel, wuvp)


def _level_sizes(c):
    sizes = []
    p = c
    while p >= 2:
        sizes.append(p)
        p //= 2
    return sizes


def _level_masks(c):
    t = np.arange(c)[:, None]
    s = np.arange(c)[None, :]
    out = []
    for p in _level_sizes(c):
        out.append((t // p == s // p) & (t % p >= p // 2) & (s % p < p // 2))
    out.append(t == s)
    return np.stack(out).astype(np.float32)


def _boundary(cum, p, c):
    half = p // 2
    if p >= 8:
        parts = [jnp.broadcast_to(cum[i * p + half - 1:i * p + half, :], (p, cum.shape[1]))
                 for i in range(c // p)]
        return jnp.concatenate(parts, axis=0) if len(parts) > 1 else parts[0]
    sub = lax.broadcasted_iota(jnp.int32, (8, cum.shape[1]), 0)
    parts = []
    for g in range(c // 8):
        lo = jnp.broadcast_to(cum[g * 8 + 1:g * 8 + 2, :], (8, cum.shape[1]))
        hi = jnp.broadcast_to(cum[g * 8 + 5:g * 8 + 6, :], (8, cum.shape[1]))
        parts.append(jnp.where(sub < 4, lo, hi))
    return jnp.concatenate(parts, axis=0)


def _hgrn_kernel(q_ref, k_ref, lf_ref, v_ref, og_ref, s0t_ref, gon_ref, mask_ref, tri_ref,
                 o_ref, snt_ref, st_sc, *, c, n_chunks, per_chunk_state):
    step = pl.program_id(2)
    sizes = _level_sizes(c)
    lane = lax.broadcasted_iota(jnp.int32, (1, LANES), 1)
    low = lane < B_DV
    keep_lo = jnp.where(low, 1.0, 0.0).astype(BF16)
    keep_hi = jnp.where(low, 0.0, 1.0).astype(BF16)

    if not per_chunk_state:
        @pl.when(step == 0)
        def _():
            st_sc[...] = jnp.zeros(st_sc.shape, F32)
            st_sc[0, 0:B_DV, :] = s0t_ref[0]
            st_sc[1, B_DV:2 * B_DV, :] = s0t_ref[1]

    tri = tri_ref[...]
    rows = [slice(ci * c, (ci + 1) * c) for ci in range(n_chunks)]
    heads = [slice(j * B_DK, (j + 1) * B_DK) for j in range(2)]
    items = [(ci, j) for ci in range(n_chunks) for j in range(2)]


    cums = []
    for ci in range(n_chunks):
        lf = lf_ref[rows[ci], :]
        hi = lf.astype(BF16)
        r1 = lf - hi.astype(F32)
        mid = r1.astype(BF16)
        lo = (r1 - mid.astype(F32)).astype(BF16)
        cums.append((_dot(tri, hi) + _dot(tri, mid) + _dot(tri, lo)) * LOG2E)

    def zeros(n):
        return jnp.zeros((n, B_DK), F32)

    a_mats = {}
    for ci, j in items:
        q = q_ref[rows[ci], heads[j]]
        k = k_ref[rows[ci], heads[j]]
        cum = cums[ci][:, heads[j]]
        kbf = k.astype(BF16)
        a = _nt(q.astype(BF16), kbf) * mask_ref[len(sizes)]
        q_cols, k_cols = [], []
        for li, p in enumerate(sizes):
            half = p // 2
            if half % 8 == 0:
                for lo in range(0, c, p):
                    mid, hi = lo + half, lo + p
                    bnd = cum[mid - 1:mid, :]
                    qu = q[mid:hi] * jnp.exp2(cum[mid:hi] - bnd)
                    kl = k[lo:mid] * jnp.exp2(bnd - cum[lo:mid])
                    q_cols.append(jnp.concatenate([x for x in (zeros(mid), qu, zeros(c - hi)) if x.shape[0]], axis=0))
                    k_cols.append(jnp.concatenate([x for x in (zeros(lo), kl, zeros(c - mid)) if x.shape[0]], axis=0))
                continue
            if p == 2:
                qt = (q * jnp.exp2(lf_ref[rows[ci], heads[j]] * LOG2E)).astype(BF16)
                kt = kbf
            else:
                bnd = _boundary(cum, p, c)
                qt = (q * jnp.exp2(jnp.minimum(cum - bnd, 0.0))).astype(BF16)
                kt = (k * jnp.exp2(jnp.minimum(bnd - cum, 0.0))).astype(BF16)
            a = a + _nt(qt, kt) * mask_ref[li]
        if q_cols:
            a = a + _nt(jnp.concatenate(q_cols, axis=1).astype(BF16), jnp.concatenate(k_cols, axis=1).astype(BF16))
        a_mats[ci, j] = a.astype(BF16)

    qhats, kvs, decays, vs = {}, {}, {}, {}
    for ci, j in items:
        cum = cums[ci][:, heads[j]]
        last = cum[c - 1:c, :]
        vs[ci, j] = v_ref[rows[ci], :] * (keep_lo if j == 0 else keep_hi)
        qhats[ci, j] = (q_ref[rows[ci], heads[j]] * jnp.exp2(cum)).astype(BF16)
        khat = (k_ref[rows[ci], heads[j]] * jnp.exp2(last - cum)).astype(BF16)
        kvs[ci, j] = _tn(vs[ci, j], khat)
        decays[ci, j] = jnp.exp2(last)

    pad = jnp.zeros((B_DV, B_DK), F32)
    if not per_chunk_state:
        st = [st_sc[0], st_sc[1]]
    for ci in range(n_chunks):
        if per_chunk_state:
            st = [jnp.concatenate([s0t_ref[ci, 0], pad], axis=0), jnp.concatenate([pad, s0t_ref[ci, 1]], axis=0)]
        o_pair = jnp.zeros((c, LANES), F32)
        for j in range(2):
            o_pair = o_pair + _dot(a_mats[ci, j], vs[ci, j]) + _nt(qhats[ci, j], st[j].astype(BF16))
            st[j] = st[j] * decays[ci, j] + kvs[ci, j]
        if per_chunk_state:
            snt_ref[ci, 0] = st[0][0:B_DV, :]
            snt_ref[ci, 1] = st[1][B_DV:2 * B_DV, :]
        o2 = o_pair * o_pair
        s_lo = jnp.sum(jnp.where(low, o2, 0.0), axis=-1, keepdims=True)
        s_hi = jnp.sum(jnp.where(low, 0.0, o2), axis=-1, keepdims=True)
        ms = jnp.where(low, s_lo, s_hi) * (1.0 / B_DV)
        ob = o_pair * lax.rsqrt(ms + EPS) * gon_ref[...] * og_ref[rows[ci], :].astype(F32)
        o_ref[rows[ci], :] = ob.astype(BF16)
    if not per_chunk_state:
        st_sc[0] = st[0]
        st_sc[1] = st[1]

        @pl.when(step == pl.num_programs(2) - 1)
        def _():
            snt_ref[0] = st_sc[0, 0:B_DV, :]
            snt_ref[1] = st_sc[1, B_DV:2 * B_DV, :]


def _hgrn(qb, kb, lf, vb, og, s0t, gon, n_streams, tc):
    b, l, dk = qb.shape
    per_chunk_state = n_streams != b
    if per_chunk_state:
        assert b == 1 and l % n_streams == 0 and l // n_streams <= CHUNK
        c = l // n_streams
    else:
        c = min(CHUNK, l)
    assert l % tc == 0 and tc % c == 0
    n_chunks = tc // c
    masks = jnp.asarray(_level_masks(c))
    tri = jnp.asarray(np.tril(np.ones((c, c), np.float32)), dtype=BF16)
    pairs = B_HEADS // 2
    wide = pl.BlockSpec((None, tc, 2 * B_DK), lambda bi, p, i: (bi, i, p))
    narrow = pl.BlockSpec((None, tc, 2 * B_DV), lambda bi, p, i: (bi, i, p))
    if per_chunk_state:
        state = pl.BlockSpec((n_chunks, 2, B_DV, B_DK), lambda bi, p, i: (i, p, 0, 0))
    else:
        state = pl.BlockSpec((None, 2, B_DV, B_DK), lambda bi, p, i: (bi, p, 0, 0))
    return pl.pallas_call(
        functools.partial(_hgrn_kernel, c=c, n_chunks=n_chunks, per_chunk_state=per_chunk_state),
        grid=(b, pairs, l // tc),
        in_specs=[wide, wide, wide, narrow, narrow, state, _const_spec(gon.shape),
                  _const_spec(masks.shape), _const_spec(tri.shape)],
        out_specs=[narrow, state],
        out_shape=[jax.ShapeDtypeStruct((b, l, B_HEADS * B_DV), BF16),
                   jax.ShapeDtypeStruct((n_streams, B_HEADS, B_DV, B_DK), F32)],
        scratch_shapes=[pltpu.VMEM((2, 2 * B_DV, B_DK), F32)],
        compiler_params=_params(("parallel", "parallel", "parallel" if per_chunk_state else "arbitrary")),
        name="hgrn",
    )(qb, kb, lf, vb, og, s0t, gon, masks, tri)


MOE_SUB = 144
MOE_ROWS = 1024
_BIG_LANE = float(1 << 20)


def _max_items(tm):
    return (tm + N_GROUPS * (MOE_SUB - 1)) // MOE_SUB


def _first_lane_of_max(x, lane):
    v = jnp.max(x, axis=-1, keepdims=True)
    return v, jnp.min(jnp.where(x == v, lane, _BIG_LANE), axis=-1, keepdims=True)


def _out_proj_kernel(oa_ref, ob_ref, sa_ref, sb_ref, x_ref, wa_ref, wb_ref, wo_ref, gffn_ref,
                     wr_ref, br_ref, tri_ref, upper_ref, x1_ref, xs_ref, slot_ref, meta_ref,
                     *, oa_transposed, tm):
    tiles = [slice(u * tm, (u + 1) * tm) for u in range(x_ref.shape[0] // tm)]
    nb = 256
    h2s = []
    for rows in tiles:
        oa = oa_ref[:, rows] if oa_transposed else oa_ref[rows, :]
        ob = ob_ref[rows, :]
        m = []
        for i in range(x_ref.shape[1] // nb):
            cols = slice(i * nb, (i + 1) * nb)
            ya = _tn(oa, wa_ref[:, cols]) if oa_transposed else _dot(oa, wa_ref[:, cols])
            yb = _dot(ob, wb_ref[:, cols])
            m.append((sa_ref[rows, cols].astype(F32) * ya + sb_ref[rows, cols].astype(F32) * yb).astype(BF16))
        x1 = x_ref[rows, :] + _dot(jnp.concatenate(m, axis=-1), wo_ref[...])
        x1_ref[rows, :] = x1
        h2s.append(_rms(x1, gffn_ref[...]).astype(BF16))

    lane = lax.broadcasted_iota(jnp.int32, (tm, LANES), 1).astype(F32)
    logits = [_dot(h2, wr_ref[...]) + br_ref[...] for h2 in h2s]
    grps = [_first_lane_of_max(jnp.where(lane < N_GROUPS, lg, NEG_INF), lane)[1] for lg in logits]
    onehots = [jnp.where(lane == grp, 1.0, 0.0) for grp in grps]
    ranks = [_dot(tri_ref[...], oh.astype(BF16)) for oh in onehots]
    items = []
    for rank in ranks:
        count = rank[tm - 1:tm, :]
        n = jnp.zeros_like(count)
        for j in range(_max_items(tm)):
            n = n + jnp.where(count > j * MOE_SUB, 1.0, 0.0)
        items.append(n)
    bases = [_dot(jnp.broadcast_to(n, (8, LANES)).astype(BF16), upper_ref[...])[0:1, :] * MOE_SUB for n in items]
    slots = [jnp.sum(oh * (base + rank - 1.0), axis=-1, keepdims=True)
             for oh, base, rank in zip(onehots, bases, ranks)]
    col = lax.broadcasted_iota(jnp.int32, (tm, MOE_ROWS), 1).astype(F32)
    for u, rows in enumerate(tiles):
        place = jnp.where(col == slots[u], 1.0, 0.0).astype(BF16)
        xs_ref[u] = _tn(place, h2s[u]).astype(BF16)
        slot_ref[rows, :] = jnp.broadcast_to(slots[u], (tm, LANES))
        meta_ref[u] = jnp.broadcast_to(items[u], (8, LANES))


def _out_proj(oa, ob, sa, sb, x, wa, wb, wo, gffn, wr, br, tm, oa_transposed):
    b, l, d = x.shape
    nt = l // tm
    per_step = 2 if nt % 2 == 0 else 1
    ts, ns = tm * per_step, nt // per_step
    assert _max_items(tm) * MOE_SUB <= MOE_ROWS
    hv = A_HEADS * V_DIM
    tri = jnp.asarray(np.tril(np.ones((tm, tm), np.float32)), dtype=BF16)
    upper = jnp.asarray(np.triu(np.ones((LANES, LANES), np.float32), 1), dtype=BF16)
    tok = lambda w: pl.BlockSpec((None, ts, w), lambda bi, i: (bi, i, 0))
    per_tile = lambda r, w: pl.BlockSpec((per_step, r, w), lambda bi, i: (bi * ns + i, 0, 0))
    oa_spec = pl.BlockSpec((None, hv, ts), lambda bi, i: (bi, 0, i)) if oa_transposed else tok(hv)
    return pl.pallas_call(
        functools.partial(_out_proj_kernel, oa_transposed=oa_transposed, tm=tm),
        grid=(b, ns),
        in_specs=[oa_spec, tok(B_HEADS * B_DV), tok(d), tok(d), tok(d), _const_spec(wa.shape),
                  _const_spec(wb.shape), _const_spec(wo.shape), _const_spec(gffn.shape),
                  _const_spec(wr.shape), _const_spec(br.shape), _const_spec(tri.shape),
                  _const_spec(upper.shape)],
        out_specs=[tok(d), per_tile(MOE_ROWS, d), tok(LANES), per_tile(8, LANES)],
        out_shape=[jax.ShapeDtypeStruct((b, l, d), F32),
                   jax.ShapeDtypeStruct((b * nt, MOE_ROWS, d), BF16),
                   jax.ShapeDtypeStruct((b, l, LANES), F32),
                   jax.ShapeDtypeStruct((b * nt, 8, LANES), F32)],
        compiler_params=_params(("parallel", "parallel")),
        name="out_proj",
    )(oa, ob, sa, sb, x, wa, wb, wo, gffn, wr, br, tri, upper)


def _work_list(meta, n_items):
    n = meta[:, 0, :N_GROUPS].astype(jnp.int32)
    tiles = n.shape[0]
    first = jnp.cumsum(n, axis=1) - n
    per_seg = n.T.reshape(-1)
    ends = jnp.cumsum(per_seg)
    total = ends[-1]
    i = jnp.minimum(jnp.arange(n_items, dtype=jnp.int32), total - 1)
    seg = jnp.sum((ends[None, :] <= i[:, None]).astype(jnp.int32), axis=1)
    grp, tile = seg // tiles, seg % tiles
    block = first[tile, grp] + i - (ends[seg] - per_seg[seg])
    return tile, block.astype(jnp.int32), grp, total[None].astype(jnp.int32)


def _moe_expert_kernel(tile_ref, block_ref, grp_ref, total_ref, xs_ref, wr_ref, br_ref, wg_ref, wu_ref,
                       wd_ref, ys_ref):
    i = pl.program_id(0)

    @pl.when(i < total_ref[0])
    def _():
        g = grp_ref[i]
        x = xs_ref[...]
        logits = _dot(x, wr_ref[...]) + br_ref[...]
        lane_i = lax.broadcasted_iota(jnp.int32, logits.shape, 1)
        lane = lane_i.astype(F32)
        is_grp = jnp.abs(lane - (EXPERTS_PER_GROUP + 0.5 * (N_GROUPS - 1))) < 0.5 * N_GROUPS
        lg = jnp.where(is_grp, logits, NEG_INF)
        gmax = jnp.max(lg, axis=-1, keepdims=True)
        denom = jnp.sum(jnp.exp(lg - gmax), axis=-1, keepdims=True)
        lg_own = jnp.sum(jnp.where(lane_i == EXPERTS_PER_GROUP + g, logits, 0.0), axis=-1, keepdims=True)
        p_grp = jnp.exp(lg_own - gmax) / denom
        le = jnp.where(lane < EXPERTS_PER_GROUP, logits, NEG_INF)
        v1, i1 = _first_lane_of_max(le, lane)
        le2 = jnp.where(lane == i1, NEG_INF, le)
        v2, i2 = _first_lane_of_max(le2, lane)
        e2 = jnp.exp(v2 - v1)
        w1 = p_grp / (1.0 + e2)
        comb = jnp.where(lane == i1, w1, 0.0) + jnp.where(lane == i2, w1 * e2, 0.0)
        hid = []
        for e in range(EXPERTS_PER_GROUP):
            a = _dot(x, wg_ref[e])
            u = _dot(x, wu_ref[e])
            s, _ = _sig_pair(a)
            hid.append((a * s * u * comb[:, e:e + 1]).astype(BF16))
        hid = jnp.concatenate(hid, axis=-1)
        wd = wd_ref[...].reshape(EXPERTS_PER_GROUP * D_FF_EXPERT, wd_ref.shape[-1])
        ys_ref[...] = _dot(hid, wd).astype(BF16)


def _moe_experts(xs, work, wrg, brg, wg, wu, wd):
    tiles, _, d = xs.shape
    tile, block, grp, total = work
    item = lambda i, t, k, g, n: (t[i], k[i], 0)
    by_group = lambda i, t, k, g, n: (g[i], 0, 0)
    grid_spec = pltpu.PrefetchScalarGridSpec(
        num_scalar_prefetch=4,
        grid=(tile.shape[0],),
        in_specs=[pl.BlockSpec((None, MOE_SUB, d), item),
                  pl.BlockSpec((None, d, LANES), by_group),
                  pl.BlockSpec((None, 1, LANES), by_group),
                  pl.BlockSpec((EXPERTS_PER_GROUP, d, D_FF_EXPERT), by_group),
                  pl.BlockSpec((EXPERTS_PER_GROUP, d, D_FF_EXPERT), by_group),
                  pl.BlockSpec((EXPERTS_PER_GROUP, D_FF_EXPERT, d), by_group)],
        out_specs=pl.BlockSpec((None, MOE_SUB, d), item))
    return pl.pallas_call(
        _moe_expert_kernel,
        grid_spec=grid_spec,
        out_shape=jax.ShapeDtypeStruct(xs.shape, BF16),
        input_output_aliases={4: 0},
        compiler_params=_params(("arbitrary",)),
        name="moe_experts",
    )(tile, block, grp, total, xs, wrg, brg, wg, wu, wd)


def _moe_combine_kernel(ys_ref, slot_ref, x1_ref, gfin_ref, y_ref, *, tm):
    tiles = [slice(u * tm, (u + 1) * tm) for u in range(x1_ref.shape[0] // tm)]
    col = lax.broadcasted_iota(jnp.int32, (tm, MOE_ROWS), 1).astype(F32)
    places = []
    for rows in tiles:
        slot = jnp.concatenate([slot_ref[rows, :]] * (MOE_ROWS // LANES), axis=-1)
        places.append(jnp.where(col == slot, 1.0, 0.0).astype(BF16))
    for u, rows in enumerate(tiles):
        y_ref[rows, :] = _rms(x1_ref[rows, :] + _dot(places[u], ys_ref[u]), gfin_ref[...])


def _moe_combine(ys, slot, x1, gfin, tm):
    b, l, d = x1.shape
    nt = l // tm
    per_step = 2 if nt % 2 == 0 else 1
    ts, ns = tm * per_step, nt // per_step
    tok = lambda w: pl.BlockSpec((None, ts, w), lambda bi, i: (bi, i, 0))
    per_tile = lambda r, w: pl.BlockSpec((per_step, r, w), lambda bi, i: (bi * ns + i, 0, 0))
    return pl.pallas_call(
        functools.partial(_moe_combine_kernel, tm=tm),
        grid=(b, ns),
        in_specs=[per_tile(MOE_ROWS, d), tok(LANES), tok(d), _const_spec(gfin.shape)],
        out_specs=tok(d),
        out_shape=jax.ShapeDtypeStruct((b, l, d), F32),
        compiler_params=_params(("parallel", "parallel")),
        name="moe_combine",
    )(ys, slot, x1, gfin)


def _rope_tables(pos):
    inv = jnp.power(ROPE_THETA, -jnp.arange(HALF, dtype=F32) / HALF)
    ang = pos.astype(F32)[:, None] * inv[None, :]
    cos, sin = jnp.cos(ang), jnp.sin(ang)
    zeros = jnp.zeros_like(sin)
    reps = LANES // ROPE_DIM
    cosp = jnp.tile(jnp.concatenate([cos, cos], axis=1), (1, reps))
    sinlo = jnp.tile(jnp.concatenate([-sin, zeros], axis=1), (1, reps))
    sinhi = jnp.tile(jnp.concatenate([zeros, sin], axis=1), (1, reps))
    return cosp, sinlo, sinhi, cos.T * (SCALE * LOG2E), sin.T * (SCALE * LOG2E)


def _prep_weights(g_mix, w_in, g_q, w_uq, g_kv, w_ukv, lb_hgrn, g_onorm, w_a_out, w_b_out, w_o,
                  g_ffn, w_rg, b_rg, w_re, b_re, w_gate, w_up, w_down, g_final):
    assert w_in.shape[0] == 1, "single-layer step"
    d = w_in.shape[1]
    w = w_in[0]
    n_a = Q_LORA + KV_LORA
    win_a = jnp.concatenate([w[:, :n_a]] + [w[:, n_a:n_a + ROPE_DIM]] * (LANES // ROPE_DIM), axis=1).astype(BF16)
    win_b = jnp.swapaxes(w, 0, 1).astype(BF16)
    per_q = NOPE_DIM + ROPE_DIM
    wuq = jnp.pad(w_uq[0].reshape(Q_LORA, A_HEADS, per_q), ((0, 0), (0, 0), (0, HEAD_PAD - per_q)))
    wuq = wuq.reshape(Q_LORA, A_HEADS * HEAD_PAD).astype(BF16)
    wukv = w_ukv[0].reshape(KV_LORA, A_HEADS, NOPE_DIM + V_DIM)
    w_uk, w_uv = wukv[..., :NOPE_DIM], wukv[..., NOPE_DIM:]
    wuk = jnp.pad(w_uk, ((0, 0), (0, 0), (0, HEAD_PAD - NOPE_DIM))).reshape(KV_LORA, A_HEADS * HEAD_PAD).astype(BF16)
    wuvt = w_uv.reshape(KV_LORA, A_HEADS * V_DIM).T.astype(BF16)
    wq2l = jnp.pad(jnp.transpose(w_uk, (1, 2, 0)), ((0, 0), (0, HEAD_PAD - NOPE_DIM), (0, 0))).astype(BF16)
    esel = jnp.zeros((HEAD_PAD, ROPE_DIM), F32).at[ROPE_LO + jnp.arange(ROPE_DIM), jnp.arange(ROPE_DIM)].set(1.0).astype(BF16)
    wuvp = w_uv.reshape(KV_LORA, A_HEADS * V_DIM).astype(BF16)
    lb = jnp.cumsum(jax.nn.softmax(lb_hgrn.astype(F32), axis=0), axis=0)[0][None, :]
    gon = jnp.tile(g_onorm[0], 2)[None, :]
    wr = jnp.pad(w_rg[0], ((0, 0), (0, LANES - N_GROUPS))).astype(BF16)
    br = jnp.pad(b_rg[0], (0, LANES - N_GROUPS))[None, :]
    rest = LANES - EXPERTS_PER_GROUP - N_GROUPS
    wrg = jnp.concatenate([w_re[0], jnp.broadcast_to(w_rg[0], (N_GROUPS, d, N_GROUPS)),
                           jnp.zeros((N_GROUPS, d, rest), F32)], axis=2).astype(BF16)
    brg = jnp.concatenate([b_re[0], jnp.broadcast_to(b_rg[0], (N_GROUPS, N_GROUPS)),
                           jnp.zeros((N_GROUPS, rest), F32)], axis=1)[:, None, :]
    return dict(
        wrg=wrg, brg=brg,
        gmix=g_mix[0][None, :], win_a=win_a, win_b=win_b, gq=g_q[0][None, :], wuq=wuq, wuqt=wuq.T,
        gkv=g_kv[0][None, :], wuk=wuk, wuvt=wuvt, wq2l=wq2l, esel=esel, wuvp=wuvp, lb=lb, gon=gon,
        wa=w_a_out[0].astype(BF16), wb=w_b_out[0].astype(BF16), wo=w_o[0].astype(BF16),
        gffn=g_ffn[0][None, :], wr=wr, br=br, wg=w_gate[0].astype(BF16), wu=w_up[0].astype(BF16),
        wd=w_down[0].astype(BF16), gfin=g_final[None, :])


def _tile(n, want):
    t = min(n, want)
    while n % t:
        t //= 2
    return t


def _run_group(x, pos, streams, past, w):
    b, l, d = x.shape
    ns, ls = streams
    tm = _tile(l, 512)
    h, ckv, kpe, q, k, vt = _mla_proj(x, w["gmix"], w["win_a"], w["gq"], w["wuq"], w["wuqt"], w["gkv"],
                                      w["wuk"], w["wuvt"], _rope_tables(pos), tm, past is None)
    qb, kb, lf, vb, og, sa, sb = _hgrn_proj(h, w["win_b"], w["lb"], tm)
    as_streams = lambda a: a.reshape(ns, ls, a.shape[-1])
    if past is None:
        oa = _attn(q, k, vt, _tile(l, 512), 256, 4)
        s0t = jnp.zeros((ns, B_HEADS, B_DV, B_DK), F32)
    else:
        cache_ckv, cache_kpe, state = past
        oa = _sample_attn(as_streams(q), cache_ckv, cache_kpe, as_streams(ckv), as_streams(kpe),
                          w["wq2l"], w["esel"], w["wuvp"]).reshape(b, l, A_HEADS * V_DIM)
        s0t = jnp.swapaxes(state.astype(F32), -1, -2)
    ob, snt = _hgrn(qb, kb, lf, vb, og, s0t, w["gon"], ns, _tile(l, 1024 if past is None else 8 * ls))
    x1, xs, slot, meta = _out_proj(oa, ob, sa, sb, x, w["wa"], w["wb"], w["wo"], w["gffn"], w["wr"],
                                   w["br"], tm, past is None)
    work = _work_list(meta, meta.shape[0] * _max_items(tm))
    ys = _moe_experts(xs, work, w["wrg"], w["brg"], w["wg"], w["wu"], w["wd"])
    y = _moe_combine(ys, slot, x1, w["gfin"], tm)
    return y, ckv, kpe, jnp.swapaxes(snt, -1, -2)


def kernel(x_prompt, x_sample, cache_ckv, cache_kpe, state_hgrn, g_mix, w_in, g_q, w_uq, g_kv, w_ukv,
           lb_hgrn, g_onorm, w_a_out, w_b_out, w_o, g_ffn, w_rg, b_rg, w_re, b_re, w_gate, w_up, w_down,
           g_final):
    w = _prep_weights(g_mix, w_in, g_q, w_uq, g_kv, w_ukv, lb_hgrn, g_onorm, w_a_out, w_b_out, w_o,
                      g_ffn, w_rg, b_rg, w_re, b_re, w_gate, w_up, w_down, g_final)
    bp, lp, d = x_prompt.shape
    y_p, ckv_p, kpe_p, st_p = _run_group(x_prompt, jnp.arange(lp, dtype=jnp.int32), (bp, lp), None, w)

    bs, ls, _ = x_sample.shape
    past_len = cache_ckv.shape[2]
    pos_s = past_len + (jnp.arange(bs * ls, dtype=jnp.int32) % ls)
    y_s, ckv_s, kpe_s, st_s = _run_group(x_sample.reshape(1, bs * ls, d), pos_s, (bs, ls),
                                         (cache_ckv[0], jnp.swapaxes(cache_kpe[0], 1, 2), state_hgrn[0]), w)
    return (y_p, y_s.reshape(bs, ls, d),
            ckv_p[None], kpe_p[None], st_p[None].astype(x_prompt.dtype),
            ckv_s.reshape(1, bs, ls, KV_LORA), kpe_s.reshape(1, bs, ls, ROPE_DIM),
            st_s[None].astype(state_hgrn.dtype))
```

```python
import functools

import numpy as np
import jax
import jax.numpy as jnp
from jax import lax
from jax.experimental import pallas as pl
from jax.experimental.pallas import tpu as pltpu

F32 = jnp.float32
BF16 = jnp.bfloat16

EPS = 1e-6
CHUNK = 64
A_HEADS = 8
Q_LORA = 384
KV_LORA = 256
NOPE_DIM = 64
ROPE_DIM = 32
V_DIM = 64
ROPE_THETA = 10000.0
B_HEADS = 8
B_DK = 128
B_DV = 64
N_GROUPS = 4
EXPERTS_PER_GROUP = 8
N_EXPERTS = N_GROUPS * EXPERTS_PER_GROUP
D_FF_EXPERT = 256

LANES = 128
MXU_TILE = 256
VMEM_BYTES_V7X = 64 * 1024 * 1024
VMEM_LIMIT = VMEM_BYTES_V7X - 8 * 1024 * 1024

HEAD_PAD = LANES
V_AUG = V_DIM + 16
SAMPLE_KEY_BLOCK = 1024
ROPE_LO = NOPE_DIM
ROPE_HI = NOPE_DIM + ROPE_DIM
HALF = ROPE_DIM // 2
SCALE = (NOPE_DIM + ROPE_DIM) ** -0.5
LOG2E = 1.4426950408889634
NEG_INF = float("-inf")


def _params(sem):
    return pltpu.CompilerParams(dimension_semantics=sem, vmem_limit_bytes=VMEM_LIMIT)


def _const_spec(shape):
    nd = len(shape)
    return pl.BlockSpec(shape, lambda *_: (0,) * nd, pipeline_mode=pl.Buffered(1))


def _rms(x, g):
    ms = jnp.mean(x * x, axis=-1, keepdims=True)
    return x * lax.rsqrt(ms + EPS) * g


def _sig_pair(x):
    e = jnp.exp(-jnp.abs(x))
    r = 1.0 / (1.0 + e)
    er = e * r
    pos = x >= 0
    return jnp.where(pos, r, er), jnp.where(pos, er, r)


def _nt(a, b):
    return lax.dot_general(a, b, (((1,), (1,)), ((), ())), preferred_element_type=F32)


def _tn(a, b):
    return lax.dot_general(a, b, (((0,), (0,)), ((), ())), preferred_element_type=F32)


def _dot(a, b):
    return jnp.dot(a, b, preferred_element_type=F32)


def _rope(x, cosp, sinlo, sinhi):
    return x * cosp + pltpu.roll(x, LANES - HALF, 1) * sinlo + pltpu.roll(x, HALF, 1) * sinhi


def _mla_proj_kernel(x_ref, gmix_ref, win_ref, gq_ref, wuq_ref, wuqt_ref, gkv_ref, wuk_ref, wuvt_ref,
                     cos_ref, sinlo_ref, sinhi_ref, cost_ref, sint_ref,
                     h_ref, ckv_ref, kpe_ref, q_ref, k_ref, vt_ref, *, q_transposed, tm):
    tiles = [slice(u * tm, (u + 1) * tm) for u in range(x_ref.shape[0] // tm)]
    lane = lax.broadcasted_iota(jnp.int32, (1, LANES), 1)
    rope_lanes = (lane // ROPE_DIM) == (ROPE_LO // ROPE_DIM)

    hs = []
    for rows in tiles:
        h = _rms(x_ref[rows, :], gmix_ref[...]).astype(BF16)
        h_ref[rows, :] = h
        hs.append(h)
    zs = [_dot(h, win_ref[...]) for h in hs]
    cqns = [_rms(z[:, :Q_LORA], gq_ref[...]).astype(BF16) for z in zs]
    ckvs = [_rms(z[:, Q_LORA:Q_LORA + KV_LORA], gkv_ref[...]) for z in zs]

    for rows, cqn in zip(tiles, cqns):
        if q_transposed:
            qt = _nt(wuqt_ref[...], cqn)
            cos_t, sin_t = cost_ref[:, rows], sint_ref[:, rows]
            for hd in range(A_HEADS):
                r0 = hd * HEAD_PAD
                q_ref[r0:r0 + ROPE_LO, rows] = (qt[r0:r0 + ROPE_LO] * (SCALE * LOG2E)).astype(BF16)
                a = qt[r0 + ROPE_LO:r0 + ROPE_LO + HALF]
                b = qt[r0 + ROPE_LO + HALF:r0 + ROPE_HI]
                q_ref[r0 + ROPE_LO:r0 + ROPE_HI, rows] = jnp.concatenate(
                    [a * cos_t - b * sin_t, a * sin_t + b * cos_t], axis=0).astype(BF16)
                q_ref[r0 + ROPE_HI:r0 + HEAD_PAD, rows] = jnp.zeros((HEAD_PAD - ROPE_HI, tm), BF16)
        else:
            q = _dot(cqn, wuq_ref[...])
            cosp, sinlo, sinhi = cos_ref[rows, :], sinlo_ref[rows, :], sinhi_ref[rows, :]
            for hd in range(A_HEADS):
                sl = slice(hd * HEAD_PAD, (hd + 1) * HEAD_PAD)
                qh = q[:, sl]
                qh = jnp.where(rope_lanes, _rope(qh, cosp, sinlo, sinhi), qh) * (SCALE * LOG2E)
                q_ref[rows, sl] = qh.astype(BF16)

    for rows, z, ckv in zip(tiles, zs, ckvs):
        ckv_ref[rows, :] = ckv
        ckv_bf = ckv.astype(BF16)
        cosp, sinlo, sinhi = cos_ref[rows, :], sinlo_ref[rows, :], sinhi_ref[rows, :]
        kpe_rot = _rope(z[:, Q_LORA + KV_LORA:], cosp, sinlo, sinhi)
        kpe_ref[rows, :] = kpe_rot[:, :ROPE_DIM]
        kpe_placed = jnp.where(rope_lanes, kpe_rot, 0.0)
        kn = _dot(ckv_bf, wuk_ref[...])
        for hd in range(A_HEADS):
            sl = slice(hd * HEAD_PAD, (hd + 1) * HEAD_PAD)
            k_ref[rows, sl] = (kn[:, sl] + kpe_placed).astype(BF16)
        vt = _nt(wuvt_ref[...], ckv_bf).astype(BF16)
        ones = jnp.ones((V_AUG - V_DIM, tm), BF16)
        for hd in range(A_HEADS):
            vt_ref[hd * V_AUG:hd * V_AUG + V_DIM, rows] = vt[hd * V_DIM:(hd + 1) * V_DIM]
            vt_ref[hd * V_AUG + V_DIM:(hd + 1) * V_AUG, rows] = ones


def _mla_proj(x, gmix, win_a, gq, wuq, wuqt, gkv, wuk, wuvt, tables, tm, q_transposed):
    b, l, d = x.shape
    hq = A_HEADS * HEAD_PAD
    hv = A_HEADS * V_AUG
    cosp, sinlo, sinhi, cos_t, sin_t = tables
    per_step = 2 if (l // tm) % 2 == 0 else 1
    ts = tm * per_step
    tok = lambda w: pl.BlockSpec((None, ts, w), lambda bi, i: (bi, i, 0))
    tok_t = lambda w: pl.BlockSpec((None, w, ts), lambda bi, i: (bi, 0, i))
    tab = pl.BlockSpec((ts, LANES), lambda bi, i: (i, 0))
    tab_t = pl.BlockSpec((HALF, ts), lambda bi, i: (0, i))
    return pl.pallas_call(
        functools.partial(_mla_proj_kernel, q_transposed=q_transposed, tm=tm),
        grid=(b, l // ts),
        in_specs=[tok(d), _const_spec(gmix.shape), _const_spec(win_a.shape), _const_spec(gq.shape),
                  _const_spec(wuq.shape), _const_spec(wuqt.shape), _const_spec(gkv.shape),
                  _const_spec(wuk.shape), _const_spec(wuvt.shape), tab, tab, tab, tab_t, tab_t],
        out_specs=[tok(d), tok(KV_LORA), tok(ROPE_DIM), tok_t(hq) if q_transposed else tok(hq), tok(hq),
                   tok_t(hv)],
        out_shape=[jax.ShapeDtypeStruct((b, l, d), BF16),
                   jax.ShapeDtypeStruct((b, l, KV_LORA), F32),
                   jax.ShapeDtypeStruct((b, l, ROPE_DIM), F32),
                   jax.ShapeDtypeStruct((b, hq, l) if q_transposed else (b, l, hq), BF16),
                   jax.ShapeDtypeStruct((b, l, hq), BF16),
                   jax.ShapeDtypeStruct((b, hv, l), BF16)],
        compiler_params=_params(("parallel", "parallel")),
        name="mla_proj",
    )(x, gmix, win_a, gq, wuq, wuqt, gkv, wuk, wuvt, cosp, sinlo, sinhi, cos_t, sin_t)


def _hgrn_proj_kernel(h_ref, wt_ref, lb_ref, qb_ref, kb_ref, lf_ref, vb_ref, og_ref, sa_ref, sb_ref, *, row0):
    h = h_ref[...]
    dk = B_HEADS * B_DK
    dv = B_HEADS * B_DV
    dm = sa_ref.shape[-1]
    nb = MXU_TILE
    lb_all = lb_ref[...]

    def z(base, i):
        return _nt(h, wt_ref[row0 + base + i * nb:row0 + base + (i + 1) * nb, :])

    for i in range(dk // nb):
        cols = slice(i * nb, (i + 1) * nb)
        zq = z(0, i)
        qb_ref[:, cols] = zq * jax.nn.sigmoid(zq)
        sf, snf = _sig_pair(z(dk, i))
        lb = lb_all[:, cols]
        lf_ref[:, cols] = jnp.log(lb + (1.0 - lb) * sf)
        kb_ref[:, cols] = (1.0 - lb) * snf
    for i in range(dv // nb):
        cols = slice(i * nb, (i + 1) * nb)
        zg = z(2 * dk + dv, i)
        og_ref[:, cols] = (zg * jax.nn.sigmoid(zg)).astype(BF16)
    for i in range(dm // nb):
        cols = slice(i * nb, (i + 1) * nb)
        sa_ref[:, cols] = jax.nn.sigmoid(z(2 * dk + 2 * dv, i)).astype(BF16)
        sb_ref[:, cols] = jax.nn.sigmoid(z(2 * dk + 2 * dv + dm, i)).astype(BF16)
    for i in range(dv // nb):
        cols = slice(i * nb, (i + 1) * nb)
        vb_ref[:, cols] = z(2 * dk, i).astype(BF16)


def _hgrn_proj(h, win_b, lb, tm):
    b, l, d = h.shape
    dk = B_HEADS * B_DK
    dv = B_HEADS * B_DV
    tok = lambda w: pl.BlockSpec((None, tm, w), lambda bi, i: (bi, i, 0))
    sds = lambda w, dt: jax.ShapeDtypeStruct((b, l, w), dt)
    return pl.pallas_call(
        functools.partial(_hgrn_proj_kernel, row0=Q_LORA + KV_LORA + ROPE_DIM),
        grid=(b, l // tm),
        in_specs=[tok(d), _const_spec(win_b.shape), _const_spec(lb.shape)],
        out_specs=[tok(dk), tok(dk), tok(dk), tok(dv), tok(dv), tok(d), tok(d)],
        out_shape=[sds(dk, F32), sds(dk, F32), sds(dk, F32), sds(dv, BF16), sds(dv, BF16),
                   sds(d, BF16), sds(d, BF16)],
        compiler_params=_params(("parallel", "parallel")),
        name="hgrn_proj",
    )(h, win_b, lb)


def _attn_kernel(qt_ref, k_ref, vt_ref, o_ref, s_sc, m_sc, acc_sc, *, t, tc, hp):
    qi = pl.program_id(2)
    n_chains = t // tc
    m_sc[...] = jnp.full(m_sc.shape, NEG_INF, F32)
    acc_sc[...] = jnp.zeros(acc_sc.shape, F32)
    qk_rows = [slice(hd * HEAD_PAD, (hd + 1) * HEAD_PAD) for hd in range(hp)]
    v_rows = [slice(hd * V_AUG, (hd + 1) * V_AUG) for hd in range(hp)]

    def scores(kb, slot):
        start = pl.multiple_of(kb * t, t)
        for hd in range(hp):
            k = k_ref[pl.ds(start, t), qk_rows[hd]]
            for ch in range(n_chains):
                cols = slice(ch * tc, (ch + 1) * tc)
                s_sc[hd, slot, :, cols] = _dot(k, qt_ref[qk_rows[hd], cols])

    def consume(kb, slot, masked):
        start = pl.multiple_of(kb * t, t)
        for hd in range(hp):
            for ch in range(n_chains):
                cols = slice(ch * tc, (ch + 1) * tc)
                nk = (ch + 1) * tc if masked else t
                vt = vt_ref[v_rows[hd], pl.ds(start, nk)]
                s = s_sc[hd, slot, 0:nk, cols]
                if masked:
                    r = lax.broadcasted_iota(jnp.int32, (nk, tc), 0) // CHUNK
                    c = (lax.broadcasted_iota(jnp.int32, (nk, tc), 1) + ch * tc) // CHUNK
                    s = jnp.where(r <= c, s, NEG_INF)
                m_prev = m_sc[hd, :, cols]
                m_new = jnp.maximum(m_prev, jnp.max(s, axis=0, keepdims=True))
                alpha = jnp.exp2(m_prev - m_new)
                p = jnp.exp2(s - m_new)
                acc_sc[hd, :, cols] = alpha * acc_sc[hd, :, cols] + _dot(vt, p.astype(BF16))
                m_sc[hd, :, cols] = m_new

    scores(0, 0)

    def body(j, carry):
        kb = 2 * j
        scores(kb + 1, 1)
        consume(kb, 0, False)
        scores(kb + 2, 0)
        consume(kb + 1, 1, False)
        return carry

    lax.fori_loop(0, qi // 2, body, 0)

    @pl.when(qi % 2 == 0)
    def _():
        consume(qi, 0, True)

    @pl.when(qi % 2 == 1)
    def _():
        scores(qi, 1)
        consume(qi - 1, 0, False)
        consume(qi, 1, True)

    for hd in range(hp):
        acc = acc_sc[hd]
        o_ref[hd * V_DIM:(hd + 1) * V_DIM, :] = (acc[:V_DIM] / acc[V_DIM:V_DIM + 1]).astype(BF16)


def _attn(qt, k, vt, t, tc, hp):
    b, l, _ = k.shape
    return pl.pallas_call(
        functools.partial(_attn_kernel, t=t, tc=min(tc, t), hp=hp),
        grid=(b, A_HEADS // hp, l // t),
        in_specs=[pl.BlockSpec((None, hp * HEAD_PAD, t), lambda bi, h, i: (bi, h, i)),
                  pl.BlockSpec((None, l, hp * HEAD_PAD), lambda bi, h, i: (bi, 0, h)),
                  pl.BlockSpec((None, hp * V_AUG, l), lambda bi, h, i: (bi, h, 0))],
        out_specs=pl.BlockSpec((None, hp * V_DIM, t), lambda bi, h, i: (bi, h, i)),
        out_shape=jax.ShapeDtypeStruct((b, A_HEADS * V_DIM, l), BF16),
        scratch_shapes=[pltpu.VMEM((hp, 2, t, t), F32), pltpu.VMEM((hp, 1, t), F32),
                        pltpu.VMEM((hp, V_AUG, t), F32)],
        compiler_params=_params(("parallel", "parallel", "arbitrary")),
        name="attn",
    )(qt, k, vt)


def _sample_attn_kernel(q_ref, cc_ref, ckt_ref, nc_ref, nk_ref, wq2l_ref, esel_ref, wuv_ref, o_ref,
                        *, past, n_new):
    qs = q_ref[...]
    heads = [qs[:, hd * HEAD_PAD:(hd + 1) * HEAD_PAD] for hd in range(A_HEADS)]
    ql = jnp.concatenate([_dot(heads[hd], wq2l_ref[hd]) for hd in range(A_HEADS)], axis=0).astype(BF16)
    qp = jnp.concatenate([_dot(heads[hd], esel_ref[...]) for hd in range(A_HEADS)], axis=0).astype(BF16)
    nc = nc_ref[...].astype(BF16)
    nk = nk_ref[...].astype(BF16)
    rows = A_HEADS * n_new

    s_n = _nt(ql, nc) + _nt(qp, nk)
    tq = lax.broadcasted_iota(jnp.int32, (rows, n_new), 0) % n_new
    tk = lax.broadcasted_iota(jnp.int32, (rows, n_new), 1)
    s_n = jnp.where((past + tk) // CHUNK <= (past + tq) // CHUNK, s_n, NEG_INF)
    m = jnp.max(s_n, axis=-1, keepdims=True)
    p_n = jnp.exp2(s_n - m)
    denom = jnp.sum(p_n, axis=-1, keepdims=True)
    acc = _dot(p_n.astype(BF16), nc)

    n_blocks = max(1, past // SAMPLE_KEY_BLOCK)
    assert past % n_blocks == 0
    kb = past // n_blocks

    def scores(i):
        cc = cc_ref[i * kb:(i + 1) * kb, :].astype(BF16)
        ckt = ckt_ref[:, i * kb:(i + 1) * kb].astype(BF16)
        return _nt(ql, cc) + _dot(qp, ckt), cc

    nxt = scores(0)
    for i in range(n_blocks):
        s_c, cc = nxt
        if i + 1 < n_blocks:
            nxt = scores(i + 1)
        m_new = jnp.maximum(m, jnp.max(s_c, axis=-1, keepdims=True))
        alpha = jnp.exp2(m - m_new)
        p_c = jnp.exp2(s_c - m_new)
        denom = alpha * denom + jnp.sum(p_c, axis=-1, keepdims=True)
        acc = alpha * acc + _dot(p_c.astype(BF16), cc)
        m = m_new
    o_lat = (acc / denom).astype(BF16)
    full = _dot(o_lat, wuv_ref[...])
    col_head = lax.broadcasted_iota(jnp.int32, (n_new, A_HEADS * V_DIM), 1) // V_DIM
    out = jnp.zeros((n_new, A_HEADS * V_DIM), F32)
    for hd in range(A_HEADS):
        out = out + jnp.where(col_head == hd, full[hd * n_new:(hd + 1) * n_new], 0.0)
    o_ref[...] = out.astype(BF16)


def _sample_attn(q, cache_ckv, cache_kpe_t, ckv_new, kpe_new, wq2l, esel, wuvp):
    nb, n_new, hq = q.shape
    past = cache_ckv.shape[1]
    hv = A_HEADS * V_DIM
    per = lambda r, w: pl.BlockSpec((None, r, w), lambda bi: (bi, 0, 0))
    return pl.pallas_call(
        functools.partial(_sample_attn_kernel, past=past, n_new=n_new),
        grid=(nb,),
        in_specs=[per(n_new, hq), per(past, KV_LORA), per(ROPE_DIM, past), per(n_new, KV_LORA),
                  per(n_new, ROPE_DIM), _const_spec(wq2l.shape), _const_spec(esel.shape),
                  _const_spec(wuvp.shape)],
        out_specs=per(n_new, hv),
        out_shape=jax.ShapeDtypeStruct((nb, n_new, hv), BF16),
        compiler_params=_params(("parallel",)),
        name="sample_attn",
    )(q, cache_ckv, cache_kpe_t, ckv_new, kpe_new, wq2l, esel, wuvp)


def _level_sizes(c):
    sizes = []
    p = c
    while p >= 2:
        sizes.append(p)
        p //= 2
    return sizes


def _level_masks(c):
    t = np.arange(c)[:, None]
    s = np.arange(c)[None, :]
    out = []
    for p in _level_sizes(c):
        out.append((t // p == s // p) & (t % p >= p // 2) & (s % p < p // 2))
    out.append(t == s)
    return np.stack(out).astype(np.float32)


def _boundary(cum, p, c):
    half = p // 2
    if p >= 8:
        parts = [jnp.broadcast_to(cum[i * p + half - 1:i * p + half, :], (p, cum.shape[1]))
                 for i in range(c // p)]
        return jnp.concatenate(parts, axis=0) if len(parts) > 1 else parts[0]
    sub = lax.broadcasted_iota(jnp.int32, (8, cum.shape[1]), 0)
    parts = []
    for g in range(c // 8):
        lo = jnp.broadcast_to(cum[g * 8 + 1:g * 8 + 2, :], (8, cum.shape[1]))
        hi = jnp.broadcast_to(cum[g * 8 + 5:g * 8 + 6, :], (8, cum.shape[1]))
        parts.append(jnp.where(sub < 4, lo, hi))
    return jnp.concatenate(parts, axis=0)


def _hgrn_kernel(q_ref, k_ref, lf_ref, v_ref, og_ref, s0t_ref, gon_ref, mask_ref, tri_ref,
                 o_ref, snt_ref, st_sc, *, c, n_chunks, per_chunk_state):
    step = pl.program_id(2)
    sizes = _level_sizes(c)
    lane = lax.broadcasted_iota(jnp.int32, (1, LANES), 1)
    low = lane < B_DV
    keep_lo = jnp.where(low, 1.0, 0.0).astype(BF16)
    keep_hi = jnp.where(low, 0.0, 1.0).astype(BF16)

    if not per_chunk_state:
        @pl.when(step == 0)
        def _():
            st_sc[...] = jnp.zeros(st_sc.shape, F32)
            st_sc[0, 0:B_DV, :] = s0t_ref[0]
            st_sc[1, B_DV:2 * B_DV, :] = s0t_ref[1]

    tri = tri_ref[...]
    rows = [slice(ci * c, (ci + 1) * c) for ci in range(n_chunks)]
    heads = [slice(j * B_DK, (j + 1) * B_DK) for j in range(2)]
    items = [(ci, j) for ci in range(n_chunks) for j in range(2)]


    cums = []
    for ci in range(n_chunks):
        lf = lf_ref[rows[ci], :]
        hi = lf.astype(BF16)
        r1 = lf - hi.astype(F32)
        mid = r1.astype(BF16)
        lo = (r1 - mid.astype(F32)).astype(BF16)
        cums.append((_dot(tri, hi) + _dot(tri, mid) + _dot(tri, lo)) * LOG2E)

    def zeros(n):
        return jnp.zeros((n, B_DK), F32)

    a_mats = {}
    for ci, j in items:
        q = q_ref[rows[ci], heads[j]]
        k = k_ref[rows[ci], heads[j]]
        cum = cums[ci][:, heads[j]]
        kbf = k.astype(BF16)
        a = _nt(q.astype(BF16), kbf) * mask_ref[len(sizes)]
        q_cols, k_cols = [], []
        for li, p in enumerate(sizes):
            half = p // 2
            if half % 8 == 0:
                for lo in range(0, c, p):
                    mid, hi = lo + half, lo + p
                    bnd = cum[mid - 1:mid, :]
                    qu = q[mid:hi] * jnp.exp2(cum[mid:hi] - bnd)
                    kl = k[lo:mid] * jnp.exp2(bnd - cum[lo:mid])
                    q_cols.append(jnp.concatenate([x for x in (zeros(mid), qu, zeros(c - hi)) if x.shape[0]], axis=0))
                    k_cols.append(jnp.concatenate([x for x in (zeros(lo), kl, zeros(c - mid)) if x.shape[0]], axis=0))
                continue
            if p == 2:
                qt = (q * jnp.exp2(lf_ref[rows[ci], heads[j]] * LOG2E)).astype(BF16)
                kt = kbf
            else:
                bnd = _boundary(cum, p, c)
                qt = (q * jnp.exp2(jnp.minimum(cum - bnd, 0.0))).astype(BF16)
                kt = (k * jnp.exp2(jnp.minimum(bnd - cum, 0.0))).astype(BF16)
            a = a + _nt(qt, kt) * mask_ref[li]
        if q_cols:
            a = a + _nt(jnp.concatenate(q_cols, axis=1).astype(BF16), jnp.concatenate(k_cols, axis=1).astype(BF16))
        a_mats[ci, j] = a.astype(BF16)

    qhats, kvs, decays, vs = {}, {}, {}, {}
    for ci, j in items:
        cum = cums[ci][:, heads[j]]
        last = cum[c - 1:c, :]
        vs[ci, j] = v_ref[rows[ci], :] * (keep_lo if j == 0 else keep_hi)
        qhats[ci, j] = (q_ref[rows[ci], heads[j]] * jnp.exp2(cum)).astype(BF16)
        khat = (k_ref[rows[ci], heads[j]] * jnp.exp2(last - cum)).astype(BF16)
        kvs[ci, j] = _tn(vs[ci, j], khat)
        decays[ci, j] = jnp.exp2(last)

    pad = jnp.zeros((B_DV, B_DK), F32)
    if not per_chunk_state:
        st = [st_sc[0], st_sc[1]]
    for ci in range(n_chunks):
        if per_chunk_state:
            st = [jnp.concatenate([s0t_ref[ci, 0], pad], axis=0), jnp.concatenate([pad, s0t_ref[ci, 1]], axis=0)]
        o_pair = jnp.zeros((c, LANES), F32)
        for j in range(2):
            o_pair = o_pair + _dot(a_mats[ci, j], vs[ci, j]) + _nt(qhats[ci, j], st[j].astype(BF16))
            st[j] = st[j] * decays[ci, j] + kvs[ci, j]
        if per_chunk_state:
            snt_ref[ci, 0] = st[0][0:B_DV, :]
            snt_ref[ci, 1] = st[1][B_DV:2 * B_DV, :]
        o2 = o_pair * o_pair
        s_lo = jnp.sum(jnp.where(low, o2, 0.0), axis=-1, keepdims=True)
        s_hi = jnp.sum(jnp.where(low, 0.0, o2), axis=-1, keepdims=True)
        ms = jnp.where(low, s_lo, s_hi) * (1.0 / B_DV)
        ob = o_pair * lax.rsqrt(ms + EPS) * gon_ref[...] * og_ref[rows[ci], :].astype(F32)
        o_ref[rows[ci], :] = ob.astype(BF16)
    if not per_chunk_state:
        st_sc[0] = st[0]
        st_sc[1] = st[1]

        @pl.when(step == pl.num_programs(2) - 1)
        def _():
            snt_ref[0] = st_sc[0, 0:B_DV, :]
            snt_ref[1] = st_sc[1, B_DV:2 * B_DV, :]


def _hgrn(qb, kb, lf, vb, og, s0t, gon, n_streams, tc):
    b, l, dk = qb.shape
    per_chunk_state = n_streams != b
    if per_chunk_state:
        assert b == 1 and l % n_streams == 0 and l // n_streams <= CHUNK
        c = l // n_streams
    else:
        c = min(CHUNK, l)
    assert l % tc == 0 and tc % c == 0
    n_chunks = tc // c
    masks = jnp.asarray(_level_masks(c))
    tri = jnp.asarray(np.tril(np.ones((c, c), np.float32)), dtype=BF16)
    pairs = B_HEADS // 2
    wide = pl.BlockSpec((None, tc, 2 * B_DK), lambda bi, p, i: (bi, i, p))
    narrow = pl.BlockSpec((None, tc, 2 * B_DV), lambda bi, p, i: (bi, i, p))
    if per_chunk_state:
        state = pl.BlockSpec((n_chunks, 2, B_DV, B_DK), lambda bi, p, i: (i, p, 0, 0))
    else:
        state = pl.BlockSpec((None, 2, B_DV, B_DK), lambda bi, p, i: (bi, p, 0, 0))
    return pl.pallas_call(
        functools.partial(_hgrn_kernel, c=c, n_chunks=n_chunks, per_chunk_state=per_chunk_state),
        grid=(b, pairs, l // tc),
        in_specs=[wide, wide, wide, narrow, narrow, state, _const_spec(gon.shape),
                  _const_spec(masks.shape), _const_spec(tri.shape)],
        out_specs=[narrow, state],
        out_shape=[jax.ShapeDtypeStruct((b, l, B_HEADS * B_DV), BF16),
                   jax.ShapeDtypeStruct((n_streams, B_HEADS, B_DV, B_DK), F32)],
        scratch_shapes=[pltpu.VMEM((2, 2 * B_DV, B_DK), F32)],
        compiler_params=_params(("parallel", "parallel", "parallel" if per_chunk_state else "arbitrary")),
        name="hgrn",
    )(qb, kb, lf, vb, og, s0t, gon, masks, tri)


MOE_SUB = 144
MOE_ROWS = 1024
_BIG_LANE = float(1 << 20)


def _max_items(tm):
    return (tm + N_GROUPS * (MOE_SUB - 1)) // MOE_SUB


def _first_lane_of_max(x, lane):
    v = jnp.max(x, axis=-1, keepdims=True)
    return v, jnp.min(jnp.where(x == v, lane, _BIG_LANE), axis=-1, keepdims=True)


def _out_proj_kernel(oa_ref, ob_ref, sa_ref, sb_ref, x_ref, wa_ref, wb_ref, wo_ref, gffn_ref,
                     wr_ref, br_ref, tri_ref, upper_ref, x1_ref, xs_ref, slot_ref, meta_ref,
                     *, oa_transposed, tm):
    tiles = [slice(u * tm, (u + 1) * tm) for u in range(x_ref.shape[0] // tm)]
    nb = MXU_TILE
    h2s = []
    for rows in tiles:
        oa = oa_ref[:, rows] if oa_transposed else oa_ref[rows, :]
        ob = ob_ref[rows, :]
        m = []
        for i in range(x_ref.shape[1] // nb):
            cols = slice(i * nb, (i + 1) * nb)
            ya = _tn(oa, wa_ref[:, cols]) if oa_transposed else _dot(oa, wa_ref[:, cols])
            yb = _dot(ob, wb_ref[:, cols])
            m.append((sa_ref[rows, cols].astype(F32) * ya + sb_ref[rows, cols].astype(F32) * yb).astype(BF16))
        x1 = x_ref[rows, :] + _dot(jnp.concatenate(m, axis=-1), wo_ref[...])
        x1_ref[rows, :] = x1
        h2s.append(_rms(x1, gffn_ref[...]).astype(BF16))

    lane = lax.broadcasted_iota(jnp.int32, (tm, LANES), 1).astype(F32)
    logits = [_dot(h2, wr_ref[...]) + br_ref[...] for h2 in h2s]
    grps = [_first_lane_of_max(jnp.where(lane < N_GROUPS, lg, NEG_INF), lane)[1] for lg in logits]
    onehots = [jnp.where(lane == grp, 1.0, 0.0) for grp in grps]
    ranks = [_dot(tri_ref[...], oh.astype(BF16)) for oh in onehots]
    items = []
    for rank in ranks:
        count = rank[tm - 1:tm, :]
        n = jnp.zeros_like(count)
        for j in range(_max_items(tm)):
            n = n + jnp.where(count > j * MOE_SUB, 1.0, 0.0)
        items.append(n)
    bases = [_dot(jnp.broadcast_to(n, (8, LANES)).astype(BF16), upper_ref[...])[0:1, :] * MOE_SUB for n in items]
    slots = [jnp.sum(oh * (base + rank - 1.0), axis=-1, keepdims=True)
             for oh, base, rank in zip(onehots, bases, ranks)]
    col = lax.broadcasted_iota(jnp.int32, (tm, MOE_ROWS), 1).astype(F32)
    for u, rows in enumerate(tiles):
        place = jnp.where(col == slots[u], 1.0, 0.0).astype(BF16)
        xs_ref[u] = _tn(place, h2s[u]).astype(BF16)
        slot_ref[rows, :] = jnp.broadcast_to(slots[u], (tm, LANES))
        meta_ref[u] = jnp.broadcast_to(items[u], (8, LANES))


def _out_proj(oa, ob, sa, sb, x, wa, wb, wo, gffn, wr, br, tm, oa_transposed):
    b, l, d = x.shape
    nt = l // tm
    per_step = 2 if nt % 2 == 0 else 1
    ts, ns = tm * per_step, nt // per_step
    assert _max_items(tm) * MOE_SUB <= MOE_ROWS
    hv = A_HEADS * V_DIM
    tri = jnp.asarray(np.tril(np.ones((tm, tm), np.float32)), dtype=BF16)
    upper = jnp.asarray(np.triu(np.ones((LANES, LANES), np.float32), 1), dtype=BF16)
    tok = lambda w: pl.BlockSpec((None, ts, w), lambda bi, i: (bi, i, 0))
    per_tile = lambda r, w: pl.BlockSpec((per_step, r, w), lambda bi, i: (bi * ns + i, 0, 0))
    oa_spec = pl.BlockSpec((None, hv, ts), lambda bi, i: (bi, 0, i)) if oa_transposed else tok(hv)
    return pl.pallas_call(
        functools.partial(_out_proj_kernel, oa_transposed=oa_transposed, tm=tm),
        grid=(b, ns),
        in_specs=[oa_spec, tok(B_HEADS * B_DV), tok(d), tok(d), tok(d), _const_spec(wa.shape),
                  _const_spec(wb.shape), _const_spec(wo.shape), _const_spec(gffn.shape),
                  _const_spec(wr.shape), _const_spec(br.shape), _const_spec(tri.shape),
                  _const_spec(upper.shape)],
        out_specs=[tok(d), per_tile(MOE_ROWS, d), tok(LANES), per_tile(8, LANES)],
        out_shape=[jax.ShapeDtypeStruct((b, l, d), F32),
                   jax.ShapeDtypeStruct((b * nt, MOE_ROWS, d), BF16),
                   jax.ShapeDtypeStruct((b, l, LANES), F32),
                   jax.ShapeDtypeStruct((b * nt, 8, LANES), F32)],
        compiler_params=_params(("parallel", "parallel")),
        name="out_proj",
    )(oa, ob, sa, sb, x, wa, wb, wo, gffn, wr, br, tri, upper)


def _work_list(meta, n_items):
    n = meta[:, 0, :N_GROUPS].astype(jnp.int32)
    tiles = n.shape[0]
    first = jnp.cumsum(n, axis=1) - n
    per_seg = n.T.reshape(-1)
    ends = jnp.cumsum(per_seg)
    total = ends[-1]
    i = jnp.minimum(jnp.arange(n_items, dtype=jnp.int32), total - 1)
    seg = jnp.sum((ends[None, :] <= i[:, None]).astype(jnp.int32), axis=1)
    grp, tile = seg // tiles, seg % tiles
    block = first[tile, grp] + i - (ends[seg] - per_seg[seg])
    return tile, block.astype(jnp.int32), grp, total[None].astype(jnp.int32)


def _moe_expert_kernel(tile_ref, block_ref, grp_ref, total_ref, xs_ref, wr_ref, br_ref, wg_ref, wu_ref,
                       wd_ref, ys_ref):
    i = pl.program_id(0)

    @pl.when(i < total_ref[0])
    def _():
        g = grp_ref[i]
        x = xs_ref[...]
        logits = _dot(x, wr_ref[...]) + br_ref[...]
        lane_i = lax.broadcasted_iota(jnp.int32, logits.shape, 1)
        lane = lane_i.astype(F32)
        is_grp = jnp.abs(lane - (EXPERTS_PER_GROUP + 0.5 * (N_GROUPS - 1))) < 0.5 * N_GROUPS
        lg = jnp.where(is_grp, logits, NEG_INF)
        gmax = jnp.max(lg, axis=-1, keepdims=True)
        denom = jnp.sum(jnp.exp(lg - gmax), axis=-1, keepdims=True)
        lg_own = jnp.sum(jnp.where(lane_i == EXPERTS_PER_GROUP + g, logits, 0.0), axis=-1, keepdims=True)
        p_grp = jnp.exp(lg_own - gmax) / denom
        le = jnp.where(lane < EXPERTS_PER_GROUP, logits, NEG_INF)
        v1, i1 = _first_lane_of_max(le, lane)
        le2 = jnp.where(lane == i1, NEG_INF, le)
        v2, i2 = _first_lane_of_max(le2, lane)
        e2 = jnp.exp(v2 - v1)
        w1 = p_grp / (1.0 + e2)
        comb = jnp.where(lane == i1, w1, 0.0) + jnp.where(lane == i2, w1 * e2, 0.0)
        hid = []
        for e in range(EXPERTS_PER_GROUP):
            a = _dot(x, wg_ref[e])
            u = _dot(x, wu_ref[e])
            s, _ = _sig_pair(a)
            hid.append((a * s * u * comb[:, e:e + 1]).astype(BF16))
        hid = jnp.concatenate(hid, axis=-1)
        wd = wd_ref[...].reshape(EXPERTS_PER_GROUP * D_FF_EXPERT, wd_ref.shape[-1])
        ys_ref[...] = _dot(hid, wd).astype(BF16)


def _moe_experts(xs, work, wrg, brg, wg, wu, wd):
    tiles, _, d = xs.shape
    tile, block, grp, total = work
    item = lambda i, t, k, g, n: (t[i], k[i], 0)
    by_group = lambda i, t, k, g, n: (g[i], 0, 0)
    grid_spec = pltpu.PrefetchScalarGridSpec(
        num_scalar_prefetch=4,
        grid=(tile.shape[0],),
        in_specs=[pl.BlockSpec((None, MOE_SUB, d), item),
                  pl.BlockSpec((None, d, LANES), by_group),
                  pl.BlockSpec((None, 1, LANES), by_group),
                  pl.BlockSpec((EXPERTS_PER_GROUP, d, D_FF_EXPERT), by_group),
                  pl.BlockSpec((EXPERTS_PER_GROUP, d, D_FF_EXPERT), by_group),
                  pl.BlockSpec((EXPERTS_PER_GROUP, D_FF_EXPERT, d), by_group)],
        out_specs=pl.BlockSpec((None, MOE_SUB, d), item))
    return pl.pallas_call(
        _moe_expert_kernel,
        grid_spec=grid_spec,
        out_shape=jax.ShapeDtypeStruct(xs.shape, BF16),
        input_output_aliases={4: 0},
        compiler_params=_params(("arbitrary",)),
        name="moe_experts",
    )(tile, block, grp, total, xs, wrg, brg, wg, wu, wd)


def _moe_combine_kernel(ys_ref, slot_ref, x1_ref, gfin_ref, y_ref, *, tm):
    tiles = [slice(u * tm, (u + 1) * tm) for u in range(x1_ref.shape[0] // tm)]
    col = lax.broadcasted_iota(jnp.int32, (tm, MOE_ROWS), 1).astype(F32)
    places = []
    for rows in tiles:
        slot = jnp.concatenate([slot_ref[rows, :]] * (MOE_ROWS // LANES), axis=-1)
        places.append(jnp.where(col == slot, 1.0, 0.0).astype(BF16))
    for u, rows in enumerate(tiles):
        y_ref[rows, :] = _rms(x1_ref[rows, :] + _dot(places[u], ys_ref[u]), gfin_ref[...])


def _moe_combine(ys, slot, x1, gfin, tm):
    b, l, d = x1.shape
    nt = l // tm
    per_step = 2 if nt % 2 == 0 else 1
    ts, ns = tm * per_step, nt // per_step
    tok = lambda w: pl.BlockSpec((None, ts, w), lambda bi, i: (bi, i, 0))
    per_tile = lambda r, w: pl.BlockSpec((per_step, r, w), lambda bi, i: (bi * ns + i, 0, 0))
    return pl.pallas_call(
        functools.partial(_moe_combine_kernel, tm=tm),
        grid=(b, ns),
        in_specs=[per_tile(MOE_ROWS, d), tok(LANES), tok(d), _const_spec(gfin.shape)],
        out_specs=tok(d),
        out_shape=jax.ShapeDtypeStruct((b, l, d), F32),
        compiler_params=_params(("parallel", "parallel")),
        name="moe_combine",
    )(ys, slot, x1, gfin)


def _rope_tables(pos):
    inv = jnp.power(ROPE_THETA, -jnp.arange(HALF, dtype=F32) / HALF)
    ang = pos.astype(F32)[:, None] * inv[None, :]
    cos, sin = jnp.cos(ang), jnp.sin(ang)
    zeros = jnp.zeros_like(sin)
    reps = LANES // ROPE_DIM
    cosp = jnp.tile(jnp.concatenate([cos, cos], axis=1), (1, reps))
    sinlo = jnp.tile(jnp.concatenate([-sin, zeros], axis=1), (1, reps))
    sinhi = jnp.tile(jnp.concatenate([zeros, sin], axis=1), (1, reps))
    return cosp, sinlo, sinhi, cos.T * (SCALE * LOG2E), sin.T * (SCALE * LOG2E)


def _prep_weights(g_mix, w_in, g_q, w_uq, g_kv, w_ukv, lb_hgrn, g_onorm, w_a_out, w_b_out, w_o,
                  g_ffn, w_rg, b_rg, w_re, b_re, w_gate, w_up, w_down, g_final):
    assert w_in.shape[0] == 1, "single-layer step"
    d = w_in.shape[1]
    w = w_in[0]
    n_a = Q_LORA + KV_LORA
    win_a = jnp.concatenate([w[:, :n_a]] + [w[:, n_a:n_a + ROPE_DIM]] * (LANES // ROPE_DIM), axis=1).astype(BF16)
    win_b = jnp.swapaxes(w, 0, 1).astype(BF16)
    per_q = NOPE_DIM + ROPE_DIM
    wuq = jnp.pad(w_uq[0].reshape(Q_LORA, A_HEADS, per_q), ((0, 0), (0, 0), (0, HEAD_PAD - per_q)))
    wuq = wuq.reshape(Q_LORA, A_HEADS * HEAD_PAD).astype(BF16)
    wukv = w_ukv[0].reshape(KV_LORA, A_HEADS, NOPE_DIM + V_DIM)
    w_uk, w_uv = wukv[..., :NOPE_DIM], wukv[..., NOPE_DIM:]
    wuk = jnp.pad(w_uk, ((0, 0), (0, 0), (0, HEAD_PAD - NOPE_DIM))).reshape(KV_LORA, A_HEADS * HEAD_PAD).astype(BF16)
    wuvt = w_uv.reshape(KV_LORA, A_HEADS * V_DIM).T.astype(BF16)
    wq2l = jnp.pad(jnp.transpose(w_uk, (1, 2, 0)), ((0, 0), (0, HEAD_PAD - NOPE_DIM), (0, 0))).astype(BF16)
    esel = jnp.zeros((HEAD_PAD, ROPE_DIM), F32).at[ROPE_LO + jnp.arange(ROPE_DIM), jnp.arange(ROPE_DIM)].set(1.0).astype(BF16)
    wuvp = w_uv.reshape(KV_LORA, A_HEADS * V_DIM).astype(BF16)
    lb = jnp.cumsum(jax.nn.softmax(lb_hgrn.astype(F32), axis=0), axis=0)[0][None, :]
    gon = jnp.tile(g_onorm[0], 2)[None, :]
    wr = jnp.pad(w_rg[0], ((0, 0), (0, LANES - N_GROUPS))).astype(BF16)
    br = jnp.pad(b_rg[0], (0, LANES - N_GROUPS))[None, :]
    rest = LANES - EXPERTS_PER_GROUP - N_GROUPS
    wrg = jnp.concatenate([w_re[0], jnp.broadcast_to(w_rg[0], (N_GROUPS, d, N_GROUPS)),
                           jnp.zeros((N_GROUPS, d, rest), F32)], axis=2).astype(BF16)
    brg = jnp.concatenate([b_re[0], jnp.broadcast_to(b_rg[0], (N_GROUPS, N_GROUPS)),
                           jnp.zeros((N_GROUPS, rest), F32)], axis=1)[:, None, :]
    return dict(
        wrg=wrg, brg=brg,
        gmix=g_mix[0][None, :], win_a=win_a, win_b=win_b, gq=g_q[0][None, :], wuq=wuq, wuqt=wuq.T,
        gkv=g_kv[0][None, :], wuk=wuk, wuvt=wuvt, wq2l=wq2l, esel=esel, wuvp=wuvp, lb=lb, gon=gon,
        wa=w_a_out[0].astype(BF16), wb=w_b_out[0].astype(BF16), wo=w_o[0].astype(BF16),
        gffn=g_ffn[0][None, :], wr=wr, br=br, wg=w_gate[0].astype(BF16), wu=w_up[0].astype(BF16),
        wd=w_down[0].astype(BF16), gfin=g_final[None, :])


def _tile(n, want):
    t = min(n, want)
    while n % t:
        t //= 2
    return t


TOKEN_TILE = 512
ATTN_TILE = 512
ATTN_CHAIN = 256
ATTN_HEADS_PER_STEP = 4
HGRN_STEP = 1024
HGRN_STREAMS_PER_STEP = 8


def _run_group(x, pos, streams, past, w):
    b, l, d = x.shape
    ns, ls = streams
    tm = _tile(l, TOKEN_TILE)
    h, ckv, kpe, q, k, vt = _mla_proj(x, w["gmix"], w["win_a"], w["gq"], w["wuq"], w["wuqt"], w["gkv"],
                                      w["wuk"], w["wuvt"], _rope_tables(pos), tm, past is None)
    qb, kb, lf, vb, og, sa, sb = _hgrn_proj(h, w["win_b"], w["lb"], tm)
    as_streams = lambda a: a.reshape(ns, ls, a.shape[-1])
    if past is None:
        oa = _attn(q, k, vt, _tile(l, ATTN_TILE), ATTN_CHAIN, ATTN_HEADS_PER_STEP)
        s0t = jnp.zeros((ns, B_HEADS, B_DV, B_DK), F32)
    else:
        cache_ckv, cache_kpe, state = past
        oa = _sample_attn(as_streams(q), cache_ckv, cache_kpe, as_streams(ckv), as_streams(kpe),
                          w["wq2l"], w["esel"], w["wuvp"]).reshape(b, l, A_HEADS * V_DIM)
        s0t = jnp.swapaxes(state.astype(F32), -1, -2)
    ob, snt = _hgrn(qb, kb, lf, vb, og, s0t, w["gon"], ns,
                    _tile(l, HGRN_STEP if past is None else HGRN_STREAMS_PER_STEP * ls))
    x1, xs, slot, meta = _out_proj(oa, ob, sa, sb, x, w["wa"], w["wb"], w["wo"], w["gffn"], w["wr"],
                                   w["br"], tm, past is None)
    work = _work_list(meta, meta.shape[0] * _max_items(tm))
    ys = _moe_experts(xs, work, w["wrg"], w["brg"], w["wg"], w["wu"], w["wd"])
    y = _moe_combine(ys, slot, x1, w["gfin"], tm)
    return y, ckv, kpe, jnp.swapaxes(snt, -1, -2)


def kernel(x_prompt, x_sample, cache_ckv, cache_kpe, state_hgrn, g_mix, w_in, g_q, w_uq, g_kv, w_ukv,
           lb_hgrn, g_onorm, w_a_out, w_b_out, w_o, g_ffn, w_rg, b_rg, w_re, b_re, w_gate, w_up, w_down,
           g_final):
    w = _prep_weights(g_mix, w_in, g_q, w_uq, g_kv, w_ukv, lb_hgrn, g_onorm, w_a_out, w_b_out, w_o,
                      g_ffn, w_rg, b_rg, w_re, b_re, w_gate, w_up, w_down, g_final)
    bp, lp, d = x_prompt.shape
    y_p, ckv_p, kpe_p, st_p = _run_group(x_prompt, jnp.arange(lp, dtype=jnp.int32), (bp, lp), None, w)

    bs, ls, _ = x_sample.shape
    past_len = cache_ckv.shape[2]
    pos_s = past_len + (jnp.arange(bs * ls, dtype=jnp.int32) % ls)
    y_s, ckv_s, kpe_s, st_s = _run_group(x_sample.reshape(1, bs * ls, d), pos_s, (bs, ls),
                                         (cache_ckv[0], jnp.swapaxes(cache_kpe[0], 1, 2), state_hgrn[0]), w)
    return (y_p, y_s.reshape(bs, ls, d),
            ckv_p[None], kpe_p[None], st_p[None].astype(x_prompt.dtype),
            ckv_s.reshape(1, bs, ls, KV_LORA), kpe_s.reshape(1, bs, ls, ROPE_DIM),
            st_s[None].astype(state_hgrn.dtype))
```

```python
import functools

import numpy as np
import jax
import jax.numpy as jnp
from jax import lax
from jax.experimental import pallas as pl
from jax.experimental.pallas import tpu as pltpu

F32 = jnp.float32
BF16 = jnp.bfloat16

EPS = 1e-6
CHUNK = 64
A_HEADS = 8
Q_LORA = 384
KV_LORA = 256
NOPE_DIM = 64
ROPE_DIM = 32
V_DIM = 64
ROPE_THETA = 10000.0
B_HEADS = 8
B_DK = 128
B_DV = 64
N_GROUPS = 4
EXPERTS_PER_GROUP = 8
N_EXPERTS = N_GROUPS * EXPERTS_PER_GROUP
D_FF_EXPERT = 256

LANES = 128
MXU_TILE = 256
VMEM_BYTES_V7X = 64 * 1024 * 1024
VMEM_LIMIT = VMEM_BYTES_V7X - 8 * 1024 * 1024

HEAD_PAD = LANES
V_AUG = V_DIM + 16
SAMPLE_KEY_BLOCK = 1024
ROPE_LO = NOPE_DIM
ROPE_HI = NOPE_DIM + ROPE_DIM
HALF = ROPE_DIM // 2
SCALE = (NOPE_DIM + ROPE_DIM) ** -0.5
LOG2E = 1.4426950408889634
NEG_INF = float("-inf")


def _params(sem):
    return pltpu.CompilerParams(dimension_semantics=sem, vmem_limit_bytes=VMEM_LIMIT)


def _const_spec(shape):
    nd = len(shape)
    return pl.BlockSpec(shape, lambda *_: (0,) * nd, pipeline_mode=pl.Buffered(1))


def _rms(x, g):
    ms = jnp.mean(x * x, axis=-1, keepdims=True)
    return x * lax.rsqrt(ms + EPS) * g


def _sig_pair(x):
    e = jnp.exp(-jnp.abs(x))
    r = 1.0 / (1.0 + e)
    er = e * r
    pos = x >= 0
    return jnp.where(pos, r, er), jnp.where(pos, er, r)


def _nt(a, b):
    return lax.dot_general(a, b, (((1,), (1,)), ((), ())), preferred_element_type=F32)


def _tn(a, b):
    return lax.dot_general(a, b, (((0,), (0,)), ((), ())), preferred_element_type=F32)


def _dot(a, b):
    return jnp.dot(a, b, preferred_element_type=F32)


def _rope(x, cosp, sinlo, sinhi):
    return x * cosp + pltpu.roll(x, LANES - HALF, 1) * sinlo + pltpu.roll(x, HALF, 1) * sinhi


def _mla_proj_kernel(x_ref, gmix_ref, win_ref, gq_ref, wuq_ref, wuqt_ref, gkv_ref, wuk_ref, wuvt_ref,
                     cos_ref, sinlo_ref, sinhi_ref, cost_ref, sint_ref,
                     h_ref, ckv_ref, kpe_ref, q_ref, k_ref, vt_ref, *, q_transposed, tm):
    tiles = [slice(u * tm, (u + 1) * tm) for u in range(x_ref.shape[0] // tm)]
    lane = lax.broadcasted_iota(jnp.int32, (1, LANES), 1)
    rope_lanes = (lane // ROPE_DIM) == (ROPE_LO // ROPE_DIM)

    hs = []
    for rows in tiles:
        h = _rms(x_ref[rows, :], gmix_ref[...]).astype(BF16)
        h_ref[rows, :] = h
        hs.append(h)
    zs = [_dot(h, win_ref[...]) for h in hs]
    cqns = [_rms(z[:, :Q_LORA], gq_ref[...]).astype(BF16) for z in zs]
    ckvs = [_rms(z[:, Q_LORA:Q_LORA + KV_LORA], gkv_ref[...]) for z in zs]

    for rows, cqn in zip(tiles, cqns):
        if q_transposed:
            qt = _nt(wuqt_ref[...], cqn)
            cos_t, sin_t = cost_ref[:, rows], sint_ref[:, rows]
            for hd in range(A_HEADS):
                r0 = hd * HEAD_PAD
                q_ref[r0:r0 + ROPE_LO, rows] = (qt[r0:r0 + ROPE_LO] * (SCALE * LOG2E)).astype(BF16)
                a = qt[r0 + ROPE_LO:r0 + ROPE_LO + HALF]
                b = qt[r0 + ROPE_LO + HALF:r0 + ROPE_HI]
                q_ref[r0 + ROPE_LO:r0 + ROPE_HI, rows] = jnp.concatenate(
                    [a * cos_t - b * sin_t, a * sin_t + b * cos_t], axis=0).astype(BF16)
                q_ref[r0 + ROPE_HI:r0 + HEAD_PAD, rows] = jnp.zeros((HEAD_PAD - ROPE_HI, tm), BF16)
        else:
            q = _dot(cqn, wuq_ref[...])
            cosp, sinlo, sinhi = cos_ref[rows, :], sinlo_ref[rows, :], sinhi_ref[rows, :]
            for hd in range(A_HEADS):
                sl = slice(hd * HEAD_PAD, (hd + 1) * HEAD_PAD)
                qh = q[:, sl]
                qh = jnp.where(rope_lanes, _rope(qh, cosp, sinlo, sinhi), qh) * (SCALE * LOG2E)
                q_ref[rows, sl] = qh.astype(BF16)

    for rows, z, ckv in zip(tiles, zs, ckvs):
        ckv_ref[rows, :] = ckv
        ckv_bf = ckv.astype(BF16)
        cosp, sinlo, sinhi = cos_ref[rows, :], sinlo_ref[rows, :], sinhi_ref[rows, :]
        kpe_rot = _rope(z[:, Q_LORA + KV_LORA:], cosp, sinlo, sinhi)
        kpe_ref[rows, :] = kpe_rot[:, :ROPE_DIM]
        kpe_placed = jnp.where(rope_lanes, kpe_rot, 0.0)
        kn = _dot(ckv_bf, wuk_ref[...])
        for hd in range(A_HEADS):
            sl = slice(hd * HEAD_PAD, (hd + 1) * HEAD_PAD)
            k_ref[rows, sl] = (kn[:, sl] + kpe_placed).astype(BF16)
        vt = _nt(wuvt_ref[...], ckv_bf).astype(BF16)
        ones = jnp.ones((V_AUG - V_DIM, tm), BF16)
        for hd in range(A_HEADS):
            vt_ref[hd * V_AUG:hd * V_AUG + V_DIM, rows] = vt[hd * V_DIM:(hd + 1) * V_DIM]
            vt_ref[hd * V_AUG + V_DIM:(hd + 1) * V_AUG, rows] = ones


def _mla_proj(x, gmix, win_a, gq, wuq, wuqt, gkv, wuk, wuvt, tables, tm, q_transposed):
    b, l, d = x.shape
    hq = A_HEADS * HEAD_PAD
    hv = A_HEADS * V_AUG
    cosp, sinlo, sinhi, cos_t, sin_t = tables
    per_step = 2 if (l // tm) % 2 == 0 else 1
    ts = tm * per_step
    tok = lambda w: pl.BlockSpec((None, ts, w), lambda bi, i: (bi, i, 0))
    tok_t = lambda w: pl.BlockSpec((None, w, ts), lambda bi, i: (bi, 0, i))
    tab = pl.BlockSpec((ts, LANES), lambda bi, i: (i, 0))
    tab_t = pl.BlockSpec((HALF, ts), lambda bi, i: (0, i))
    return pl.pallas_call(
        functools.partial(_mla_proj_kernel, q_transposed=q_transposed, tm=tm),
        grid=(b, l // ts),
        in_specs=[tok(d), _const_spec(gmix.shape), _const_spec(win_a.shape), _const_spec(gq.shape),
                  _const_spec(wuq.shape), _const_spec(wuqt.shape), _const_spec(gkv.shape),
                  _const_spec(wuk.shape), _const_spec(wuvt.shape), tab, tab, tab, tab_t, tab_t],
        out_specs=[tok(d), tok(KV_LORA), tok(ROPE_DIM), tok_t(hq) if q_transposed else tok(hq), tok(hq),
                   tok_t(hv)],
        out_shape=[jax.ShapeDtypeStruct((b, l, d), BF16),
                   jax.ShapeDtypeStruct((b, l, KV_LORA), F32),
                   jax.ShapeDtypeStruct((b, l, ROPE_DIM), F32),
                   jax.ShapeDtypeStruct((b, hq, l) if q_transposed else (b, l, hq), BF16),
                   jax.ShapeDtypeStruct((b, l, hq), BF16),
                   jax.ShapeDtypeStruct((b, hv, l), BF16)],
        compiler_params=_params(("parallel", "parallel")),
        name="mla_proj",
    )(x, gmix, win_a, gq, wuq, wuqt, gkv, wuk, wuvt, cosp, sinlo, sinhi, cos_t, sin_t)


def _hgrn_proj_kernel(h_ref, wt_ref, lb_ref, qb_ref, kb_ref, lf_ref, vb_ref, og_ref, sa_ref, sb_ref, *, row0):
    h = h_ref[...]
    dk = B_HEADS * B_DK
    dv = B_HEADS * B_DV
    dm = sa_ref.shape[-1]
    nb = MXU_TILE
    lb_all = lb_ref[...]

    def z(base, i):
        return _nt(h, wt_ref[row0 + base + i * nb:row0 + base + (i + 1) * nb, :])

    for i in range(dk // nb):
        cols = slice(i * nb, (i + 1) * nb)
        zq = z(0, i)
        qb_ref[:, cols] = zq * jax.nn.sigmoid(zq)
        sf, snf = _sig_pair(z(dk, i))
        lb = lb_all[:, cols]
        lf_ref[:, cols] = jnp.log(lb + (1.0 - lb) * sf)
        kb_ref[:, cols] = (1.0 - lb) * snf
    for i in range(dv // nb):
        cols = slice(i * nb, (i + 1) * nb)
        vb_ref[:, cols] = z(2 * dk, i).astype(BF16)
        zg = z(2 * dk + dv, i)
        og_ref[:, cols] = (zg * jax.nn.sigmoid(zg)).astype(BF16)
    for i in range(dm // nb):
        cols = slice(i * nb, (i + 1) * nb)
        sa_ref[:, cols] = jax.nn.sigmoid(z(2 * dk + 2 * dv, i)).astype(BF16)
        sb_ref[:, cols] = jax.nn.sigmoid(z(2 * dk + 2 * dv + dm, i)).astype(BF16)


def _hgrn_proj(h, win_b, lb, tm):
    b, l, d = h.shape
    dk = B_HEADS * B_DK
    dv = B_HEADS * B_DV
    tok = lambda w: pl.BlockSpec((None, tm, w), lambda bi, i: (bi, i, 0))
    sds = lambda w, dt: jax.ShapeDtypeStruct((b, l, w), dt)
    return pl.pallas_call(
        functools.partial(_hgrn_proj_kernel, row0=Q_LORA + KV_LORA + ROPE_DIM),
        grid=(b, l // tm),
        in_specs=[tok(d), _const_spec(win_b.shape), _const_spec(lb.shape)],
        out_specs=[tok(dk), tok(dk), tok(dk), tok(dv), tok(dv), tok(d), tok(d)],
        out_shape=[sds(dk, F32), sds(dk, F32), sds(dk, F32), sds(dv, BF16), sds(dv, BF16),
                   sds(d, BF16), sds(d, BF16)],
        compiler_params=_params(("parallel", "parallel")),
        name="hgrn_proj",
    )(h, win_b, lb)


def _attn_kernel(qt_ref, k_ref, vt_ref, o_ref, s_sc, m_sc, acc_sc, *, t, tc, hp):
    qi = pl.program_id(2)
    n_chains = t // tc
    m_sc[...] = jnp.full(m_sc.shape, NEG_INF, F32)
    acc_sc[...] = jnp.zeros(acc_sc.shape, F32)
    qk_rows = [slice(hd * HEAD_PAD, (hd + 1) * HEAD_PAD) for hd in range(hp)]
    v_rows = [slice(hd * V_AUG, (hd + 1) * V_AUG) for hd in range(hp)]

    def scores(kb, slot):
        start = pl.multiple_of(kb * t, t)
        for hd in range(hp):
            k = k_ref[pl.ds(start, t), qk_rows[hd]]
            for ch in range(n_chains):
                cols = slice(ch * tc, (ch + 1) * tc)
                s_sc[hd, slot, :, cols] = _dot(k, qt_ref[qk_rows[hd], cols])

    def consume(kb, slot, masked):
        start = pl.multiple_of(kb * t, t)
        for hd in range(hp):
            for ch in range(n_chains):
                cols = slice(ch * tc, (ch + 1) * tc)
                nk = (ch + 1) * tc if masked else t
                vt = vt_ref[v_rows[hd], pl.ds(start, nk)]
                s = s_sc[hd, slot, 0:nk, cols]
                if masked:
                    r = lax.broadcasted_iota(jnp.int32, (nk, tc), 0) // CHUNK
                    c = (lax.broadcasted_iota(jnp.int32, (nk, tc), 1) + ch * tc) // CHUNK
                    s = jnp.where(r <= c, s, NEG_INF)
                m_prev = m_sc[hd, :, cols]
                m_new = jnp.maximum(m_prev, jnp.max(s, axis=0, keepdims=True))
                alpha = jnp.exp2(m_prev - m_new)
                p = jnp.exp2(s - m_new)
                acc_sc[hd, :, cols] = alpha * acc_sc[hd, :, cols] + _dot(vt, p.astype(BF16))
                m_sc[hd, :, cols] = m_new

    scores(0, 0)

    def body(j, carry):
        kb = 2 * j
        scores(kb + 1, 1)
        consume(kb, 0, False)
        scores(kb + 2, 0)
        consume(kb + 1, 1, False)
        return carry

    lax.fori_loop(0, qi // 2, body, 0)

    @pl.when(qi % 2 == 0)
    def _():
        consume(qi, 0, True)

    @pl.when(qi % 2 == 1)
    def _():
        scores(qi, 1)
        consume(qi - 1, 0, False)
        consume(qi, 1, True)

    for hd in range(hp):
        acc = acc_sc[hd]
        o_ref[hd * V_DIM:(hd + 1) * V_DIM, :] = (acc[:V_DIM] / acc[V_DIM:V_DIM + 1]).astype(BF16)


def _attn(qt, k, vt, t, tc, hp):
    b, l, _ = k.shape
    once = dict(pipeline_mode=pl.Buffered(1)) if hp == A_HEADS else {}
    return pl.pallas_call(
        functools.partial(_attn_kernel, t=t, tc=min(tc, t), hp=hp),
        grid=(b, A_HEADS // hp, l // t),
        in_specs=[pl.BlockSpec((None, hp * HEAD_PAD, t), lambda bi, h, i: (bi, h, i)),
                  pl.BlockSpec((None, l, hp * HEAD_PAD), lambda bi, h, i: (bi, 0, h), **once),
                  pl.BlockSpec((None, hp * V_AUG, l), lambda bi, h, i: (bi, h, 0), **once)],
        out_specs=pl.BlockSpec((None, hp * V_DIM, t), lambda bi, h, i: (bi, h, i)),
        out_shape=jax.ShapeDtypeStruct((b, A_HEADS * V_DIM, l), BF16),
        scratch_shapes=[pltpu.VMEM((hp, 2, t, t), F32), pltpu.VMEM((hp, 1, t), F32),
                        pltpu.VMEM((hp, V_AUG, t), F32)],
        compiler_params=_params(("parallel", "parallel", "arbitrary")),
        name="attn",
    )(qt, k, vt)


def _sample_attn_kernel(q_ref, cc_ref, ckt_ref, nc_ref, nk_ref, wq2l_ref, esel_ref, wuv_ref, o_ref,
                        *, past, n_new):
    qs = q_ref[...]
    heads = [qs[:, hd * HEAD_PAD:(hd + 1) * HEAD_PAD] for hd in range(A_HEADS)]
    ql = jnp.concatenate([_dot(heads[hd], wq2l_ref[hd]) for hd in range(A_HEADS)], axis=0).astype(BF16)
    qp = jnp.concatenate([_dot(heads[hd], esel_ref[...]) for hd in range(A_HEADS)], axis=0).astype(BF16)
    nc = nc_ref[...].astype(BF16)
    nk = nk_ref[...].astype(BF16)
    rows = A_HEADS * n_new

    s_n = _nt(ql, nc) + _nt(qp, nk)
    tq = lax.broadcasted_iota(jnp.int32, (rows, n_new), 0) % n_new
    tk = lax.broadcasted_iota(jnp.int32, (rows, n_new), 1)
    s_n = jnp.where((past + tk) // CHUNK <= (past + tq) // CHUNK, s_n, NEG_INF)
    m = jnp.max(s_n, axis=-1, keepdims=True)
    p_n = jnp.exp2(s_n - m)
    denom = jnp.sum(p_n, axis=-1, keepdims=True)
    acc = _dot(p_n.astype(BF16), nc)

    n_blocks = max(1, past // SAMPLE_KEY_BLOCK)
    assert past % n_blocks == 0
    kb = past // n_blocks

    def scores(i):
        cc = cc_ref[i * kb:(i + 1) * kb, :].astype(BF16)
        ckt = ckt_ref[:, i * kb:(i + 1) * kb].astype(BF16)
        return _nt(ql, cc) + _dot(qp, ckt), cc

    nxt = scores(0)
    for i in range(n_blocks):
        s_c, cc = nxt
        if i + 1 < n_blocks:
            nxt = scores(i + 1)
        m_new = jnp.maximum(m, jnp.max(s_c, axis=-1, keepdims=True))
        alpha = jnp.exp2(m - m_new)
        p_c = jnp.exp2(s_c - m_new)
        denom = alpha * denom + jnp.sum(p_c, axis=-1, keepdims=True)
        acc = alpha * acc + _dot(p_c.astype(BF16), cc)
        m = m_new
    o_lat = (acc / denom).astype(BF16)
    full = _dot(o_lat, wuv_ref[...])
    col_head = lax.broadcasted_iota(jnp.int32, (n_new, A_HEADS * V_DIM), 1) // V_DIM
    out = jnp.zeros((n_new, A_HEADS * V_DIM), F32)
    for hd in range(A_HEADS):
        out = out + jnp.where(col_head == hd, full[hd * n_new:(hd + 1) * n_new], 0.0)
    o_ref[...] = out.astype(BF16)


def _sample_attn(q, cache_ckv, cache_kpe_t, ckv_new, kpe_new, wq2l, esel, wuvp):
    nb, n_new, hq = q.shape
    past = cache_ckv.shape[1]
    hv = A_HEADS * V_DIM
    per = lambda r, w: pl.BlockSpec((None, r, w), lambda bi: (bi, 0, 0))
    return pl.pallas_call(
        functools.partial(_sample_attn_kernel, past=past, n_new=n_new),
        grid=(nb,),
        in_specs=[per(n_new, hq), per(past, KV_LORA), per(ROPE_DIM, past), per(n_new, KV_LORA),
                  per(n_new, ROPE_DIM), _const_spec(wq2l.shape), _const_spec(esel.shape),
                  _const_spec(wuvp.shape)],
        out_specs=per(n_new, hv),
        out_shape=jax.ShapeDtypeStruct((nb, n_new, hv), BF16),
        compiler_params=_params(("parallel",)),
        name="sample_attn",
    )(q, cache_ckv, cache_kpe_t, ckv_new, kpe_new, wq2l, esel, wuvp)


def _level_sizes(c):
    sizes = []
    p = c
    while p >= 2:
        sizes.append(p)
        p //= 2
    return sizes


def _level_masks(c):
    t = np.arange(c)[:, None]
    s = np.arange(c)[None, :]
    out = []
    for p in _level_sizes(c):
        out.append((t // p == s // p) & (t % p >= p // 2) & (s % p < p // 2))
    out.append(t == s)
    return np.stack(out).astype(np.float32)


def _boundary(cum, p, c):
    half = p // 2
    if p >= 8:
        parts = [jnp.broadcast_to(cum[i * p + half - 1:i * p + half, :], (p, cum.shape[1]))
                 for i in range(c // p)]
        return jnp.concatenate(parts, axis=0) if len(parts) > 1 else parts[0]
    sub = lax.broadcasted_iota(jnp.int32, (8, cum.shape[1]), 0)
    parts = []
    for g in range(c // 8):
        lo = jnp.broadcast_to(cum[g * 8 + 1:g * 8 + 2, :], (8, cum.shape[1]))
        hi = jnp.broadcast_to(cum[g * 8 + 5:g * 8 + 6, :], (8, cum.shape[1]))
        parts.append(jnp.where(sub < 4, lo, hi))
    return jnp.concatenate(parts, axis=0)


def _hgrn_kernel(q_ref, k_ref, lf_ref, v_ref, og_ref, s0t_ref, gon_ref, mask_ref, tri_ref,
                 o_ref, snt_ref, st_sc, *, c, n_chunks, per_chunk_state):
    step = pl.program_id(2)
    sizes = _level_sizes(c)
    lane = lax.broadcasted_iota(jnp.int32, (1, LANES), 1)
    low = lane < B_DV
    keep_lo = jnp.where(low, 1.0, 0.0).astype(BF16)
    keep_hi = jnp.where(low, 0.0, 1.0).astype(BF16)

    if not per_chunk_state:
        @pl.when(step == 0)
        def _():
            st_sc[...] = jnp.zeros(st_sc.shape, F32)
            st_sc[0, 0:B_DV, :] = s0t_ref[0]
            st_sc[1, B_DV:2 * B_DV, :] = s0t_ref[1]

    tri = tri_ref[...]
    rows = [slice(ci * c, (ci + 1) * c) for ci in range(n_chunks)]
    heads = [slice(j * B_DK, (j + 1) * B_DK) for j in range(2)]
    items = [(ci, j) for ci in range(n_chunks) for j in range(2)]


    cums = []
    for ci in range(n_chunks):
        lf = lf_ref[rows[ci], :]
        hi = lf.astype(BF16)
        r1 = lf - hi.astype(F32)
        mid = r1.astype(BF16)
        lo = (r1 - mid.astype(F32)).astype(BF16)
        cums.append((_dot(tri, hi) + _dot(tri, mid) + _dot(tri, lo)) * LOG2E)

    def zeros(n):
        return jnp.zeros((n, B_DK), F32)

    a_mats = {}
    for ci, j in items:
        q = q_ref[rows[ci], heads[j]]
        k = k_ref[rows[ci], heads[j]]
        cum = cums[ci][:, heads[j]]
        kbf = k.astype(BF16)
        a = _nt(q.astype(BF16), kbf) * mask_ref[len(sizes)]
        q_cols, k_cols = [], []
        for li, p in enumerate(sizes):
            half = p // 2
            if half % 8 == 0:
                for lo in range(0, c, p):
                    mid, hi = lo + half, lo + p
                    bnd = cum[mid - 1:mid, :]
                    qu = q[mid:hi] * jnp.exp2(cum[mid:hi] - bnd)
                    kl = k[lo:mid] * jnp.exp2(bnd - cum[lo:mid])
                    q_cols.append(jnp.concatenate([x for x in (zeros(mid), qu, zeros(c - hi)) if x.shape[0]], axis=0))
                    k_cols.append(jnp.concatenate([x for x in (zeros(lo), kl, zeros(c - mid)) if x.shape[0]], axis=0))
                continue
            if p == 2:
                qt = (q * jnp.exp2(lf_ref[rows[ci], heads[j]] * LOG2E)).astype(BF16)
                kt = kbf
            else:
                bnd = _boundary(cum, p, c)
                qt = (q * jnp.exp2(jnp.minimum(cum - bnd, 0.0))).astype(BF16)
                kt = (k * jnp.exp2(jnp.minimum(bnd - cum, 0.0))).astype(BF16)
            a = a + _nt(qt, kt) * mask_ref[li]
        if q_cols:
            a = a + _nt(jnp.concatenate(q_cols, axis=1).astype(BF16), jnp.concatenate(k_cols, axis=1).astype(BF16))
        a_mats[ci, j] = a.astype(BF16)

    qhats, kvs, decays, vs = {}, {}, {}, {}
    for ci, j in items:
        cum = cums[ci][:, heads[j]]
        last = cum[c - 1:c, :]
        vs[ci, j] = v_ref[rows[ci], :] * (keep_lo if j == 0 else keep_hi)
        qhats[ci, j] = (q_ref[rows[ci], heads[j]] * jnp.exp2(cum)).astype(BF16)
        khat = (k_ref[rows[ci], heads[j]] * jnp.exp2(last - cum)).astype(BF16)
        kvs[ci, j] = _tn(vs[ci, j], khat)
        decays[ci, j] = jnp.exp2(last)

    pad = jnp.zeros((B_DV, B_DK), F32)
    if not per_chunk_state:
        st = [st_sc[0], st_sc[1]]
    for ci in range(n_chunks):
        if per_chunk_state:
            st = [jnp.concatenate([s0t_ref[ci, 0], pad], axis=0), jnp.concatenate([pad, s0t_ref[ci, 1]], axis=0)]
        o_pair = jnp.zeros((c, LANES), F32)
        for j in range(2):
            o_pair = o_pair + _dot(a_mats[ci, j], vs[ci, j]) + _nt(qhats[ci, j], st[j].astype(BF16))
            st[j] = st[j] * decays[ci, j] + kvs[ci, j]
        if per_chunk_state:
            snt_ref[ci, 0] = st[0][0:B_DV, :]
            snt_ref[ci, 1] = st[1][B_DV:2 * B_DV, :]
        o2 = o_pair * o_pair
        s_lo = jnp.sum(jnp.where(low, o2, 0.0), axis=-1, keepdims=True)
        s_hi = jnp.sum(jnp.where(low, 0.0, o2), axis=-1, keepdims=True)
        ms = jnp.where(low, s_lo, s_hi) * (1.0 / B_DV)
        ob = o_pair * lax.rsqrt(ms + EPS) * gon_ref[...] * og_ref[rows[ci], :].astype(F32)
        o_ref[rows[ci], :] = ob.astype(BF16)
    if not per_chunk_state:
        st_sc[0] = st[0]
        st_sc[1] = st[1]

        @pl.when(step == pl.num_programs(2) - 1)
        def _():
            snt_ref[0] = st_sc[0, 0:B_DV, :]
            snt_ref[1] = st_sc[1, B_DV:2 * B_DV, :]


def _hgrn(qb, kb, lf, vb, og, s0t, gon, n_streams, tc):
    b, l, dk = qb.shape
    per_chunk_state = n_streams != b
    if per_chunk_state:
        assert b == 1 and l % n_streams == 0 and l // n_streams <= CHUNK
        c = l // n_streams
    else:
        c = min(CHUNK, l)
    assert l % tc == 0 and tc % c == 0
    n_chunks = tc // c
    masks = jnp.asarray(_level_masks(c))
    tri = jnp.asarray(np.tril(np.ones((c, c), np.float32)), dtype=BF16)
    pairs = B_HEADS // 2
    wide = pl.BlockSpec((None, tc, 2 * B_DK), lambda bi, p, i: (bi, i, p))
    narrow = pl.BlockSpec((None, tc, 2 * B_DV), lambda bi, p, i: (bi, i, p))
    if per_chunk_state:
        state = pl.BlockSpec((n_chunks, 2, B_DV, B_DK), lambda bi, p, i: (i, p, 0, 0))
    else:
        state = pl.BlockSpec((None, 2, B_DV, B_DK), lambda bi, p, i: (bi, p, 0, 0))
    return pl.pallas_call(
        functools.partial(_hgrn_kernel, c=c, n_chunks=n_chunks, per_chunk_state=per_chunk_state),
        grid=(b, pairs, l // tc),
        in_specs=[wide, wide, wide, narrow, narrow, state, _const_spec(gon.shape),
                  _const_spec(masks.shape), _const_spec(tri.shape)],
        out_specs=[narrow, state],
        out_shape=[jax.ShapeDtypeStruct((b, l, B_HEADS * B_DV), BF16),
                   jax.ShapeDtypeStruct((n_streams, B_HEADS, B_DV, B_DK), F32)],
        scratch_shapes=[pltpu.VMEM((2, 2 * B_DV, B_DK), F32)],
        compiler_params=_params(("parallel", "parallel", "parallel" if per_chunk_state else "arbitrary")),
        name="hgrn",
    )(qb, kb, lf, vb, og, s0t, gon, masks, tri)


MOE_SUB = 144
MOE_ROWS = 1024
_BIG_LANE = float(1 << 20)


def _max_items(tm):
    return (tm + N_GROUPS * (MOE_SUB - 1)) // MOE_SUB


def _first_lane_of_max(x, lane):
    v = jnp.max(x, axis=-1, keepdims=True)
    return v, jnp.min(jnp.where(x == v, lane, _BIG_LANE), axis=-1, keepdims=True)


def _out_proj_kernel(oa_ref, ob_ref, sa_ref, sb_ref, x_ref, wa_ref, wb_ref, wo_ref, gffn_ref,
                     wr_ref, br_ref, tri_ref, upper_ref, x1_ref, xs_ref, slot_ref, meta_ref,
                     *, oa_transposed, tm):
    tiles = [slice(u * tm, (u + 1) * tm) for u in range(x_ref.shape[0] // tm)]
    nb = MXU_TILE
    h2s = []
    for rows in tiles:
        oa = oa_ref[:, rows] if oa_transposed else oa_ref[rows, :]
        ob = ob_ref[rows, :]
        m = []
        for i in range(x_ref.shape[1] // nb):
            cols = slice(i * nb, (i + 1) * nb)
            ya = _tn(oa, wa_ref[:, cols]) if oa_transposed else _dot(oa, wa_ref[:, cols])
            yb = _dot(ob, wb_ref[:, cols])
            m.append((sa_ref[rows, cols].astype(F32) * ya + sb_ref[rows, cols].astype(F32) * yb).astype(BF16))
        x1 = x_ref[rows, :] + _dot(jnp.concatenate(m, axis=-1), wo_ref[...])
        x1_ref[rows, :] = x1
        h2s.append(_rms(x1, gffn_ref[...]).astype(BF16))

    lane = lax.broadcasted_iota(jnp.int32, (tm, LANES), 1).astype(F32)
    logits = [_dot(h2, wr_ref[...]) + br_ref[...] for h2 in h2s]
    grps = [_first_lane_of_max(jnp.where(lane < N_GROUPS, lg, NEG_INF), lane)[1] for lg in logits]
    onehots = [jnp.where(lane == grp, 1.0, 0.0) for grp in grps]
    ranks = [_dot(tri_ref[...], oh.astype(BF16)) for oh in onehots]
    items = []
    for rank in ranks:
        count = rank[tm - 1:tm, :]
        n = jnp.zeros_like(count)
        for j in range(_max_items(tm)):
            n = n + jnp.where(count > j * MOE_SUB, 1.0, 0.0)
        items.append(n)
    bases = [_dot(jnp.broadcast_to(n, (8, LANES)).astype(BF16), upper_ref[...])[0:1, :] * MOE_SUB for n in items]
    slots = [jnp.sum(oh * (base + rank - 1.0), axis=-1, keepdims=True)
             for oh, base, rank in zip(onehots, bases, ranks)]
    col = lax.broadcasted_iota(jnp.int32, (tm, MOE_ROWS), 1).astype(F32)
    for u, rows in enumerate(tiles):
        place = jnp.where(col == slots[u], 1.0, 0.0).astype(BF16)
        xs_ref[u] = _tn(place, h2s[u]).astype(BF16)
        slot_ref[rows, :] = jnp.broadcast_to(slots[u], (tm, LANES))
        meta_ref[u] = jnp.broadcast_to(items[u], (8, LANES))


def _out_proj(oa, ob, sa, sb, x, wa, wb, wo, gffn, wr, br, tm, oa_transposed):
    b, l, d = x.shape
    nt = l // tm
    per_step = 2 if nt % 2 == 0 else 1
    ts, ns = tm * per_step, nt // per_step
    assert _max_items(tm) * MOE_SUB <= MOE_ROWS
    hv = A_HEADS * V_DIM
    tri = jnp.asarray(np.tril(np.ones((tm, tm), np.float32)), dtype=BF16)
    upper = jnp.asarray(np.triu(np.ones((LANES, LANES), np.float32), 1), dtype=BF16)
    tok = lambda w: pl.BlockSpec((None, ts, w), lambda bi, i: (bi, i, 0))
    per_tile = lambda r, w: pl.BlockSpec((per_step, r, w), lambda bi, i: (bi * ns + i, 0, 0))
    oa_spec = pl.BlockSpec((None, hv, ts), lambda bi, i: (bi, 0, i)) if oa_transposed else tok(hv)
    return pl.pallas_call(
        functools.partial(_out_proj_kernel, oa_transposed=oa_transposed, tm=tm),
        grid=(b, ns),
        in_specs=[oa_spec, tok(B_HEADS * B_DV), tok(d), tok(d), tok(d), _const_spec(wa.shape),
                  _const_spec(wb.shape), _const_spec(wo.shape), _const_spec(gffn.shape),
                  _const_spec(wr.shape), _const_spec(br.shape), _const_spec(tri.shape),
                  _const_spec(upper.shape)],
        out_specs=[tok(d), per_tile(MOE_ROWS, d), tok(LANES), per_tile(8, LANES)],
        out_shape=[jax.ShapeDtypeStruct((b, l, d), F32),
                   jax.ShapeDtypeStruct((b * nt, MOE_ROWS, d), BF16),
                   jax.ShapeDtypeStruct((b, l, LANES), F32),
                   jax.ShapeDtypeStruct((b * nt, 8, LANES), F32)],
        compiler_params=_params(("parallel", "parallel")),
        name="out_proj",
    )(oa, ob, sa, sb, x, wa, wb, wo, gffn, wr, br, tri, upper)


def _work_list(meta, n_items):
    n = meta[:, 0, :N_GROUPS].astype(jnp.int32)
    tiles = n.shape[0]
    first = jnp.cumsum(n, axis=1) - n
    per_seg = n.T.reshape(-1)
    ends = jnp.cumsum(per_seg)
    total = ends[-1]
    i = jnp.minimum(jnp.arange(n_items, dtype=jnp.int32), total - 1)
    seg = jnp.sum((ends[None, :] <= i[:, None]).astype(jnp.int32), axis=1)
    grp, tile = seg // tiles, seg % tiles
    block = first[tile, grp] + i - (ends[seg] - per_seg[seg])
    return tile, block.astype(jnp.int32), grp, total[None].astype(jnp.int32)


def _moe_expert_kernel(tile_ref, block_ref, grp_ref, total_ref, xs_ref, wr_ref, br_ref, wg_ref, wu_ref,
                       wd_ref, ys_ref):
    i = pl.program_id(0)

    @pl.when(i < total_ref[0])
    def _():
        g = grp_ref[i]
        x = xs_ref[...]
        logits = _dot(x, wr_ref[...]) + br_ref[...]
        lane_i = lax.broadcasted_iota(jnp.int32, logits.shape, 1)
        lane = lane_i.astype(F32)
        is_grp = jnp.abs(lane - (EXPERTS_PER_GROUP + 0.5 * (N_GROUPS - 1))) < 0.5 * N_GROUPS
        lg = jnp.where(is_grp, logits, NEG_INF)
        gmax = jnp.max(lg, axis=-1, keepdims=True)
        denom = jnp.sum(jnp.exp(lg - gmax), axis=-1, keepdims=True)
        lg_own = jnp.sum(jnp.where(lane_i == EXPERTS_PER_GROUP + g, logits, 0.0), axis=-1, keepdims=True)
        p_grp = jnp.exp(lg_own - gmax) / denom
        le = jnp.where(lane < EXPERTS_PER_GROUP, logits, NEG_INF)
        v1, i1 = _first_lane_of_max(le, lane)
        le2 = jnp.where(lane == i1, NEG_INF, le)
        v2, i2 = _first_lane_of_max(le2, lane)
        e2 = jnp.exp(v2 - v1)
        w1 = p_grp / (1.0 + e2)
        comb = jnp.where(lane == i1, w1, 0.0) + jnp.where(lane == i2, w1 * e2, 0.0)
        hid = []
        for e in range(EXPERTS_PER_GROUP):
            a = _dot(x, wg_ref[e])
            u = _dot(x, wu_ref[e])
            s, _ = _sig_pair(a)
            hid.append((a * s * u * comb[:, e:e + 1]).astype(BF16))
        hid = jnp.concatenate(hid, axis=-1)
        wd = wd_ref[...].reshape(EXPERTS_PER_GROUP * D_FF_EXPERT, wd_ref.shape[-1])
        ys_ref[...] = _dot(hid, wd).astype(BF16)


def _moe_experts(xs, work, wrg, brg, wg, wu, wd):
    tiles, _, d = xs.shape
    tile, block, grp, total = work
    item = lambda i, t, k, g, n: (t[i], k[i], 0)
    by_group = lambda i, t, k, g, n: (g[i], 0, 0)
    grid_spec = pltpu.PrefetchScalarGridSpec(
        num_scalar_prefetch=4,
        grid=(tile.shape[0],),
        in_specs=[pl.BlockSpec((None, MOE_SUB, d), item),
                  pl.BlockSpec((None, d, LANES), by_group),
                  pl.BlockSpec((None, 1, LANES), by_group),
                  pl.BlockSpec((EXPERTS_PER_GROUP, d, D_FF_EXPERT), by_group),
                  pl.BlockSpec((EXPERTS_PER_GROUP, d, D_FF_EXPERT), by_group),
                  pl.BlockSpec((EXPERTS_PER_GROUP, D_FF_EXPERT, d), by_group)],
        out_specs=pl.BlockSpec((None, MOE_SUB, d), item))
    return pl.pallas_call(
        _moe_expert_kernel,
        grid_spec=grid_spec,
        out_shape=jax.ShapeDtypeStruct(xs.shape, BF16),
        input_output_aliases={4: 0},
        compiler_params=_params(("arbitrary",)),
        name="moe_experts",
    )(tile, block, grp, total, xs, wrg, brg, wg, wu, wd)


def _moe_combine_kernel(ys_ref, slot_ref, x1_ref, gfin_ref, y_ref, *, tm):
    tiles = [slice(u * tm, (u + 1) * tm) for u in range(x1_ref.shape[0] // tm)]
    col = lax.broadcasted_iota(jnp.int32, (tm, MOE_ROWS), 1).astype(F32)
    places = []
    for rows in tiles:
        slot = jnp.concatenate([slot_ref[rows, :]] * (MOE_ROWS // LANES), axis=-1)
        places.append(jnp.where(col == slot, 1.0, 0.0).astype(BF16))
    for u, rows in enumerate(tiles):
        y_ref[rows, :] = _rms(x1_ref[rows, :] + _dot(places[u], ys_ref[u]), gfin_ref[...])


def _moe_combine(ys, slot, x1, gfin, tm):
    b, l, d = x1.shape
    nt = l // tm
    per_step = 2 if nt % 2 == 0 else 1
    ts, ns = tm * per_step, nt // per_step
    tok = lambda w: pl.BlockSpec((None, ts, w), lambda bi, i: (bi, i, 0))
    per_tile = lambda r, w: pl.BlockSpec((per_step, r, w), lambda bi, i: (bi * ns + i, 0, 0))
    return pl.pallas_call(
        functools.partial(_moe_combine_kernel, tm=tm),
        grid=(b, ns),
        in_specs=[per_tile(MOE_ROWS, d), tok(LANES), tok(d), _const_spec(gfin.shape)],
        out_specs=tok(d),
        out_shape=jax.ShapeDtypeStruct((b, l, d), F32),
        compiler_params=_params(("parallel", "parallel")),
        name="moe_combine",
    )(ys, slot, x1, gfin)


def _rope_tables(pos):
    inv = jnp.power(ROPE_THETA, -jnp.arange(HALF, dtype=F32) / HALF)
    ang = pos.astype(F32)[:, None] * inv[None, :]
    cos, sin = jnp.cos(ang), jnp.sin(ang)
    zeros = jnp.zeros_like(sin)
    reps = LANES // ROPE_DIM
    cosp = jnp.tile(jnp.concatenate([cos, cos], axis=1), (1, reps))
    sinlo = jnp.tile(jnp.concatenate([-sin, zeros], axis=1), (1, reps))
    sinhi = jnp.tile(jnp.concatenate([zeros, sin], axis=1), (1, reps))
    return cosp, sinlo, sinhi, cos.T * (SCALE * LOG2E), sin.T * (SCALE * LOG2E)


def _prep_weights(g_mix, w_in, g_q, w_uq, g_kv, w_ukv, lb_hgrn, g_onorm, w_a_out, w_b_out, w_o,
                  g_ffn, w_rg, b_rg, w_re, b_re, w_gate, w_up, w_down, g_final):
    assert w_in.shape[0] == 1, "single-layer step"
    d = w_in.shape[1]
    w = w_in[0]
    n_a = Q_LORA + KV_LORA
    win_a = jnp.concatenate([w[:, :n_a]] + [w[:, n_a:n_a + ROPE_DIM]] * (LANES // ROPE_DIM), axis=1).astype(BF16)
    win_b = jnp.swapaxes(w, 0, 1).astype(BF16)
    per_q = NOPE_DIM + ROPE_DIM
    wuq = jnp.pad(w_uq[0].reshape(Q_LORA, A_HEADS, per_q), ((0, 0), (0, 0), (0, HEAD_PAD - per_q)))
    wuq = wuq.reshape(Q_LORA, A_HEADS * HEAD_PAD).astype(BF16)
    wukv = w_ukv[0].reshape(KV_LORA, A_HEADS, NOPE_DIM + V_DIM)
    w_uk, w_uv = wukv[..., :NOPE_DIM], wukv[..., NOPE_DIM:]
    wuk = jnp.pad(w_uk, ((0, 0), (0, 0), (0, HEAD_PAD - NOPE_DIM))).reshape(KV_LORA, A_HEADS * HEAD_PAD).astype(BF16)
    wuvt = w_uv.reshape(KV_LORA, A_HEADS * V_DIM).T.astype(BF16)
    wq2l = jnp.pad(jnp.transpose(w_uk, (1, 2, 0)), ((0, 0), (0, HEAD_PAD - NOPE_DIM), (0, 0))).astype(BF16)
    esel = jnp.zeros((HEAD_PAD, ROPE_DIM), F32).at[ROPE_LO + jnp.arange(ROPE_DIM), jnp.arange(ROPE_DIM)].set(1.0).astype(BF16)
    wuvp = w_uv.reshape(KV_LORA, A_HEADS * V_DIM).astype(BF16)
    lb = jnp.cumsum(jax.nn.softmax(lb_hgrn.astype(F32), axis=0), axis=0)[0][None, :]
    gon = jnp.tile(g_onorm[0], 2)[None, :]
    wr = jnp.pad(w_rg[0], ((0, 0), (0, LANES - N_GROUPS))).astype(BF16)
    br = jnp.pad(b_rg[0], (0, LANES - N_GROUPS))[None, :]
    rest = LANES - EXPERTS_PER_GROUP - N_GROUPS
    wrg = jnp.concatenate([w_re[0], jnp.broadcast_to(w_rg[0], (N_GROUPS, d, N_GROUPS)),
                           jnp.zeros((N_GROUPS, d, rest), F32)], axis=2).astype(BF16)
    brg = jnp.concatenate([b_re[0], jnp.broadcast_to(b_rg[0], (N_GROUPS, N_GROUPS)),
                           jnp.zeros((N_GROUPS, rest), F32)], axis=1)[:, None, :]
    return dict(
        wrg=wrg, brg=brg,
        gmix=g_mix[0][None, :], win_a=win_a, win_b=win_b, gq=g_q[0][None, :], wuq=wuq, wuqt=wuq.T,
        gkv=g_kv[0][None, :], wuk=wuk, wuvt=wuvt, wq2l=wq2l, esel=esel, wuvp=wuvp, lb=lb, gon=gon,
        wa=w_a_out[0].astype(BF16), wb=w_b_out[0].astype(BF16), wo=w_o[0].astype(BF16),
        gffn=g_ffn[0][None, :], wr=wr, br=br, wg=w_gate[0].astype(BF16), wu=w_up[0].astype(BF16),
        wd=w_down[0].astype(BF16), gfin=g_final[None, :])


def _tile(n, want):
    t = min(n, want)
    while n % t:
        t //= 2
    return t


TOKEN_TILE = 512
ATTN_TILE = 512
ATTN_CHAIN = 256
ATTN_HEADS_PER_STEP = 8
HGRN_STEP = 1024
HGRN_STREAMS_PER_STEP = 8


def _run_group(x, pos, streams, past, w):
    b, l, d = x.shape
    ns, ls = streams
    tm = _tile(l, TOKEN_TILE)
    h, ckv, kpe, q, k, vt = _mla_proj(x, w["gmix"], w["win_a"], w["gq"], w["wuq"], w["wuqt"], w["gkv"],
                                      w["wuk"], w["wuvt"], _rope_tables(pos), tm, past is None)
    qb, kb, lf, vb, og, sa, sb = _hgrn_proj(h, w["win_b"], w["lb"], tm)
    as_streams = lambda a: a.reshape(ns, ls, a.shape[-1])
    if past is None:
        oa = _attn(q, k, vt, _tile(l, ATTN_TILE), ATTN_CHAIN, ATTN_HEADS_PER_STEP)
        s0t = jnp.zeros((ns, B_HEADS, B_DV, B_DK), F32)
    else:
        cache_ckv, cache_kpe, state = past
        oa = _sample_attn(as_streams(q), cache_ckv, cache_kpe, as_streams(ckv), as_streams(kpe),
                          w["wq2l"], w["esel"], w["wuvp"]).reshape(b, l, A_HEADS * V_DIM)
        s0t = jnp.swapaxes(state.astype(F32), -1, -2)
    ob, snt = _hgrn(qb, kb, lf, vb, og, s0t, w["gon"], ns,
                    _tile(l, HGRN_STEP if past is None else HGRN_STREAMS_PER_STEP * ls))
    x1, xs, slot, meta = _out_proj(oa, ob, sa, sb, x, w["wa"], w["wb"], w["wo"], w["gffn"], w["wr"],
                                   w["br"], tm, past is None)
    work = _work_list(meta, meta.shape[0] * _max_items(tm))
    ys = _moe_experts(xs, work, w["wrg"], w["brg"], w["wg"], w["wu"], w["wd"])
    y = _moe_combine(ys, slot, x1, w["gfin"], tm)
    return y, ckv, kpe, jnp.swapaxes(snt, -1, -2)


def kernel(x_prompt, x_sample, cache_ckv, cache_kpe, state_hgrn, g_mix, w_in, g_q, w_uq, g_kv, w_ukv,
           lb_hgrn, g_onorm, w_a_out, w_b_out, w_o, g_ffn, w_rg, b_rg, w_re, b_re, w_gate, w_up, w_down,
           g_final):
    w = _prep_weights(g_mix, w_in, g_q, w_uq, g_kv, w_ukv, lb_hgrn, g_onorm, w_a_out, w_b_out, w_o,
                      g_ffn, w_rg, b_rg, w_re, b_re, w_gate, w_up, w_down, g_final)
    bp, lp, d = x_prompt.shape
    y_p, ckv_p, kpe_p, st_p = _run_group(x_prompt, jnp.arange(lp, dtype=jnp.int32), (bp, lp), None, w)

    bs, ls, _ = x_sample.shape
    past_len = cache_ckv.shape[2]
    pos_s = past_len + (jnp.arange(bs * ls, dtype=jnp.int32) % ls)
    y_s, ckv_s, kpe_s, st_s = _run_group(x_sample.reshape(1, bs * ls, d), pos_s, (bs, ls),
                                         (cache_ckv[0], jnp.swapaxes(cache_kpe[0], 1, 2), state_hgrn[0]), w)
    return (y_p, y_s.reshape(bs, ls, d),
            ckv_p[None], kpe_p[None], st_p[None].astype(x_prompt.dtype),
            ckv_s.reshape(1, bs, ls, KV_LORA), kpe_s.reshape(1, bs, ls, ROPE_DIM),
            st_s[None].astype(state_hgrn.dtype))
```

```python
import functools

import numpy as np
import jax
import jax.numpy as jnp
from jax import lax
from jax.experimental import pallas as pl
from jax.experimental.pallas import tpu as pltpu

F32 = jnp.float32
BF16 = jnp.bfloat16

EPS = 1e-6
CHUNK = 64
A_HEADS = 8
Q_LORA = 384
KV_LORA = 256
NOPE_DIM = 64
ROPE_DIM = 32
V_DIM = 64
ROPE_THETA = 10000.0
B_HEADS = 8
B_DK = 128
B_DV = 64
N_GROUPS = 4
EXPERTS_PER_GROUP = 8
N_EXPERTS = N_GROUPS * EXPERTS_PER_GROUP
D_FF_EXPERT = 256

LANES = 128
MXU_TILE = 256
VMEM_BYTES_V7X = 64 * 1024 * 1024
VMEM_LIMIT = VMEM_BYTES_V7X - 8 * 1024 * 1024

HEAD_PAD = LANES
V_AUG = V_DIM + 16
SAMPLE_KEY_BLOCK = 1024
ROPE_LO = NOPE_DIM
ROPE_HI = NOPE_DIM + ROPE_DIM
HALF = ROPE_DIM // 2
SCALE = (NOPE_DIM + ROPE_DIM) ** -0.5
LOG2E = 1.4426950408889634
NEG_INF = float("-inf")


def _params(sem):
    return pltpu.CompilerParams(dimension_semantics=sem, vmem_limit_bytes=VMEM_LIMIT)


def _const_spec(shape):
    nd = len(shape)
    return pl.BlockSpec(shape, lambda *_: (0,) * nd, pipeline_mode=pl.Buffered(1))


def _rms(x, g):
    ms = jnp.mean(x * x, axis=-1, keepdims=True)
    return x * lax.rsqrt(ms + EPS) * g


def _sig_pair(x):
    e = jnp.exp(-jnp.abs(x))
    r = 1.0 / (1.0 + e)
    er = e * r
    pos = x >= 0
    return jnp.where(pos, r, er), jnp.where(pos, er, r)


def _nt(a, b):
    return lax.dot_general(a, b, (((1,), (1,)), ((), ())), preferred_element_type=F32)


def _tn(a, b):
    return lax.dot_general(a, b, (((0,), (0,)), ((), ())), preferred_element_type=F32)


def _dot(a, b):
    return jnp.dot(a, b, preferred_element_type=F32)


def _rope(x, cosp, sinlo, sinhi):
    return x * cosp + pltpu.roll(x, LANES - HALF, 1) * sinlo + pltpu.roll(x, HALF, 1) * sinhi


def _mla_proj_kernel(x_ref, gmix_ref, win_ref, gq_ref, wuq_ref, wuqt_ref, gkv_ref, wuk_ref, wuvt_ref,
                     cos_ref, sinlo_ref, sinhi_ref, cost_ref, sint_ref,
                     h_ref, ckv_ref, kpe_ref, q_ref, k_ref, vt_ref, *, q_transposed, tm):
    tiles = [slice(u * tm, (u + 1) * tm) for u in range(x_ref.shape[0] // tm)]
    lane = lax.broadcasted_iota(jnp.int32, (1, LANES), 1)
    rope_lanes = (lane // ROPE_DIM) == (ROPE_LO // ROPE_DIM)

    hs = []
    for rows in tiles:
        h = _rms(x_ref[rows, :], gmix_ref[...]).astype(BF16)
        h_ref[rows, :] = h
        hs.append(h)
    zs = [_nt(h, win_ref[...]) for h in hs]
    cqns = [_rms(z[:, :Q_LORA], gq_ref[...]).astype(BF16) for z in zs]
    ckvs = [_rms(z[:, Q_LORA:Q_LORA + KV_LORA], gkv_ref[...]) for z in zs]

    for rows, cqn in zip(tiles, cqns):
        if q_transposed:
            qt = _nt(wuqt_ref[...], cqn)
            cos_t, sin_t = cost_ref[:, rows], sint_ref[:, rows]
            for hd in range(A_HEADS):
                r0 = hd * HEAD_PAD
                q_ref[r0:r0 + ROPE_LO, rows] = (qt[r0:r0 + ROPE_LO] * (SCALE * LOG2E)).astype(BF16)
                a = qt[r0 + ROPE_LO:r0 + ROPE_LO + HALF]
                b = qt[r0 + ROPE_LO + HALF:r0 + ROPE_HI]
                q_ref[r0 + ROPE_LO:r0 + ROPE_HI, rows] = jnp.concatenate(
                    [a * cos_t - b * sin_t, a * sin_t + b * cos_t], axis=0).astype(BF16)
                q_ref[r0 + ROPE_HI:r0 + HEAD_PAD, rows] = jnp.zeros((HEAD_PAD - ROPE_HI, tm), BF16)
        else:
            q = _dot(cqn, wuq_ref[...])
            cosp, sinlo, sinhi = cos_ref[rows, :], sinlo_ref[rows, :], sinhi_ref[rows, :]
            for hd in range(A_HEADS):
                sl = slice(hd * HEAD_PAD, (hd + 1) * HEAD_PAD)
                qh = q[:, sl]
                qh = jnp.where(rope_lanes, _rope(qh, cosp, sinlo, sinhi), qh) * (SCALE * LOG2E)
                q_ref[rows, sl] = qh.astype(BF16)

    for rows, z, ckv in zip(tiles, zs, ckvs):
        ckv_ref[rows, :] = ckv
        ckv_bf = ckv.astype(BF16)
        cosp, sinlo, sinhi = cos_ref[rows, :], sinlo_ref[rows, :], sinhi_ref[rows, :]
        kpe_rot = _rope(z[:, Q_LORA + KV_LORA:], cosp, sinlo, sinhi)
        kpe_ref[rows, :] = kpe_rot[:, :ROPE_DIM]
        kpe_placed = jnp.where(rope_lanes, kpe_rot, 0.0)
        kn = _dot(ckv_bf, wuk_ref[...])
        for hd in range(A_HEADS):
            sl = slice(hd * HEAD_PAD, (hd + 1) * HEAD_PAD)
            k_ref[rows, sl] = (kn[:, sl] + kpe_placed).astype(BF16)
        vt = _nt(wuvt_ref[...], ckv_bf).astype(BF16)
        ones = jnp.ones((V_AUG - V_DIM, tm), BF16)
        for hd in range(A_HEADS):
            vt_ref[hd * V_AUG:hd * V_AUG + V_DIM, rows] = vt[hd * V_DIM:(hd + 1) * V_DIM]
            vt_ref[hd * V_AUG + V_DIM:(hd + 1) * V_AUG, rows] = ones


def _mla_proj(x, gmix, win_a, gq, wuq, wuqt, gkv, wuk, wuvt, tables, tm, q_transposed):
    b, l, d = x.shape
    hq = A_HEADS * HEAD_PAD
    hv = A_HEADS * V_AUG
    cosp, sinlo, sinhi, cos_t, sin_t = tables
    per_step = 2 if (l // tm) % 2 == 0 else 1
    ts = tm * per_step
    tok = lambda w: pl.BlockSpec((None, ts, w), lambda bi, i: (bi, i, 0))
    tok_t = lambda w: pl.BlockSpec((None, w, ts), lambda bi, i: (bi, 0, i))
    tab = pl.BlockSpec((ts, LANES), lambda bi, i: (i, 0))
    tab_t = pl.BlockSpec((HALF, ts), lambda bi, i: (0, i))
    return pl.pallas_call(
        functools.partial(_mla_proj_kernel, q_transposed=q_transposed, tm=tm),
        grid=(b, l // ts),
        in_specs=[tok(d), _const_spec(gmix.shape), _const_spec(win_a.shape), _const_spec(gq.shape),
                  _const_spec(wuq.shape), _const_spec(wuqt.shape), _const_spec(gkv.shape),
                  _const_spec(wuk.shape), _const_spec(wuvt.shape), tab, tab, tab, tab_t, tab_t],
        out_specs=[tok(d), tok(KV_LORA), tok(ROPE_DIM), tok_t(hq) if q_transposed else tok(hq), tok(hq),
                   tok_t(hv)],
        out_shape=[jax.ShapeDtypeStruct((b, l, d), BF16),
                   jax.ShapeDtypeStruct((b, l, KV_LORA), F32),
                   jax.ShapeDtypeStruct((b, l, ROPE_DIM), F32),
                   jax.ShapeDtypeStruct((b, hq, l) if q_transposed else (b, l, hq), BF16),
                   jax.ShapeDtypeStruct((b, l, hq), BF16),
                   jax.ShapeDtypeStruct((b, hv, l), BF16)],
        compiler_params=_params(("parallel", "parallel")),
        name="mla_proj",
    )(x, gmix, win_a, gq, wuq, wuqt, gkv, wuk, wuvt, cosp, sinlo, sinhi, cos_t, sin_t)


def _hgrn_proj_kernel(h_ref, wt_ref, lb_ref, qb_ref, kb_ref, lf_ref, vb_ref, og_ref, sa_ref, sb_ref, *, row0):
    h = h_ref[...]
    dk = B_HEADS * B_DK
    dv = B_HEADS * B_DV
    dm = sa_ref.shape[-1]
    nb = MXU_TILE
    lb_all = lb_ref[...]

    def z(base, i):
        return _nt(h, wt_ref[row0 + base + i * nb:row0 + base + (i + 1) * nb, :])

    for i in range(dk // nb):
        cols = slice(i * nb, (i + 1) * nb)
        zq = z(0, i)
        qb_ref[:, cols] = zq * jax.nn.sigmoid(zq)
        sf, snf = _sig_pair(z(dk, i))
        lb = lb_all[:, cols]
        lf_ref[:, cols] = jnp.log(lb + (1.0 - lb) * sf)
        kb_ref[:, cols] = (1.0 - lb) * snf
    for i in range(dv // nb):
        cols = slice(i * nb, (i + 1) * nb)
        vb_ref[:, cols] = z(2 * dk, i).astype(BF16)
        zg = z(2 * dk + dv, i)
        og_ref[:, cols] = (zg * jax.nn.sigmoid(zg)).astype(BF16)
    for i in range(dm // nb):
        cols = slice(i * nb, (i + 1) * nb)
        sa_ref[:, cols] = jax.nn.sigmoid(z(2 * dk + 2 * dv, i)).astype(BF16)
        sb_ref[:, cols] = jax.nn.sigmoid(z(2 * dk + 2 * dv + dm, i)).astype(BF16)


def _hgrn_proj(h, win_b, lb, tm):
    b, l, d = h.shape
    dk = B_HEADS * B_DK
    dv = B_HEADS * B_DV
    tok = lambda w: pl.BlockSpec((None, tm, w), lambda bi, i: (bi, i, 0))
    sds = lambda w, dt: jax.ShapeDtypeStruct((b, l, w), dt)
    return pl.pallas_call(
        functools.partial(_hgrn_proj_kernel, row0=Q_LORA + KV_LORA + ROPE_DIM),
        grid=(b, l // tm),
        in_specs=[tok(d), _const_spec(win_b.shape), _const_spec(lb.shape)],
        out_specs=[tok(dk), tok(dk), tok(dk), tok(dv), tok(dv), tok(d), tok(d)],
        out_shape=[sds(dk, F32), sds(dk, F32), sds(dk, F32), sds(dv, BF16), sds(dv, BF16),
                   sds(d, BF16), sds(d, BF16)],
        compiler_params=_params(("parallel", "parallel")),
        name="hgrn_proj",
    )(h, win_b, lb)


def _attn_kernel(qt_ref, k_ref, vt_ref, o_ref, s_sc, m_sc, acc_sc, *, t, tc, hp):
    qi = pl.program_id(2)
    n_chains = t // tc
    m_sc[...] = jnp.full(m_sc.shape, NEG_INF, F32)
    acc_sc[...] = jnp.zeros(acc_sc.shape, F32)
    qk_rows = [slice(hd * HEAD_PAD, (hd + 1) * HEAD_PAD) for hd in range(hp)]
    v_rows = [slice(hd * V_AUG, (hd + 1) * V_AUG) for hd in range(hp)]

    def scores(kb, slot):
        start = pl.multiple_of(kb * t, t)
        for hd in range(hp):
            k = k_ref[pl.ds(start, t), qk_rows[hd]]
            for ch in range(n_chains):
                cols = slice(ch * tc, (ch + 1) * tc)
                s_sc[hd, slot, :, cols] = _dot(k, qt_ref[qk_rows[hd], cols])

    def consume(kb, slot, masked):
        start = pl.multiple_of(kb * t, t)
        for hd in range(hp):
            for ch in range(n_chains):
                cols = slice(ch * tc, (ch + 1) * tc)
                nk = (ch + 1) * tc if masked else t
                vt = vt_ref[v_rows[hd], pl.ds(start, nk)]
                s = s_sc[hd, slot, 0:nk, cols]
                if masked:
                    r = lax.broadcasted_iota(jnp.int32, (nk, tc), 0) // CHUNK
                    c = (lax.broadcasted_iota(jnp.int32, (nk, tc), 1) + ch * tc) // CHUNK
                    s = jnp.where(r <= c, s, NEG_INF)
                m_prev = m_sc[hd, :, cols]
                m_new = jnp.maximum(m_prev, jnp.max(s, axis=0, keepdims=True))
                alpha = jnp.exp2(m_prev - m_new)
                p = jnp.exp2(s - m_new)
                acc_sc[hd, :, cols] = alpha * acc_sc[hd, :, cols] + _dot(vt, p.astype(BF16))
                m_sc[hd, :, cols] = m_new

    scores(0, 0)

    def body(j, carry):
        kb = 2 * j
        scores(kb + 1, 1)
        consume(kb, 0, False)
        scores(kb + 2, 0)
        consume(kb + 1, 1, False)
        return carry

    lax.fori_loop(0, qi // 2, body, 0)

    @pl.when(qi % 2 == 0)
    def _():
        consume(qi, 0, True)

    @pl.when(qi % 2 == 1)
    def _():
        scores(qi, 1)
        consume(qi - 1, 0, False)
        consume(qi, 1, True)

    for hd in range(hp):
        acc = acc_sc[hd]
        o_ref[hd * V_DIM:(hd + 1) * V_DIM, :] = (acc[:V_DIM] / acc[V_DIM:V_DIM + 1]).astype(BF16)


def _attn(qt, k, vt, t, tc, hp):
    b, l, _ = k.shape
    once = dict(pipeline_mode=pl.Buffered(1)) if hp == A_HEADS else {}
    return pl.pallas_call(
        functools.partial(_attn_kernel, t=t, tc=min(tc, t), hp=hp),
        grid=(b, A_HEADS // hp, l // t),
        in_specs=[pl.BlockSpec((None, hp * HEAD_PAD, t), lambda bi, h, i: (bi, h, i)),
                  pl.BlockSpec((None, l, hp * HEAD_PAD), lambda bi, h, i: (bi, 0, h), **once),
                  pl.BlockSpec((None, hp * V_AUG, l), lambda bi, h, i: (bi, h, 0), **once)],
        out_specs=pl.BlockSpec((None, hp * V_DIM, t), lambda bi, h, i: (bi, h, i)),
        out_shape=jax.ShapeDtypeStruct((b, A_HEADS * V_DIM, l), BF16),
        scratch_shapes=[pltpu.VMEM((hp, 2, t, t), F32), pltpu.VMEM((hp, 1, t), F32),
                        pltpu.VMEM((hp, V_AUG, t), F32)],
        compiler_params=_params(("parallel", "parallel", "arbitrary")),
        name="attn",
    )(qt, k, vt)


def _sample_attn_kernel(q_ref, cc_ref, ckt_ref, nc_ref, nk_ref, wq2l_ref, esel_ref, wuv_ref, o_ref,
                        *, past, n_new):
    qs = q_ref[...]
    heads = [qs[:, hd * HEAD_PAD:(hd + 1) * HEAD_PAD] for hd in range(A_HEADS)]
    ql = jnp.concatenate([_dot(heads[hd], wq2l_ref[hd]) for hd in range(A_HEADS)], axis=0).astype(BF16)
    qp = jnp.concatenate([_dot(heads[hd], esel_ref[...]) for hd in range(A_HEADS)], axis=0).astype(BF16)
    nc = nc_ref[...].astype(BF16)
    nk = nk_ref[...].astype(BF16)
    rows = A_HEADS * n_new

    s_n = _nt(ql, nc) + _nt(qp, nk)
    tq = lax.broadcasted_iota(jnp.int32, (rows, n_new), 0) % n_new
    tk = lax.broadcasted_iota(jnp.int32, (rows, n_new), 1)
    s_n = jnp.where((past + tk) // CHUNK <= (past + tq) // CHUNK, s_n, NEG_INF)
    m = jnp.max(s_n, axis=-1, keepdims=True)
    p_n = jnp.exp2(s_n - m)
    denom = jnp.sum(p_n, axis=-1, keepdims=True)
    acc = _dot(p_n.astype(BF16), nc)

    n_blocks = max(1, past // SAMPLE_KEY_BLOCK)
    assert past % n_blocks == 0
    kb = past // n_blocks

    def scores(i):
        cc = cc_ref[i * kb:(i + 1) * kb, :].astype(BF16)
        ckt = ckt_ref[:, i * kb:(i + 1) * kb].astype(BF16)
        return _nt(ql, cc) + _dot(qp, ckt), cc

    nxt = scores(0)
    for i in range(n_blocks):
        s_c, cc = nxt
        if i + 1 < n_blocks:
            nxt = scores(i + 1)
        m_new = jnp.maximum(m, jnp.max(s_c, axis=-1, keepdims=True))
        alpha = jnp.exp2(m - m_new)
        p_c = jnp.exp2(s_c - m_new)
        denom = alpha * denom + jnp.sum(p_c, axis=-1, keepdims=True)
        acc = alpha * acc + _dot(p_c.astype(BF16), cc)
        m = m_new
    o_lat = (acc / denom).astype(BF16)
    full = _dot(o_lat, wuv_ref[...])
    col_head = lax.broadcasted_iota(jnp.int32, (n_new, A_HEADS * V_DIM), 1) // V_DIM
    out = jnp.zeros((n_new, A_HEADS * V_DIM), F32)
    for hd in range(A_HEADS):
        out = out + jnp.where(col_head == hd, full[hd * n_new:(hd + 1) * n_new], 0.0)
    o_ref[...] = out.astype(BF16)


def _sample_attn(q, cache_ckv, cache_kpe_t, ckv_new, kpe_new, wq2l, esel, wuvp):
    nb, n_new, hq = q.shape
    past = cache_ckv.shape[1]
    hv = A_HEADS * V_DIM
    per = lambda r, w: pl.BlockSpec((None, r, w), lambda bi: (bi, 0, 0))
    return pl.pallas_call(
        functools.partial(_sample_attn_kernel, past=past, n_new=n_new),
        grid=(nb,),
        in_specs=[per(n_new, hq), per(past, KV_LORA), per(ROPE_DIM, past), per(n_new, KV_LORA),
                  per(n_new, ROPE_DIM), _const_spec(wq2l.shape), _const_spec(esel.shape),
                  _const_spec(wuvp.shape)],
        out_specs=per(n_new, hv),
        out_shape=jax.ShapeDtypeStruct((nb, n_new, hv), BF16),
        compiler_params=_params(("parallel",)),
        name="sample_attn",
    )(q, cache_ckv, cache_kpe_t, ckv_new, kpe_new, wq2l, esel, wuvp)


def _level_sizes(c):
    sizes = []
    p = c
    while p >= 2:
        sizes.append(p)
        p //= 2
    return sizes


def _level_masks(c):
    t = np.arange(c)[:, None]
    s = np.arange(c)[None, :]
    out = []
    for p in _level_sizes(c):
        out.append((t // p == s // p) & (t % p >= p // 2) & (s % p < p // 2))
    out.append(t == s)
    return np.stack(out).astype(np.float32)


def _boundary(cum, p, c):
    half = p // 2
    if p >= 8:
        parts = [jnp.broadcast_to(cum[i * p + half - 1:i * p + half, :], (p, cum.shape[1]))
                 for i in range(c // p)]
        return jnp.concatenate(parts, axis=0) if len(parts) > 1 else parts[0]
    sub = lax.broadcasted_iota(jnp.int32, (8, cum.shape[1]), 0)
    parts = []
    for g in range(c // 8):
        lo = jnp.broadcast_to(cum[g * 8 + 1:g * 8 + 2, :], (8, cum.shape[1]))
        hi = jnp.broadcast_to(cum[g * 8 + 5:g * 8 + 6, :], (8, cum.shape[1]))
        parts.append(jnp.where(sub < 4, lo, hi))
    return jnp.concatenate(parts, axis=0)


def _hgrn_kernel(q_ref, k_ref, lf_ref, v_ref, og_ref, s0t_ref, gon_ref, mask_ref, tri_ref,
                 o_ref, snt_ref, st_sc, *, c, n_chunks, per_chunk_state):
    step = pl.program_id(2)
    sizes = _level_sizes(c)
    lane = lax.broadcasted_iota(jnp.int32, (1, LANES), 1)
    low = lane < B_DV
    keep_lo = jnp.where(low, 1.0, 0.0).astype(BF16)
    keep_hi = jnp.where(low, 0.0, 1.0).astype(BF16)

    if not per_chunk_state:
        @pl.when(step == 0)
        def _():
            st_sc[...] = jnp.zeros(st_sc.shape, F32)
            st_sc[0, 0:B_DV, :] = s0t_ref[0]
            st_sc[1, B_DV:2 * B_DV, :] = s0t_ref[1]

    tri = tri_ref[...]
    rows = [slice(ci * c, (ci + 1) * c) for ci in range(n_chunks)]
    heads = [slice(j * B_DK, (j + 1) * B_DK) for j in range(2)]
    items = [(ci, j) for ci in range(n_chunks) for j in range(2)]


    cums = []
    for ci in range(n_chunks):
        lf = lf_ref[rows[ci], :]
        hi = lf.astype(BF16)
        r1 = lf - hi.astype(F32)
        mid = r1.astype(BF16)
        lo = (r1 - mid.astype(F32)).astype(BF16)
        cums.append((_dot(tri, hi) + _dot(tri, mid) + _dot(tri, lo)) * LOG2E)

    def zeros(n):
        return jnp.zeros((n, B_DK), F32)

    a_mats = {}
    for ci, j in items:
        q = q_ref[rows[ci], heads[j]]
        k = k_ref[rows[ci], heads[j]]
        cum = cums[ci][:, heads[j]]
        kbf = k.astype(BF16)
        a = _nt(q.astype(BF16), kbf) * mask_ref[len(sizes)]
        q_cols, k_cols = [], []
        for li, p in enumerate(sizes):
            half = p // 2
            if half % 8 == 0:
                for lo in range(0, c, p):
                    mid, hi = lo + half, lo + p
                    bnd = cum[mid - 1:mid, :]
                    qu = q[mid:hi] * jnp.exp2(cum[mid:hi] - bnd)
                    kl = k[lo:mid] * jnp.exp2(bnd - cum[lo:mid])
                    q_cols.append(jnp.concatenate([x for x in (zeros(mid), qu, zeros(c - hi)) if x.shape[0]], axis=0))
                    k_cols.append(jnp.concatenate([x for x in (zeros(lo), kl, zeros(c - mid)) if x.shape[0]], axis=0))
                continue
            if p == 2:
                qt = (q * jnp.exp2(lf_ref[rows[ci], heads[j]] * LOG2E)).astype(BF16)
                kt = kbf
            else:
                bnd = _boundary(cum, p, c)
                qt = (q * jnp.exp2(jnp.minimum(cum - bnd, 0.0))).astype(BF16)
                kt = (k * jnp.exp2(jnp.minimum(bnd - cum, 0.0))).astype(BF16)
            a = a + _nt(qt, kt) * mask_ref[li]
        if q_cols:
            a = a + _nt(jnp.concatenate(q_cols, axis=1).astype(BF16), jnp.concatenate(k_cols, axis=1).astype(BF16))
        a_mats[ci, j] = a.astype(BF16)

    qhats, kvs, decays, vs = {}, {}, {}, {}
    for ci, j in items:
        cum = cums[ci][:, heads[j]]
        last = cum[c - 1:c, :]
        vs[ci, j] = v_ref[rows[ci], :] * (keep_lo if j == 0 else keep_hi)
        qhats[ci, j] = (q_ref[rows[ci], heads[j]] * jnp.exp2(cum)).astype(BF16)
        khat = (k_ref[rows[ci], heads[j]] * jnp.exp2(last - cum)).astype(BF16)
        kvs[ci, j] = _tn(vs[ci, j], khat)
        decays[ci, j] = jnp.exp2(last)

    pad = jnp.zeros((B_DV, B_DK), F32)
    if not per_chunk_state:
        st = [st_sc[0], st_sc[1]]
    for ci in range(n_chunks):
        if per_chunk_state:
            st = [jnp.concatenate([s0t_ref[ci, 0], pad], axis=0), jnp.concatenate([pad, s0t_ref[ci, 1]], axis=0)]
        o_pair = jnp.zeros((c, LANES), F32)
        for j in range(2):
            o_pair = o_pair + _dot(a_mats[ci, j], vs[ci, j]) + _nt(qhats[ci, j], st[j].astype(BF16))
            st[j] = st[j] * decays[ci, j] + kvs[ci, j]
        if per_chunk_state:
            snt_ref[ci, 0] = st[0][0:B_DV, :]
            snt_ref[ci, 1] = st[1][B_DV:2 * B_DV, :]
        o2 = o_pair * o_pair
        s_lo = jnp.sum(jnp.where(low, o2, 0.0), axis=-1, keepdims=True)
        s_hi = jnp.sum(jnp.where(low, 0.0, o2), axis=-1, keepdims=True)
        ms = jnp.where(low, s_lo, s_hi) * (1.0 / B_DV)
        ob = o_pair * lax.rsqrt(ms + EPS) * gon_ref[...] * og_ref[rows[ci], :].astype(F32)
        o_ref[rows[ci], :] = ob.astype(BF16)
    if not per_chunk_state:
        st_sc[0] = st[0]
        st_sc[1] = st[1]

        @pl.when(step == pl.num_programs(2) - 1)
        def _():
            snt_ref[0] = st_sc[0, 0:B_DV, :]
            snt_ref[1] = st_sc[1, B_DV:2 * B_DV, :]


def _hgrn(qb, kb, lf, vb, og, s0t, gon, n_streams, tc):
    b, l, dk = qb.shape
    per_chunk_state = n_streams != b
    if per_chunk_state:
        assert b == 1 and l % n_streams == 0 and l // n_streams <= CHUNK
        c = l // n_streams
    else:
        c = min(CHUNK, l)
    assert l % tc == 0 and tc % c == 0
    n_chunks = tc // c
    masks = jnp.asarray(_level_masks(c))
    tri = jnp.asarray(np.tril(np.ones((c, c), np.float32)), dtype=BF16)
    pairs = B_HEADS // 2
    wide = pl.BlockSpec((None, tc, 2 * B_DK), lambda bi, p, i: (bi, i, p))
    narrow = pl.BlockSpec((None, tc, 2 * B_DV), lambda bi, p, i: (bi, i, p))
    if per_chunk_state:
        state = pl.BlockSpec((n_chunks, 2, B_DV, B_DK), lambda bi, p, i: (i, p, 0, 0))
    else:
        state = pl.BlockSpec((None, 2, B_DV, B_DK), lambda bi, p, i: (bi, p, 0, 0))
    return pl.pallas_call(
        functools.partial(_hgrn_kernel, c=c, n_chunks=n_chunks, per_chunk_state=per_chunk_state),
        grid=(b, pairs, l // tc),
        in_specs=[wide, wide, wide, narrow, narrow, state, _const_spec(gon.shape),
                  _const_spec(masks.shape), _const_spec(tri.shape)],
        out_specs=[narrow, state],
        out_shape=[jax.ShapeDtypeStruct((b, l, B_HEADS * B_DV), BF16),
                   jax.ShapeDtypeStruct((n_streams, B_HEADS, B_DV, B_DK), F32)],
        scratch_shapes=[pltpu.VMEM((2, 2 * B_DV, B_DK), F32)],
        compiler_params=_params(("parallel", "parallel", "parallel" if per_chunk_state else "arbitrary")),
        name="hgrn",
    )(qb, kb, lf, vb, og, s0t, gon, masks, tri)


MOE_SUB = 144
MOE_ROWS = 1024
_BIG_LANE = float(1 << 20)


def _max_items(tm):
    return (tm + N_GROUPS * (MOE_SUB - 1)) // MOE_SUB


def _first_lane_of_max(x, lane):
    v = jnp.max(x, axis=-1, keepdims=True)
    return v, jnp.min(jnp.where(x == v, lane, _BIG_LANE), axis=-1, keepdims=True)


def _out_proj_kernel(oa_ref, ob_ref, sa_ref, sb_ref, x_ref, wa_ref, wb_ref, wo_ref, gffn_ref,
                     wr_ref, br_ref, tri_ref, upper_ref, x1_ref, xs_ref, slot_ref, meta_ref,
                     *, oa_transposed, tm):
    tiles = [slice(u * tm, (u + 1) * tm) for u in range(x_ref.shape[0] // tm)]
    nb = MXU_TILE
    h2s = []
    for rows in tiles:
        oa = oa_ref[:, rows] if oa_transposed else oa_ref[rows, :]
        ob = ob_ref[rows, :]
        m = []
        for i in range(x_ref.shape[1] // nb):
            cols = slice(i * nb, (i + 1) * nb)
            ya = _tn(oa, wa_ref[:, cols]) if oa_transposed else _dot(oa, wa_ref[:, cols])
            yb = _dot(ob, wb_ref[:, cols])
            m.append((sa_ref[rows, cols].astype(F32) * ya + sb_ref[rows, cols].astype(F32) * yb).astype(BF16))
        x1 = x_ref[rows, :] + _dot(jnp.concatenate(m, axis=-1), wo_ref[...])
        x1_ref[rows, :] = x1
        h2s.append(_rms(x1, gffn_ref[...]).astype(BF16))

    lane = lax.broadcasted_iota(jnp.int32, (tm, LANES), 1).astype(F32)
    logits = [_dot(h2, wr_ref[...]) + br_ref[...] for h2 in h2s]
    grps = [_first_lane_of_max(jnp.where(lane < N_GROUPS, lg, NEG_INF), lane)[1] for lg in logits]
    onehots = [jnp.where(lane == grp, 1.0, 0.0) for grp in grps]
    ranks = [_dot(tri_ref[...], oh.astype(BF16)) for oh in onehots]
    items = []
    for rank in ranks:
        count = rank[tm - 1:tm, :]
        n = jnp.zeros_like(count)
        for j in range(_max_items(tm)):
            n = n + jnp.where(count > j * MOE_SUB, 1.0, 0.0)
        items.append(n)
    bases = [_dot(jnp.broadcast_to(n, (8, LANES)).astype(BF16), upper_ref[...])[0:1, :] * MOE_SUB for n in items]
    slots = [jnp.sum(oh * (base + rank - 1.0), axis=-1, keepdims=True)
             for oh, base, rank in zip(onehots, bases, ranks)]
    col = lax.broadcasted_iota(jnp.int32, (tm, MOE_ROWS), 1).astype(F32)
    for u, rows in enumerate(tiles):
        place = jnp.where(col == slots[u], 1.0, 0.0).astype(BF16)
        xs_ref[u] = _tn(place, h2s[u]).astype(BF16)
        slot_ref[rows, :] = jnp.broadcast_to(slots[u], (tm, LANES))
        meta_ref[u] = jnp.broadcast_to(items[u], (8, LANES))


def _out_proj(oa, ob, sa, sb, x, wa, wb, wo, gffn, wr, br, tm, oa_transposed):
    b, l, d = x.shape
    nt = l // tm
    per_step = 2 if nt % 2 == 0 else 1
    ts, ns = tm * per_step, nt // per_step
    assert _max_items(tm) * MOE_SUB <= MOE_ROWS
    hv = A_HEADS * V_DIM
    tri = jnp.asarray(np.tril(np.ones((tm, tm), np.float32)), dtype=BF16)
    upper = jnp.asarray(np.triu(np.ones((LANES, LANES), np.float32), 1), dtype=BF16)
    tok = lambda w: pl.BlockSpec((None, ts, w), lambda bi, i: (bi, i, 0))
    per_tile = lambda r, w: pl.BlockSpec((per_step, r, w), lambda bi, i: (bi * ns + i, 0, 0))
    oa_spec = pl.BlockSpec((None, hv, ts), lambda bi, i: (bi, 0, i)) if oa_transposed else tok(hv)
    return pl.pallas_call(
        functools.partial(_out_proj_kernel, oa_transposed=oa_transposed, tm=tm),
        grid=(b, ns),
        in_specs=[oa_spec, tok(B_HEADS * B_DV), tok(d), tok(d), tok(d), _const_spec(wa.shape),
                  _const_spec(wb.shape), _const_spec(wo.shape), _const_spec(gffn.shape),
                  _const_spec(wr.shape), _const_spec(br.shape), _const_spec(tri.shape),
                  _const_spec(upper.shape)],
        out_specs=[tok(d), per_tile(MOE_ROWS, d), tok(LANES), per_tile(8, LANES)],
        out_shape=[jax.ShapeDtypeStruct((b, l, d), F32),
                   jax.ShapeDtypeStruct((b * nt, MOE_ROWS, d), BF16),
                   jax.ShapeDtypeStruct((b, l, LANES), F32),
                   jax.ShapeDtypeStruct((b * nt, 8, LANES), F32)],
        compiler_params=_params(("parallel", "parallel")),
        name="out_proj",
    )(oa, ob, sa, sb, x, wa, wb, wo, gffn, wr, br, tri, upper)


def _work_list(meta, n_items):
    n = meta[:, 0, :N_GROUPS].astype(jnp.int32)
    tiles = n.shape[0]
    first = jnp.cumsum(n, axis=1) - n
    per_seg = n.T.reshape(-1)
    ends = jnp.cumsum(per_seg)
    total = ends[-1]
    i = jnp.minimum(jnp.arange(n_items, dtype=jnp.int32), total - 1)
    seg = jnp.sum((ends[None, :] <= i[:, None]).astype(jnp.int32), axis=1)
    grp, tile = seg // tiles, seg % tiles
    block = first[tile, grp] + i - (ends[seg] - per_seg[seg])
    return tile, block.astype(jnp.int32), grp, total[None].astype(jnp.int32)


def _moe_expert_kernel(tile_ref, block_ref, grp_ref, total_ref, xs_ref, wr_ref, br_ref, wg_ref, wu_ref,
                       wd_ref, ys_ref):
    i = pl.program_id(0)

    @pl.when(i < total_ref[0])
    def _():
        g = grp_ref[i]
        x = xs_ref[...]
        logits = _dot(x, wr_ref[...]) + br_ref[...]
        lane_i = lax.broadcasted_iota(jnp.int32, logits.shape, 1)
        lane = lane_i.astype(F32)
        is_grp = jnp.abs(lane - (EXPERTS_PER_GROUP + 0.5 * (N_GROUPS - 1))) < 0.5 * N_GROUPS
        lg = jnp.where(is_grp, logits, NEG_INF)
        gmax = jnp.max(lg, axis=-1, keepdims=True)
        denom = jnp.sum(jnp.exp(lg - gmax), axis=-1, keepdims=True)
        lg_own = jnp.sum(jnp.where(lane_i == EXPERTS_PER_GROUP + g, logits, 0.0), axis=-1, keepdims=True)
        p_grp = jnp.exp(lg_own - gmax) / denom
        le = jnp.where(lane < EXPERTS_PER_GROUP, logits, NEG_INF)
        v1, i1 = _first_lane_of_max(le, lane)
        le2 = jnp.where(lane == i1, NEG_INF, le)
        v2, i2 = _first_lane_of_max(le2, lane)
        e2 = jnp.exp(v2 - v1)
        w1 = p_grp / (1.0 + e2)
        comb = jnp.where(lane == i1, w1, 0.0) + jnp.where(lane == i2, w1 * e2, 0.0)
        hid = []
        for e in range(EXPERTS_PER_GROUP):
            a = _dot(x, wg_ref[e])
            u = _dot(x, wu_ref[e])
            s, _ = _sig_pair(a)
            hid.append((a * s * u * comb[:, e:e + 1]).astype(BF16))
        hid = jnp.concatenate(hid, axis=-1)
        wd = wd_ref[...].reshape(EXPERTS_PER_GROUP * D_FF_EXPERT, wd_ref.shape[-1])
        ys_ref[...] = _dot(hid, wd).astype(BF16)


def _moe_experts(xs, work, wrg, brg, wg, wu, wd):
    tiles, _, d = xs.shape
    tile, block, grp, total = work
    item = lambda i, t, k, g, n: (t[i], k[i], 0)
    by_group = lambda i, t, k, g, n: (g[i], 0, 0)
    grid_spec = pltpu.PrefetchScalarGridSpec(
        num_scalar_prefetch=4,
        grid=(tile.shape[0],),
        in_specs=[pl.BlockSpec((None, MOE_SUB, d), item),
                  pl.BlockSpec((None, d, LANES), by_group),
                  pl.BlockSpec((None, 1, LANES), by_group),
                  pl.BlockSpec((EXPERTS_PER_GROUP, d, D_FF_EXPERT), by_group),
                  pl.BlockSpec((EXPERTS_PER_GROUP, d, D_FF_EXPERT), by_group),
                  pl.BlockSpec((EXPERTS_PER_GROUP, D_FF_EXPERT, d), by_group)],
        out_specs=pl.BlockSpec((None, MOE_SUB, d), item))
    return pl.pallas_call(
        _moe_expert_kernel,
        grid_spec=grid_spec,
        out_shape=jax.ShapeDtypeStruct(xs.shape, BF16),
        input_output_aliases={4: 0},
        compiler_params=_params(("arbitrary",)),
        name="moe_experts",
    )(tile, block, grp, total, xs, wrg, brg, wg, wu, wd)


def _moe_combine_kernel(ys_ref, slot_ref, x1_ref, gfin_ref, y_ref, *, tm):
    tiles = [slice(u * tm, (u + 1) * tm) for u in range(x1_ref.shape[0] // tm)]
    col = lax.broadcasted_iota(jnp.int32, (tm, MOE_ROWS), 1).astype(F32)
    places = []
    for rows in tiles:
        slot = jnp.concatenate([slot_ref[rows, :]] * (MOE_ROWS // LANES), axis=-1)
        places.append(jnp.where(col == slot, 1.0, 0.0).astype(BF16))
    for u, rows in enumerate(tiles):
        y_ref[rows, :] = _rms(x1_ref[rows, :] + _dot(places[u], ys_ref[u]), gfin_ref[...])


def _moe_combine(ys, slot, x1, gfin, tm):
    b, l, d = x1.shape
    nt = l // tm
    per_step = 2 if nt % 2 == 0 else 1
    ts, ns = tm * per_step, nt // per_step
    tok = lambda w: pl.BlockSpec((None, ts, w), lambda bi, i: (bi, i, 0))
    per_tile = lambda r, w: pl.BlockSpec((per_step, r, w), lambda bi, i: (bi * ns + i, 0, 0))
    return pl.pallas_call(
        functools.partial(_moe_combine_kernel, tm=tm),
        grid=(b, ns),
        in_specs=[per_tile(MOE_ROWS, d), tok(LANES), tok(d), _const_spec(gfin.shape)],
        out_specs=tok(d),
        out_shape=jax.ShapeDtypeStruct((b, l, d), F32),
        compiler_params=_params(("parallel", "parallel")),
        name="moe_combine",
    )(ys, slot, x1, gfin)


def _rope_tables(pos):
    inv = jnp.power(ROPE_THETA, -jnp.arange(HALF, dtype=F32) / HALF)
    ang = pos.astype(F32)[:, None] * inv[None, :]
    cos, sin = jnp.cos(ang), jnp.sin(ang)
    zeros = jnp.zeros_like(sin)
    reps = LANES // ROPE_DIM
    cosp = jnp.tile(jnp.concatenate([cos, cos], axis=1), (1, reps))
    sinlo = jnp.tile(jnp.concatenate([-sin, zeros], axis=1), (1, reps))
    sinhi = jnp.tile(jnp.concatenate([zeros, sin], axis=1), (1, reps))
    return cosp, sinlo, sinhi, cos.T * (SCALE * LOG2E), sin.T * (SCALE * LOG2E)


def _prep_weights(g_mix, w_in, g_q, w_uq, g_kv, w_ukv, lb_hgrn, g_onorm, w_a_out, w_b_out, w_o,
                  g_ffn, w_rg, b_rg, w_re, b_re, w_gate, w_up, w_down, g_final):
    assert w_in.shape[0] == 1, "single-layer step"
    d = w_in.shape[1]
    w = w_in[0]
    n_a = Q_LORA + KV_LORA
    win_b = jnp.swapaxes(w, 0, 1).astype(BF16)
    win_a = jnp.concatenate([win_b[:n_a]] + [win_b[n_a:n_a + ROPE_DIM]] * (LANES // ROPE_DIM), axis=0)
    per_q = NOPE_DIM + ROPE_DIM
    wuq = jnp.pad(w_uq[0].reshape(Q_LORA, A_HEADS, per_q), ((0, 0), (0, 0), (0, HEAD_PAD - per_q)))
    wuq = wuq.reshape(Q_LORA, A_HEADS * HEAD_PAD).astype(BF16)
    wukv = w_ukv[0].reshape(KV_LORA, A_HEADS, NOPE_DIM + V_DIM)
    w_uk, w_uv = wukv[..., :NOPE_DIM], wukv[..., NOPE_DIM:]
    wuk = jnp.pad(w_uk, ((0, 0), (0, 0), (0, HEAD_PAD - NOPE_DIM))).reshape(KV_LORA, A_HEADS * HEAD_PAD).astype(BF16)
    wuvt = w_uv.reshape(KV_LORA, A_HEADS * V_DIM).T.astype(BF16)
    wq2l = jnp.pad(jnp.transpose(w_uk, (1, 2, 0)), ((0, 0), (0, HEAD_PAD - NOPE_DIM), (0, 0))).astype(BF16)
    esel = jnp.zeros((HEAD_PAD, ROPE_DIM), F32).at[ROPE_LO + jnp.arange(ROPE_DIM), jnp.arange(ROPE_DIM)].set(1.0).astype(BF16)
    wuvp = w_uv.reshape(KV_LORA, A_HEADS * V_DIM).astype(BF16)
    lb = jnp.cumsum(jax.nn.softmax(lb_hgrn.astype(F32), axis=0), axis=0)[0][None, :]
    gon = jnp.tile(g_onorm[0], 2)[None, :]
    wr = jnp.pad(w_rg[0], ((0, 0), (0, LANES - N_GROUPS))).astype(BF16)
    br = jnp.pad(b_rg[0], (0, LANES - N_GROUPS))[None, :]
    rest = LANES - EXPERTS_PER_GROUP - N_GROUPS
    wrg = jnp.concatenate([w_re[0], jnp.broadcast_to(w_rg[0], (N_GROUPS, d, N_GROUPS)),
                           jnp.zeros((N_GROUPS, d, rest), F32)], axis=2).astype(BF16)
    brg = jnp.concatenate([b_re[0], jnp.broadcast_to(b_rg[0], (N_GROUPS, N_GROUPS)),
                           jnp.zeros((N_GROUPS, rest), F32)], axis=1)[:, None, :]
    return dict(
        wrg=wrg, brg=brg,
        gmix=g_mix[0][None, :], win_a=win_a, win_b=win_b, gq=g_q[0][None, :], wuq=wuq, wuqt=wuq.T,
        gkv=g_kv[0][None, :], wuk=wuk, wuvt=wuvt, wq2l=wq2l, esel=esel, wuvp=wuvp, lb=lb, gon=gon,
        wa=w_a_out[0].astype(BF16), wb=w_b_out[0].astype(BF16), wo=w_o[0].astype(BF16),
        gffn=g_ffn[0][None, :], wr=wr, br=br, wg=w_gate[0].astype(BF16), wu=w_up[0].astype(BF16),
        wd=w_down[0].astype(BF16), gfin=g_final[None, :])


def _tile(n, want):
    t = min(n, want)
    while n % t:
        t //= 2
    return t


TOKEN_TILE = 512
ATTN_TILE = 512
ATTN_CHAIN = 256
ATTN_HEADS_PER_STEP = 8
HGRN_STEP = 1024
HGRN_STREAMS_PER_STEP = 8


def _run_group(x, pos, streams, past, w):
    b, l, d = x.shape
    ns, ls = streams
    tm = _tile(l, TOKEN_TILE)
    h, ckv, kpe, q, k, vt = _mla_proj(x, w["gmix"], w["win_a"], w["gq"], w["wuq"], w["wuqt"], w["gkv"],
                                      w["wuk"], w["wuvt"], _rope_tables(pos), tm, past is None)
    qb, kb, lf, vb, og, sa, sb = _hgrn_proj(h, w["win_b"], w["lb"], tm)
    as_streams = lambda a: a.reshape(ns, ls, a.shape[-1])
    if past is None:
        oa = _attn(q, k, vt, _tile(l, ATTN_TILE), ATTN_CHAIN, ATTN_HEADS_PER_STEP)
        s0t = jnp.zeros((ns, B_HEADS, B_DV, B_DK), F32)
    else:
        cache_ckv, cache_kpe, state = past
        oa = _sample_attn(as_streams(q), cache_ckv, cache_kpe, as_streams(ckv), as_streams(kpe),
                          w["wq2l"], w["esel"], w["wuvp"]).reshape(b, l, A_HEADS * V_DIM)
        s0t = jnp.swapaxes(state.astype(F32), -1, -2)
    ob, snt = _hgrn(qb, kb, lf, vb, og, s0t, w["gon"], ns,
                    _tile(l, HGRN_STEP if past is None else HGRN_STREAMS_PER_STEP * ls))
    x1, xs, slot, meta = _out_proj(oa, ob, sa, sb, x, w["wa"], w["wb"], w["wo"], w["gffn"], w["wr"],
                                   w["br"], tm, past is None)
    work = _work_list(meta, meta.shape[0] * _max_items(tm))
    ys = _moe_experts(xs, work, w["wrg"], w["brg"], w["wg"], w["wu"], w["wd"])
    y = _moe_combine(ys, slot, x1, w["gfin"], tm)
    return y, ckv, kpe, jnp.swapaxes(snt, -1, -2)


def kernel(x_prompt, x_sample, cache_ckv, cache_kpe, state_hgrn, g_mix, w_in, g_q, w_uq, g_kv, w_ukv,
           lb_hgrn, g_onorm, w_a_out, w_b_out, w_o, g_ffn, w_rg, b_rg, w_re, b_re, w_gate, w_up, w_down,
           g_final):
    w = _prep_weights(g_mix, w_in, g_q, w_uq, g_kv, w_ukv, lb_hgrn, g_onorm, w_a_out, w_b_out, w_o,
                      g_ffn, w_rg, b_rg, w_re, b_re, w_gate, w_up, w_down, g_final)
    bp, lp, d = x_prompt.shape
    y_p, ckv_p, kpe_p, st_p = _run_group(x_prompt, jnp.arange(lp, dtype=jnp.int32), (bp, lp), None, w)

    bs, ls, _ = x_sample.shape
    past_len = cache_ckv.shape[2]
    pos_s = past_len + (jnp.arange(bs * ls, dtype=jnp.int32) % ls)
    y_s, ckv_s, kpe_s, st_s = _run_group(x_sample.reshape(1, bs * ls, d), pos_s, (bs, ls),
                                         (cache_ckv[0], jnp.swapaxes(cache_kpe[0], 1, 2), state_hgrn[0]), w)
    return (y_p, y_s.reshape(bs, ls, d),
            ckv_p[None], kpe_p[None], st_p[None].astype(x_prompt.dtype),
            ckv_s.reshape(1, bs, ls, KV_LORA), kpe_s.reshape(1, bs, ls, ROPE_DIM),
            st_s[None].astype(state_hgrn.dtype))
```

```python
import functools

import numpy as np
import jax
import jax.numpy as jnp
from jax import lax
from jax.experimental import pallas as pl
from jax.experimental.pallas import tpu as pltpu

F32 = jnp.float32
BF16 = jnp.bfloat16

EPS = 1e-6
CHUNK = 64
A_HEADS = 8
Q_LORA = 384
KV_LORA = 256
NOPE_DIM = 64
ROPE_DIM = 32
V_DIM = 64
ROPE_THETA = 10000.0
B_HEADS = 8
B_DK = 128
B_DV = 64
N_GROUPS = 4
EXPERTS_PER_GROUP = 8
N_EXPERTS = N_GROUPS * EXPERTS_PER_GROUP
D_FF_EXPERT = 256

LANES = 128
MXU_TILE = 256
VMEM_BYTES_V7X = 64 * 1024 * 1024
VMEM_LIMIT = VMEM_BYTES_V7X - 8 * 1024 * 1024

HEAD_PAD = LANES
V_AUG = V_DIM + 16
SAMPLE_KEY_BLOCK = 1024
ROPE_LO = NOPE_DIM
ROPE_HI = NOPE_DIM + ROPE_DIM
HALF = ROPE_DIM // 2
SCALE = (NOPE_DIM + ROPE_DIM) ** -0.5
LOG2E = 1.4426950408889634
NEG_INF = float("-inf")


def _params(sem):
    return pltpu.CompilerParams(dimension_semantics=sem, vmem_limit_bytes=VMEM_LIMIT)


def _const_spec(shape):
    nd = len(shape)
    return pl.BlockSpec(shape, lambda *_: (0,) * nd, pipeline_mode=pl.Buffered(1))


def _rms(x, g):
    ms = jnp.mean(x * x, axis=-1, keepdims=True)
    return x * lax.rsqrt(ms + EPS) * g


def _sig_pair(x):
    e = jnp.exp(-jnp.abs(x))
    r = 1.0 / (1.0 + e)
    er = e * r
    pos = x >= 0
    return jnp.where(pos, r, er), jnp.where(pos, er, r)


def _nt(a, b):
    return lax.dot_general(a, b, (((1,), (1,)), ((), ())), preferred_element_type=F32)


def _tn(a, b):
    return lax.dot_general(a, b, (((0,), (0,)), ((), ())), preferred_element_type=F32)


def _dot(a, b):
    return jnp.dot(a, b, preferred_element_type=F32)


def _rope(x, cosp, sinlo, sinhi):
    return x * cosp + pltpu.roll(x, LANES - HALF, 1) * sinlo + pltpu.roll(x, HALF, 1) * sinhi


def _mla_proj_kernel(x_ref, gmix_ref, win_ref, gq_ref, wuq_ref, wuqt_ref, gkv_ref, wuk_ref, wuvt_ref,
                     cos_ref, sinlo_ref, sinhi_ref, cost_ref, sint_ref,
                     h_ref, ckv_ref, kpe_ref, q_ref, k_ref, vt_ref, *, q_transposed, tm):
    tiles = [slice(u * tm, (u + 1) * tm) for u in range(x_ref.shape[0] // tm)]
    lane = lax.broadcasted_iota(jnp.int32, (1, LANES), 1)
    rope_lanes = (lane // ROPE_DIM) == (ROPE_LO // ROPE_DIM)

    hs = []
    for rows in tiles:
        h = _rms(x_ref[rows, :], gmix_ref[...]).astype(BF16)
        h_ref[rows, :] = h
        hs.append(h)
    zs = [_nt(h, win_ref[...]) for h in hs]
    cqns = [_rms(z[:, :Q_LORA], gq_ref[...]).astype(BF16) for z in zs]
    ckvs = [_rms(z[:, Q_LORA:Q_LORA + KV_LORA], gkv_ref[...]) for z in zs]

    for rows, cqn in zip(tiles, cqns):
        if q_transposed:
            qt = _nt(wuqt_ref[...], cqn)
            cos_t, sin_t = cost_ref[:, rows], sint_ref[:, rows]
            for hd in range(A_HEADS):
                r0 = hd * HEAD_PAD
                q_ref[r0:r0 + ROPE_LO, rows] = (qt[r0:r0 + ROPE_LO] * (SCALE * LOG2E)).astype(BF16)
                a = qt[r0 + ROPE_LO:r0 + ROPE_LO + HALF]
                b = qt[r0 + ROPE_LO + HALF:r0 + ROPE_HI]
                q_ref[r0 + ROPE_LO:r0 + ROPE_HI, rows] = jnp.concatenate(
                    [a * cos_t - b * sin_t, a * sin_t + b * cos_t], axis=0).astype(BF16)
                q_ref[r0 + ROPE_HI:r0 + HEAD_PAD, rows] = jnp.zeros((HEAD_PAD - ROPE_HI, tm), BF16)
        else:
            q = _dot(cqn, wuq_ref[...])
            cosp, sinlo, sinhi = cos_ref[rows, :], sinlo_ref[rows, :], sinhi_ref[rows, :]
            for hd in range(A_HEADS):
                sl = slice(hd * HEAD_PAD, (hd + 1) * HEAD_PAD)
                qh = q[:, sl]
                qh = jnp.where(rope_lanes, _rope(qh, cosp, sinlo, sinhi), qh) * (SCALE * LOG2E)
                q_ref[rows, sl] = qh.astype(BF16)

    for rows, z, ckv in zip(tiles, zs, ckvs):
        ckv_ref[rows, :] = ckv
        ckv_bf = ckv.astype(BF16)
        cosp, sinlo, sinhi = cos_ref[rows, :], sinlo_ref[rows, :], sinhi_ref[rows, :]
        kpe_rot = _rope(z[:, Q_LORA + KV_LORA:], cosp, sinlo, sinhi)
        if q_transposed:
            kpe_ref[:, rows] = kpe_rot.T[:ROPE_DIM]
        else:
            kpe_ref[rows, :] = kpe_rot[:, :ROPE_DIM]
        kpe_placed = jnp.where(rope_lanes, kpe_rot, 0.0)
        kn = _dot(ckv_bf, wuk_ref[...])
        for hd in range(A_HEADS):
            sl = slice(hd * HEAD_PAD, (hd + 1) * HEAD_PAD)
            k_ref[rows, sl] = (kn[:, sl] + kpe_placed).astype(BF16)
        vt = _nt(wuvt_ref[...], ckv_bf).astype(BF16)
        ones = jnp.ones((V_AUG - V_DIM, tm), BF16)
        for hd in range(A_HEADS):
            vt_ref[hd * V_AUG:hd * V_AUG + V_DIM, rows] = vt[hd * V_DIM:(hd + 1) * V_DIM]
            vt_ref[hd * V_AUG + V_DIM:(hd + 1) * V_AUG, rows] = ones


def _mla_proj(x, gmix, win_a, gq, wuq, wuqt, gkv, wuk, wuvt, tables, tm, q_transposed):
    b, l, d = x.shape
    hq = A_HEADS * HEAD_PAD
    hv = A_HEADS * V_AUG
    cosp, sinlo, sinhi, cos_t, sin_t = tables
    per_step = 2 if (l // tm) % 2 == 0 else 1
    ts = tm * per_step
    tok = lambda w: pl.BlockSpec((None, ts, w), lambda bi, i: (bi, i, 0))
    tok_t = lambda w: pl.BlockSpec((None, w, ts), lambda bi, i: (bi, 0, i))
    tab = pl.BlockSpec((ts, LANES), lambda bi, i: (i, 0))
    tab_t = pl.BlockSpec((HALF, ts), lambda bi, i: (0, i))
    return pl.pallas_call(
        functools.partial(_mla_proj_kernel, q_transposed=q_transposed, tm=tm),
        grid=(b, l // ts),
        in_specs=[tok(d), _const_spec(gmix.shape), _const_spec(win_a.shape), _const_spec(gq.shape),
                  _const_spec(wuq.shape), _const_spec(wuqt.shape), _const_spec(gkv.shape),
                  _const_spec(wuk.shape), _const_spec(wuvt.shape), tab, tab, tab, tab_t, tab_t],
        out_specs=[tok(d), tok(KV_LORA), tok_t(ROPE_DIM) if q_transposed else tok(ROPE_DIM),
                   tok_t(hq) if q_transposed else tok(hq), tok(hq),
                   tok_t(hv)],
        out_shape=[jax.ShapeDtypeStruct((b, l, d), BF16),
                   jax.ShapeDtypeStruct((b, l, KV_LORA), F32),
                   jax.ShapeDtypeStruct((b, ROPE_DIM, l) if q_transposed else (b, l, ROPE_DIM), F32),
                   jax.ShapeDtypeStruct((b, hq, l) if q_transposed else (b, l, hq), BF16),
                   jax.ShapeDtypeStruct((b, l, hq), BF16),
                   jax.ShapeDtypeStruct((b, hv, l), BF16)],
        compiler_params=_params(("parallel", "parallel")),
        name="mla_proj",
    )(x, gmix, win_a, gq, wuq, wuqt, gkv, wuk, wuvt, cosp, sinlo, sinhi, cos_t, sin_t)


def _hgrn_proj_kernel(h_ref, wt_ref, lb_ref, qb_ref, kb_ref, lf_ref, vb_ref, og_ref, sa_ref, sb_ref, *, row0):
    h = h_ref[...]
    dk = B_HEADS * B_DK
    dv = B_HEADS * B_DV
    dm = sa_ref.shape[-1]
    nb = MXU_TILE
    lb_all = lb_ref[...]

    def z(base, i):
        return _nt(h, wt_ref[row0 + base + i * nb:row0 + base + (i + 1) * nb, :])

    for i in range(dk // nb):
        cols = slice(i * nb, (i + 1) * nb)
        zq = z(0, i)
        qb_ref[:, cols] = zq * jax.nn.sigmoid(zq)
        sf, snf = _sig_pair(z(dk, i))
        lb = lb_all[:, cols]
        lf_ref[:, cols] = jnp.log(lb + (1.0 - lb) * sf)
        kb_ref[:, cols] = (1.0 - lb) * snf
    for i in range(dv // nb):
        cols = slice(i * nb, (i + 1) * nb)
        vb_ref[:, cols] = z(2 * dk, i).astype(BF16)
        zg = z(2 * dk + dv, i)
        og_ref[:, cols] = (zg * jax.nn.sigmoid(zg)).astype(BF16)
    for i in range(dm // nb):
        cols = slice(i * nb, (i + 1) * nb)
        sa_ref[:, cols] = jax.nn.sigmoid(z(2 * dk + 2 * dv, i)).astype(BF16)
        sb_ref[:, cols] = jax.nn.sigmoid(z(2 * dk + 2 * dv + dm, i)).astype(BF16)


def _hgrn_proj(h, win_b, lb, tm):
    b, l, d = h.shape
    dk = B_HEADS * B_DK
    dv = B_HEADS * B_DV
    tok = lambda w: pl.BlockSpec((None, tm, w), lambda bi, i: (bi, i, 0))
    sds = lambda w, dt: jax.ShapeDtypeStruct((b, l, w), dt)
    return pl.pallas_call(
        functools.partial(_hgrn_proj_kernel, row0=Q_LORA + KV_LORA + ROPE_DIM),
        grid=(b, l // tm),
        in_specs=[tok(d), _const_spec(win_b.shape), _const_spec(lb.shape)],
        out_specs=[tok(dk), tok(dk), tok(dk), tok(dv), tok(dv), tok(d), tok(d)],
        out_shape=[sds(dk, F32), sds(dk, F32), sds(dk, F32), sds(dv, BF16), sds(dv, BF16),
                   sds(d, BF16), sds(d, BF16)],
        compiler_params=_params(("parallel", "parallel")),
        name="hgrn_proj",
    )(h, win_b, lb)


def _attn_kernel(qt_ref, k_ref, vt_ref, o_ref, s_sc, m_sc, acc_sc, *, t, tc, hp):
    qi = pl.program_id(2)
    n_chains = t // tc
    m_sc[...] = jnp.full(m_sc.shape, NEG_INF, F32)
    acc_sc[...] = jnp.zeros(acc_sc.shape, F32)
    qk_rows = [slice(hd * HEAD_PAD, (hd + 1) * HEAD_PAD) for hd in range(hp)]
    v_rows = [slice(hd * V_AUG, (hd + 1) * V_AUG) for hd in range(hp)]

    def scores(kb, slot):
        start = pl.multiple_of(kb * t, t)
        for hd in range(hp):
            k = k_ref[pl.ds(start, t), qk_rows[hd]]
            for ch in range(n_chains):
                cols = slice(ch * tc, (ch + 1) * tc)
                s_sc[hd, slot, :, cols] = _dot(k, qt_ref[qk_rows[hd], cols])

    def consume(kb, slot, masked):
        start = pl.multiple_of(kb * t, t)
        for hd in range(hp):
            for ch in range(n_chains):
                cols = slice(ch * tc, (ch + 1) * tc)
                nk = (ch + 1) * tc if masked else t
                vt = vt_ref[v_rows[hd], pl.ds(start, nk)]
                s = s_sc[hd, slot, 0:nk, cols]
                if masked:
                    r = lax.broadcasted_iota(jnp.int32, (nk, tc), 0) // CHUNK
                    c = (lax.broadcasted_iota(jnp.int32, (nk, tc), 1) + ch * tc) // CHUNK
                    s = jnp.where(r <= c, s, NEG_INF)
                m_prev = m_sc[hd, :, cols]
                m_new = jnp.maximum(m_prev, jnp.max(s, axis=0, keepdims=True))
                alpha = jnp.exp2(m_prev - m_new)
                p = jnp.exp2(s - m_new)
                acc_sc[hd, :, cols] = alpha * acc_sc[hd, :, cols] + _dot(vt, p.astype(BF16))
                m_sc[hd, :, cols] = m_new

    scores(0, 0)

    def body(j, carry):
        kb = 2 * j
        scores(kb + 1, 1)
        consume(kb, 0, False)
        scores(kb + 2, 0)
        consume(kb + 1, 1, False)
        return carry

    lax.fori_loop(0, qi // 2, body, 0)

    @pl.when(qi % 2 == 0)
    def _():
        consume(qi, 0, True)

    @pl.when(qi % 2 == 1)
    def _():
        scores(qi, 1)
        consume(qi - 1, 0, False)
        consume(qi, 1, True)

    for hd in range(hp):
        acc = acc_sc[hd]
        o_ref[hd * V_DIM:(hd + 1) * V_DIM, :] = (acc[:V_DIM] / acc[V_DIM:V_DIM + 1]).astype(BF16)


def _attn(qt, k, vt, t, tc, hp):
    b, l, _ = k.shape
    once = dict(pipeline_mode=pl.Buffered(1)) if hp == A_HEADS else {}
    return pl.pallas_call(
        functools.partial(_attn_kernel, t=t, tc=min(tc, t), hp=hp),
        grid=(b, A_HEADS // hp, l // t),
        in_specs=[pl.BlockSpec((None, hp * HEAD_PAD, t), lambda bi, h, i: (bi, h, i)),
                  pl.BlockSpec((None, l, hp * HEAD_PAD), lambda bi, h, i: (bi, 0, h), **once),
                  pl.BlockSpec((None, hp * V_AUG, l), lambda bi, h, i: (bi, h, 0), **once)],
        out_specs=pl.BlockSpec((None, hp * V_DIM, t), lambda bi, h, i: (bi, h, i)),
        out_shape=jax.ShapeDtypeStruct((b, A_HEADS * V_DIM, l), BF16),
        scratch_shapes=[pltpu.VMEM((hp, 2, t, t), F32), pltpu.VMEM((hp, 1, t), F32),
                        pltpu.VMEM((hp, V_AUG, t), F32)],
        compiler_params=_params(("parallel", "parallel", "arbitrary")),
        name="attn",
    )(qt, k, vt)


def _sample_attn_kernel(q_ref, cc_ref, ckt_ref, nc_ref, nk_ref, wq2l_ref, esel_ref, wuv_ref, o_ref,
                        *, past, n_new):
    qs = q_ref[...]
    heads = [qs[:, hd * HEAD_PAD:(hd + 1) * HEAD_PAD] for hd in range(A_HEADS)]
    ql = jnp.concatenate([_dot(heads[hd], wq2l_ref[hd]) for hd in range(A_HEADS)], axis=0).astype(BF16)
    qp = jnp.concatenate([_dot(heads[hd], esel_ref[...]) for hd in range(A_HEADS)], axis=0).astype(BF16)
    nc = nc_ref[...].astype(BF16)
    nk = nk_ref[...].astype(BF16)
    rows = A_HEADS * n_new

    s_n = _nt(ql, nc) + _nt(qp, nk)
    tq = lax.broadcasted_iota(jnp.int32, (rows, n_new), 0) % n_new
    tk = lax.broadcasted_iota(jnp.int32, (rows, n_new), 1)
    s_n = jnp.where((past + tk) // CHUNK <= (past + tq) // CHUNK, s_n, NEG_INF)
    m = jnp.max(s_n, axis=-1, keepdims=True)
    p_n = jnp.exp2(s_n - m)
    denom = jnp.sum(p_n, axis=-1, keepdims=True)
    acc = _dot(p_n.astype(BF16), nc)

    n_blocks = max(1, past // SAMPLE_KEY_BLOCK)
    assert past % n_blocks == 0
    kb = past // n_blocks

    def scores(i):
        cc = cc_ref[i * kb:(i + 1) * kb, :].astype(BF16)
        ckt = ckt_ref[:, i * kb:(i + 1) * kb].astype(BF16)
        return _nt(ql, cc) + _dot(qp, ckt), cc

    nxt = scores(0)
    for i in range(n_blocks):
        s_c, cc = nxt
        if i + 1 < n_blocks:
            nxt = scores(i + 1)
        m_new = jnp.maximum(m, jnp.max(s_c, axis=-1, keepdims=True))
        alpha = jnp.exp2(m - m_new)
        p_c = jnp.exp2(s_c - m_new)
        denom = alpha * denom + jnp.sum(p_c, axis=-1, keepdims=True)
        acc = alpha * acc + _dot(p_c.astype(BF16), cc)
        m = m_new
    o_lat = (acc / denom).astype(BF16)
    full = _dot(o_lat, wuv_ref[...])
    col_head = lax.broadcasted_iota(jnp.int32, (n_new, A_HEADS * V_DIM), 1) // V_DIM
    out = jnp.zeros((n_new, A_HEADS * V_DIM), F32)
    for hd in range(A_HEADS):
        out = out + jnp.where(col_head == hd, full[hd * n_new:(hd + 1) * n_new], 0.0)
    o_ref[...] = out.astype(BF16)


def _sample_attn(q, cache_ckv, cache_kpe_t, ckv_new, kpe_new, wq2l, esel, wuvp):
    nb, n_new, hq = q.shape
    past = cache_ckv.shape[1]
    hv = A_HEADS * V_DIM
    per = lambda r, w: pl.BlockSpec((None, r, w), lambda bi: (bi, 0, 0))
    return pl.pallas_call(
        functools.partial(_sample_attn_kernel, past=past, n_new=n_new),
        grid=(nb,),
        in_specs=[per(n_new, hq), per(past, KV_LORA), per(ROPE_DIM, past), per(n_new, KV_LORA),
                  per(n_new, ROPE_DIM), _const_spec(wq2l.shape), _const_spec(esel.shape),
                  _const_spec(wuvp.shape)],
        out_specs=per(n_new, hv),
        out_shape=jax.ShapeDtypeStruct((nb, n_new, hv), BF16),
        compiler_params=_params(("parallel",)),
        name="sample_attn",
    )(q, cache_ckv, cache_kpe_t, ckv_new, kpe_new, wq2l, esel, wuvp)


def _level_sizes(c):
    sizes = []
    p = c
    while p >= 2:
        sizes.append(p)
        p //= 2
    return sizes


def _level_masks(c):
    t = np.arange(c)[:, None]
    s = np.arange(c)[None, :]
    out = []
    for p in _level_sizes(c):
        out.append((t // p == s // p) & (t % p >= p // 2) & (s % p < p // 2))
    out.append(t == s)
    return np.stack(out).astype(np.float32)


def _boundary(cum, p, c):
    half = p // 2
    if p >= 8:
        parts = [jnp.broadcast_to(cum[i * p + half - 1:i * p + half, :], (p, cum.shape[1]))
                 for i in range(c // p)]
        return jnp.concatenate(parts, axis=0) if len(parts) > 1 else parts[0]
    sub = lax.broadcasted_iota(jnp.int32, (8, cum.shape[1]), 0)
    parts = []
    for g in range(c // 8):
        lo = jnp.broadcast_to(cum[g * 8 + 1:g * 8 + 2, :], (8, cum.shape[1]))
        hi = jnp.broadcast_to(cum[g * 8 + 5:g * 8 + 6, :], (8, cum.shape[1]))
        parts.append(jnp.where(sub < 4, lo, hi))
    return jnp.concatenate(parts, axis=0)


def _hgrn_kernel(q_ref, k_ref, lf_ref, v_ref, og_ref, s0t_ref, gon_ref, mask_ref, tri_ref,
                 o_ref, snt_ref, st_sc, *, c, n_chunks, per_chunk_state):
    step = pl.program_id(2)
    sizes = _level_sizes(c)
    lane = lax.broadcasted_iota(jnp.int32, (1, LANES), 1)
    low = lane < B_DV
    keep_lo = jnp.where(low, 1.0, 0.0).astype(BF16)
    keep_hi = jnp.where(low, 0.0, 1.0).astype(BF16)

    if not per_chunk_state:
        @pl.when(step == 0)
        def _():
            st_sc[...] = jnp.zeros(st_sc.shape, F32)
            st_sc[0, 0:B_DV, :] = s0t_ref[0]
            st_sc[1, B_DV:2 * B_DV, :] = s0t_ref[1]

    tri = tri_ref[...]
    rows = [slice(ci * c, (ci + 1) * c) for ci in range(n_chunks)]
    heads = [slice(j * B_DK, (j + 1) * B_DK) for j in range(2)]
    items = [(ci, j) for ci in range(n_chunks) for j in range(2)]


    cums = []
    for ci in range(n_chunks):
        lf = lf_ref[rows[ci], :]
        hi = lf.astype(BF16)
        r1 = lf - hi.astype(F32)
        mid = r1.astype(BF16)
        lo = (r1 - mid.astype(F32)).astype(BF16)
        cums.append((_dot(tri, hi) + _dot(tri, mid) + _dot(tri, lo)) * LOG2E)

    def zeros(n):
        return jnp.zeros((n, B_DK), F32)

    a_mats = {}
    for ci, j in items:
        q = q_ref[rows[ci], heads[j]]
        k = k_ref[rows[ci], heads[j]]
        cum = cums[ci][:, heads[j]]
        kbf = k.astype(BF16)
        a = _nt(q.astype(BF16), kbf) * mask_ref[len(sizes)]
        q_cols, k_cols = [], []
        for li, p in enumerate(sizes):
            half = p // 2
            if half % 8 == 0:
                for lo in range(0, c, p):
                    mid, hi = lo + half, lo + p
                    bnd = cum[mid - 1:mid, :]
                    qu = q[mid:hi] * jnp.exp2(cum[mid:hi] - bnd)
                    kl = k[lo:mid] * jnp.exp2(bnd - cum[lo:mid])
                    q_cols.append(jnp.concatenate([x for x in (zeros(mid), qu, zeros(c - hi)) if x.shape[0]], axis=0))
                    k_cols.append(jnp.concatenate([x for x in (zeros(lo), kl, zeros(c - mid)) if x.shape[0]], axis=0))
                continue
            if p == 2:
                qt = (q * jnp.exp2(lf_ref[rows[ci], heads[j]] * LOG2E)).astype(BF16)
                kt = kbf
            else:
                bnd = _boundary(cum, p, c)
                qt = (q * jnp.exp2(jnp.minimum(cum - bnd, 0.0))).astype(BF16)
                kt = (k * jnp.exp2(jnp.minimum(bnd - cum, 0.0))).astype(BF16)
            a = a + _nt(qt, kt) * mask_ref[li]
        if q_cols:
            a = a + _nt(jnp.concatenate(q_cols, axis=1).astype(BF16), jnp.concatenate(k_cols, axis=1).astype(BF16))
        a_mats[ci, j] = a.astype(BF16)

    qhats, kvs, decays, vs = {}, {}, {}, {}
    for ci, j in items:
        cum = cums[ci][:, heads[j]]
        last = cum[c - 1:c, :]
        vs[ci, j] = v_ref[rows[ci], :] * (keep_lo if j == 0 else keep_hi)
        qhats[ci, j] = (q_ref[rows[ci], heads[j]] * jnp.exp2(cum)).astype(BF16)
        khat = (k_ref[rows[ci], heads[j]] * jnp.exp2(last - cum)).astype(BF16)
        kvs[ci, j] = _tn(vs[ci, j], khat)
        decays[ci, j] = jnp.exp2(last)

    pad = jnp.zeros((B_DV, B_DK), F32)
    if not per_chunk_state:
        st = [st_sc[0], st_sc[1]]
    for ci in range(n_chunks):
        if per_chunk_state:
            st = [jnp.concatenate([s0t_ref[ci, 0], pad], axis=0), jnp.concatenate([pad, s0t_ref[ci, 1]], axis=0)]
        o_pair = jnp.zeros((c, LANES), F32)
        for j in range(2):
            o_pair = o_pair + _dot(a_mats[ci, j], vs[ci, j]) + _nt(qhats[ci, j], st[j].astype(BF16))
            st[j] = st[j] * decays[ci, j] + kvs[ci, j]
        if per_chunk_state:
            snt_ref[ci, 0] = st[0][0:B_DV, :]
            snt_ref[ci, 1] = st[1][B_DV:2 * B_DV, :]
        o2 = o_pair * o_pair
        s_lo = jnp.sum(jnp.where(low, o2, 0.0), axis=-1, keepdims=True)
        s_hi = jnp.sum(jnp.where(low, 0.0, o2), axis=-1, keepdims=True)
        ms = jnp.where(low, s_lo, s_hi) * (1.0 / B_DV)
        ob = o_pair * lax.rsqrt(ms + EPS) * gon_ref[...] * og_ref[rows[ci], :].astype(F32)
        o_ref[rows[ci], :] = ob.astype(BF16)
    if not per_chunk_state:
        st_sc[0] = st[0]
        st_sc[1] = st[1]

        @pl.when(step == pl.num_programs(2) - 1)
        def _():
            snt_ref[0] = st_sc[0, 0:B_DV, :]
            snt_ref[1] = st_sc[1, B_DV:2 * B_DV, :]


def _hgrn(qb, kb, lf, vb, og, s0t, gon, n_streams, tc):
    b, l, dk = qb.shape
    per_chunk_state = n_streams != b
    if per_chunk_state:
        assert b == 1 and l % n_streams == 0 and l // n_streams <= CHUNK
        c = l // n_streams
    else:
        c = min(CHUNK, l)
    assert l % tc == 0 and tc % c == 0
    n_chunks = tc // c
    masks = jnp.asarray(_level_masks(c))
    tri = jnp.asarray(np.tril(np.ones((c, c), np.float32)), dtype=BF16)
    pairs = B_HEADS // 2
    wide = pl.BlockSpec((None, tc, 2 * B_DK), lambda bi, p, i: (bi, i, p))
    narrow = pl.BlockSpec((None, tc, 2 * B_DV), lambda bi, p, i: (bi, i, p))
    if per_chunk_state:
        state = pl.BlockSpec((n_chunks, 2, B_DV, B_DK), lambda bi, p, i: (i, p, 0, 0))
    else:
        state = pl.BlockSpec((None, 2, B_DV, B_DK), lambda bi, p, i: (bi, p, 0, 0))
    return pl.pallas_call(
        functools.partial(_hgrn_kernel, c=c, n_chunks=n_chunks, per_chunk_state=per_chunk_state),
        grid=(b, pairs, l // tc),
        in_specs=[wide, wide, wide, narrow, narrow, state, _const_spec(gon.shape),
                  _const_spec(masks.shape), _const_spec(tri.shape)],
        out_specs=[narrow, state],
        out_shape=[jax.ShapeDtypeStruct((b, l, B_HEADS * B_DV), BF16),
                   jax.ShapeDtypeStruct((n_streams, B_HEADS, B_DV, B_DK), F32)],
        scratch_shapes=[pltpu.VMEM((2, 2 * B_DV, B_DK), F32)],
        compiler_params=_params(("parallel", "parallel", "parallel" if per_chunk_state else "arbitrary")),
        name="hgrn",
    )(qb, kb, lf, vb, og, s0t, gon, masks, tri)


MOE_SUB = 144
MOE_ROWS = 1024
_BIG_LANE = float(1 << 20)


def _max_items(tm):
    return (tm + N_GROUPS * (MOE_SUB - 1)) // MOE_SUB


def _first_lane_of_max(x, lane):
    v = jnp.max(x, axis=-1, keepdims=True)
    return v, jnp.min(jnp.where(x == v, lane, _BIG_LANE), axis=-1, keepdims=True)


def _out_proj_kernel(oa_ref, ob_ref, sa_ref, sb_ref, x_ref, wa_ref, wb_ref, wo_ref, gffn_ref,
                     wr_ref, br_ref, tri_ref, upper_ref, x1_ref, xs_ref, slot_ref, meta_ref,
                     *, oa_transposed, tm):
    tiles = [slice(u * tm, (u + 1) * tm) for u in range(x_ref.shape[0] // tm)]
    nb = MXU_TILE
    h2s = []
    for rows in tiles:
        oa = oa_ref[:, rows] if oa_transposed else oa_ref[rows, :]
        ob = ob_ref[rows, :]
        m = []
        for i in range(x_ref.shape[1] // nb):
            cols = slice(i * nb, (i + 1) * nb)
            ya = _tn(oa, wa_ref[:, cols]) if oa_transposed else _dot(oa, wa_ref[:, cols])
            yb = _dot(ob, wb_ref[:, cols])
            m.append((sa_ref[rows, cols].astype(F32) * ya + sb_ref[rows, cols].astype(F32) * yb).astype(BF16))
        x1 = x_ref[rows, :] + _dot(jnp.concatenate(m, axis=-1), wo_ref[...])
        x1_ref[rows, :] = x1
        h2s.append(_rms(x1, gffn_ref[...]).astype(BF16))

    lane = lax.broadcasted_iota(jnp.int32, (tm, LANES), 1).astype(F32)
    logits = [_dot(h2, wr_ref[...]) + br_ref[...] for h2 in h2s]
    grps = [_first_lane_of_max(jnp.where(lane < N_GROUPS, lg, NEG_INF), lane)[1] for lg in logits]
    onehots = [jnp.where(lane == grp, 1.0, 0.0) for grp in grps]
    ranks = [_dot(tri_ref[...], oh.astype(BF16)) for oh in onehots]
    items = []
    for rank in ranks:
        count = rank[tm - 1:tm, :]
        n = jnp.zeros_like(count)
        for j in range(_max_items(tm)):
            n = n + jnp.where(count > j * MOE_SUB, 1.0, 0.0)
        items.append(n)
    bases = [_dot(jnp.broadcast_to(n, (8, LANES)).astype(BF16), upper_ref[...])[0:1, :] * MOE_SUB for n in items]
    slots = [jnp.sum(oh * (base + rank - 1.0), axis=-1, keepdims=True)
             for oh, base, rank in zip(onehots, bases, ranks)]
    col = lax.broadcasted_iota(jnp.int32, (tm, MOE_ROWS), 1).astype(F32)
    for u, rows in enumerate(tiles):
        place = jnp.where(col == slots[u], 1.0, 0.0).astype(BF16)
        xs_ref[u] = _tn(place, h2s[u]).astype(BF16)
        slot_ref[rows, :] = jnp.broadcast_to(slots[u], (tm, LANES))
        meta_ref[u] = jnp.broadcast_to(items[u], (8, LANES))


def _out_proj(oa, ob, sa, sb, x, wa, wb, wo, gffn, wr, br, tm, oa_transposed):
    b, l, d = x.shape
    nt = l // tm
    per_step = 2 if nt % 2 == 0 else 1
    ts, ns = tm * per_step, nt // per_step
    assert _max_items(tm) * MOE_SUB <= MOE_ROWS
    hv = A_HEADS * V_DIM
    tri = jnp.asarray(np.tril(np.ones((tm, tm), np.float32)), dtype=BF16)
    upper = jnp.asarray(np.triu(np.ones((LANES, LANES), np.float32), 1), dtype=BF16)
    tok = lambda w: pl.BlockSpec((None, ts, w), lambda bi, i: (bi, i, 0))
    per_tile = lambda r, w: pl.BlockSpec((per_step, r, w), lambda bi, i: (bi * ns + i, 0, 0))
    oa_spec = pl.BlockSpec((None, hv, ts), lambda bi, i: (bi, 0, i)) if oa_transposed else tok(hv)
    return pl.pallas_call(
        functools.partial(_out_proj_kernel, oa_transposed=oa_transposed, tm=tm),
        grid=(b, ns),
        in_specs=[oa_spec, tok(B_HEADS * B_DV), tok(d), tok(d), tok(d), _const_spec(wa.shape),
                  _const_spec(wb.shape), _const_spec(wo.shape), _const_spec(gffn.shape),
                  _const_spec(wr.shape), _const_spec(br.shape), _const_spec(tri.shape),
                  _const_spec(upper.shape)],
        out_specs=[tok(d), per_tile(MOE_ROWS, d), tok(LANES), per_tile(8, LANES)],
        out_shape=[jax.ShapeDtypeStruct((b, l, d), F32),
                   jax.ShapeDtypeStruct((b * nt, MOE_ROWS, d), BF16),
                   jax.ShapeDtypeStruct((b, l, LANES), F32),
                   jax.ShapeDtypeStruct((b * nt, 8, LANES), F32)],
        compiler_params=_params(("parallel", "parallel")),
        name="out_proj",
    )(oa, ob, sa, sb, x, wa, wb, wo, gffn, wr, br, tri, upper)


def _work_list(meta, n_items):
    n = meta[:, 0, :N_GROUPS].astype(jnp.int32)
    tiles = n.shape[0]
    first = jnp.cumsum(n, axis=1) - n
    per_seg = n.T.reshape(-1)
    ends = jnp.cumsum(per_seg)
    total = ends[-1]
    i = jnp.minimum(jnp.arange(n_items, dtype=jnp.int32), total - 1)
    seg = jnp.sum((ends[None, :] <= i[:, None]).astype(jnp.int32), axis=1)
    grp, tile = seg // tiles, seg % tiles
    block = first[tile, grp] + i - (ends[seg] - per_seg[seg])
    return tile, block.astype(jnp.int32), grp, total[None].astype(jnp.int32)


def _moe_expert_kernel(tile_ref, block_ref, grp_ref, total_ref, xs_ref, wr_ref, br_ref, wg_ref, wu_ref,
                       wd_ref, ys_ref):
    i = pl.program_id(0)

    @pl.when(i < total_ref[0])
    def _():
        g = grp_ref[i]
        x = xs_ref[...]
        logits = _dot(x, wr_ref[...]) + br_ref[...]
        lane_i = lax.broadcasted_iota(jnp.int32, logits.shape, 1)
        lane = lane_i.astype(F32)
        is_grp = jnp.abs(lane - (EXPERTS_PER_GROUP + 0.5 * (N_GROUPS - 1))) < 0.5 * N_GROUPS
        lg = jnp.where(is_grp, logits, NEG_INF)
        gmax = jnp.max(lg, axis=-1, keepdims=True)
        denom = jnp.sum(jnp.exp(lg - gmax), axis=-1, keepdims=True)
        lg_own = jnp.sum(jnp.where(lane_i == EXPERTS_PER_GROUP + g, logits, 0.0), axis=-1, keepdims=True)
        p_grp = jnp.exp(lg_own - gmax) / denom
        le = jnp.where(lane < EXPERTS_PER_GROUP, logits, NEG_INF)
        v1, i1 = _first_lane_of_max(le, lane)
        le2 = jnp.where(lane == i1, NEG_INF, le)
        v2, i2 = _first_lane_of_max(le2, lane)
        e2 = jnp.exp(v2 - v1)
        w1 = p_grp / (1.0 + e2)
        comb = jnp.where(lane == i1, w1, 0.0) + jnp.where(lane == i2, w1 * e2, 0.0)
        hid = []
        for e in range(EXPERTS_PER_GROUP):
            a = _dot(x, wg_ref[e])
            u = _dot(x, wu_ref[e])
            s, _ = _sig_pair(a)
            hid.append((a * s * u * comb[:, e:e + 1]).astype(BF16))
        hid = jnp.concatenate(hid, axis=-1)
        wd = wd_ref[...].reshape(EXPERTS_PER_GROUP * D_FF_EXPERT, wd_ref.shape[-1])
        ys_ref[...] = _dot(hid, wd).astype(BF16)


def _moe_experts(xs, work, wrg, brg, wg, wu, wd):
    tiles, _, d = xs.shape
    tile, block, grp, total = work
    item = lambda i, t, k, g, n: (t[i], k[i], 0)
    by_group = lambda i, t, k, g, n: (g[i], 0, 0)
    grid_spec = pltpu.PrefetchScalarGridSpec(
        num_scalar_prefetch=4,
        grid=(tile.shape[0],),
        in_specs=[pl.BlockSpec((None, MOE_SUB, d), item),
                  pl.BlockSpec((None, d, LANES), by_group),
                  pl.BlockSpec((None, 1, LANES), by_group),
                  pl.BlockSpec((EXPERTS_PER_GROUP, d, D_FF_EXPERT), by_group),
                  pl.BlockSpec((EXPERTS_PER_GROUP, d, D_FF_EXPERT), by_group),
                  pl.BlockSpec((EXPERTS_PER_GROUP, D_FF_EXPERT, d), by_group)],
        out_specs=pl.BlockSpec((None, MOE_SUB, d), item))
    return pl.pallas_call(
        _moe_expert_kernel,
        grid_spec=grid_spec,
        out_shape=jax.ShapeDtypeStruct(xs.shape, BF16),
        input_output_aliases={4: 0},
        compiler_params=_params(("arbitrary",)),
        name="moe_experts",
    )(tile, block, grp, total, xs, wrg, brg, wg, wu, wd)


def _moe_combine_kernel(ys_ref, slot_ref, x1_ref, gfin_ref, y_ref, *, tm):
    tiles = [slice(u * tm, (u + 1) * tm) for u in range(x1_ref.shape[0] // tm)]
    col = lax.broadcasted_iota(jnp.int32, (tm, MOE_ROWS), 1).astype(F32)
    places = []
    for rows in tiles:
        slot = jnp.concatenate([slot_ref[rows, :]] * (MOE_ROWS // LANES), axis=-1)
        places.append(jnp.where(col == slot, 1.0, 0.0).astype(BF16))
    for u, rows in enumerate(tiles):
        y_ref[rows, :] = _rms(x1_ref[rows, :] + _dot(places[u], ys_ref[u]), gfin_ref[...])


def _moe_combine(ys, slot, x1, gfin, tm):
    b, l, d = x1.shape
    nt = l // tm
    per_step = 2 if nt % 2 == 0 else 1
    ts, ns = tm * per_step, nt // per_step
    tok = lambda w: pl.BlockSpec((None, ts, w), lambda bi, i: (bi, i, 0))
    per_tile = lambda r, w: pl.BlockSpec((per_step, r, w), lambda bi, i: (bi * ns + i, 0, 0))
    return pl.pallas_call(
        functools.partial(_moe_combine_kernel, tm=tm),
        grid=(b, ns),
        in_specs=[per_tile(MOE_ROWS, d), tok(LANES), tok(d), _const_spec(gfin.shape)],
        out_specs=tok(d),
        out_shape=jax.ShapeDtypeStruct((b, l, d), F32),
        compiler_params=_params(("parallel", "parallel")),
        name="moe_combine",
    )(ys, slot, x1, gfin)


def _rope_tables(pos):
    inv = jnp.power(ROPE_THETA, -jnp.arange(HALF, dtype=F32) / HALF)
    ang = pos.astype(F32)[:, None] * inv[None, :]
    cos, sin = jnp.cos(ang), jnp.sin(ang)
    zeros = jnp.zeros_like(sin)
    reps = LANES // ROPE_DIM
    cosp = jnp.tile(jnp.concatenate([cos, cos], axis=1), (1, reps))
    sinlo = jnp.tile(jnp.concatenate([-sin, zeros], axis=1), (1, reps))
    sinhi = jnp.tile(jnp.concatenate([zeros, sin], axis=1), (1, reps))
    return cosp, sinlo, sinhi, cos.T * (SCALE * LOG2E), sin.T * (SCALE * LOG2E)


def _prep_weights(g_mix, w_in, g_q, w_uq, g_kv, w_ukv, lb_hgrn, g_onorm, w_a_out, w_b_out, w_o,
                  g_ffn, w_rg, b_rg, w_re, b_re, w_gate, w_up, w_down, g_final):
    assert w_in.shape[0] == 1, "single-layer step"
    d = w_in.shape[1]
    w = w_in[0]
    n_a = Q_LORA + KV_LORA
    win_b = jnp.swapaxes(w, 0, 1).astype(BF16)
    win_a = jnp.concatenate([win_b[:n_a]] + [win_b[n_a:n_a + ROPE_DIM]] * (LANES // ROPE_DIM), axis=0)
    per_q = NOPE_DIM + ROPE_DIM
    wuq = jnp.pad(w_uq[0].reshape(Q_LORA, A_HEADS, per_q), ((0, 0), (0, 0), (0, HEAD_PAD - per_q)))
    wuq = wuq.reshape(Q_LORA, A_HEADS * HEAD_PAD).astype(BF16)
    wukv = w_ukv[0].reshape(KV_LORA, A_HEADS, NOPE_DIM + V_DIM)
    w_uk, w_uv = wukv[..., :NOPE_DIM], wukv[..., NOPE_DIM:]
    wuk = jnp.pad(w_uk, ((0, 0), (0, 0), (0, HEAD_PAD - NOPE_DIM))).reshape(KV_LORA, A_HEADS * HEAD_PAD).astype(BF16)
    wuvt = w_uv.reshape(KV_LORA, A_HEADS * V_DIM).T.astype(BF16)
    wq2l = jnp.pad(jnp.transpose(w_uk, (1, 2, 0)), ((0, 0), (0, HEAD_PAD - NOPE_DIM), (0, 0))).astype(BF16)
    esel = jnp.zeros((HEAD_PAD, ROPE_DIM), F32).at[ROPE_LO + jnp.arange(ROPE_DIM), jnp.arange(ROPE_DIM)].set(1.0).astype(BF16)
    wuvp = w_uv.reshape(KV_LORA, A_HEADS * V_DIM).astype(BF16)
    lb = jnp.cumsum(jax.nn.softmax(lb_hgrn.astype(F32), axis=0), axis=0)[0][None, :]
    gon = jnp.tile(g_onorm[0], 2)[None, :]
    wr = jnp.pad(w_rg[0], ((0, 0), (0, LANES - N_GROUPS))).astype(BF16)
    br = jnp.pad(b_rg[0], (0, LANES - N_GROUPS))[None, :]
    rest = LANES - EXPERTS_PER_GROUP - N_GROUPS
    wrg = jnp.concatenate([w_re[0], jnp.broadcast_to(w_rg[0], (N_GROUPS, d, N_GROUPS)),
                           jnp.zeros((N_GROUPS, d, rest), F32)], axis=2).astype(BF16)
    brg = jnp.concatenate([b_re[0], jnp.broadcast_to(b_rg[0], (N_GROUPS, N_GROUPS)),
                           jnp.zeros((N_GROUPS, rest), F32)], axis=1)[:, None, :]
    return dict(
        wrg=wrg, brg=brg,
        gmix=g_mix[0][None, :], win_a=win_a, win_b=win_b, gq=g_q[0][None, :], wuq=wuq, wuqt=wuq.T,
        gkv=g_kv[0][None, :], wuk=wuk, wuvt=wuvt, wq2l=wq2l, esel=esel, wuvp=wuvp, lb=lb, gon=gon,
        wa=w_a_out[0].astype(BF16), wb=w_b_out[0].astype(BF16), wo=w_o[0].astype(BF16),
        gffn=g_ffn[0][None, :], wr=wr, br=br, wg=w_gate[0].astype(BF16), wu=w_up[0].astype(BF16),
        wd=w_down[0].astype(BF16), gfin=g_final[None, :])


def _tile(n, want):
    t = min(n, want)
    while n % t:
        t //= 2
    return t


TOKEN_TILE = 512
HGRN_PROJ_TILE = 1024
ATTN_TILE = 512
ATTN_CHAIN = 256
ATTN_HEADS_PER_STEP = 8
HGRN_STEP = 1024
HGRN_STREAMS_PER_STEP = 8


def _run_group(x, pos, streams, past, w):
    b, l, d = x.shape
    ns, ls = streams
    tm = _tile(l, TOKEN_TILE)
    h, ckv, kpe, q, k, vt = _mla_proj(x, w["gmix"], w["win_a"], w["gq"], w["wuq"], w["wuqt"], w["gkv"],
                                      w["wuk"], w["wuvt"], _rope_tables(pos), tm, past is None)
    qb, kb, lf, vb, og, sa, sb = _hgrn_proj(h, w["win_b"], w["lb"], _tile(l, HGRN_PROJ_TILE))
    as_streams = lambda a: a.reshape(ns, ls, a.shape[-1])
    if past is None:
        oa = _attn(q, k, vt, _tile(l, ATTN_TILE), ATTN_CHAIN, ATTN_HEADS_PER_STEP)
        s0t = jnp.zeros((ns, B_HEADS, B_DV, B_DK), F32)
    else:
        cache_ckv, cache_kpe, state = past
        oa = _sample_attn(as_streams(q), cache_ckv, cache_kpe, as_streams(ckv), as_streams(kpe),
                          w["wq2l"], w["esel"], w["wuvp"]).reshape(b, l, A_HEADS * V_DIM)
        s0t = jnp.swapaxes(state.astype(F32), -1, -2)
    ob, snt = _hgrn(qb, kb, lf, vb, og, s0t, w["gon"], ns,
                    _tile(l, HGRN_STEP if past is None else HGRN_STREAMS_PER_STEP * ls))
    x1, xs, slot, meta = _out_proj(oa, ob, sa, sb, x, w["wa"], w["wb"], w["wo"], w["gffn"], w["wr"],
                                   w["br"], tm, past is None)
    work = _work_list(meta, meta.shape[0] * _max_items(tm))
    ys = _moe_experts(xs, work, w["wrg"], w["brg"], w["wg"], w["wu"], w["wd"])
    y = _moe_combine(ys, slot, x1, w["gfin"], tm)
    return y, ckv, kpe, jnp.swapaxes(snt, -1, -2)


def kernel(x_prompt, x_sample, cache_ckv, cache_kpe, state_hgrn, g_mix, w_in, g_q, w_uq, g_kv, w_ukv,
           lb_hgrn, g_onorm, w_a_out, w_b_out, w_o, g_ffn, w_rg, b_rg, w_re, b_re, w_gate, w_up, w_down,
           g_final):
    w = _prep_weights(g_mix, w_in, g_q, w_uq, g_kv, w_ukv, lb_hgrn, g_onorm, w_a_out, w_b_out, w_o,
                      g_ffn, w_rg, b_rg, w_re, b_re, w_gate, w_up, w_down, g_final)
    bp, lp, d = x_prompt.shape
    y_p, ckv_p, kpe_p, st_p = _run_group(x_prompt, jnp.arange(lp, dtype=jnp.int32), (bp, lp), None, w)

    bs, ls, _ = x_sample.shape
    past_len = cache_ckv.shape[2]
    pos_s = past_len + (jnp.arange(bs * ls, dtype=jnp.int32) % ls)
    y_s, ckv_s, kpe_s, st_s = _run_group(x_sample.reshape(1, bs * ls, d), pos_s, (bs, ls),
                                         (cache_ckv[0], jnp.swapaxes(cache_kpe[0], 1, 2), state_hgrn[0]), w)
    return (y_p, y_s.reshape(bs, ls, d),
            ckv_p[None], jnp.swapaxes(kpe_p, 1, 2)[None], st_p[None].astype(x_prompt.dtype),
            ckv_s.reshape(1, bs, ls, KV_LORA), kpe_s.reshape(1, bs, ls, ROPE_DIM),
            st_s[None].astype(state_hgrn.dtype))
```

```python
import functools

import numpy as np
import jax
import jax.numpy as jnp
from jax import lax
from jax.experimental import pallas as pl
from jax.experimental.pallas import tpu as pltpu

F32 = jnp.float32
BF16 = jnp.bfloat16

EPS = 1e-6
CHUNK = 64
A_HEADS = 8
Q_LORA = 384
KV_LORA = 256
NOPE_DIM = 64
ROPE_DIM = 32
V_DIM = 64
ROPE_THETA = 10000.0
B_HEADS = 8
B_DK = 128
B_DV = 64
N_GROUPS = 4
EXPERTS_PER_GROUP = 8
N_EXPERTS = N_GROUPS * EXPERTS_PER_GROUP
D_FF_EXPERT = 256

LANES = 128
MXU_TILE = 256
VMEM_BYTES_V7X = 64 * 1024 * 1024
VMEM_LIMIT = VMEM_BYTES_V7X - 8 * 1024 * 1024

HEAD_PAD = LANES
V_AUG = V_DIM + 16
SAMPLE_KEY_BLOCK = 1024
ROPE_LO = NOPE_DIM
ROPE_HI = NOPE_DIM + ROPE_DIM
HALF = ROPE_DIM // 2
SCALE = (NOPE_DIM + ROPE_DIM) ** -0.5
LOG2E = 1.4426950408889634
NEG_INF = float("-inf")


def _params(sem):
    return pltpu.CompilerParams(dimension_semantics=sem, vmem_limit_bytes=VMEM_LIMIT)


def _const_spec(shape):
    nd = len(shape)
    return pl.BlockSpec(shape, lambda *_: (0,) * nd, pipeline_mode=pl.Buffered(1))


def _rms(x, g):
    ms = jnp.mean(x * x, axis=-1, keepdims=True)
    return x * lax.rsqrt(ms + EPS) * g


def _sig_pair(x):
    e = jnp.exp(-jnp.abs(x))
    r = 1.0 / (1.0 + e)
    er = e * r
    pos = x >= 0
    return jnp.where(pos, r, er), jnp.where(pos, er, r)


def _nt(a, b):
    return lax.dot_general(a, b, (((1,), (1,)), ((), ())), preferred_element_type=F32)


def _tn(a, b):
    return lax.dot_general(a, b, (((0,), (0,)), ((), ())), preferred_element_type=F32)


def _dot(a, b):
    return jnp.dot(a, b, preferred_element_type=F32)


def _rope(x, cosp, sinlo, sinhi):
    return x * cosp + pltpu.roll(x, LANES - HALF, 1) * sinlo + pltpu.roll(x, HALF, 1) * sinhi


def _mla_proj_kernel(x_ref, gmix_ref, win_ref, gq_ref, wuq_ref, wuqt_ref, gkv_ref, wuk_ref, wuvt_ref,
                     cos_ref, sinlo_ref, sinhi_ref, cost_ref, sint_ref,
                     h_ref, ckv_ref, kpe_ref, q_ref, k_ref, vt_ref, *, q_transposed, tm):
    tiles = [slice(u * tm, (u + 1) * tm) for u in range(x_ref.shape[0] // tm)]
    lane = lax.broadcasted_iota(jnp.int32, (1, LANES), 1)
    rope_lanes = (lane // ROPE_DIM) == (ROPE_LO // ROPE_DIM)

    hs = []
    for rows in tiles:
        h = _rms(x_ref[rows, :], gmix_ref[...]).astype(BF16)
        h_ref[rows, :] = h
        hs.append(h)
    zs = [_nt(h, win_ref[...]) for h in hs]
    cqns = [_rms(z[:, :Q_LORA], gq_ref[...]).astype(BF16) for z in zs]
    ckvs = [_rms(z[:, Q_LORA:Q_LORA + KV_LORA], gkv_ref[...]) for z in zs]

    for rows, cqn in zip(tiles, cqns):
        if q_transposed:
            qt = _nt(wuqt_ref[...], cqn)
            cos_t, sin_t = cost_ref[:, rows], sint_ref[:, rows]
            for hd in range(A_HEADS):
                r0 = hd * HEAD_PAD
                q_ref[r0:r0 + ROPE_LO, rows] = (qt[r0:r0 + ROPE_LO] * (SCALE * LOG2E)).astype(BF16)
                a = qt[r0 + ROPE_LO:r0 + ROPE_LO + HALF]
                b = qt[r0 + ROPE_LO + HALF:r0 + ROPE_HI]
                q_ref[r0 + ROPE_LO:r0 + ROPE_HI, rows] = jnp.concatenate(
                    [a * cos_t - b * sin_t, a * sin_t + b * cos_t], axis=0).astype(BF16)
                q_ref[r0 + ROPE_HI:r0 + HEAD_PAD, rows] = jnp.zeros((HEAD_PAD - ROPE_HI, tm), BF16)
        else:
            q = _dot(cqn, wuq_ref[...])
            cosp, sinlo, sinhi = cos_ref[rows, :], sinlo_ref[rows, :], sinhi_ref[rows, :]
            for hd in range(A_HEADS):
                sl = slice(hd * HEAD_PAD, (hd + 1) * HEAD_PAD)
                qh = q[:, sl]
                qh = jnp.where(rope_lanes, _rope(qh, cosp, sinlo, sinhi), qh) * (SCALE * LOG2E)
                q_ref[rows, sl] = qh.astype(BF16)

    for rows, z, ckv in zip(tiles, zs, ckvs):
        ckv_ref[rows, :] = ckv
        ckv_bf = ckv.astype(BF16)
        cosp, sinlo, sinhi = cos_ref[rows, :], sinlo_ref[rows, :], sinhi_ref[rows, :]
        kpe_rot = _rope(z[:, Q_LORA + KV_LORA:], cosp, sinlo, sinhi)
        if q_transposed:
            kpe_ref[:, rows] = kpe_rot.T[:ROPE_DIM]
        else:
            kpe_ref[rows, :] = kpe_rot[:, :ROPE_DIM]
        kpe_placed = jnp.where(rope_lanes, kpe_rot, 0.0)
        kn = _dot(ckv_bf, wuk_ref[...])
        for hd in range(A_HEADS):
            sl = slice(hd * HEAD_PAD, (hd + 1) * HEAD_PAD)
            k_ref[rows, sl] = (kn[:, sl] + kpe_placed).astype(BF16)
        vt = _nt(wuvt_ref[...], ckv_bf).astype(BF16)
        ones = jnp.ones((V_AUG - V_DIM, tm), BF16)
        for hd in range(A_HEADS):
            vt_ref[hd * V_AUG:hd * V_AUG + V_DIM, rows] = vt[hd * V_DIM:(hd + 1) * V_DIM]
            vt_ref[hd * V_AUG + V_DIM:(hd + 1) * V_AUG, rows] = ones


def _mla_proj(x, gmix, win_a, gq, wuq, wuqt, gkv, wuk, wuvt, tables, tm, q_transposed):
    b, l, d = x.shape
    hq = A_HEADS * HEAD_PAD
    hv = A_HEADS * V_AUG
    cosp, sinlo, sinhi, cos_t, sin_t = tables
    per_step = 2 if (l // tm) % 2 == 0 else 1
    ts = tm * per_step
    tok = lambda w: pl.BlockSpec((None, ts, w), lambda bi, i: (bi, i, 0))
    tok_t = lambda w: pl.BlockSpec((None, w, ts), lambda bi, i: (bi, 0, i))
    tab = pl.BlockSpec((ts, LANES), lambda bi, i: (i, 0))
    tab_t = pl.BlockSpec((HALF, ts), lambda bi, i: (0, i))
    return pl.pallas_call(
        functools.partial(_mla_proj_kernel, q_transposed=q_transposed, tm=tm),
        grid=(b, l // ts),
        in_specs=[tok(d), _const_spec(gmix.shape), _const_spec(win_a.shape), _const_spec(gq.shape),
                  _const_spec(wuq.shape), _const_spec(wuqt.shape), _const_spec(gkv.shape),
                  _const_spec(wuk.shape), _const_spec(wuvt.shape), tab, tab, tab, tab_t, tab_t],
        out_specs=[tok(d), tok(KV_LORA), tok_t(ROPE_DIM) if q_transposed else tok(ROPE_DIM),
                   tok_t(hq) if q_transposed else tok(hq), tok(hq),
                   tok_t(hv)],
        out_shape=[jax.ShapeDtypeStruct((b, l, d), BF16),
                   jax.ShapeDtypeStruct((b, l, KV_LORA), F32),
                   jax.ShapeDtypeStruct((b, ROPE_DIM, l) if q_transposed else (b, l, ROPE_DIM), F32),
                   jax.ShapeDtypeStruct((b, hq, l) if q_transposed else (b, l, hq), BF16),
                   jax.ShapeDtypeStruct((b, l, hq), BF16),
                   jax.ShapeDtypeStruct((b, hv, l), BF16)],
        compiler_params=_params(("parallel", "parallel")),
        name="mla_proj",
    )(x, gmix, win_a, gq, wuq, wuqt, gkv, wuk, wuvt, cosp, sinlo, sinhi, cos_t, sin_t)


def _hgrn_proj_kernel(h_ref, wt_ref, lb_ref, qb_ref, kb_ref, lf_ref, vb_ref, og_ref, sa_ref, sb_ref, *, row0):
    h = h_ref[...]
    dk = B_HEADS * B_DK
    dv = B_HEADS * B_DV
    dm = sa_ref.shape[-1]
    nb = MXU_TILE
    lb_all = lb_ref[...]

    def z(base, i):
        return _nt(h, wt_ref[row0 + base + i * nb:row0 + base + (i + 1) * nb, :])

    for i in range(dk // nb):
        cols = slice(i * nb, (i + 1) * nb)
        zq = z(0, i)
        qb_ref[:, cols] = zq * jax.nn.sigmoid(zq)
        sf, snf = _sig_pair(z(dk, i))
        lb = lb_all[:, cols]
        lf_ref[:, cols] = jnp.log(lb + (1.0 - lb) * sf)
        kb_ref[:, cols] = (1.0 - lb) * snf
    for i in range(dv // nb):
        cols = slice(i * nb, (i + 1) * nb)
        vb_ref[:, cols] = z(2 * dk, i).astype(BF16)
        zg = z(2 * dk + dv, i)
        og_ref[:, cols] = (zg * jax.nn.sigmoid(zg)).astype(BF16)
    for i in range(dm // nb):
        cols = slice(i * nb, (i + 1) * nb)
        sa_ref[:, cols] = jax.nn.sigmoid(z(2 * dk + 2 * dv, i)).astype(BF16)
        sb_ref[:, cols] = jax.nn.sigmoid(z(2 * dk + 2 * dv + dm, i)).astype(BF16)


def _hgrn_proj(h, win_b, lb, tm):
    b, l, d = h.shape
    dk = B_HEADS * B_DK
    dv = B_HEADS * B_DV
    tok = lambda w: pl.BlockSpec((None, tm, w), lambda bi, i: (bi, i, 0))
    sds = lambda w, dt: jax.ShapeDtypeStruct((b, l, w), dt)
    return pl.pallas_call(
        functools.partial(_hgrn_proj_kernel, row0=Q_LORA + KV_LORA + ROPE_DIM),
        grid=(b, l // tm),
        in_specs=[tok(d), _const_spec(win_b.shape), _const_spec(lb.shape)],
        out_specs=[tok(dk), tok(dk), tok(dk), tok(dv), tok(dv), tok(d), tok(d)],
        out_shape=[sds(dk, F32), sds(dk, F32), sds(dk, F32), sds(dv, BF16), sds(dv, BF16),
                   sds(d, BF16), sds(d, BF16)],
        compiler_params=_params(("parallel", "parallel")),
        name="hgrn_proj",
    )(h, win_b, lb)


def _attn_kernel(qt_ref, k_ref, vt_ref, o_ref, s_sc, m_sc, acc_sc, *, t, tc, hp):
    qi = pl.program_id(2)
    n_chains = t // tc
    m_sc[...] = jnp.full(m_sc.shape, NEG_INF, F32)
    acc_sc[...] = jnp.zeros(acc_sc.shape, F32)
    qk_rows = [slice(hd * HEAD_PAD, (hd + 1) * HEAD_PAD) for hd in range(hp)]
    v_rows = [slice(hd * V_AUG, (hd + 1) * V_AUG) for hd in range(hp)]

    def scores(kb, slot):
        start = pl.multiple_of(kb * t, t)
        for hd in range(hp):
            k = k_ref[pl.ds(start, t), qk_rows[hd]]
            for ch in range(n_chains):
                cols = slice(ch * tc, (ch + 1) * tc)
                s_sc[hd, slot, :, cols] = _dot(k, qt_ref[qk_rows[hd], cols])

    def consume(kb, slot, masked):
        start = pl.multiple_of(kb * t, t)
        for hd in range(hp):
            for ch in range(n_chains):
                cols = slice(ch * tc, (ch + 1) * tc)
                nk = (ch + 1) * tc if masked else t
                vt = vt_ref[v_rows[hd], pl.ds(start, nk)]
                s = s_sc[hd, slot, 0:nk, cols]
                if masked:
                    r = lax.broadcasted_iota(jnp.int32, (nk, tc), 0) // CHUNK
                    c = (lax.broadcasted_iota(jnp.int32, (nk, tc), 1) + ch * tc) // CHUNK
                    s = jnp.where(r <= c, s, NEG_INF)
                m_prev = m_sc[hd, :, cols]
                m_new = jnp.maximum(m_prev, jnp.max(s, axis=0, keepdims=True))
                alpha = jnp.exp2(m_prev - m_new)
                p = jnp.exp2(s - m_new)
                acc_sc[hd, :, cols] = alpha * acc_sc[hd, :, cols] + _dot(vt, p.astype(BF16))
                m_sc[hd, :, cols] = m_new

    scores(0, 0)

    def body(j, carry):
        kb = 2 * j
        scores(kb + 1, 1)
        consume(kb, 0, False)
        scores(kb + 2, 0)
        consume(kb + 1, 1, False)
        return carry

    lax.fori_loop(0, qi // 2, body, 0)

    @pl.when(qi % 2 == 0)
    def _():
        consume(qi, 0, True)

    @pl.when(qi % 2 == 1)
    def _():
        scores(qi, 1)
        consume(qi - 1, 0, False)
        consume(qi, 1, True)

    for hd in range(hp):
        acc = acc_sc[hd]
        o_ref[hd * V_DIM:(hd + 1) * V_DIM, :] = (acc[:V_DIM] / acc[V_DIM:V_DIM + 1]).astype(BF16)


def _attn(qt, k, vt, t, tc, hp):
    b, l, _ = k.shape
    once = dict(pipeline_mode=pl.Buffered(1)) if hp == A_HEADS else {}
    return pl.pallas_call(
        functools.partial(_attn_kernel, t=t, tc=min(tc, t), hp=hp),
        grid=(b, A_HEADS // hp, l // t),
        in_specs=[pl.BlockSpec((None, hp * HEAD_PAD, t), lambda bi, h, i: (bi, h, i)),
                  pl.BlockSpec((None, l, hp * HEAD_PAD), lambda bi, h, i: (bi, 0, h), **once),
                  pl.BlockSpec((None, hp * V_AUG, l), lambda bi, h, i: (bi, h, 0), **once)],
        out_specs=pl.BlockSpec((None, hp * V_DIM, t), lambda bi, h, i: (bi, h, i)),
        out_shape=jax.ShapeDtypeStruct((b, A_HEADS * V_DIM, l), BF16),
        scratch_shapes=[pltpu.VMEM((hp, 2, t, t), F32), pltpu.VMEM((hp, 1, t), F32),
                        pltpu.VMEM((hp, V_AUG, t), F32)],
        compiler_params=_params(("parallel", "parallel", "arbitrary")),
        name="attn",
    )(qt, k, vt)


def _sample_attn_kernel(q_ref, cc_ref, ckt_ref, nc_ref, nk_ref, wq2l_ref, esel_ref, wuv_ref, o_ref,
                        *, past, n_new):
    qs = q_ref[...]
    heads = [qs[:, hd * HEAD_PAD:(hd + 1) * HEAD_PAD] for hd in range(A_HEADS)]
    ql = jnp.concatenate([_dot(heads[hd], wq2l_ref[hd]) for hd in range(A_HEADS)], axis=0).astype(BF16)
    qp = jnp.concatenate([_dot(heads[hd], esel_ref[...]) for hd in range(A_HEADS)], axis=0).astype(BF16)
    nc = nc_ref[...].astype(BF16)
    nk = nk_ref[...].astype(BF16)
    rows = A_HEADS * n_new

    s_n = _nt(ql, nc) + _nt(qp, nk)
    tq = lax.broadcasted_iota(jnp.int32, (rows, n_new), 0) % n_new
    tk = lax.broadcasted_iota(jnp.int32, (rows, n_new), 1)
    s_n = jnp.where((past + tk) // CHUNK <= (past + tq) // CHUNK, s_n, NEG_INF)
    m = jnp.max(s_n, axis=-1, keepdims=True)
    p_n = jnp.exp2(s_n - m)
    denom = jnp.sum(p_n, axis=-1, keepdims=True)
    acc = _dot(p_n.astype(BF16), nc)

    n_blocks = max(1, past // SAMPLE_KEY_BLOCK)
    assert past % n_blocks == 0
    kb = past // n_blocks

    def scores(i):
        cc = cc_ref[i * kb:(i + 1) * kb, :].astype(BF16)
        ckt = ckt_ref[:, i * kb:(i + 1) * kb].astype(BF16)
        return _nt(ql, cc) + _dot(qp, ckt), cc

    nxt = scores(0)
    for i in range(n_blocks):
        s_c, cc = nxt
        if i + 1 < n_blocks:
            nxt = scores(i + 1)
        m_new = jnp.maximum(m, jnp.max(s_c, axis=-1, keepdims=True))
        alpha = jnp.exp2(m - m_new)
        p_c = jnp.exp2(s_c - m_new)
        denom = alpha * denom + jnp.sum(p_c, axis=-1, keepdims=True)
        acc = alpha * acc + _dot(p_c.astype(BF16), cc)
        m = m_new
    o_lat = (acc / denom).astype(BF16)
    full = _dot(o_lat, wuv_ref[...])
    col_head = lax.broadcasted_iota(jnp.int32, (n_new, A_HEADS * V_DIM), 1) // V_DIM
    out = jnp.zeros((n_new, A_HEADS * V_DIM), F32)
    for hd in range(A_HEADS):
        out = out + jnp.where(col_head == hd, full[hd * n_new:(hd + 1) * n_new], 0.0)
    o_ref[...] = out.astype(BF16)


def _sample_attn(q, cache_ckv, cache_kpe_t, ckv_new, kpe_new, wq2l, esel, wuvp):
    nb, n_new, hq = q.shape
    past = cache_ckv.shape[1]
    hv = A_HEADS * V_DIM
    per = lambda r, w: pl.BlockSpec((None, r, w), lambda bi: (bi, 0, 0))
    return pl.pallas_call(
        functools.partial(_sample_attn_kernel, past=past, n_new=n_new),
        grid=(nb,),
        in_specs=[per(n_new, hq), per(past, KV_LORA), per(ROPE_DIM, past), per(n_new, KV_LORA),
                  per(n_new, ROPE_DIM), _const_spec(wq2l.shape), _const_spec(esel.shape),
                  _const_spec(wuvp.shape)],
        out_specs=per(n_new, hv),
        out_shape=jax.ShapeDtypeStruct((nb, n_new, hv), BF16),
        compiler_params=_params(("parallel",)),
        name="sample_attn",
    )(q, cache_ckv, cache_kpe_t, ckv_new, kpe_new, wq2l, esel, wuvp)


def _level_sizes(c):
    sizes = []
    p = c
    while p >= 2:
        sizes.append(p)
        p //= 2
    return sizes


def _level_masks(c):
    t = np.arange(c)[:, None]
    s = np.arange(c)[None, :]
    out = []
    for p in _level_sizes(c):
        out.append((t // p == s // p) & (t % p >= p // 2) & (s % p < p // 2))
    out.append(t == s)
    return np.stack(out).astype(np.float32)


def _boundary(cum, p, c):
    half = p // 2
    if p >= 8:
        parts = [jnp.broadcast_to(cum[i * p + half - 1:i * p + half, :], (p, cum.shape[1]))
                 for i in range(c // p)]
        return jnp.concatenate(parts, axis=0) if len(parts) > 1 else parts[0]
    sub = lax.broadcasted_iota(jnp.int32, (8, cum.shape[1]), 0)
    parts = []
    for g in range(c // 8):
        lo = jnp.broadcast_to(cum[g * 8 + 1:g * 8 + 2, :], (8, cum.shape[1]))
        hi = jnp.broadcast_to(cum[g * 8 + 5:g * 8 + 6, :], (8, cum.shape[1]))
        parts.append(jnp.where(sub < 4, lo, hi))
    return jnp.concatenate(parts, axis=0)


def _hgrn_kernel(q_ref, k_ref, lf_ref, v_ref, og_ref, s0t_ref, gon_ref, mask_ref, tri_ref,
                 o_ref, snt_ref, st_sc, *, c, n_chunks, per_chunk_state):
    step = pl.program_id(2)
    sizes = _level_sizes(c)
    lane = lax.broadcasted_iota(jnp.int32, (1, LANES), 1)
    low = lane < B_DV
    keep_lo = jnp.where(low, 1.0, 0.0).astype(BF16)
    keep_hi = jnp.where(low, 0.0, 1.0).astype(BF16)

    if not per_chunk_state:
        @pl.when(step == 0)
        def _():
            st_sc[...] = jnp.zeros(st_sc.shape, F32)
            st_sc[0, 0:B_DV, :] = s0t_ref[0]
            st_sc[1, B_DV:2 * B_DV, :] = s0t_ref[1]

    tri = tri_ref[...]
    rows = [slice(ci * c, (ci + 1) * c) for ci in range(n_chunks)]
    heads = [slice(j * B_DK, (j + 1) * B_DK) for j in range(2)]
    items = [(ci, j) for ci in range(n_chunks) for j in range(2)]


    cums = []
    for ci in range(n_chunks):
        lf = lf_ref[rows[ci], :]
        hi = lf.astype(BF16)
        r1 = lf - hi.astype(F32)
        mid = r1.astype(BF16)
        lo = (r1 - mid.astype(F32)).astype(BF16)
        cums.append((_dot(tri, hi) + _dot(tri, mid) + _dot(tri, lo)) * LOG2E)

    def zeros(n):
        return jnp.zeros((n, B_DK), F32)

    a_mats = {}
    for ci, j in items:
        q = q_ref[rows[ci], heads[j]]
        k = k_ref[rows[ci], heads[j]]
        cum = cums[ci][:, heads[j]]
        kbf = k.astype(BF16)
        a = _nt(q.astype(BF16), kbf) * mask_ref[len(sizes)]
        q_cols, k_cols = [], []
        for li, p in enumerate(sizes):
            half = p // 2
            if half % 8 == 0:
                for lo in range(0, c, p):
                    mid, hi = lo + half, lo + p
                    bnd = cum[mid - 1:mid, :]
                    qu = q[mid:hi] * jnp.exp2(cum[mid:hi] - bnd)
                    kl = k[lo:mid] * jnp.exp2(bnd - cum[lo:mid])
                    q_cols.append(jnp.concatenate([x for x in (zeros(mid), qu, zeros(c - hi)) if x.shape[0]], axis=0))
                    k_cols.append(jnp.concatenate([x for x in (zeros(lo), kl, zeros(c - mid)) if x.shape[0]], axis=0))
                continue
            if p == 2:
                qt = (q * jnp.exp2(lf_ref[rows[ci], heads[j]] * LOG2E)).astype(BF16)
                kt = kbf
            else:
                bnd = _boundary(cum, p, c)
                qt = (q * jnp.exp2(jnp.minimum(cum - bnd, 0.0))).astype(BF16)
                kt = (k * jnp.exp2(jnp.minimum(bnd - cum, 0.0))).astype(BF16)
            a = a + _nt(qt, kt) * mask_ref[li]
        if q_cols:
            a = a + _nt(jnp.concatenate(q_cols, axis=1).astype(BF16), jnp.concatenate(k_cols, axis=1).astype(BF16))
        a_mats[ci, j] = a.astype(BF16)

    qhats, kvs, decays, vs = {}, {}, {}, {}
    for ci, j in items:
        cum = cums[ci][:, heads[j]]
        last = cum[c - 1:c, :]
        vs[ci, j] = v_ref[rows[ci], :] * (keep_lo if j == 0 else keep_hi)
        qhats[ci, j] = (q_ref[rows[ci], heads[j]] * jnp.exp2(cum)).astype(BF16)
        khat = (k_ref[rows[ci], heads[j]] * jnp.exp2(last - cum)).astype(BF16)
        kvs[ci, j] = _tn(vs[ci, j], khat)
        decays[ci, j] = jnp.exp2(last)

    pad = jnp.zeros((B_DV, B_DK), F32)
    if not per_chunk_state:
        st = [st_sc[0], st_sc[1]]
    for ci in range(n_chunks):
        if per_chunk_state:
            st = [jnp.concatenate([s0t_ref[ci, 0], pad], axis=0), jnp.concatenate([pad, s0t_ref[ci, 1]], axis=0)]
        o_pair = jnp.zeros((c, LANES), F32)
        for j in range(2):
            o_pair = o_pair + _dot(a_mats[ci, j], vs[ci, j]) + _nt(qhats[ci, j], st[j].astype(BF16))
            st[j] = st[j] * decays[ci, j] + kvs[ci, j]
        if per_chunk_state:
            snt_ref[ci, 0] = st[0][0:B_DV, :]
            snt_ref[ci, 1] = st[1][B_DV:2 * B_DV, :]
        o2 = o_pair * o_pair
        s_lo = jnp.sum(jnp.where(low, o2, 0.0), axis=-1, keepdims=True)
        s_hi = jnp.sum(jnp.where(low, 0.0, o2), axis=-1, keepdims=True)
        ms = jnp.where(low, s_lo, s_hi) * (1.0 / B_DV)
        ob = o_pair * lax.rsqrt(ms + EPS) * gon_ref[...] * og_ref[rows[ci], :].astype(F32)
        o_ref[rows[ci], :] = ob.astype(BF16)
    if not per_chunk_state:
        st_sc[0] = st[0]
        st_sc[1] = st[1]

        @pl.when(step == pl.num_programs(2) - 1)
        def _():
            snt_ref[0] = st_sc[0, 0:B_DV, :]
            snt_ref[1] = st_sc[1, B_DV:2 * B_DV, :]


def _hgrn(qb, kb, lf, vb, og, s0t, gon, n_streams, tc):
    b, l, dk = qb.shape
    per_chunk_state = n_streams != b
    if per_chunk_state:
        assert b == 1 and l % n_streams == 0 and l // n_streams <= CHUNK
        c = l // n_streams
    else:
        c = min(CHUNK, l)
    assert l % tc == 0 and tc % c == 0
    n_chunks = tc // c
    masks = jnp.asarray(_level_masks(c))
    tri = jnp.asarray(np.tril(np.ones((c, c), np.float32)), dtype=BF16)
    pairs = B_HEADS // 2
    wide = pl.BlockSpec((None, tc, 2 * B_DK), lambda bi, p, i: (bi, i, p))
    narrow = pl.BlockSpec((None, tc, 2 * B_DV), lambda bi, p, i: (bi, i, p))
    if per_chunk_state:
        state = pl.BlockSpec((n_chunks, 2, B_DV, B_DK), lambda bi, p, i: (i, p, 0, 0))
    else:
        state = pl.BlockSpec((None, 2, B_DV, B_DK), lambda bi, p, i: (bi, p, 0, 0))
    return pl.pallas_call(
        functools.partial(_hgrn_kernel, c=c, n_chunks=n_chunks, per_chunk_state=per_chunk_state),
        grid=(b, pairs, l // tc),
        in_specs=[wide, wide, wide, narrow, narrow, state, _const_spec(gon.shape),
                  _const_spec(masks.shape), _const_spec(tri.shape)],
        out_specs=[narrow, state],
        out_shape=[jax.ShapeDtypeStruct((b, l, B_HEADS * B_DV), BF16),
                   jax.ShapeDtypeStruct((n_streams, B_HEADS, B_DV, B_DK), F32)],
        scratch_shapes=[pltpu.VMEM((2, 2 * B_DV, B_DK), F32)],
        compiler_params=_params(("parallel", "parallel", "parallel" if per_chunk_state else "arbitrary")),
        name="hgrn",
    )(qb, kb, lf, vb, og, s0t, gon, masks, tri)


MOE_SUB = 144
MOE_ROWS = 1024
_BIG_LANE = float(1 << 20)


def _max_items(tm):
    return (tm + N_GROUPS * (MOE_SUB - 1)) // MOE_SUB


def _first_lane_of_max(x, lane):
    v = jnp.max(x, axis=-1, keepdims=True)
    return v, jnp.min(jnp.where(x == v, lane, _BIG_LANE), axis=-1, keepdims=True)


def _out_proj_kernel(oa_ref, ob_ref, sa_ref, sb_ref, x_ref, wa_ref, wb_ref, wo_ref, gffn_ref,
                     wr_ref, br_ref, tri_ref, upper_ref, x1_ref, xs_ref, slot_ref, meta_ref,
                     *, oa_transposed, tm):
    tiles = [slice(u * tm, (u + 1) * tm) for u in range(x_ref.shape[0] // tm)]
    nb = MXU_TILE
    h2s = []
    for rows in tiles:
        oa = oa_ref[:, rows] if oa_transposed else oa_ref[rows, :]
        ob = ob_ref[rows, :]
        m = []
        for i in range(x_ref.shape[1] // nb):
            cols = slice(i * nb, (i + 1) * nb)
            ya = _tn(oa, wa_ref[:, cols]) if oa_transposed else _dot(oa, wa_ref[:, cols])
            yb = _dot(ob, wb_ref[:, cols])
            m.append((sa_ref[rows, cols].astype(F32) * ya + sb_ref[rows, cols].astype(F32) * yb).astype(BF16))
        x1 = x_ref[rows, :] + _dot(jnp.concatenate(m, axis=-1), wo_ref[...])
        x1_ref[rows, :] = x1
        h2s.append(_rms(x1, gffn_ref[...]).astype(BF16))

    lane = lax.broadcasted_iota(jnp.int32, (tm, LANES), 1).astype(F32)
    logits = [_dot(h2, wr_ref[...]) + br_ref[...] for h2 in h2s]
    grps = [_first_lane_of_max(jnp.where(lane < N_GROUPS, lg, NEG_INF), lane)[1] for lg in logits]
    onehots = [jnp.where(lane == grp, 1.0, 0.0) for grp in grps]
    ranks = [_dot(tri_ref[...], oh.astype(BF16)) for oh in onehots]
    items = []
    for rank in ranks:
        count = rank[tm - 1:tm, :]
        n = jnp.zeros_like(count)
        for j in range(_max_items(tm)):
            n = n + jnp.where(count > j * MOE_SUB, 1.0, 0.0)
        items.append(n)
    bases = [_dot(jnp.broadcast_to(n, (8, LANES)).astype(BF16), upper_ref[...])[0:1, :] * MOE_SUB for n in items]
    slots = [jnp.sum(oh * (base + rank - 1.0), axis=-1, keepdims=True)
             for oh, base, rank in zip(onehots, bases, ranks)]
    col = lax.broadcasted_iota(jnp.int32, (tm, MOE_ROWS), 1).astype(F32)
    for u, rows in enumerate(tiles):
        place = jnp.where(col == slots[u], 1.0, 0.0).astype(BF16)
        xs_ref[u] = _tn(place, h2s[u]).astype(BF16)
        slot_ref[rows, :] = jnp.broadcast_to(slots[u], (tm, LANES))
        meta_ref[u] = jnp.broadcast_to(items[u], (8, LANES))


def _out_proj(oa, ob, sa, sb, x, wa, wb, wo, gffn, wr, br, tm, oa_transposed):
    b, l, d = x.shape
    nt = l // tm
    per_step = 2 if nt % 2 == 0 else 1
    ts, ns = tm * per_step, nt // per_step
    assert _max_items(tm) * MOE_SUB <= MOE_ROWS
    hv = A_HEADS * V_DIM
    tri = jnp.asarray(np.tril(np.ones((tm, tm), np.float32)), dtype=BF16)
    upper = jnp.asarray(np.triu(np.ones((LANES, LANES), np.float32), 1), dtype=BF16)
    tok = lambda w: pl.BlockSpec((None, ts, w), lambda bi, i: (bi, i, 0))
    per_tile = lambda r, w: pl.BlockSpec((per_step, r, w), lambda bi, i: (bi * ns + i, 0, 0))
    oa_spec = pl.BlockSpec((None, hv, ts), lambda bi, i: (bi, 0, i)) if oa_transposed else tok(hv)
    return pl.pallas_call(
        functools.partial(_out_proj_kernel, oa_transposed=oa_transposed, tm=tm),
        grid=(b, ns),
        in_specs=[oa_spec, tok(B_HEADS * B_DV), tok(d), tok(d), tok(d), _const_spec(wa.shape),
                  _const_spec(wb.shape), _const_spec(wo.shape), _const_spec(gffn.shape),
                  _const_spec(wr.shape), _const_spec(br.shape), _const_spec(tri.shape),
                  _const_spec(upper.shape)],
        out_specs=[tok(d), per_tile(MOE_ROWS, d), tok(LANES), per_tile(8, LANES)],
        out_shape=[jax.ShapeDtypeStruct((b, l, d), F32),
                   jax.ShapeDtypeStruct((b * nt, MOE_ROWS, d), BF16),
                   jax.ShapeDtypeStruct((b, l, LANES), F32),
                   jax.ShapeDtypeStruct((b * nt, 8, LANES), F32)],
        compiler_params=_params(("parallel", "parallel")),
        name="out_proj",
    )(oa, ob, sa, sb, x, wa, wb, wo, gffn, wr, br, tri, upper)


def _work_list(meta, n_items):
    n = meta[:, 0, :N_GROUPS].astype(jnp.int32)
    tiles = n.shape[0]
    first = jnp.cumsum(n, axis=1) - n
    per_seg = n.T.reshape(-1)
    ends = jnp.cumsum(per_seg)
    total = ends[-1]
    i = jnp.minimum(jnp.arange(n_items, dtype=jnp.int32), total - 1)
    seg = jnp.sum((ends[None, :] <= i[:, None]).astype(jnp.int32), axis=1)
    grp, tile = seg // tiles, seg % tiles
    block = first[tile, grp] + i - (ends[seg] - per_seg[seg])
    return tile, block.astype(jnp.int32), grp, total[None].astype(jnp.int32)


def _moe_expert_kernel(tile_ref, block_ref, grp_ref, total_ref, xs_ref, wr_ref, br_ref, wg_ref, wu_ref,
                       wd_ref, ys_ref):
    i = pl.program_id(0)

    @pl.when(i < total_ref[0])
    def _():
        g = grp_ref[i]
        x = xs_ref[...]
        logits = _dot(x, wr_ref[...]) + br_ref[...]
        lane_i = lax.broadcasted_iota(jnp.int32, logits.shape, 1)
        lane = lane_i.astype(F32)
        is_grp = jnp.abs(lane - (EXPERTS_PER_GROUP + 0.5 * (N_GROUPS - 1))) < 0.5 * N_GROUPS
        lg = jnp.where(is_grp, logits, NEG_INF)
        gmax = jnp.max(lg, axis=-1, keepdims=True)
        denom = jnp.sum(jnp.exp(lg - gmax), axis=-1, keepdims=True)
        lg_own = jnp.sum(jnp.where(lane_i == EXPERTS_PER_GROUP + g, logits, 0.0), axis=-1, keepdims=True)
        p_grp = jnp.exp(lg_own - gmax) / denom
        le = jnp.where(lane < EXPERTS_PER_GROUP, logits, NEG_INF)
        v1, i1 = _first_lane_of_max(le, lane)
        le2 = jnp.where(lane == i1, NEG_INF, le)
        v2, i2 = _first_lane_of_max(le2, lane)
        e2 = jnp.exp(v2 - v1)
        w1 = p_grp / (1.0 + e2)
        comb = jnp.where(lane == i1, w1, 0.0) + jnp.where(lane == i2, w1 * e2, 0.0)
        hid = []
        for e in range(EXPERTS_PER_GROUP):
            a = _dot(x, wg_ref[e])
            u = _dot(x, wu_ref[e])
            s, _ = _sig_pair(a)
            hid.append((a * s * u * comb[:, e:e + 1]).astype(BF16))
        hid = jnp.concatenate(hid, axis=-1)
        wd = wd_ref[...].reshape(EXPERTS_PER_GROUP * D_FF_EXPERT, wd_ref.shape[-1])
        ys_ref[...] = _dot(hid, wd).astype(BF16)


def _moe_experts(xs, work, wrg, brg, wg, wu, wd):
    tiles, _, d = xs.shape
    tile, block, grp, total = work
    item = lambda i, t, k, g, n: (t[i], k[i], 0)
    by_group = lambda i, t, k, g, n: (g[i], 0, 0)
    grid_spec = pltpu.PrefetchScalarGridSpec(
        num_scalar_prefetch=4,
        grid=(tile.shape[0],),
        in_specs=[pl.BlockSpec((None, MOE_SUB, d), item),
                  pl.BlockSpec((None, d, LANES), by_group),
                  pl.BlockSpec((None, 1, LANES), by_group),
                  pl.BlockSpec((EXPERTS_PER_GROUP, d, D_FF_EXPERT), by_group),
                  pl.BlockSpec((EXPERTS_PER_GROUP, d, D_FF_EXPERT), by_group),
                  pl.BlockSpec((EXPERTS_PER_GROUP, D_FF_EXPERT, d), by_group)],
        out_specs=pl.BlockSpec((None, MOE_SUB, d), item))
    return pl.pallas_call(
        _moe_expert_kernel,
        grid_spec=grid_spec,
        out_shape=jax.ShapeDtypeStruct(xs.shape, BF16),
        input_output_aliases={4: 0},
        compiler_params=_params(("arbitrary",)),
        name="moe_experts",
    )(tile, block, grp, total, xs, wrg, brg, wg, wu, wd)


def _moe_combine_kernel(ys_ref, slot_ref, x1_ref, gfin_ref, y_ref, *, tm):
    tiles = [slice(u * tm, (u + 1) * tm) for u in range(x1_ref.shape[0] // tm)]
    col = lax.broadcasted_iota(jnp.int32, (tm, MOE_ROWS), 1).astype(F32)
    places = []
    for rows in tiles:
        slot = jnp.concatenate([slot_ref[rows, :]] * (MOE_ROWS // LANES), axis=-1)
        places.append(jnp.where(col == slot, 1.0, 0.0).astype(BF16))
    for u, rows in enumerate(tiles):
        y_ref[rows, :] = _rms(x1_ref[rows, :] + _dot(places[u], ys_ref[u]), gfin_ref[...])


def _moe_combine(ys, slot, x1, gfin, tm):
    b, l, d = x1.shape
    nt = l // tm
    per_step = 2 if nt % 2 == 0 else 1
    ts, ns = tm * per_step, nt // per_step
    tok = lambda w: pl.BlockSpec((None, ts, w), lambda bi, i: (bi, i, 0))
    per_tile = lambda r, w: pl.BlockSpec((per_step, r, w), lambda bi, i: (bi * ns + i, 0, 0))
    return pl.pallas_call(
        functools.partial(_moe_combine_kernel, tm=tm),
        grid=(b, ns),
        in_specs=[per_tile(MOE_ROWS, d), tok(LANES), tok(d), _const_spec(gfin.shape)],
        out_specs=tok(d),
        out_shape=jax.ShapeDtypeStruct((b, l, d), F32),
        compiler_params=_params(("parallel", "parallel")),
        name="moe_combine",
    )(ys, slot, x1, gfin)


def _rope_tables(pos):
    inv = jnp.power(ROPE_THETA, -jnp.arange(HALF, dtype=F32) / HALF)
    ang = pos.astype(F32)[:, None] * inv[None, :]
    cos, sin = jnp.cos(ang), jnp.sin(ang)
    zeros = jnp.zeros_like(sin)
    reps = LANES // ROPE_DIM
    cosp = jnp.tile(jnp.concatenate([cos, cos], axis=1), (1, reps))
    sinlo = jnp.tile(jnp.concatenate([-sin, zeros], axis=1), (1, reps))
    sinhi = jnp.tile(jnp.concatenate([zeros, sin], axis=1), (1, reps))
    return cosp, sinlo, sinhi, cos.T * (SCALE * LOG2E), sin.T * (SCALE * LOG2E)


def _prep_weights(g_mix, w_in, g_q, w_uq, g_kv, w_ukv, lb_hgrn, g_onorm, w_a_out, w_b_out, w_o,
                  g_ffn, w_rg, b_rg, w_re, b_re, w_gate, w_up, w_down, g_final):
    assert w_in.shape[0] == 1, "single-layer step"
    d = w_in.shape[1]
    w = w_in[0]
    n_a = Q_LORA + KV_LORA
    win_b = jnp.swapaxes(w, 0, 1).astype(BF16)
    win_a = jnp.concatenate([win_b[:n_a]] + [win_b[n_a:n_a + ROPE_DIM]] * (LANES // ROPE_DIM), axis=0)
    per_q = NOPE_DIM + ROPE_DIM
    wuq = jnp.pad(w_uq[0].reshape(Q_LORA, A_HEADS, per_q), ((0, 0), (0, 0), (0, HEAD_PAD - per_q)))
    wuq = wuq.reshape(Q_LORA, A_HEADS * HEAD_PAD).astype(BF16)
    wukv = w_ukv[0].reshape(KV_LORA, A_HEADS, NOPE_DIM + V_DIM)
    w_uk, w_uv = wukv[..., :NOPE_DIM], wukv[..., NOPE_DIM:]
    wuk = jnp.pad(w_uk, ((0, 0), (0, 0), (0, HEAD_PAD - NOPE_DIM))).reshape(KV_LORA, A_HEADS * HEAD_PAD).astype(BF16)
    wuvt = w_uv.reshape(KV_LORA, A_HEADS * V_DIM).T.astype(BF16)
    wq2l = jnp.pad(jnp.transpose(w_uk, (1, 2, 0)), ((0, 0), (0, HEAD_PAD - NOPE_DIM), (0, 0))).astype(BF16)
    esel = jnp.zeros((HEAD_PAD, ROPE_DIM), F32).at[ROPE_LO + jnp.arange(ROPE_DIM), jnp.arange(ROPE_DIM)].set(1.0).astype(BF16)
    wuvp = w_uv.reshape(KV_LORA, A_HEADS * V_DIM).astype(BF16)
    lb = jnp.cumsum(jax.nn.softmax(lb_hgrn.astype(F32), axis=0), axis=0)[0][None, :]
    gon = jnp.tile(g_onorm[0], 2)[None, :]
    wr = jnp.pad(w_rg[0], ((0, 0), (0, LANES - N_GROUPS))).astype(BF16)
    br = jnp.pad(b_rg[0], (0, LANES - N_GROUPS))[None, :]
    rest = LANES - EXPERTS_PER_GROUP - N_GROUPS
    wrg = jnp.concatenate([w_re[0], jnp.broadcast_to(w_rg[0], (N_GROUPS, d, N_GROUPS)),
                           jnp.zeros((N_GROUPS, d, rest), F32)], axis=2).astype(BF16)
    brg = jnp.concatenate([b_re[0], jnp.broadcast_to(b_rg[0], (N_GROUPS, N_GROUPS)),
                           jnp.zeros((N_GROUPS, rest), F32)], axis=1)[:, None, :]
    return dict(
        wrg=wrg, brg=brg,
        gmix=g_mix[0][None, :], win_a=win_a, win_b=win_b, gq=g_q[0][None, :], wuq=wuq, wuqt=wuq.T,
        gkv=g_kv[0][None, :], wuk=wuk, wuvt=wuvt, wq2l=wq2l, esel=esel, wuvp=wuvp, lb=lb, gon=gon,
        wa=w_a_out[0].astype(BF16), wb=w_b_out[0].astype(BF16), wo=w_o[0].astype(BF16),
        gffn=g_ffn[0][None, :], wr=wr, br=br, wg=w_gate[0].astype(BF16), wu=w_up[0].astype(BF16),
        wd=w_down[0].astype(BF16), gfin=g_final[None, :])


def _tile(n, want):
    t = min(n, want)
    while n % t:
        t //= 2
    return t


TOKEN_TILE = 512
HGRN_PROJ_TILE = 1024
ATTN_TILE = 512
ATTN_CHAIN = 256
ATTN_HEADS_PER_STEP = 8
HGRN_STEP = 2048
HGRN_STREAMS_PER_STEP = 16


def _run_group(x, pos, streams, past, w):
    b, l, d = x.shape
    ns, ls = streams
    tm = _tile(l, TOKEN_TILE)
    h, ckv, kpe, q, k, vt = _mla_proj(x, w["gmix"], w["win_a"], w["gq"], w["wuq"], w["wuqt"], w["gkv"],
                                      w["wuk"], w["wuvt"], _rope_tables(pos), tm, past is None)
    qb, kb, lf, vb, og, sa, sb = _hgrn_proj(h, w["win_b"], w["lb"], _tile(l, HGRN_PROJ_TILE))
    as_streams = lambda a: a.reshape(ns, ls, a.shape[-1])
    if past is None:
        oa = _attn(q, k, vt, _tile(l, ATTN_TILE), ATTN_CHAIN, ATTN_HEADS_PER_STEP)
        s0t = jnp.zeros((ns, B_HEADS, B_DV, B_DK), F32)
    else:
        cache_ckv, cache_kpe, state = past
        oa = _sample_attn(as_streams(q), cache_ckv, cache_kpe, as_streams(ckv), as_streams(kpe),
                          w["wq2l"], w["esel"], w["wuvp"]).reshape(b, l, A_HEADS * V_DIM)
        s0t = jnp.swapaxes(state.astype(F32), -1, -2)
    ob, snt = _hgrn(qb, kb, lf, vb, og, s0t, w["gon"], ns,
                    _tile(l, HGRN_STEP if past is None else HGRN_STREAMS_PER_STEP * ls))
    x1, xs, slot, meta = _out_proj(oa, ob, sa, sb, x, w["wa"], w["wb"], w["wo"], w["gffn"], w["wr"],
                                   w["br"], tm, past is None)
    work = _work_list(meta, meta.shape[0] * _max_items(tm))
    ys = _moe_experts(xs, work, w["wrg"], w["brg"], w["wg"], w["wu"], w["wd"])
    y = _moe_combine(ys, slot, x1, w["gfin"], tm)
    return y, ckv, kpe, jnp.swapaxes(snt, -1, -2)


def kernel(x_prompt, x_sample, cache_ckv, cache_kpe, state_hgrn, g_mix, w_in, g_q, w_uq, g_kv, w_ukv,
           lb_hgrn, g_onorm, w_a_out, w_b_out, w_o, g_ffn, w_rg, b_rg, w_re, b_re, w_gate, w_up, w_down,
           g_final):
    w = _prep_weights(g_mix, w_in, g_q, w_uq, g_kv, w_ukv, lb_hgrn, g_onorm, w_a_out, w_b_out, w_o,
                      g_ffn, w_rg, b_rg, w_re, b_re, w_gate, w_up, w_down, g_final)
    bp, lp, d = x_prompt.shape
    y_p, ckv_p, kpe_p, st_p = _run_group(x_prompt, jnp.arange(lp, dtype=jnp.int32), (bp, lp), None, w)

    bs, ls, _ = x_sample.shape
    past_len = cache_ckv.shape[2]
    pos_s = past_len + (jnp.arange(bs * ls, dtype=jnp.int32) % ls)
    y_s, ckv_s, kpe_s, st_s = _run_group(x_sample.reshape(1, bs * ls, d), pos_s, (bs, ls),
                                         (cache_ckv[0], jnp.swapaxes(cache_kpe[0], 1, 2), state_hgrn[0]), w)
    return (y_p, y_s.reshape(bs, ls, d),
            ckv_p[None], jnp.swapaxes(kpe_p, 1, 2)[None], st_p[None].astype(x_prompt.dtype),
            ckv_s.reshape(1, bs, ls, KV_LORA), kpe_s.reshape(1, bs, ls, ROPE_DIM),
            st_s[None].astype(state_hgrn.dtype))
```

```python
import functools

import numpy as np
import jax
import jax.numpy as jnp
from jax import lax
from jax.experimental import pallas as pl
from jax.experimental.pallas import tpu as pltpu

F32 = jnp.float32
BF16 = jnp.bfloat16

EPS = 1e-6
CHUNK = 64
A_HEADS = 8
Q_LORA = 384
KV_LORA = 256
NOPE_DIM = 64
ROPE_DIM = 32
V_DIM = 64
ROPE_THETA = 10000.0
B_HEADS = 8
B_DK = 128
B_DV = 64
N_GROUPS = 4
EXPERTS_PER_GROUP = 8
N_EXPERTS = N_GROUPS * EXPERTS_PER_GROUP
D_FF_EXPERT = 256

LANES = 128
MXU_TILE = 256
VMEM_BYTES_V7X = 64 * 1024 * 1024
VMEM_LIMIT = VMEM_BYTES_V7X - 8 * 1024 * 1024

HEAD_PAD = LANES
V_AUG = V_DIM + 16
SAMPLE_KEY_BLOCK = 1024
ROPE_LO = NOPE_DIM
ROPE_HI = NOPE_DIM + ROPE_DIM
HALF = ROPE_DIM // 2
SCALE = (NOPE_DIM + ROPE_DIM) ** -0.5
LOG2E = 1.4426950408889634
NEG_INF = float("-inf")


def _params(sem):
    return pltpu.CompilerParams(dimension_semantics=sem, vmem_limit_bytes=VMEM_LIMIT)


def _const_spec(shape):
    nd = len(shape)
    return pl.BlockSpec(shape, lambda *_: (0,) * nd, pipeline_mode=pl.Buffered(1))


def _rms(x, g):
    ms = jnp.mean(x * x, axis=-1, keepdims=True)
    return x * lax.rsqrt(ms + EPS) * g


def _sig_pair(x):
    e = jnp.exp(-jnp.abs(x))
    r = 1.0 / (1.0 + e)
    er = e * r
    pos = x >= 0
    return jnp.where(pos, r, er), jnp.where(pos, er, r)


def _nt(a, b):
    return lax.dot_general(a, b, (((1,), (1,)), ((), ())), preferred_element_type=F32)


def _tn(a, b):
    return lax.dot_general(a, b, (((0,), (0,)), ((), ())), preferred_element_type=F32)


def _dot(a, b):
    return jnp.dot(a, b, preferred_element_type=F32)


def _rope(x, cosp, sinlo, sinhi):
    return x * cosp + pltpu.roll(x, LANES - HALF, 1) * sinlo + pltpu.roll(x, HALF, 1) * sinhi


def _mla_proj_kernel(x_ref, gmix_ref, win_ref, gq_ref, wuq_ref, wuqt_ref, gkv_ref, wuk_ref, wuvt_ref,
                     cos_ref, sinlo_ref, sinhi_ref, cost_ref, sint_ref,
                     h_ref, ckv_ref, kpe_ref, q_ref, k_ref, vt_ref, *, q_transposed, tm):
    tiles = [slice(u * tm, (u + 1) * tm) for u in range(x_ref.shape[0] // tm)]
    lane = lax.broadcasted_iota(jnp.int32, (1, LANES), 1)
    rope_lanes = (lane // ROPE_DIM) == (ROPE_LO // ROPE_DIM)

    hs = []
    for rows in tiles:
        h = _rms(x_ref[rows, :], gmix_ref[...]).astype(BF16)
        h_ref[rows, :] = h
        hs.append(h)
    zs = [_nt(h, win_ref[...]) for h in hs]
    cqns = [_rms(z[:, :Q_LORA], gq_ref[...]).astype(BF16) for z in zs]
    ckvs = [_rms(z[:, Q_LORA:Q_LORA + KV_LORA], gkv_ref[...]) for z in zs]

    for rows, cqn in zip(tiles, cqns):
        if q_transposed:
            qt = _nt(wuqt_ref[...], cqn)
            cos_t, sin_t = cost_ref[:, rows], sint_ref[:, rows]
            for hd in range(A_HEADS):
                r0 = hd * HEAD_PAD
                q_ref[r0:r0 + ROPE_LO, rows] = (qt[r0:r0 + ROPE_LO] * (SCALE * LOG2E)).astype(BF16)
                a = qt[r0 + ROPE_LO:r0 + ROPE_LO + HALF]
                b = qt[r0 + ROPE_LO + HALF:r0 + ROPE_HI]
                q_ref[r0 + ROPE_LO:r0 + ROPE_HI, rows] = jnp.concatenate(
                    [a * cos_t - b * sin_t, a * sin_t + b * cos_t], axis=0).astype(BF16)
                q_ref[r0 + ROPE_HI:r0 + HEAD_PAD, rows] = jnp.zeros((HEAD_PAD - ROPE_HI, tm), BF16)
        else:
            q = _dot(cqn, wuq_ref[...])
            cosp, sinlo, sinhi = cos_ref[rows, :], sinlo_ref[rows, :], sinhi_ref[rows, :]
            for hd in range(A_HEADS):
                sl = slice(hd * HEAD_PAD, (hd + 1) * HEAD_PAD)
                qh = q[:, sl]
                qh = jnp.where(rope_lanes, _rope(qh, cosp, sinlo, sinhi), qh) * (SCALE * LOG2E)
                q_ref[rows, sl] = qh.astype(BF16)

    for rows, z, ckv in zip(tiles, zs, ckvs):
        ckv_ref[rows, :] = ckv
        ckv_bf = ckv.astype(BF16)
        cosp, sinlo, sinhi = cos_ref[rows, :], sinlo_ref[rows, :], sinhi_ref[rows, :]
        kpe_rot = _rope(z[:, Q_LORA + KV_LORA:], cosp, sinlo, sinhi)
        if q_transposed:
            kpe_ref[:, rows] = kpe_rot.T[:ROPE_DIM]
        else:
            kpe_ref[rows, :] = kpe_rot[:, :ROPE_DIM]
        kpe_placed = jnp.where(rope_lanes, kpe_rot, 0.0)
        kn = _dot(ckv_bf, wuk_ref[...])
        for hd in range(A_HEADS):
            sl = slice(hd * HEAD_PAD, (hd + 1) * HEAD_PAD)
            k_ref[rows, sl] = (kn[:, sl] + kpe_placed).astype(BF16)
        vt = _nt(wuvt_ref[...], ckv_bf).astype(BF16)
        ones = jnp.ones((V_AUG - V_DIM, tm), BF16)
        for hd in range(A_HEADS):
            vt_ref[hd * V_AUG:hd * V_AUG + V_DIM, rows] = vt[hd * V_DIM:(hd + 1) * V_DIM]
            vt_ref[hd * V_AUG + V_DIM:(hd + 1) * V_AUG, rows] = ones


def _mla_proj(x, gmix, win_a, gq, wuq, wuqt, gkv, wuk, wuvt, tables, tm, q_transposed):
    b, l, d = x.shape
    hq = A_HEADS * HEAD_PAD
    hv = A_HEADS * V_AUG
    cosp, sinlo, sinhi, cos_t, sin_t = tables
    per_step = 2 if (l // tm) % 2 == 0 else 1
    ts = tm * per_step
    tok = lambda w: pl.BlockSpec((None, ts, w), lambda bi, i: (bi, i, 0))
    tok_t = lambda w: pl.BlockSpec((None, w, ts), lambda bi, i: (bi, 0, i))
    tab = pl.BlockSpec((ts, LANES), lambda bi, i: (i, 0))
    tab_t = pl.BlockSpec((HALF, ts), lambda bi, i: (0, i))
    return pl.pallas_call(
        functools.partial(_mla_proj_kernel, q_transposed=q_transposed, tm=tm),
        grid=(b, l // ts),
        in_specs=[tok(d), _const_spec(gmix.shape), _const_spec(win_a.shape), _const_spec(gq.shape),
                  _const_spec(wuq.shape), _const_spec(wuqt.shape), _const_spec(gkv.shape),
                  _const_spec(wuk.shape), _const_spec(wuvt.shape), tab, tab, tab, tab_t, tab_t],
        out_specs=[tok(d), tok(KV_LORA), tok_t(ROPE_DIM) if q_transposed else tok(ROPE_DIM),
                   tok_t(hq) if q_transposed else tok(hq), tok(hq),
                   tok_t(hv)],
        out_shape=[jax.ShapeDtypeStruct((b, l, d), BF16),
                   jax.ShapeDtypeStruct((b, l, KV_LORA), F32),
                   jax.ShapeDtypeStruct((b, ROPE_DIM, l) if q_transposed else (b, l, ROPE_DIM), F32),
                   jax.ShapeDtypeStruct((b, hq, l) if q_transposed else (b, l, hq), BF16),
                   jax.ShapeDtypeStruct((b, l, hq), BF16),
                   jax.ShapeDtypeStruct((b, hv, l), BF16)],
        compiler_params=_params(("parallel", "parallel")),
        name="mla_proj",
    )(x, gmix, win_a, gq, wuq, wuqt, gkv, wuk, wuvt, cosp, sinlo, sinhi, cos_t, sin_t)


def _hgrn_proj_kernel(h_ref, wt_ref, lb_ref, qb_ref, kb_ref, lf_ref, vb_ref, og_ref, sa_ref, sb_ref, *, row0):
    h = h_ref[...]
    dk = B_HEADS * B_DK
    dv = B_HEADS * B_DV
    dm = sa_ref.shape[-1]
    nb = MXU_TILE
    lb_all = lb_ref[...]

    def z(base, i):
        return _nt(h, wt_ref[row0 + base + i * nb:row0 + base + (i + 1) * nb, :])

    for i in range(dk // nb):
        cols = slice(i * nb, (i + 1) * nb)
        zq = z(0, i)
        qb_ref[:, cols] = zq * jax.nn.sigmoid(zq)
        sf, snf = _sig_pair(z(dk, i))
        lb = lb_all[:, cols]
        lf_ref[:, cols] = jnp.log(lb + (1.0 - lb) * sf)
        kb_ref[:, cols] = (1.0 - lb) * snf
    for i in range(dv // nb):
        cols = slice(i * nb, (i + 1) * nb)
        vb_ref[:, cols] = z(2 * dk, i).astype(BF16)
        zg = z(2 * dk + dv, i)
        og_ref[:, cols] = (zg * jax.nn.sigmoid(zg)).astype(BF16)
    for i in range(dm // nb):
        cols = slice(i * nb, (i + 1) * nb)
        sa_ref[:, cols] = jax.nn.sigmoid(z(2 * dk + 2 * dv, i)).astype(BF16)
        sb_ref[:, cols] = jax.nn.sigmoid(z(2 * dk + 2 * dv + dm, i)).astype(BF16)


def _hgrn_proj(h, win_b, lb, tm):
    b, l, d = h.shape
    dk = B_HEADS * B_DK
    dv = B_HEADS * B_DV
    tok = lambda w: pl.BlockSpec((None, tm, w), lambda bi, i: (bi, i, 0))
    sds = lambda w, dt: jax.ShapeDtypeStruct((b, l, w), dt)
    return pl.pallas_call(
        functools.partial(_hgrn_proj_kernel, row0=Q_LORA + KV_LORA + ROPE_DIM),
        grid=(b, l // tm),
        in_specs=[tok(d), _const_spec(win_b.shape), _const_spec(lb.shape)],
        out_specs=[tok(dk), tok(dk), tok(dk), tok(dv), tok(dv), tok(d), tok(d)],
        out_shape=[sds(dk, F32), sds(dk, F32), sds(dk, F32), sds(dv, BF16), sds(dv, BF16),
                   sds(d, BF16), sds(d, BF16)],
        compiler_params=_params(("parallel", "parallel")),
        name="hgrn_proj",
    )(h, win_b, lb)


def _attn_kernel(qt_ref, k_ref, vt_ref, o_ref, s_sc, m_sc, acc_sc, *, t, tc, hp):
    qi = pl.program_id(2)
    n_chains = t // tc
    m_sc[...] = jnp.full(m_sc.shape, NEG_INF, F32)
    acc_sc[...] = jnp.zeros(acc_sc.shape, F32)
    qk_rows = [slice(hd * HEAD_PAD, (hd + 1) * HEAD_PAD) for hd in range(hp)]
    v_rows = [slice(hd * V_AUG, (hd + 1) * V_AUG) for hd in range(hp)]

    def scores(kb, slot):
        start = pl.multiple_of(kb * t, t)
        for hd in range(hp):
            k = k_ref[pl.ds(start, t), qk_rows[hd]]
            for ch in range(n_chains):
                cols = slice(ch * tc, (ch + 1) * tc)
                s_sc[hd, slot, :, cols] = _dot(k, qt_ref[qk_rows[hd], cols])

    def consume(kb, slot, masked):
        start = pl.multiple_of(kb * t, t)
        for hd in range(hp):
            for ch in range(n_chains):
                cols = slice(ch * tc, (ch + 1) * tc)
                nk = (ch + 1) * tc if masked else t
                vt = vt_ref[v_rows[hd], pl.ds(start, nk)]
                s = s_sc[hd, slot, 0:nk, cols]
                if masked:
                    r = lax.broadcasted_iota(jnp.int32, (nk, tc), 0) // CHUNK
                    c = (lax.broadcasted_iota(jnp.int32, (nk, tc), 1) + ch * tc) // CHUNK
                    s = jnp.where(r <= c, s, NEG_INF)
                m_prev = m_sc[hd, :, cols]
                m_new = jnp.maximum(m_prev, jnp.max(s, axis=0, keepdims=True))
                alpha = jnp.exp2(m_prev - m_new)
                p = jnp.exp2(s - m_new)
                acc_sc[hd, :, cols] = alpha * acc_sc[hd, :, cols] + _dot(vt, p.astype(BF16))
                m_sc[hd, :, cols] = m_new

    scores(0, 0)

    def body(j, carry):
        kb = 2 * j
        scores(kb + 1, 1)
        consume(kb, 0, False)
        scores(kb + 2, 0)
        consume(kb + 1, 1, False)
        return carry

    lax.fori_loop(0, qi // 2, body, 0)

    @pl.when(qi % 2 == 0)
    def _():
        consume(qi, 0, True)

    @pl.when(qi % 2 == 1)
    def _():
        scores(qi, 1)
        consume(qi - 1, 0, False)
        consume(qi, 1, True)

    for hd in range(hp):
        acc = acc_sc[hd]
        o_ref[hd * V_DIM:(hd + 1) * V_DIM, :] = (acc[:V_DIM] / acc[V_DIM:V_DIM + 1]).astype(BF16)


def _attn(qt, k, vt, t, tc, hp):
    b, l, _ = k.shape
    once = dict(pipeline_mode=pl.Buffered(1)) if hp == A_HEADS else {}
    return pl.pallas_call(
        functools.partial(_attn_kernel, t=t, tc=min(tc, t), hp=hp),
        grid=(b, A_HEADS // hp, l // t),
        in_specs=[pl.BlockSpec((None, hp * HEAD_PAD, t), lambda bi, h, i: (bi, h, i)),
                  pl.BlockSpec((None, l, hp * HEAD_PAD), lambda bi, h, i: (bi, 0, h), **once),
                  pl.BlockSpec((None, hp * V_AUG, l), lambda bi, h, i: (bi, h, 0), **once)],
        out_specs=pl.BlockSpec((None, hp * V_DIM, t), lambda bi, h, i: (bi, h, i)),
        out_shape=jax.ShapeDtypeStruct((b, A_HEADS * V_DIM, l), BF16),
        scratch_shapes=[pltpu.VMEM((hp, 2, t, t), F32), pltpu.VMEM((hp, 1, t), F32),
                        pltpu.VMEM((hp, V_AUG, t), F32)],
        compiler_params=_params(("parallel", "parallel", "arbitrary")),
        name="attn",
    )(qt, k, vt)


def _sample_attn_kernel(q_ref, cc_ref, ckt_ref, nc_ref, nk_ref, wq2l_ref, esel_ref, wuv_ref, o_ref,
                        *, past, n_new):
    qs = q_ref[...]
    heads = [qs[:, hd * HEAD_PAD:(hd + 1) * HEAD_PAD] for hd in range(A_HEADS)]
    ql = jnp.concatenate([_dot(heads[hd], wq2l_ref[hd]) for hd in range(A_HEADS)], axis=0).astype(BF16)
    qp = jnp.concatenate([_dot(heads[hd], esel_ref[...]) for hd in range(A_HEADS)], axis=0).astype(BF16)
    nc = nc_ref[...].astype(BF16)
    nk = nk_ref[...].astype(BF16)
    rows = A_HEADS * n_new

    s_n = _nt(ql, nc) + _nt(qp, nk)
    tq = lax.broadcasted_iota(jnp.int32, (rows, n_new), 0) % n_new
    tk = lax.broadcasted_iota(jnp.int32, (rows, n_new), 1)
    s_n = jnp.where((past + tk) // CHUNK <= (past + tq) // CHUNK, s_n, NEG_INF)
    m = jnp.max(s_n, axis=-1, keepdims=True)
    p_n = jnp.exp2(s_n - m)
    denom = jnp.sum(p_n, axis=-1, keepdims=True)
    acc = _dot(p_n.astype(BF16), nc)

    n_blocks = max(1, past // SAMPLE_KEY_BLOCK)
    assert past % n_blocks == 0
    kb = past // n_blocks

    def scores(i):
        cc = cc_ref[i * kb:(i + 1) * kb, :].astype(BF16)
        ckt = ckt_ref[:, i * kb:(i + 1) * kb].astype(BF16)
        return _nt(ql, cc) + _dot(qp, ckt), cc

    nxt = scores(0)
    for i in range(n_blocks):
        s_c, cc = nxt
        if i + 1 < n_blocks:
            nxt = scores(i + 1)
        m_new = jnp.maximum(m, jnp.max(s_c, axis=-1, keepdims=True))
        alpha = jnp.exp2(m - m_new)
        p_c = jnp.exp2(s_c - m_new)
        denom = alpha * denom + jnp.sum(p_c, axis=-1, keepdims=True)
        acc = alpha * acc + _dot(p_c.astype(BF16), cc)
        m = m_new
    o_lat = (acc / denom).astype(BF16)
    full = _dot(o_lat, wuv_ref[...])
    col_head = lax.broadcasted_iota(jnp.int32, (n_new, A_HEADS * V_DIM), 1) // V_DIM
    out = jnp.zeros((n_new, A_HEADS * V_DIM), F32)
    for hd in range(A_HEADS):
        out = out + jnp.where(col_head == hd, full[hd * n_new:(hd + 1) * n_new], 0.0)
    o_ref[...] = out.astype(BF16)


def _sample_attn(q, cache_ckv, cache_kpe_t, ckv_new, kpe_new, wq2l, esel, wuvp):
    nb, n_new, hq = q.shape
    past = cache_ckv.shape[1]
    hv = A_HEADS * V_DIM
    per = lambda r, w: pl.BlockSpec((None, r, w), lambda bi: (bi, 0, 0))
    return pl.pallas_call(
        functools.partial(_sample_attn_kernel, past=past, n_new=n_new),
        grid=(nb,),
        in_specs=[per(n_new, hq), per(past, KV_LORA), per(ROPE_DIM, past), per(n_new, KV_LORA),
                  per(n_new, ROPE_DIM), _const_spec(wq2l.shape), _const_spec(esel.shape),
                  _const_spec(wuvp.shape)],
        out_specs=per(n_new, hv),
        out_shape=jax.ShapeDtypeStruct((nb, n_new, hv), BF16),
        compiler_params=_params(("parallel",)),
        name="sample_attn",
    )(q, cache_ckv, cache_kpe_t, ckv_new, kpe_new, wq2l, esel, wuvp)


def _level_sizes(c):
    sizes = []
    p = c
    while p >= 2:
        sizes.append(p)
        p //= 2
    return sizes


def _level_masks(c):
    t = np.arange(c)[:, None]
    s = np.arange(c)[None, :]
    out = []
    for p in _level_sizes(c):
        out.append((t // p == s // p) & (t % p >= p // 2) & (s % p < p // 2))
    out.append(t == s)
    return np.stack(out).astype(np.float32)


def _boundary(cum, p, c):
    half = p // 2
    if p >= 8:
        parts = [jnp.broadcast_to(cum[i * p + half - 1:i * p + half, :], (p, cum.shape[1]))
                 for i in range(c // p)]
        return jnp.concatenate(parts, axis=0) if len(parts) > 1 else parts[0]
    sub = lax.broadcasted_iota(jnp.int32, (8, cum.shape[1]), 0)
    parts = []
    for g in range(c // 8):
        lo = jnp.broadcast_to(cum[g * 8 + 1:g * 8 + 2, :], (8, cum.shape[1]))
        hi = jnp.broadcast_to(cum[g * 8 + 5:g * 8 + 6, :], (8, cum.shape[1]))
        parts.append(jnp.where(sub < 4, lo, hi))
    return jnp.concatenate(parts, axis=0)


def _hgrn_kernel(q_ref, k_ref, lf_ref, v_ref, og_ref, s0t_ref, gon_ref, mask_ref, tri_ref,
                 o_ref, snt_ref, st_sc, *, c, n_chunks, per_chunk_state):
    step = pl.program_id(2)
    sizes = _level_sizes(c)
    lane = lax.broadcasted_iota(jnp.int32, (1, LANES), 1)
    low = lane < B_DV
    keep_lo = jnp.where(low, 1.0, 0.0).astype(BF16)
    keep_hi = jnp.where(low, 0.0, 1.0).astype(BF16)

    if not per_chunk_state:
        @pl.when(step == 0)
        def _():
            st_sc[...] = jnp.zeros(st_sc.shape, F32)
            st_sc[0, 0:B_DV, :] = s0t_ref[0]
            st_sc[1, B_DV:2 * B_DV, :] = s0t_ref[1]

    tri = tri_ref[...]
    rows = [slice(ci * c, (ci + 1) * c) for ci in range(n_chunks)]
    heads = [slice(j * B_DK, (j + 1) * B_DK) for j in range(2)]
    items = [(ci, j) for ci in range(n_chunks) for j in range(2)]


    cums = []
    for ci in range(n_chunks):
        lf = lf_ref[rows[ci], :]
        hi = lf.astype(BF16)
        r1 = lf - hi.astype(F32)
        mid = r1.astype(BF16)
        lo = (r1 - mid.astype(F32)).astype(BF16)
        cums.append((_dot(tri, hi) + _dot(tri, mid) + _dot(tri, lo)) * LOG2E)

    def zeros(n):
        return jnp.zeros((n, B_DK), F32)

    a_mats = {}
    for ci, j in items:
        q = q_ref[rows[ci], heads[j]]
        k = k_ref[rows[ci], heads[j]]
        cum = cums[ci][:, heads[j]]
        kbf = k.astype(BF16)
        a = _nt(q.astype(BF16), kbf) * mask_ref[len(sizes)]
        q_cols, k_cols = [], []
        for li, p in enumerate(sizes):
            half = p // 2
            if half % 8 == 0:
                for lo in range(0, c, p):
                    mid, hi = lo + half, lo + p
                    bnd = cum[mid - 1:mid, :]
                    qu = q[mid:hi] * jnp.exp2(cum[mid:hi] - bnd)
                    kl = k[lo:mid] * jnp.exp2(bnd - cum[lo:mid])
                    q_cols.append(jnp.concatenate([x for x in (zeros(mid), qu, zeros(c - hi)) if x.shape[0]], axis=0))
                    k_cols.append(jnp.concatenate([x for x in (zeros(lo), kl, zeros(c - mid)) if x.shape[0]], axis=0))
                continue
            if p == 2:
                qt = (q * jnp.exp2(lf_ref[rows[ci], heads[j]] * LOG2E)).astype(BF16)
                kt = kbf
            else:
                bnd = _boundary(cum, p, c)
                qt = (q * jnp.exp2(jnp.minimum(cum - bnd, 0.0))).astype(BF16)
                kt = (k * jnp.exp2(jnp.minimum(bnd - cum, 0.0))).astype(BF16)
            a = a + _nt(qt, kt) * mask_ref[li]
        if q_cols:
            a = a + _nt(jnp.concatenate(q_cols, axis=1).astype(BF16), jnp.concatenate(k_cols, axis=1).astype(BF16))
        a_mats[ci, j] = a.astype(BF16)

    qhats, kvs, decays, vs = {}, {}, {}, {}
    for ci, j in items:
        cum = cums[ci][:, heads[j]]
        last = cum[c - 1:c, :]
        vs[ci, j] = v_ref[rows[ci], :] * (keep_lo if j == 0 else keep_hi)
        qhats[ci, j] = (q_ref[rows[ci], heads[j]] * jnp.exp2(cum)).astype(BF16)
        khat = (k_ref[rows[ci], heads[j]] * jnp.exp2(last - cum)).astype(BF16)
        kvs[ci, j] = _tn(vs[ci, j], khat)
        decays[ci, j] = jnp.exp2(last)

    pad = jnp.zeros((B_DV, B_DK), F32)
    if not per_chunk_state:
        st = [st_sc[0], st_sc[1]]
    for ci in range(n_chunks):
        if per_chunk_state:
            st = [jnp.concatenate([s0t_ref[ci, 0], pad], axis=0), jnp.concatenate([pad, s0t_ref[ci, 1]], axis=0)]
        o_pair = jnp.zeros((c, LANES), F32)
        for j in range(2):
            o_pair = o_pair + _dot(a_mats[ci, j], vs[ci, j]) + _nt(qhats[ci, j], st[j].astype(BF16))
            st[j] = st[j] * decays[ci, j] + kvs[ci, j]
        if per_chunk_state:
            snt_ref[ci, 0] = st[0][0:B_DV, :]
            snt_ref[ci, 1] = st[1][B_DV:2 * B_DV, :]
        o2 = o_pair * o_pair
        s_lo = jnp.sum(jnp.where(low, o2, 0.0), axis=-1, keepdims=True)
        s_hi = jnp.sum(jnp.where(low, 0.0, o2), axis=-1, keepdims=True)
        ms = jnp.where(low, s_lo, s_hi) * (1.0 / B_DV)
        ob = o_pair * lax.rsqrt(ms + EPS) * gon_ref[...] * og_ref[rows[ci], :].astype(F32)
        o_ref[rows[ci], :] = ob.astype(BF16)
    if not per_chunk_state:
        st_sc[0] = st[0]
        st_sc[1] = st[1]

        @pl.when(step == pl.num_programs(2) - 1)
        def _():
            snt_ref[0] = st_sc[0, 0:B_DV, :]
            snt_ref[1] = st_sc[1, B_DV:2 * B_DV, :]


def _hgrn(qb, kb, lf, vb, og, s0t, gon, n_streams, tc):
    b, l, dk = qb.shape
    per_chunk_state = n_streams != b
    if per_chunk_state:
        assert b == 1 and l % n_streams == 0 and l // n_streams <= CHUNK
        c = l // n_streams
    else:
        c = min(CHUNK, l)
    assert l % tc == 0 and tc % c == 0
    n_chunks = tc // c
    masks = jnp.asarray(_level_masks(c))
    tri = jnp.asarray(np.tril(np.ones((c, c), np.float32)), dtype=BF16)
    pairs = B_HEADS // 2
    wide = pl.BlockSpec((None, tc, 2 * B_DK), lambda bi, p, i: (bi, i, p))
    narrow = pl.BlockSpec((None, tc, 2 * B_DV), lambda bi, p, i: (bi, i, p))
    if per_chunk_state:
        state = pl.BlockSpec((n_chunks, 2, B_DV, B_DK), lambda bi, p, i: (i, p, 0, 0))
    else:
        state = pl.BlockSpec((None, 2, B_DV, B_DK), lambda bi, p, i: (bi, p, 0, 0))
    return pl.pallas_call(
        functools.partial(_hgrn_kernel, c=c, n_chunks=n_chunks, per_chunk_state=per_chunk_state),
        grid=(b, pairs, l // tc),
        in_specs=[wide, wide, wide, narrow, narrow, state, _const_spec(gon.shape),
                  _const_spec(masks.shape), _const_spec(tri.shape)],
        out_specs=[narrow, state],
        out_shape=[jax.ShapeDtypeStruct((b, l, B_HEADS * B_DV), BF16),
                   jax.ShapeDtypeStruct((n_streams, B_HEADS, B_DV, B_DK), F32)],
        scratch_shapes=[pltpu.VMEM((2, 2 * B_DV, B_DK), F32)],
        compiler_params=_params(("parallel", "parallel", "parallel" if per_chunk_state else "arbitrary")),
        name="hgrn",
    )(qb, kb, lf, vb, og, s0t, gon, masks, tri)


MOE_SUB = 144
MOE_ROWS = 1024
_BIG_LANE = float(1 << 20)


def _max_items(tm):
    return (tm + N_GROUPS * (MOE_SUB - 1)) // MOE_SUB


def _first_lane_of_max(x, lane):
    v = jnp.max(x, axis=-1, keepdims=True)
    return v, jnp.min(jnp.where(x == v, lane, _BIG_LANE), axis=-1, keepdims=True)


def _out_proj_kernel(oa_ref, ob_ref, sa_ref, sb_ref, x_ref, wa_ref, wb_ref, wo_ref, gffn_ref,
                     wr_ref, br_ref, tri_ref, upper_ref, x1_ref, xs_ref, slot_ref, meta_ref,
                     *, oa_transposed, tm):
    tiles = [slice(u * tm, (u + 1) * tm) for u in range(x_ref.shape[0] // tm)]
    nb = MXU_TILE
    h2s = []
    for rows in tiles:
        oa = oa_ref[:, rows] if oa_transposed else oa_ref[rows, :]
        ob = ob_ref[rows, :]
        m = []
        for i in range(x_ref.shape[1] // nb):
            cols = slice(i * nb, (i + 1) * nb)
            ya = _tn(oa, wa_ref[:, cols]) if oa_transposed else _dot(oa, wa_ref[:, cols])
            yb = _dot(ob, wb_ref[:, cols])
            m.append((sa_ref[rows, cols].astype(F32) * ya + sb_ref[rows, cols].astype(F32) * yb).astype(BF16))
        x1 = x_ref[rows, :] + _dot(jnp.concatenate(m, axis=-1), wo_ref[...])
        x1_ref[rows, :] = x1
        h2s.append(_rms(x1, gffn_ref[...]).astype(BF16))

    lane = lax.broadcasted_iota(jnp.int32, (tm, LANES), 1).astype(F32)
    logits = [_dot(h2, wr_ref[...]) + br_ref[...] for h2 in h2s]
    grps = [_first_lane_of_max(jnp.where(lane < N_GROUPS, lg, NEG_INF), lane)[1] for lg in logits]
    onehots = [jnp.where(lane == grp, 1.0, 0.0) for grp in grps]
    ranks = [_dot(tri_ref[...], oh.astype(BF16)) for oh in onehots]
    items = []
    for rank in ranks:
        count = rank[tm - 1:tm, :]
        n = jnp.zeros_like(count)
        for j in range(_max_items(tm)):
            n = n + jnp.where(count > j * MOE_SUB, 1.0, 0.0)
        items.append(n)
    bases = [_dot(jnp.broadcast_to(n, (8, LANES)).astype(BF16), upper_ref[...])[0:1, :] * MOE_SUB for n in items]
    slots = [jnp.sum(oh * (base + rank - 1.0), axis=-1, keepdims=True)
             for oh, base, rank in zip(onehots, bases, ranks)]
    col = lax.broadcasted_iota(jnp.int32, (tm, MOE_ROWS), 1).astype(F32)
    for u, rows in enumerate(tiles):
        place = jnp.where(col == slots[u], 1.0, 0.0).astype(BF16)
        xs_ref[u] = _tn(place, h2s[u]).astype(BF16)
        slot_ref[rows, :] = jnp.broadcast_to(slots[u], (tm, LANES))
        meta_ref[u] = jnp.broadcast_to(items[u], (8, LANES))


def _out_proj(oa, ob, sa, sb, x, wa, wb, wo, gffn, wr, br, tm, oa_transposed):
    b, l, d = x.shape
    nt = l // tm
    per_step = 2 if nt % 2 == 0 else 1
    ts, ns = tm * per_step, nt // per_step
    assert _max_items(tm) * MOE_SUB <= MOE_ROWS
    hv = A_HEADS * V_DIM
    tri = jnp.asarray(np.tril(np.ones((tm, tm), np.float32)), dtype=BF16)
    upper = jnp.asarray(np.triu(np.ones((LANES, LANES), np.float32), 1), dtype=BF16)
    tok = lambda w: pl.BlockSpec((None, ts, w), lambda bi, i: (bi, i, 0))
    per_tile = lambda r, w: pl.BlockSpec((per_step, r, w), lambda bi, i: (bi * ns + i, 0, 0))
    oa_spec = pl.BlockSpec((None, hv, ts), lambda bi, i: (bi, 0, i)) if oa_transposed else tok(hv)
    return pl.pallas_call(
        functools.partial(_out_proj_kernel, oa_transposed=oa_transposed, tm=tm),
        grid=(b, ns),
        in_specs=[oa_spec, tok(B_HEADS * B_DV), tok(d), tok(d), tok(d), _const_spec(wa.shape),
                  _const_spec(wb.shape), _const_spec(wo.shape), _const_spec(gffn.shape),
                  _const_spec(wr.shape), _const_spec(br.shape), _const_spec(tri.shape),
                  _const_spec(upper.shape)],
        out_specs=[tok(d), per_tile(MOE_ROWS, d), tok(LANES), per_tile(8, LANES)],
        out_shape=[jax.ShapeDtypeStruct((b, l, d), F32),
                   jax.ShapeDtypeStruct((b * nt, MOE_ROWS, d), BF16),
                   jax.ShapeDtypeStruct((b, l, LANES), F32),
                   jax.ShapeDtypeStruct((b * nt, 8, LANES), F32)],
        compiler_params=_params(("parallel", "parallel")),
        name="out_proj",
    )(oa, ob, sa, sb, x, wa, wb, wo, gffn, wr, br, tri, upper)


def _work_list(meta, n_items):
    n = meta[:, 0, :N_GROUPS].astype(jnp.int32)
    tiles = n.shape[0]
    first = jnp.cumsum(n, axis=1) - n
    per_seg = n.T.reshape(-1)
    ends = jnp.cumsum(per_seg)
    total = ends[-1]
    i = jnp.minimum(jnp.arange(n_items, dtype=jnp.int32), total - 1)
    seg = jnp.sum((ends[None, :] <= i[:, None]).astype(jnp.int32), axis=1)
    grp, tile = seg // tiles, seg % tiles
    block = first[tile, grp] + i - (ends[seg] - per_seg[seg])
    return tile, block.astype(jnp.int32), grp, total[None].astype(jnp.int32)


def _moe_expert_kernel(tile_ref, block_ref, grp_ref, total_ref, xs_ref, wr_ref, br_ref, wg_ref, wu_ref,
                       wd_ref, ys_ref):
    i = pl.program_id(0)

    @pl.when(i < total_ref[0])
    def _():
        g = grp_ref[i]
        x = xs_ref[...]
        logits = _dot(x, wr_ref[...]) + br_ref[...]
        lane_i = lax.broadcasted_iota(jnp.int32, logits.shape, 1)
        lane = lane_i.astype(F32)
        is_grp = jnp.abs(lane - (EXPERTS_PER_GROUP + 0.5 * (N_GROUPS - 1))) < 0.5 * N_GROUPS
        lg = jnp.where(is_grp, logits, NEG_INF)
        gmax = jnp.max(lg, axis=-1, keepdims=True)
        denom = jnp.sum(jnp.exp(lg - gmax), axis=-1, keepdims=True)
        lg_own = jnp.sum(jnp.where(lane_i == EXPERTS_PER_GROUP + g, logits, 0.0), axis=-1, keepdims=True)
        p_grp = jnp.exp(lg_own - gmax) / denom
        le = jnp.where(lane < EXPERTS_PER_GROUP, logits, NEG_INF)
        v1, i1 = _first_lane_of_max(le, lane)
        le2 = jnp.where(lane == i1, NEG_INF, le)
        v2, i2 = _first_lane_of_max(le2, lane)
        e2 = jnp.exp(v2 - v1)
        w1 = p_grp / (1.0 + e2)
        comb = jnp.where(lane == i1, w1, 0.0) + jnp.where(lane == i2, w1 * e2, 0.0)
        hid = []
        for e in range(EXPERTS_PER_GROUP):
            a = _dot(x, wg_ref[e])
            u = _dot(x, wu_ref[e])
            hid.append((a * jax.nn.sigmoid(a) * u * comb[:, e:e + 1]).astype(BF16))
        hid = jnp.concatenate(hid, axis=-1)
        wd = wd_ref[...].reshape(EXPERTS_PER_GROUP * D_FF_EXPERT, wd_ref.shape[-1])
        ys_ref[...] = _dot(hid, wd).astype(BF16)


def _moe_experts(xs, work, wrg, brg, wg, wu, wd):
    tiles, _, d = xs.shape
    tile, block, grp, total = work
    item = lambda i, t, k, g, n: (t[i], k[i], 0)
    by_group = lambda i, t, k, g, n: (g[i], 0, 0)
    grid_spec = pltpu.PrefetchScalarGridSpec(
        num_scalar_prefetch=4,
        grid=(tile.shape[0],),
        in_specs=[pl.BlockSpec((None, MOE_SUB, d), item),
                  pl.BlockSpec((None, d, LANES), by_group),
                  pl.BlockSpec((None, 1, LANES), by_group),
                  pl.BlockSpec((EXPERTS_PER_GROUP, d, D_FF_EXPERT), by_group),
                  pl.BlockSpec((EXPERTS_PER_GROUP, d, D_FF_EXPERT), by_group),
                  pl.BlockSpec((EXPERTS_PER_GROUP, D_FF_EXPERT, d), by_group)],
        out_specs=pl.BlockSpec((None, MOE_SUB, d), item))
    return pl.pallas_call(
        _moe_expert_kernel,
        grid_spec=grid_spec,
        out_shape=jax.ShapeDtypeStruct(xs.shape, BF16),
        input_output_aliases={4: 0},
        compiler_params=_params(("arbitrary",)),
        name="moe_experts",
    )(tile, block, grp, total, xs, wrg, brg, wg, wu, wd)


def _moe_combine_kernel(ys_ref, slot_ref, x1_ref, gfin_ref, y_ref, *, tm):
    tiles = [slice(u * tm, (u + 1) * tm) for u in range(x1_ref.shape[0] // tm)]
    col = lax.broadcasted_iota(jnp.int32, (tm, MOE_ROWS), 1).astype(F32)
    places = []
    for rows in tiles:
        slot = jnp.concatenate([slot_ref[rows, :]] * (MOE_ROWS // LANES), axis=-1)
        places.append(jnp.where(col == slot, 1.0, 0.0).astype(BF16))
    for u, rows in enumerate(tiles):
        y_ref[rows, :] = _rms(x1_ref[rows, :] + _dot(places[u], ys_ref[u]), gfin_ref[...])


def _moe_combine(ys, slot, x1, gfin, tm):
    b, l, d = x1.shape
    nt = l // tm
    per_step = 2 if nt % 2 == 0 else 1
    ts, ns = tm * per_step, nt // per_step
    tok = lambda w: pl.BlockSpec((None, ts, w), lambda bi, i: (bi, i, 0))
    per_tile = lambda r, w: pl.BlockSpec((per_step, r, w), lambda bi, i: (bi * ns + i, 0, 0))
    return pl.pallas_call(
        functools.partial(_moe_combine_kernel, tm=tm),
        grid=(b, ns),
        in_specs=[per_tile(MOE_ROWS, d), tok(LANES), tok(d), _const_spec(gfin.shape)],
        out_specs=tok(d),
        out_shape=jax.ShapeDtypeStruct((b, l, d), F32),
        compiler_params=_params(("parallel", "parallel")),
        name="moe_combine",
    )(ys, slot, x1, gfin)


def _rope_tables(pos):
    inv = jnp.power(ROPE_THETA, -jnp.arange(HALF, dtype=F32) / HALF)
    ang = pos.astype(F32)[:, None] * inv[None, :]
    cos, sin = jnp.cos(ang), jnp.sin(ang)
    zeros = jnp.zeros_like(sin)
    reps = LANES // ROPE_DIM
    cosp = jnp.tile(jnp.concatenate([cos, cos], axis=1), (1, reps))
    sinlo = jnp.tile(jnp.concatenate([-sin, zeros], axis=1), (1, reps))
    sinhi = jnp.tile(jnp.concatenate([zeros, sin], axis=1), (1, reps))
    return cosp, sinlo, sinhi, cos.T * (SCALE * LOG2E), sin.T * (SCALE * LOG2E)


def _prep_weights(g_mix, w_in, g_q, w_uq, g_kv, w_ukv, lb_hgrn, g_onorm, w_a_out, w_b_out, w_o,
                  g_ffn, w_rg, b_rg, w_re, b_re, w_gate, w_up, w_down, g_final):
    assert w_in.shape[0] == 1, "single-layer step"
    d = w_in.shape[1]
    w = w_in[0]
    n_a = Q_LORA + KV_LORA
    win_b = jnp.swapaxes(w, 0, 1).astype(BF16)
    win_a = jnp.concatenate([win_b[:n_a]] + [win_b[n_a:n_a + ROPE_DIM]] * (LANES // ROPE_DIM), axis=0)
    per_q = NOPE_DIM + ROPE_DIM
    wuq = jnp.pad(w_uq[0].reshape(Q_LORA, A_HEADS, per_q), ((0, 0), (0, 0), (0, HEAD_PAD - per_q)))
    wuq = wuq.reshape(Q_LORA, A_HEADS * HEAD_PAD).astype(BF16)
    wukv = w_ukv[0].reshape(KV_LORA, A_HEADS, NOPE_DIM + V_DIM)
    w_uk, w_uv = wukv[..., :NOPE_DIM], wukv[..., NOPE_DIM:]
    wuk = jnp.pad(w_uk, ((0, 0), (0, 0), (0, HEAD_PAD - NOPE_DIM))).reshape(KV_LORA, A_HEADS * HEAD_PAD).astype(BF16)
    wuvt = w_uv.reshape(KV_LORA, A_HEADS * V_DIM).T.astype(BF16)
    wq2l = jnp.pad(jnp.transpose(w_uk, (1, 2, 0)), ((0, 0), (0, HEAD_PAD - NOPE_DIM), (0, 0))).astype(BF16)
    esel = jnp.zeros((HEAD_PAD, ROPE_DIM), F32).at[ROPE_LO + jnp.arange(ROPE_DIM), jnp.arange(ROPE_DIM)].set(1.0).astype(BF16)
    wuvp = w_uv.reshape(KV_LORA, A_HEADS * V_DIM).astype(BF16)
    lb = jnp.cumsum(jax.nn.softmax(lb_hgrn.astype(F32), axis=0), axis=0)[0][None, :]
    gon = jnp.tile(g_onorm[0], 2)[None, :]
    wr = jnp.pad(w_rg[0], ((0, 0), (0, LANES - N_GROUPS))).astype(BF16)
    br = jnp.pad(b_rg[0], (0, LANES - N_GROUPS))[None, :]
    rest = LANES - EXPERTS_PER_GROUP - N_GROUPS
    wrg = jnp.concatenate([w_re[0], jnp.broadcast_to(w_rg[0], (N_GROUPS, d, N_GROUPS)),
                           jnp.zeros((N_GROUPS, d, rest), F32)], axis=2).astype(BF16)
    brg = jnp.concatenate([b_re[0], jnp.broadcast_to(b_rg[0], (N_GROUPS, N_GROUPS)),
                           jnp.zeros((N_GROUPS, rest), F32)], axis=1)[:, None, :]
    return dict(
        wrg=wrg, brg=brg,
        gmix=g_mix[0][None, :], win_a=win_a, win_b=win_b, gq=g_q[0][None, :], wuq=wuq, wuqt=wuq.T,
        gkv=g_kv[0][None, :], wuk=wuk, wuvt=wuvt, wq2l=wq2l, esel=esel, wuvp=wuvp, lb=lb, gon=gon,
        wa=w_a_out[0].astype(BF16), wb=w_b_out[0].astype(BF16), wo=w_o[0].astype(BF16),
        gffn=g_ffn[0][None, :], wr=wr, br=br, wg=w_gate[0].astype(BF16), wu=w_up[0].astype(BF16),
        wd=w_down[0].astype(BF16), gfin=g_final[None, :])


def _tile(n, want):
    t = min(n, want)
    while n % t:
        t //= 2
    return t


TOKEN_TILE = 512
HGRN_PROJ_TILE = 1024
ATTN_TILE = 512
ATTN_CHAIN = 256
ATTN_HEADS_PER_STEP = 8
HGRN_STEP = 2048
HGRN_STREAMS_PER_STEP = 16


def _run_group(x, pos, streams, past, w):
    b, l, d = x.shape
    ns, ls = streams
    tm = _tile(l, TOKEN_TILE)
    h, ckv, kpe, q, k, vt = _mla_proj(x, w["gmix"], w["win_a"], w["gq"], w["wuq"], w["wuqt"], w["gkv"],
                                      w["wuk"], w["wuvt"], _rope_tables(pos), tm, past is None)
    qb, kb, lf, vb, og, sa, sb = _hgrn_proj(h, w["win_b"], w["lb"], _tile(l, HGRN_PROJ_TILE))
    as_streams = lambda a: a.reshape(ns, ls, a.shape[-1])
    if past is None:
        oa = _attn(q, k, vt, _tile(l, ATTN_TILE), ATTN_CHAIN, ATTN_HEADS_PER_STEP)
        s0t = jnp.zeros((ns, B_HEADS, B_DV, B_DK), F32)
    else:
        cache_ckv, cache_kpe, state = past
        oa = _sample_attn(as_streams(q), cache_ckv, cache_kpe, as_streams(ckv), as_streams(kpe),
                          w["wq2l"], w["esel"], w["wuvp"]).reshape(b, l, A_HEADS * V_DIM)
        s0t = jnp.swapaxes(state.astype(F32), -1, -2)
    ob, snt = _hgrn(qb, kb, lf, vb, og, s0t, w["gon"], ns,
                    _tile(l, HGRN_STEP if past is None else HGRN_STREAMS_PER_STEP * ls))
    x1, xs, slot, meta = _out_proj(oa, ob, sa, sb, x, w["wa"], w["wb"], w["wo"], w["gffn"], w["wr"],
                                   w["br"], tm, past is None)
    work = _work_list(meta, meta.shape[0] * _max_items(tm))
    ys = _moe_experts(xs, work, w["wrg"], w["brg"], w["wg"], w["wu"], w["wd"])
    y = _moe_combine(ys, slot, x1, w["gfin"], tm)
    return y, ckv, kpe, jnp.swapaxes(snt, -1, -2)


def kernel(x_prompt, x_sample, cache_ckv, cache_kpe, state_hgrn, g_mix, w_in, g_q, w_uq, g_kv, w_ukv,
           lb_hgrn, g_onorm, w_a_out, w_b_out, w_o, g_ffn, w_rg, b_rg, w_re, b_re, w_gate, w_up, w_down,
           g_final):
    w = _prep_weights(g_mix, w_in, g_q, w_uq, g_kv, w_ukv, lb_hgrn, g_onorm, w_a_out, w_b_out, w_o,
                      g_ffn, w_rg, b_rg, w_re, b_re, w_gate, w_up, w_down, g_final)
    bp, lp, d = x_prompt.shape
    y_p, ckv_p, kpe_p, st_p = _run_group(x_prompt, jnp.arange(lp, dtype=jnp.int32), (bp, lp), None, w)

    bs, ls, _ = x_sample.shape
    past_len = cache_ckv.shape[2]
    pos_s = past_len + (jnp.arange(bs * ls, dtype=jnp.int32) % ls)
    y_s, ckv_s, kpe_s, st_s = _run_group(x_sample.reshape(1, bs * ls, d), pos_s, (bs, ls),
                                         (cache_ckv[0], jnp.swapaxes(cache_kpe[0], 1, 2), state_hgrn[0]), w)
    return (y_p, y_s.reshape(bs, ls, d),
            ckv_p[None], jnp.swapaxes(kpe_p, 1, 2)[None], st_p[None].astype(x_prompt.dtype),
            ckv_s.reshape(1, bs, ls, KV_LORA), kpe_s.reshape(1, bs, ls, ROPE_DIM),
            st_s[None].astype(state_hgrn.dtype))
```
